```python
import jax
import jax.numpy as jnp
from jax import lax
import numpy as np

D_MODEL = 2048
BATCH = 2
SEQ = 4096
DEPTH = 2

GRID_W = 64
CTX_LEN = 256
EPS = 1e-6
ROPE_THETA = 10000.0
NEG_INF = -1e30
Q_BLOCK = 128

MLA_HEADS = 8
MLA_Q_RANK = 512
MLA_KV_RANK = 512
MLA_NOPE = 128
MLA_ROPE = 64
MLA_QK = MLA_NOPE + MLA_ROPE
MLA_V = 128
NA_HEADS = 4
NA_HD = 128
NA_ROWS = 8
NA_COLS = 16
NA_QC = 16
NA_KC = 2 * NA_QC
GQA_HEADS = 8
GQA_KV_HEADS = 2
GQA_HD = 64
GQA_WINDOW = 128
GQA_BLOCK = GQA_WINDOW
MOE_GROUPS = 4
MOE_PER_GROUP = 8
MOE_EXPERTS = MOE_GROUPS * MOE_PER_GROUP
MOE_TOPK = 2
MOE_HIDDEN = 512

IN_SIZES = (MLA_Q_RANK, MLA_KV_RANK, MLA_ROPE,
            NA_HEADS * NA_HD, NA_HEADS * NA_HD, NA_HEADS * NA_HD,
            GQA_HEADS * GQA_HD, GQA_KV_HEADS * GQA_HD, GQA_KV_HEADS * GQA_HD,
            D_MODEL, D_MODEL, D_MODEL)
IN_COLS = sum(IN_SIZES)
IN_SPLITS = tuple(int(s) for s in np.cumsum(IN_SIZES)[:-1])

kernel_name = 'hybrid_mla_natten_swa_hmoe_dit'


def rms_norm(x, g):
    xf = x.astype(jnp.float32)
    xf = xf * lax.rsqrt(jnp.mean(xf * xf, axis=-1, keepdims=True) + EPS)
    return (xf * g.astype(jnp.float32)).astype(x.dtype)


def modulate(h, shift, scale):
    return h * (1 + scale) + shift


def split_heads(t, n_heads):
    return t.reshape(t.shape[0], t.shape[1], n_heads, -1)


def axial_rope_tables(n_tokens, rot_dim):
    t = jnp.arange(n_tokens)
    row = (t // GRID_W).astype(jnp.float32)
    col = (t % GRID_W).astype(jnp.float32)
    n_freq = rot_dim // 4
    inv = ROPE_THETA ** (-jnp.arange(n_freq, dtype=jnp.float32) / n_freq)
    ang = jnp.concatenate([row[:, None] * inv, col[:, None] * inv], axis=-1)
    return jnp.cos(ang), jnp.sin(ang)


def apply_rope(x, cos, sin):
    half = x.shape[-1] // 2
    xf = x.astype(jnp.float32)
    x1, x2 = xf[..., :half], xf[..., half:]
    c, s = cos[:, None, :], sin[:, None, :]
    return jnp.concatenate([x1 * c - x2 * s, x1 * s + x2 * c], axis=-1).astype(x.dtype)


def attn_probs(s, sink=None):
    if sink is None:
        return jax.nn.softmax(s, axis=-1)
    s = jnp.concatenate([s, jnp.broadcast_to(sink, s.shape[:-1] + (1,))], axis=-1)
    return jax.nn.softmax(s, axis=-1)[..., :-1]


def context_attention(q, k, v, sink=None):
    B, C, Hq, dq = q.shape
    Hk = k.shape[2]
    G = Hq // Hk
    qg = q.reshape(B, C, Hk, G, dq)
    s = jnp.einsum('bqhgd,bshd->bhgqs', qg, k).astype(jnp.float32) * (dq ** -0.5)
    snk = None if sink is None else sink.astype(jnp.float32).reshape(Hk, G)[None, :, :, None, None]
    p = attn_probs(s, snk).astype(v.dtype)
    o = jnp.einsum('bhgqs,bshd->bqhgd', p, v)
    return o.reshape(B, C, Hq * v.shape[-1])


def mla_query(cq, p, rope):
    B, T, _ = cq.shape
    q = (rms_norm(cq, p['mla_q_norm_g']) @ p['mla_w_uq']).reshape(B, T, MLA_HEADS, MLA_QK)
    q = rms_norm(q, p['mla_qn_g'])
    if rope is not None:
        q = jnp.concatenate([q[..., :MLA_NOPE], apply_rope(q[..., MLA_NOPE:], *rope)], axis=-1)
    return q


def mla_key_value(ckv, krope, p, rope):
    B, T, _ = ckv.shape
    kv = (rms_norm(ckv, p['mla_kv_norm_g']) @ p['mla_w_ukv']).reshape(B, T, MLA_HEADS, MLA_NOPE + MLA_V)
    k_rope = jnp.broadcast_to(krope[:, :, None, :], (B, T, MLA_HEADS, MLA_ROPE))
    k = rms_norm(jnp.concatenate([kv[..., :MLA_NOPE], k_rope], axis=-1), p['mla_kn_g'])
    if rope is not None:
        k = jnp.concatenate([k[..., :MLA_NOPE], apply_rope(k[..., MLA_NOPE:], *rope)], axis=-1)
    return k, kv[..., MLA_NOPE:]


def mla_latent_attention(q, k_all, v_all):
    B, S, H, dq = q.shape
    nb = S // Q_BLOCK
    qb = q.reshape(B, nb, Q_BLOCK, H, dq).transpose(1, 0, 2, 3, 4)

    def one_block(q_blk):
        s = jnp.einsum('bqhd,bkhd->bhqk', q_blk, k_all).astype(jnp.float32) * (dq ** -0.5)
        pr = jax.nn.softmax(s, axis=-1).astype(v_all.dtype)
        return jnp.einsum('bhqk,bkhd->bqhd', pr, v_all)

    o = lax.map(one_block, qb)
    return o.transpose(1, 0, 2, 3, 4).reshape(B, S, H * v_all.shape[-1])


def neighbourhood_latent_attention(q, k, v, k_ctx, v_ctx, rpb):
    B, S, H, d = q.shape
    rows = S // GRID_W
    kr = min(NA_ROWS, rows)
    ncb = GRID_W // NA_QC
    r = jnp.arange(rows)
    key_rows = jnp.clip(r - kr // 2, 0, rows - kr)[:, None] + jnp.arange(kr)[None, :]
    qcol = jnp.arange(ncb)[:, None] * NA_QC + jnp.arange(NA_QC)[None, :]
    col_start = jnp.clip(qcol - NA_COLS // 2, 0, GRID_W - NA_COLS)
    blk_start = jnp.clip(jnp.arange(ncb) * NA_QC - NA_COLS // 2, 0, GRID_W - NA_KC)
    key_cols = blk_start[:, None] + jnp.arange(NA_KC)[None, :]
    kc = key_cols[:, None, :]
    in_win = (kc >= col_start[..., None]) & (kc < col_start[..., None] + NA_COLS)
    dr = key_rows - r[:, None] + NA_ROWS - 1
    dc = jnp.clip(kc - qcol[..., None], 1 - NA_COLS, NA_COLS - 1) + NA_COLS - 1
    bias = rpb[:, dr[:, None, None, :, None], dc[None, :, :, None, :]]

    def gather(t):
        grid = t.reshape(B, rows, GRID_W, H, d)
        return grid[:, key_rows[:, None, :, None], key_cols[None, :, None, :]]

    kg, vg = gather(k), gather(v)
    qg = q.reshape(B, rows, ncb, NA_QC, H, d)
    scale = d ** -0.5
    s_loc = jnp.einsum('brjqhd,brjakhd->bhrjqak', qg, kg).astype(jnp.float32) * scale + bias.astype(jnp.float32)
    s_loc = jnp.where(in_win[None, None, None, :, :, None, :], s_loc, NEG_INF)
    n_loc = kr * NA_KC
    s_loc = s_loc.reshape(B, H, rows, ncb, NA_QC, n_loc)
    s_ctx = jnp.einsum('brjqhd,bshd->bhrjqs', qg, k_ctx).astype(jnp.float32) * scale
    pr = jax.nn.softmax(jnp.concatenate([s_loc, s_ctx], axis=-1), axis=-1).astype(v.dtype)
    p_loc = pr[..., :n_loc].reshape(B, H, rows, ncb, NA_QC, kr, NA_KC)
    o = (jnp.einsum('bhrjqak,brjakhd->brjqhd', p_loc, vg)
         + jnp.einsum('bhrjqs,bshd->brjqhd', pr[..., n_loc:], v_ctx))
    return o.reshape(B, S, H * d)


def window_gqa_latent_attention(q, k, v, k_ctx, v_ctx, sink):
    B, S, Hq, d = q.shape
    Hk = k.shape[2]
    G = Hq // Hk
    W = GQA_BLOCK
    nb = S // W

    def band(t):
        tp = jnp.pad(t, ((0, 0), (W, W), (0, 0), (0, 0))).reshape(B, nb + 2, W, Hk, d)
        return jnp.concatenate([tp[:, :-2], tp[:, 1:-1], tp[:, 2:]], axis=2)

    kw, vw = band(k), band(v)
    qb = q.reshape(B, nb, W, Hk, G, d)
    scale = d ** -0.5
    s_loc = jnp.einsum('bnqhgd,bnkhd->bhgnqk', qb, kw).astype(jnp.float32) * scale
    qi = jnp.arange(W)[:, None]
    kj = jnp.arange(3 * W)[None, :]
    kpos = jnp.arange(nb)[:, None, None] * W - W + kj[None]
    valid = (jnp.abs(kj - W - qi) <= GQA_WINDOW)[None] & (kpos >= 0) & (kpos < S)
    s_loc = jnp.where(valid, s_loc, NEG_INF)
    s_ctx = jnp.einsum('bnqhgd,bshd->bhgnqs', qb, k_ctx).astype(jnp.float32) * scale
    snk = sink.astype(jnp.float32).reshape(Hk, G)[None, :, :, None, None, None]
    pr = attn_probs(jnp.concatenate([s_loc, s_ctx], axis=-1), snk).astype(v.dtype)
    n_loc = 3 * W
    o = (jnp.einsum('bhgnqk,bnkhd->bnqhgd', pr[..., :n_loc], vw)
         + jnp.einsum('bhgnqs,bshd->bnqhgd', pr[..., n_loc:], v_ctx))
    return o.reshape(B, S, Hq * d)


def merge_branches(o_a, o_b, o_c, gate_a, gate_b, gate_c, p):
    y = (jax.nn.sigmoid(gate_a) * (o_a @ p['w_o_mla'])
         + jax.nn.sigmoid(gate_b) * (o_b @ p['w_o_na'])
         + jax.nn.sigmoid(gate_c) * (o_c @ p['w_o_gqa']))
    return y @ p['w_out']


def token_mixer(h_lat, h_ctx, p, ctx_out):
    B, S, _ = h_lat.shape
    (cq, ckv, krope, na_q, na_k, na_v, g_q, g_k, g_v,
     gate_a, gate_b, gate_c) = jnp.split(h_lat @ p['w_in'], IN_SPLITS, axis=-1)
    (cq_c, ckv_c, krope_c, na_q_c, na_k_c, na_v_c, g_q_c, g_k_c, g_v_c,
     gate_a_c, gate_b_c, gate_c_c) = jnp.split(h_ctx @ p['w_in'], IN_SPLITS, axis=-1)
    rope_mla = axial_rope_tables(S, MLA_ROPE)
    rope_gqa = axial_rope_tables(S, GQA_HD)

    ka_l, va_l = mla_key_value(ckv, krope, p, rope_mla)
    ka_c, va_c = mla_key_value(ckv_c, krope_c, p, None)
    qa_l = mla_query(cq, p, rope_mla)
    oa_l = mla_latent_attention(qa_l, jnp.concatenate([ka_l, ka_c], axis=1),
                                jnp.concatenate([va_l, va_c], axis=1))

    kb_l = rms_norm(split_heads(na_k, NA_HEADS), p['na_kn_g'])
    vb_l = split_heads(na_v, NA_HEADS)
    kb_c = rms_norm(split_heads(na_k_c, NA_HEADS), p['na_kn_g'])
    vb_c = split_heads(na_v_c, NA_HEADS)
    qb_l = rms_norm(split_heads(na_q, NA_HEADS), p['na_qn_g'])
    ob_l = neighbourhood_latent_attention(qb_l, kb_l, vb_l, kb_c, vb_c, p['na_rpb'])

    kc_l = apply_rope(rms_norm(split_heads(g_k, GQA_KV_HEADS), p['gqa_kn_g']), *rope_gqa)
    vc_l = split_heads(g_v, GQA_KV_HEADS)
    kc_c = rms_norm(split_heads(g_k_c, GQA_KV_HEADS), p['gqa_kn_g'])
    vc_c = split_heads(g_v_c, GQA_KV_HEADS)
    qc_l = apply_rope(rms_norm(split_heads(g_q, GQA_HEADS), p['gqa_qn_g']), *rope_gqa)
    oc_l = window_gqa_latent_attention(qc_l, kc_l, vc_l, kc_c, vc_c, p['gqa_sink'])

    y_lat = merge_branches(oa_l, ob_l, oc_l, gate_a, gate_b, gate_c, p)
    if not ctx_out:
        return y_lat, None
    oa_c = context_attention(mla_query(cq_c, p, None), ka_c, va_c)
    ob_c = context_attention(rms_norm(split_heads(na_q_c, NA_HEADS), p['na_qn_g']), kb_c, vb_c)
    oc_c = context_attention(rms_norm(split_heads(g_q_c, GQA_HEADS), p['gqa_qn_g']), kc_c, vc_c, p['gqa_sink'])
    y_ctx = merge_branches(oa_c, ob_c, oc_c, gate_a_c, gate_b_c, gate_c_c, p)
    return y_lat, y_ctx


def hier_moe(h, p):
    B, T, _ = h.shape
    gp = jax.nn.softmax((h @ p['moe_w_group'] + p['moe_b_group']).astype(jnp.float32), axis=-1)
    g_w, g_idx = lax.top_k(gp, 1)
    g_onehot = jax.nn.one_hot(g_idx[..., 0], MOE_GROUPS, dtype=jnp.float32)
    el = (h @ p['moe_w_expert'] + p['moe_b_expert']).astype(jnp.float32)
    el = el.reshape(B, T, MOE_GROUPS, MOE_PER_GROUP)
    el_g = jnp.einsum('btge,btg->bte', el, g_onehot)
    top_l, top_i = lax.top_k(el_g, MOE_TOPK)
    w_sel = jax.nn.softmax(top_l, axis=-1) * g_w
    expert_id = g_idx * MOE_PER_GROUP + top_i
    combine = jnp.einsum('btk,btke->bte', w_sel, jax.nn.one_hot(expert_id, MOE_EXPERTS, dtype=jnp.float32))
    hg = jnp.einsum('btd,edf->btef', h, p['moe_w_gate'])
    hu = jnp.einsum('btd,edf->btef', h, p['moe_w_up'])
    act = jax.nn.silu(hg) * hu * combine[..., None].astype(h.dtype)
    return jnp.einsum('btef,efd->btd', act, p['moe_w_down'])


def setup_inputs(seed: int = 0) -> dict:
    key = jax.random.key(seed)
    ks = iter(jax.random.split(key, 40))
    L, D = DEPTH, D_MODEL

    def nrm(shape, scale):
        return jax.random.normal(next(ks), shape, jnp.float32) * scale

    def gain(shape):
        return 1.0 + nrm(shape, 0.02)

    return {
        'x': nrm((BATCH, SEQ, D), 1.0),
        'c': nrm((BATCH, D), 1.0),
        'ctx': nrm((BATCH, CTX_LEN, D), 1.0),
        'c_ctx': nrm((D,), 1.0),
        'ada_w': nrm((L, D, 6 * D), 0.5 * D ** -0.5),
        'ada_b': nrm((L, 6 * D), 0.02),
        'norm_mix_g': gain((L, D)),
        'norm_ffn_g': gain((L, D)),
        'w_in': nrm((L, D, IN_COLS), D ** -0.5),
        'mla_q_norm_g': gain((L, MLA_Q_RANK)),
        'mla_w_uq': nrm((L, MLA_Q_RANK, MLA_HEADS * MLA_QK), MLA_Q_RANK ** -0.5),
        'mla_kv_norm_g': gain((L, MLA_KV_RANK)),
        'mla_w_ukv': nrm((L, MLA_KV_RANK, MLA_HEADS * (MLA_NOPE + MLA_V)), MLA_KV_RANK ** -0.5),
        'mla_qn_g': gain((L, MLA_QK)),
        'mla_kn_g': gain((L, MLA_QK)),
        'na_qn_g': gain((L, NA_HD)),
        'na_kn_g': gain((L, NA_HD)),
        'na_rpb': nrm((L, NA_HEADS, 2 * NA_ROWS - 1, 2 * NA_COLS - 1), 0.1),
        'gqa_qn_g': gain((L, GQA_HD)),
        'gqa_kn_g': gain((L, GQA_HD)),
        'gqa_sink': nrm((L, GQA_HEADS), 0.5),
        'w_o_mla': nrm((L, MLA_HEADS * MLA_V, D), (MLA_HEADS * MLA_V) ** -0.5),
        'w_o_na': nrm((L, NA_HEADS * NA_HD, D), (NA_HEADS * NA_HD) ** -0.5),
        'w_o_gqa': nrm((L, GQA_HEADS * GQA_HD, D), (GQA_HEADS * GQA_HD) ** -0.5),
        'w_out': nrm((L, D, D), D ** -0.5),
        'moe_w_group': nrm((L, D, MOE_GROUPS), D ** -0.5),
        'moe_b_group': nrm((L, MOE_GROUPS), 0.01),
        'moe_w_expert': nrm((L, D, MOE_EXPERTS), D ** -0.5),
        'moe_b_expert': nrm((L, MOE_EXPERTS), 0.01),
        'moe_w_gate': nrm((L, MOE_EXPERTS, D, MOE_HIDDEN), D ** -0.5),
        'moe_w_up': nrm((L, MOE_EXPERTS, D, MOE_HIDDEN), D ** -0.5),
        'moe_w_down': nrm((L, MOE_EXPERTS, MOE_HIDDEN, D), MOE_HIDDEN ** -0.5),
    }


def reference(x, c, ctx, c_ctx, ada_w, ada_b, norm_mix_g, norm_ffn_g, w_in,
              mla_q_norm_g, mla_w_uq, mla_kv_norm_g, mla_w_ukv, mla_qn_g, mla_kn_g,
              na_qn_g, na_kn_g, na_rpb, gqa_qn_g, gqa_kn_g, gqa_sink,
              w_o_mla, w_o_na, w_o_gqa, w_out,
              moe_w_group, moe_b_group, moe_w_expert, moe_b_expert,
              moe_w_gate, moe_w_up, moe_w_down):
    x_lat, x_ctx = x, ctx
    for l in range(DEPTH):
        ctx_out = l < DEPTH - 1
        p = {
            'w_in': w_in[l], 'mla_q_norm_g': mla_q_norm_g[l], 'mla_w_uq': mla_w_uq[l],
            'mla_kv_norm_g': mla_kv_norm_g[l], 'mla_w_ukv': mla_w_ukv[l],
            'mla_qn_g': mla_qn_g[l], 'mla_kn_g': mla_kn_g[l],
            'na_qn_g': na_qn_g[l], 'na_kn_g': na_kn_g[l], 'na_rpb': na_rpb[l],
            'gqa_qn_g': gqa_qn_g[l], 'gqa_kn_g': gqa_kn_g[l], 'gqa_sink': gqa_sink[l],
            'w_o_mla': w_o_mla[l], 'w_o_na': w_o_na[l], 'w_o_gqa': w_o_gqa[l], 'w_out': w_out[l],
            'moe_w_group': moe_w_group[l], 'moe_b_group': moe_b_group[l],
            'moe_w_expert': moe_w_expert[l], 'moe_b_expert': moe_b_expert[l],
            'moe_w_gate': moe_w_gate[l], 'moe_w_up': moe_w_up[l], 'moe_w_down': moe_w_down[l],
        }
        mod_l = jax.nn.silu(c) @ ada_w[l] + ada_b[l]
        mod_c = jax.nn.silu(c_ctx) @ ada_w[l] + ada_b[l]
        sh1, sc1, g1, sh2, sc2, g2 = jnp.split(mod_l[:, None, :], 6, axis=-1)
        csh1, csc1, cg1, csh2, csc2, cg2 = jnp.split(mod_c, 6, axis=-1)

        h_lat = modulate(rms_norm(x_lat, norm_mix_g[l]), sh1, sc1)
        h_ctx = modulate(rms_norm(x_ctx, norm_mix_g[l]), csh1, csc1)
        y_lat, y_ctx = token_mixer(h_lat, h_ctx, p, ctx_out)
        x_lat = x_lat + g1 * y_lat
        h = modulate(rms_norm(x_lat, norm_ffn_g[l]), sh2, sc2)
        x_lat = x_lat + g2 * hier_moe(h, p)
        if ctx_out:
            x_ctx = x_ctx + cg1 * y_ctx
            hc = modulate(rms_norm(x_ctx, norm_ffn_g[l]), csh2, csc2)
            x_ctx = x_ctx + cg2 * hier_moe(hc, p)
    return x_lat
```

```python
import functools

import numpy as np
import jax
import jax.numpy as jnp
from jax import lax
from jax.experimental import pallas as pl
from jax.experimental.pallas import tpu as pltpu

D_MODEL = 2048
BATCH = 2
SEQ = 4096
DEPTH = 2
GRID_W = 64
CTX_LEN = 256
EPS = 1e-6
ROPE_THETA = 10000.0
NEG_INF = -1e30

MLA_HEADS = 8
MLA_Q_RANK = 512
MLA_KV_RANK = 512
MLA_NOPE = 128
MLA_ROPE = 64
MLA_QK = MLA_NOPE + MLA_ROPE
MLA_V = 128
NA_HEADS = 4
NA_HD = 128
NA_ROWS = 8
NA_COLS = 16
GQA_HEADS = 8
GQA_KV_HEADS = 2
GQA_HD = 64
GQA_WINDOW = 128
MOE_GROUPS = 4
MOE_PER_GROUP = 8
MOE_EXPERTS = MOE_GROUPS * MOE_PER_GROUP
MOE_TOPK = 2
MOE_HIDDEN = 512

IN_SIZES = (MLA_Q_RANK, MLA_KV_RANK, MLA_ROPE,
            NA_HEADS * NA_HD, NA_HEADS * NA_HD, NA_HEADS * NA_HD,
            GQA_HEADS * GQA_HD, GQA_KV_HEADS * GQA_HD, GQA_KV_HEADS * GQA_HD,
            D_MODEL, D_MODEL, D_MODEL)
IN_SPLITS = tuple(int(s) for s in np.cumsum(IN_SIZES)[:-1])

N_LAT = BATCH * SEQ
N_CTX = BATCH * CTX_LEN
N_TOK = N_LAT + N_CTX
ROWS = SEQ // GRID_W

V7X_LANES = 128
V7X_VMEM_LIMIT = 56 * 1024 * 1024

_PACK_ORDER = (9, 10, 11, 0, 1, 3, 4, 5, 6, 7, 8, 2)
_PACK_COLS = 9728
_PACK_OFF = {}
_off = 0
for _i in _PACK_ORDER:
    _PACK_OFF[_i] = _off
    _off += IN_SIZES[_i]

ROUTE_COLS = V7X_LANES
MOE_TM = 256
TOK_TM = 256
MM_TM = 512

F32 = jnp.float32
BF16 = jnp.bfloat16


def _cparams(sem):
    return pltpu.CompilerParams(dimension_semantics=sem, vmem_limit_bytes=V7X_VMEM_LIMIT)


def _seg(i, tm):
    return jnp.minimum(i // (SEQ // tm), 2)


def _ada_kernel(c_ref, w_ref, b_ref, o_ref):
    c = c_ref[...]
    a = c * jax.nn.sigmoid(c)
    o_ref[0] = jnp.dot(a, w_ref[0], preferred_element_type=F32,
                       precision=lax.Precision.HIGHEST) + b_ref[0]


def _ada(c_rows, ada_w, ada_b):
    tn = 1024
    n = 6 * D_MODEL
    return pl.pallas_call(
        _ada_kernel,
        grid=(DEPTH, n // tn),
        in_specs=[
            pl.BlockSpec((8, D_MODEL), lambda l, j: (0, 0)),
            pl.BlockSpec((1, D_MODEL, tn), lambda l, j: (l, 0, j)),
            pl.BlockSpec((1, 1, tn), lambda l, j: (l, 0, j)),
        ],
        out_specs=pl.BlockSpec((1, 8, tn), lambda l, j: (l, 0, j)),
        out_shape=jax.ShapeDtypeStruct((DEPTH, 8, n), F32),
        compiler_params=_cparams(("arbitrary", "arbitrary")),
        name="ada",
    )(c_rows, ada_w, ada_b.reshape(DEPTH, 1, n))


def _norm_mod_kernel(x_ref, g_ref, sh_ref, sc_ref, h_ref):
    x = x_ref[...]
    xn = x * lax.rsqrt(jnp.mean(x * x, axis=-1, keepdims=True) + EPS) * g_ref[...]
    h_ref[...] = (xn * (1.0 + sc_ref[0]) + sh_ref[0]).astype(h_ref.dtype)


def _norm_mod_route_kernel(x_ref, g_ref, sh_ref, sc_ref, wr_ref, br_ref, h_ref, lg_ref):
    x = x_ref[...]
    xn = x * lax.rsqrt(jnp.mean(x * x, axis=-1, keepdims=True) + EPS) * g_ref[...]
    h = xn * (1.0 + sc_ref[0]) + sh_ref[0]
    h_ref[...] = h
    lg_ref[...] = jnp.dot(h, wr_ref[...], preferred_element_type=F32,
                          precision=lax.Precision.HIGHEST) + br_ref[...]


def _norm_mod(x, g, mod, k_shift, k_scale, route=None):
    m = x.shape[0]
    tm = TOK_TM
    base = [
        pl.BlockSpec((tm, D_MODEL), lambda i: (i, 0)),
        pl.BlockSpec((1, D_MODEL), lambda i: (0, 0)),
        pl.BlockSpec((1, 1, D_MODEL), lambda i: (_seg(i, tm) * 6 + k_shift, 0, 0)),
        pl.BlockSpec((1, 1, D_MODEL), lambda i: (_seg(i, tm) * 6 + k_scale, 0, 0)),
    ]
    if route is None:
        return pl.pallas_call(
            _norm_mod_kernel,
            grid=(m // tm,),
            in_specs=base,
            out_specs=pl.BlockSpec((tm, D_MODEL), lambda i: (i, 0)),
            out_shape=jax.ShapeDtypeStruct((m, D_MODEL), BF16),
            compiler_params=_cparams(("arbitrary",)),
            name="norm_mod",
        )(x, g.reshape(1, D_MODEL), mod, mod)
    wr, br = route
    return pl.pallas_call(
        _norm_mod_route_kernel,
        grid=(m // tm,),
        in_specs=base + [
            pl.BlockSpec((D_MODEL, ROUTE_COLS), lambda i: (0, 0)),
            pl.BlockSpec((1, ROUTE_COLS), lambda i: (0, 0)),
        ],
        out_specs=[pl.BlockSpec((tm, D_MODEL), lambda i: (i, 0)),
                   pl.BlockSpec((tm, ROUTE_COLS), lambda i: (i, 0))],
        out_shape=[jax.ShapeDtypeStruct((m, D_MODEL), F32),
                   jax.ShapeDtypeStruct((m, ROUTE_COLS), F32)],
        compiler_params=_cparams(("arbitrary",)),
        name="norm_mod_route",
    )(x, g.reshape(1, D_MODEL), mod, mod, wr, br)


def _mm_kernel(x_ref, w_ref, o_ref, wb_ref):
    @pl.when(pl.program_id(1) == 0)
    def _():
        wb_ref[...] = w_ref[...].astype(BF16)

    o_ref[...] = jnp.dot(x_ref[...], wb_ref[...], preferred_element_type=F32).astype(o_ref.dtype)


def _mm_res_kernel(x_ref, w_ref, r_ref, g_ref, o_ref, wb_ref):
    @pl.when(pl.program_id(1) == 0)
    def _():
        wb_ref[...] = w_ref[...].astype(BF16)

    acc = jnp.dot(x_ref[...], wb_ref[...], preferred_element_type=F32)
    o_ref[...] = r_ref[...] + g_ref[0] * acc


def _mm(x, w, out_dtype, tn):
    m, k = x.shape
    n = w.shape[1]
    tm = MM_TM
    return pl.pallas_call(
        _mm_kernel,
        grid=(n // tn, m // tm),
        in_specs=[pl.BlockSpec((tm, k), lambda j, i: (i, 0)),
                  pl.BlockSpec((k, tn), lambda j, i: (0, j))],
        out_specs=pl.BlockSpec((tm, tn), lambda j, i: (i, j)),
        out_shape=jax.ShapeDtypeStruct((m, n), out_dtype),
        scratch_shapes=[pltpu.VMEM((k, tn), BF16)],
        compiler_params=_cparams(("arbitrary", "arbitrary")),
        name="mm",
    )(x, w)


def _mm_res(x, w, res, mod, k_gate, tn):
    m, k = x.shape
    n = w.shape[1]
    tm = MM_TM
    nj = n // tn
    return pl.pallas_call(
        _mm_res_kernel,
        grid=(nj, m // tm),
        in_specs=[pl.BlockSpec((tm, k), lambda j, i: (i, 0)),
                  pl.BlockSpec((k, tn), lambda j, i: (0, j)),
                  pl.BlockSpec((tm, tn), lambda j, i: (i, j)),
                  pl.BlockSpec((1, 1, tn), lambda j, i: (_seg(i, tm) * 6 + k_gate, 0, j))],
        out_specs=pl.BlockSpec((tm, tn), lambda j, i: (i, j)),
        out_shape=jax.ShapeDtypeStruct((m, n), F32),
        scratch_shapes=[pltpu.VMEM((k, tn), BF16)],
        compiler_params=_cparams(("arbitrary", "arbitrary")),
        name="mm_res",
    )(x, w, res, mod)


def _attn_kernel(*refs, has_k2, has_sink, tk, n_chunks):
    refs = list(refs)
    sink_ref = refs.pop(0) if has_sink else None
    q_ref, k1_ref, v1_ref = refs[:3]
    refs = refs[3:]
    if has_k2:
        k2_ref, v2_ref = refs[:2]
        refs = refs[2:]
    o_ref = refs[0]

    q = q_ref[0, 0]
    tq = q.shape[0]
    dv = v1_ref.shape[-1]
    if has_sink:
        snk = sink_ref[pl.program_id(1)]
        m0 = jnp.full((tq, 1), snk, F32)
        l0 = jnp.ones((tq, 1), F32)
    else:
        m0 = jnp.full((tq, 1), NEG_INF, F32)
        l0 = jnp.zeros((tq, 1), F32)
    acc0 = jnp.zeros((tq, dv), F32)

    def step(kc, vc, carry):
        m, l, acc = carry
        s = lax.dot_general(q, kc, (((1,), (1,)), ((), ())), preferred_element_type=F32)
        m_new = jnp.maximum(m, jnp.max(s, axis=-1, keepdims=True))
        a = jnp.exp(m - m_new)
        p = jnp.exp(s - m_new)
        l = a * l + jnp.sum(p, axis=-1, keepdims=True)
        acc = a * acc + jnp.dot(p.astype(BF16), vc, preferred_element_type=F32)
        return m_new, l, acc

    def body(c, carry):
        off = pl.multiple_of(c * tk, tk)
        return step(k1_ref[0, 0, pl.ds(off, tk), :], v1_ref[0, 0, pl.ds(off, tk), :], carry)

    carry = lax.fori_loop(0, n_chunks, body, (m0, l0, acc0))
    if has_k2:
        carry = step(k2_ref[0, 0], v2_ref[0, 0], carry)
    _, l, acc = carry
    o_ref[0, 0] = (acc / l).astype(o_ref.dtype)


def _attn(q, k1, v1, k2=None, v2=None, sink=None, *, tq, tk):
    b, hq, sq, dq = q.shape
    hk, s1 = k1.shape[1], k1.shape[2]
    dv = v1.shape[-1]
    g = hq // hk
    has_k2 = k2 is not None
    has_sink = sink is not None
    nsp = 1 if has_sink else 0

    def qmap(bi, h, i, *_):
        return (bi, h, i, 0)

    def kmap(bi, h, i, *_):
        return (bi, h // g, 0, 0)

    in_specs = [pl.BlockSpec((1, 1, tq, dq), qmap),
                pl.BlockSpec((1, 1, s1, dq), kmap),
                pl.BlockSpec((1, 1, s1, dv), kmap)]
    args = [q, k1, v1]
    if has_k2:
        s2 = k2.shape[2]
        in_specs += [pl.BlockSpec((1, 1, s2, dq), kmap), pl.BlockSpec((1, 1, s2, dv), kmap)]
        args += [k2, v2]
    kern = functools.partial(_attn_kernel, has_k2=has_k2, has_sink=has_sink, tk=tk,
                             n_chunks=s1 // tk)
    gs = pltpu.PrefetchScalarGridSpec(
        num_scalar_prefetch=nsp,
        grid=(b, hq, sq // tq),
        in_specs=in_specs,
        out_specs=pl.BlockSpec((1, 1, tq, dv), qmap),
    )
    pre = [sink.astype(F32)] if has_sink else []
    return pl.pallas_call(
        kern,
        grid_spec=gs,
        out_shape=jax.ShapeDtypeStruct((b, hq, sq, dv), BF16),
        compiler_params=_cparams(("arbitrary", "arbitrary", "arbitrary")),
        name="attn",
    )(*pre, *args)


_NA_KEYS = NA_ROWS * GRID_W


def _na_kernel(q_ref, k_ref, v_ref, kc_ref, vc_ref, bias_ref, o_ref):
    r = pl.program_id(2)
    start = pl.multiple_of(jnp.clip(r - NA_ROWS // 2, 0, ROWS - NA_ROWS) * GRID_W, GRID_W)
    q = q_ref[0, 0]
    k = k_ref[0, 0, pl.ds(start, _NA_KEYS), :]
    v = v_ref[0, 0, pl.ds(start, _NA_KEYS), :]
    nt = (((1,), (1,)), ((), ()))
    s = lax.dot_general(q, k, nt, preferred_element_type=F32) + bias_ref[0, 0]
    sc = lax.dot_general(q, kc_ref[0, 0], nt, preferred_element_type=F32)
    m = jnp.maximum(jnp.max(s, axis=-1, keepdims=True), jnp.max(sc, axis=-1, keepdims=True))
    p = jnp.exp(s - m)
    pc = jnp.exp(sc - m)
    l = jnp.sum(p, axis=-1, keepdims=True) + jnp.sum(pc, axis=-1, keepdims=True)
    o = (jnp.dot(p.astype(BF16), v, preferred_element_type=F32)
         + jnp.dot(pc.astype(BF16), vc_ref[0, 0], preferred_element_type=F32))
    o_ref[0] = (o / l).astype(o_ref.dtype)


def _na_pattern(r):
    half = NA_ROWS // 2
    return jnp.where(r < half, r, jnp.where(r <= ROWS - half, half, r - (ROWS - NA_ROWS)))


def _na_attn(q, k, v, kc, vc, bias):
    def kmap(b, h, r):
        return (b, h, 0, 0)

    return pl.pallas_call(
        _na_kernel,
        grid=(BATCH, NA_HEADS, ROWS),
        in_specs=[pl.BlockSpec((1, 1, GRID_W, NA_HD), lambda b, h, r: (b, h, r, 0)),
                  pl.BlockSpec((1, 1, SEQ, NA_HD), kmap),
                  pl.BlockSpec((1, 1, SEQ, NA_HD), kmap),
                  pl.BlockSpec((1, 1, CTX_LEN, NA_HD), kmap),
                  pl.BlockSpec((1, 1, CTX_LEN, NA_HD), kmap),
                  pl.BlockSpec((1, 1, GRID_W, _NA_KEYS), lambda b, h, r: (h, _na_pattern(r), 0, 0))],
        out_specs=pl.BlockSpec((1, GRID_W, NA_HD), lambda b, h, r: (b, r, h)),
        out_shape=jax.ShapeDtypeStruct((BATCH, SEQ, NA_HEADS * NA_HD), BF16),
        compiler_params=_cparams(("arbitrary", "arbitrary", "arbitrary")),
        name="na_attn",
    )(q, k, v, kc, vc, bias)


def _na_bias_table(rpb):
    half = NA_ROWS // 2
    r_rep = np.array(list(range(half)) + [half] + list(range(ROWS - half + 1, ROWS)))
    start = np.clip(r_rep - half, 0, ROWS - NA_ROWS)
    dr = start[:, None] + np.arange(NA_ROWS)[None, :] - r_rep[:, None] + NA_ROWS - 1
    qc = np.arange(GRID_W)
    kcol = np.arange(GRID_W)
    col_start = np.clip(qc - NA_COLS // 2, 0, GRID_W - NA_COLS)
    in_win = (kcol[None, :] >= col_start[:, None]) & (kcol[None, :] < col_start[:, None] + NA_COLS)
    dc = np.clip(kcol[None, :] - qc[:, None], 1 - NA_COLS, NA_COLS - 1) + NA_COLS - 1
    b = rpb[:, dr[:, None, :, None], dc[None, :, None, :]]
    b = jnp.where(in_win[None, None, :, None, :], b.astype(F32), NEG_INF)
    return b.reshape(NA_HEADS, NA_ROWS, GRID_W, _NA_KEYS)


_GQA_G = GQA_HEADS // GQA_KV_HEADS
_GQA_BAND = 3 * GQA_WINDOW


def _gqa_kernel(sink_ref, q_ref, k_ref, v_ref, kc_ref, vc_ref, o_ref):
    hk = pl.program_id(1)
    n = pl.program_id(2)
    w = GQA_WINDOW
    start = pl.multiple_of(jnp.clip((n - 1) * w, 0, SEQ - _GQA_BAND), w)
    q = q_ref[0, 0, 0]
    k = k_ref[0, 0, pl.ds(start, _GQA_BAND), :]
    v = v_ref[0, 0, pl.ds(start, _GQA_BAND), :]
    nt = (((1,), (1,)), ((), ()))
    s = lax.dot_general(q, k, nt, preferred_element_type=F32)
    rows = lax.broadcasted_iota(jnp.int32, s.shape, 0)
    cols = lax.broadcasted_iota(jnp.int32, s.shape, 1)
    qpos = n * w + (rows & (w - 1))
    kpos = start + cols
    s = jnp.where(jnp.abs(kpos - qpos) <= GQA_WINDOW, s, NEG_INF)
    sc = lax.dot_general(q, kc_ref[0, 0], nt, preferred_element_type=F32)
    grow = lax.broadcasted_iota(jnp.int32, (_GQA_G * w, 1), 0) // w
    snk = jnp.full((_GQA_G * w, 1), sink_ref[hk * _GQA_G], F32)
    for g in range(1, _GQA_G):
        snk = jnp.where(grow == g, sink_ref[hk * _GQA_G + g], snk)
    m = jnp.maximum(jnp.maximum(jnp.max(s, axis=-1, keepdims=True),
                                jnp.max(sc, axis=-1, keepdims=True)), snk)
    p = jnp.exp(s - m)
    pc = jnp.exp(sc - m)
    l = jnp.sum(p, axis=-1, keepdims=True) + jnp.sum(pc, axis=-1, keepdims=True) + jnp.exp(snk - m)
    o = (jnp.dot(p.astype(BF16), v, preferred_element_type=F32)
         + jnp.dot(pc.astype(BF16), vc_ref[0, 0], preferred_element_type=F32))
    o_ref[0, 0, 0] = (o / l).astype(o_ref.dtype)


def _gqa_attn(q, k, v, kc, vc, sink):
    nb = SEQ // GQA_WINDOW
    rows = _GQA_G * GQA_WINDOW

    def qmap(b, h, n, *_):
        return (b, h, n, 0, 0)

    def kmap(b, h, n, *_):
        return (b, h, 0, 0)

    gs = pltpu.PrefetchScalarGridSpec(
        num_scalar_prefetch=1,
        grid=(BATCH, GQA_KV_HEADS, nb),
        in_specs=[pl.BlockSpec((1, 1, 1, rows, GQA_HD), qmap),
                  pl.BlockSpec((1, 1, SEQ, GQA_HD), kmap),
                  pl.BlockSpec((1, 1, SEQ, GQA_HD), kmap),
                  pl.BlockSpec((1, 1, CTX_LEN, GQA_HD), kmap),
                  pl.BlockSpec((1, 1, CTX_LEN, GQA_HD), kmap)],
        out_specs=pl.BlockSpec((1, 1, 1, rows, GQA_HD), qmap),
    )
    return pl.pallas_call(
        _gqa_kernel,
        grid_spec=gs,
        out_shape=jax.ShapeDtypeStruct((BATCH, GQA_KV_HEADS, nb, rows, GQA_HD), BF16),
        compiler_params=_cparams(("arbitrary", "arbitrary", "arbitrary")),
        name="gqa_attn",
    )(sink.astype(F32), q, k, v, kc, vc)


def _merge_kernel(ga_ref, gb_ref, gc_ref, oa_ref, ob_ref, oc_ref, wa_ref, wb_ref, wc_ref, y_ref):
    def branch(g_ref, o_ref, w_ref):
        return jax.nn.sigmoid(g_ref[...].astype(F32)) * jnp.dot(
            o_ref[...], w_ref[...], preferred_element_type=F32)

    y = branch(ga_ref, oa_ref, wa_ref) + branch(gb_ref, ob_ref, wb_ref) + branch(gc_ref, oc_ref, wc_ref)
    y_ref[...] = y.astype(y_ref.dtype)


def _merge(proj, oa, ob, oc, wa, wb, wc):
    m = oa.shape[0]
    tm = TOK_TM
    ka, kb, kc = oa.shape[1], ob.shape[1], oc.shape[1]
    return pl.pallas_call(
        _merge_kernel,
        grid=(m // tm,),
        in_specs=[pl.BlockSpec((tm, D_MODEL), lambda i: (i, 0)),
                  pl.BlockSpec((tm, D_MODEL), lambda i: (i, 1)),
                  pl.BlockSpec((tm, D_MODEL), lambda i: (i, 2)),
                  pl.BlockSpec((tm, ka), lambda i: (i, 0)),
                  pl.BlockSpec((tm, kb), lambda i: (i, 0)),
                  pl.BlockSpec((tm, kc), lambda i: (i, 0)),
                  pl.BlockSpec((ka, D_MODEL), lambda i: (0, 0)),
                  pl.BlockSpec((kb, D_MODEL), lambda i: (0, 0)),
                  pl.BlockSpec((kc, D_MODEL), lambda i: (0, 0))],
        out_specs=pl.BlockSpec((tm, D_MODEL), lambda i: (i, 0)),
        out_shape=jax.ShapeDtypeStruct((m, D_MODEL), BF16),
        compiler_params=_cparams(("arbitrary",)),
        name="merge",
    )(proj, proj, proj, oa, ob, oc, wa, wb, wc)


def _moe_ffn_kernel(te_ref, tok_ref, nu_ref, h_hbm, sw_ref, wg_ref, wu_ref, wd_ref, y_ref,
                    xbuf, sem, wgb, wub, wdb):
    i = pl.program_id(0)
    tm = MOE_TM
    slot = i % 2
    n_used = nu_ref[0]

    def row_copy(tok, s, r):
        return pltpu.make_async_copy(h_hbm.at[pl.ds(tok, 1)], xbuf.at[s, pl.ds(r, 1)], sem.at[s])

    def start_gather(tile, s):
        base = tile * tm

        def body(r, c):
            row_copy(tok_ref[base + r], s, r).start()
            return c

        lax.fori_loop(0, tm, body, 0)

    @pl.when(i == 0)
    def _():
        start_gather(0, 0)

    @pl.when(i + 1 < n_used)
    def _():
        start_gather(i + 1, 1 - slot)

    @pl.when(i < n_used)
    def _():
        @pl.when((i == 0) | (te_ref[i] != te_ref[jnp.maximum(i - 1, 0)]))
        def _():
            wgb[...] = wg_ref[0].astype(BF16)
            wub[...] = wu_ref[0].astype(BF16)
            wdb[...] = wd_ref[0].astype(BF16)

        def wbody(r, c):
            row_copy(0, slot, r).wait()
            return c

        lax.fori_loop(0, tm, wbody, 0)
        x = xbuf[slot].astype(BF16)
        hg = jnp.dot(x, wgb[...], preferred_element_type=F32)
        hu = jnp.dot(x, wub[...], preferred_element_type=F32)
        act = (hg * jax.nn.sigmoid(hg)) * hu * sw_ref[...]
        y_ref[...] = jnp.dot(act.astype(BF16), wdb[...], preferred_element_type=F32)

    @pl.when(i >= n_used)
    def _():
        y_ref[...] = jnp.zeros_like(y_ref)


def _moe_ffn(h, tile_expert, slot_token, n_used, slot_w, wg, wu, wd):
    p = slot_token.shape[0]
    tm = MOE_TM
    nt = p // tm
    gs = pltpu.PrefetchScalarGridSpec(
        num_scalar_prefetch=3,
        grid=(nt,),
        in_specs=[pl.BlockSpec(memory_space=pl.ANY),
                  pl.BlockSpec((tm, 1), lambda i, te, tok, nu: (i, 0)),
                  pl.BlockSpec((1, D_MODEL, MOE_HIDDEN), lambda i, te, tok, nu: (te[i], 0, 0)),
                  pl.BlockSpec((1, D_MODEL, MOE_HIDDEN), lambda i, te, tok, nu: (te[i], 0, 0)),
                  pl.BlockSpec((1, MOE_HIDDEN, D_MODEL), lambda i, te, tok, nu: (te[i], 0, 0))],
        out_specs=pl.BlockSpec((tm, D_MODEL), lambda i, te, tok, nu: (i, 0)),
        scratch_shapes=[pltpu.VMEM((2, tm, D_MODEL), F32),
                        pltpu.SemaphoreType.DMA((2,)),
                        pltpu.VMEM((D_MODEL, MOE_HIDDEN), BF16),
                        pltpu.VMEM((D_MODEL, MOE_HIDDEN), BF16),
                        pltpu.VMEM((MOE_HIDDEN, D_MODEL), BF16)],
    )
    return pl.pallas_call(
        _moe_ffn_kernel,
        grid_spec=gs,
        out_shape=jax.ShapeDtypeStruct((p, D_MODEL), F32),
        compiler_params=_cparams(("arbitrary",)),
        name="moe_ffn",
    )(tile_expert, slot_token, n_used, h, slot_w, wg, wu, wd)


def _moe_combine_kernel(pos_ref, y_hbm, x_ref, g_ref, o_ref, ybuf, sem):
    i = pl.program_id(0)
    nt = pl.num_programs(0)
    tm = TOK_TM
    slot = i % 2

    def row_copy(src, s, r):
        return pltpu.make_async_copy(y_hbm.at[pl.ds(src, 1)], ybuf.at[s, pl.ds(r, 1)], sem.at[s])

    def start_gather(tile, s):
        base = tile * tm

        def body(r, c):
            row_copy(pos_ref[2 * (base + r)], s, r).start()
            row_copy(pos_ref[2 * (base + r) + 1], s, tm + r).start()
            return c

        lax.fori_loop(0, tm, body, 0)

    @pl.when(i == 0)
    def _():
        start_gather(0, 0)

    @pl.when(i + 1 < nt)
    def _():
        start_gather(i + 1, 1 - slot)

    def wbody(r, c):
        row_copy(0, slot, r).wait()
        return c

    lax.fori_loop(0, 2 * tm, wbody, 0)
    y = ybuf[slot, pl.ds(0, tm), :] + ybuf[slot, pl.ds(tm, tm), :]
    o_ref[...] = x_ref[...] + g_ref[0] * y


def _moe_combine(y, pos, x, mod, k_gate):
    m = x.shape[0]
    tm = TOK_TM
    gs = pltpu.PrefetchScalarGridSpec(
        num_scalar_prefetch=1,
        grid=(m // tm,),
        in_specs=[pl.BlockSpec(memory_space=pl.ANY),
                  pl.BlockSpec((tm, D_MODEL), lambda i, pos: (i, 0)),
                  pl.BlockSpec((1, 1, D_MODEL), lambda i, pos: (_seg(i, tm) * 6 + k_gate, 0, 0))],
        out_specs=pl.BlockSpec((tm, D_MODEL), lambda i, pos: (i, 0)),
        scratch_shapes=[pltpu.VMEM((2, 2 * tm, D_MODEL), F32),
                        pltpu.SemaphoreType.DMA((2,))],
    )
    return pl.pallas_call(
        _moe_combine_kernel,
        grid_spec=gs,
        out_shape=jax.ShapeDtypeStruct((m, D_MODEL), F32),
        compiler_params=_cparams(("arbitrary",)),
        name="moe_combine",
    )(pos, y, x, mod)


def _route(logits, m):
    tm = MOE_TM
    gp = jax.nn.softmax(logits[:, :MOE_GROUPS], axis=-1)
    g_w, g_idx = lax.top_k(gp, 1)
    el = logits[:, MOE_GROUPS:MOE_GROUPS + MOE_EXPERTS].reshape(m, MOE_GROUPS, MOE_PER_GROUP)
    el_g = jnp.take_along_axis(el, g_idx[:, :, None], axis=1)[:, 0]
    top_l, top_i = lax.top_k(el_g, MOE_TOPK)
    w_sel = jax.nn.softmax(top_l, axis=-1) * g_w
    eid = (g_idx * MOE_PER_GROUP + top_i).astype(jnp.int32)

    a = m * MOE_TOPK
    e_flat = eid.reshape(a)
    onehot = (e_flat[:, None] == jnp.arange(MOE_EXPERTS, dtype=jnp.int32)[None, :]).astype(jnp.int32)
    csum = jnp.cumsum(onehot, axis=0)
    rank = jnp.take_along_axis(csum, e_flat[:, None], axis=1)[:, 0] - 1
    counts = csum[-1]
    padded = ((counts + tm - 1) // tm) * tm
    ends = jnp.cumsum(padded)
    starts = ends - padded
    pos = (starts[e_flat] + rank).astype(jnp.int32)
    p = a + MOE_EXPERTS * tm
    slot_token = jnp.zeros((p,), jnp.int32).at[pos].set(jnp.arange(a, dtype=jnp.int32) // MOE_TOPK)
    slot_w = jnp.zeros((p,), F32).at[pos].set(w_sel.reshape(a)).reshape(p, 1)
    n_used = (ends[-1] // tm).astype(jnp.int32).reshape(1)
    tile_start = jnp.arange(p // tm, dtype=jnp.int32) * tm
    last_e = jnp.max(jnp.where(counts > 0, jnp.arange(MOE_EXPERTS, dtype=jnp.int32), 0))
    tile_expert = jnp.minimum(jnp.searchsorted(ends, tile_start, side="right").astype(jnp.int32), last_e)
    return tile_expert, slot_token, n_used, slot_w, pos


def _rms(x, g):
    xf = x.astype(F32)
    return xf * lax.rsqrt(jnp.mean(xf * xf, axis=-1, keepdims=True) + EPS) * g.astype(F32)


def _rope_tables(rot_dim):
    t = jnp.arange(SEQ)
    row = (t // GRID_W).astype(F32)
    col = (t % GRID_W).astype(F32)
    n_freq = rot_dim // 4
    inv = ROPE_THETA ** (-jnp.arange(n_freq, dtype=F32) / n_freq)
    ang = jnp.concatenate([row[:, None] * inv, col[:, None] * inv], axis=-1)
    return jnp.cos(ang), jnp.sin(ang)


def _rope(x, cos, sin):
    half = x.shape[-1] // 2
    x1, x2 = x[..., :half], x[..., half:]
    c, s = cos[:, None, :], sin[:, None, :]
    return jnp.concatenate([x1 * c - x2 * s, x1 * s + x2 * c], axis=-1)


def _split_lat_ctx(t):
    rest = t.shape[1:]
    return t[:N_LAT].reshape((BATCH, SEQ) + rest), t[N_LAT:].reshape((BATCH, CTX_LEN) + rest)


def _bhsd(t):
    return jnp.transpose(t, (0, 2, 1, 3)).astype(BF16)


def _col(proj, idx):
    off = _PACK_OFF[idx]
    return proj[:, off:off + IN_SIZES[idx]]


def _pack_w_in(w):
    parts = jnp.split(w, IN_SPLITS, axis=1)
    pad = jnp.zeros((D_MODEL, _PACK_COLS - sum(IN_SIZES)), w.dtype)
    return jnp.concatenate([parts[i] for i in _PACK_ORDER] + [pad], axis=1).astype(BF16)


def _token_mixer(h, p, ctx_out):
    proj = _mm(h, _pack_w_in(p['w_in']), BF16, 512)

    rope_mla = _rope_tables(MLA_ROPE)
    rope_gqa = _rope_tables(GQA_HD)

    cqn = _rms(_col(proj, 0), p['mla_q_norm_g']).astype(BF16)
    q_all = _mm(cqn, p['mla_w_uq'], F32, 512).reshape(N_TOK, MLA_HEADS, MLA_QK)
    q_all = _rms(q_all, p['mla_qn_g']) * (MLA_QK ** -0.5)
    q_l, q_c = _split_lat_ctx(q_all)
    q_l = jnp.concatenate([q_l[..., :MLA_NOPE], _rope(q_l[..., MLA_NOPE:], *rope_mla)], axis=-1)

    ckvn = _rms(_col(proj, 1), p['mla_kv_norm_g']).astype(BF16)
    kv_all = _mm(ckvn, p['mla_w_ukv'], F32, 512).reshape(N_TOK, MLA_HEADS, MLA_NOPE + MLA_V)
    krope = jnp.broadcast_to(_col(proj, 2).astype(F32)[:, None, :], (N_TOK, MLA_HEADS, MLA_ROPE))
    k_all = _rms(jnp.concatenate([kv_all[..., :MLA_NOPE], krope], axis=-1), p['mla_kn_g'])
    k_l, k_c = _split_lat_ctx(k_all)
    k_l = jnp.concatenate([k_l[..., :MLA_NOPE], _rope(k_l[..., MLA_NOPE:], *rope_mla)], axis=-1)
    v_l, v_c = _split_lat_ctx(kv_all[..., MLA_NOPE:])
    ka_l, ka_c, va_l, va_c = _bhsd(k_l), _bhsd(k_c), _bhsd(v_l), _bhsd(v_c)
    oa_l = _attn(_bhsd(q_l), ka_l, va_l, ka_c, va_c, tq=256, tk=512)

    nq = _rms(_col(proj, 3).reshape(N_TOK, NA_HEADS, NA_HD), p['na_qn_g']) * (NA_HD ** -0.5)
    nk = _rms(_col(proj, 4).reshape(N_TOK, NA_HEADS, NA_HD), p['na_kn_g'])
    nv = _col(proj, 5).reshape(N_TOK, NA_HEADS, NA_HD)
    nq_l, nq_c = _split_lat_ctx(nq)
    nk_l, nk_c = _split_lat_ctx(nk)
    nv_l, nv_c = _split_lat_ctx(nv)
    kb_c, vb_c = _bhsd(nk_c), _bhsd(nv_c)
    ob_l = _na_attn(_bhsd(nq_l), _bhsd(nk_l), _bhsd(nv_l), kb_c, vb_c, _na_bias_table(p['na_rpb']))

    gq = _rms(_col(proj, 6).reshape(N_TOK, GQA_HEADS, GQA_HD), p['gqa_qn_g'])
    gk = _rms(_col(proj, 7).reshape(N_TOK, GQA_KV_HEADS, GQA_HD), p['gqa_kn_g'])
    gv = _col(proj, 8).reshape(N_TOK, GQA_KV_HEADS, GQA_HD)
    gq_l, gq_c = _split_lat_ctx(gq)
    gk_l, gk_c = _split_lat_ctx(gk)
    gv_l, gv_c = _split_lat_ctx(gv)
    gq_l = _rope(gq_l, *rope_gqa) * (GQA_HD ** -0.5)
    gk_l = _rope(gk_l, *rope_gqa)
    kc_c, vc_c = _bhsd(gk_c), _bhsd(gv_c)
    nb = SEQ // GQA_WINDOW
    gq_blk = gq_l.reshape(BATCH, nb, GQA_WINDOW, GQA_KV_HEADS, _GQA_G, GQA_HD)
    gq_blk = jnp.transpose(gq_blk, (0, 3, 1, 4, 2, 5)).reshape(
        BATCH, GQA_KV_HEADS, nb, _GQA_G * GQA_WINDOW, GQA_HD).astype(BF16)
    oc_blk = _gqa_attn(gq_blk, _bhsd(gk_l), _bhsd(gv_l), kc_c, vc_c, p['gqa_sink'])
    oc_l = oc_blk.reshape(BATCH, GQA_KV_HEADS, nb, _GQA_G, GQA_WINDOW, GQA_HD)
    oc_l = jnp.transpose(oc_l, (0, 2, 4, 1, 3, 5)).reshape(N_LAT, GQA_HEADS * GQA_HD)

    def flat(o):
        return jnp.transpose(o, (0, 2, 1, 3)).reshape(o.shape[0] * o.shape[2], -1)

    oa = flat(oa_l)
    ob = ob_l.reshape(N_LAT, NA_HEADS * NA_HD)
    oc = oc_l
    if ctx_out:
        oa_c = _attn(_bhsd(q_c), ka_c, va_c, tq=CTX_LEN, tk=CTX_LEN)
        ob_c = _attn(_bhsd(nq_c), kb_c, vb_c, tq=CTX_LEN, tk=CTX_LEN)
        oc_c = _attn(_bhsd(gq_c * (GQA_HD ** -0.5)), kc_c, vc_c, sink=p['gqa_sink'],
                     tq=CTX_LEN, tk=CTX_LEN)
        oa = jnp.concatenate([oa, flat(oa_c)], axis=0)
        ob = jnp.concatenate([ob, flat(ob_c)], axis=0)
        oc = jnp.concatenate([oc, flat(oc_c)], axis=0)
    return _merge(proj, oa, ob, oc, p['w_o_mla'].astype(BF16), p['w_o_na'].astype(BF16),
                  p['w_o_gqa'].astype(BF16))


def _moe(x, mod, norm_g, p):
    m = x.shape[0]
    wr = jnp.concatenate([p['moe_w_group'], p['moe_w_expert'],
                          jnp.zeros((D_MODEL, ROUTE_COLS - MOE_GROUPS - MOE_EXPERTS), F32)], axis=1)
    br = jnp.concatenate([p['moe_b_group'], p['moe_b_expert'],
                          jnp.zeros((ROUTE_COLS - MOE_GROUPS - MOE_EXPERTS,), F32)]).reshape(1, ROUTE_COLS)
    h, logits = _norm_mod(x, norm_g, mod, 3, 4, route=(wr, br))
    tile_expert, slot_token, n_used, slot_w, pos = _route(logits, m)
    y = _moe_ffn(h, tile_expert, slot_token, n_used, slot_w,
                 p['moe_w_gate'], p['moe_w_up'], p['moe_w_down'])
    return _moe_combine(y, pos, x, mod, 5)


def kernel(x, c, ctx, c_ctx, ada_w, ada_b, norm_mix_g, norm_ffn_g, w_in,
           mla_q_norm_g, mla_w_uq, mla_kv_norm_g, mla_w_ukv, mla_qn_g, mla_kn_g,
           na_qn_g, na_kn_g, na_rpb, gqa_qn_g, gqa_kn_g, gqa_sink,
           w_o_mla, w_o_na, w_o_gqa, w_out,
           moe_w_group, moe_b_group, moe_w_expert, moe_b_expert,
           moe_w_gate, moe_w_up, moe_w_down):
    xt = jnp.concatenate([x.reshape(N_LAT, D_MODEL), ctx.reshape(N_CTX, D_MODEL)], axis=0)
    c_rows = jnp.concatenate([c, c_ctx[None, :], jnp.zeros((8 - BATCH - 1, D_MODEL), F32)], axis=0)
    mod_all = _ada(c_rows, ada_w, ada_b)
    for l in range(DEPTH):
        ctx_out = l < DEPTH - 1
        p = {
            'w_in': w_in[l], 'mla_q_norm_g': mla_q_norm_g[l], 'mla_w_uq': mla_w_uq[l],
            'mla_kv_norm_g': mla_kv_norm_g[l], 'mla_w_ukv': mla_w_ukv[l],
            'mla_qn_g': mla_qn_g[l], 'mla_kn_g': mla_kn_g[l],
            'na_qn_g': na_qn_g[l], 'na_kn_g': na_kn_g[l], 'na_rpb': na_rpb[l],
            'gqa_qn_g': gqa_qn_g[l], 'gqa_kn_g': gqa_kn_g[l], 'gqa_sink': gqa_sink[l],
            'w_o_mla': w_o_mla[l], 'w_o_na': w_o_na[l], 'w_o_gqa': w_o_gqa[l],
            'moe_w_group': moe_w_group[l], 'moe_b_group': moe_b_group[l],
            'moe_w_expert': moe_w_expert[l], 'moe_b_expert': moe_b_expert[l],
            'moe_w_gate': moe_w_gate[l], 'moe_w_up': moe_w_up[l], 'moe_w_down': moe_w_down[l],
        }
        mod = mod_all[l].reshape(8 * 6, 1, D_MODEL)
        h = _norm_mod(xt, norm_mix_g[l], mod, 0, 1)
        y = _token_mixer(h, p, ctx_out)
        m = y.shape[0]
        x_mid = _mm_res(y, w_out[l], xt[:m], mod, 2, 512)
        xt = _moe(x_mid, mod, norm_ffn_g[l], p)
    return xt[:N_LAT].reshape(BATCH, SEQ, D_MODEL)
```

```python
import functools

import numpy as np
import jax
import jax.numpy as jnp
from jax import lax
from jax.experimental import pallas as pl
from jax.experimental.pallas import tpu as pltpu

D_MODEL = 2048
BATCH = 2
SEQ = 4096
DEPTH = 2
GRID_W = 64
CTX_LEN = 256
EPS = 1e-6
ROPE_THETA = 10000.0
NEG_INF = -1e30

MLA_HEADS = 8
MLA_Q_RANK = 512
MLA_KV_RANK = 512
MLA_NOPE = 128
MLA_ROPE = 64
MLA_QK = MLA_NOPE + MLA_ROPE
MLA_V = 128
NA_HEADS = 4
NA_HD = 128
NA_ROWS = 8
NA_COLS = 16
GQA_HEADS = 8
GQA_KV_HEADS = 2
GQA_HD = 64
GQA_WINDOW = 128
MOE_GROUPS = 4
MOE_PER_GROUP = 8
MOE_EXPERTS = MOE_GROUPS * MOE_PER_GROUP
MOE_TOPK = 2
MOE_HIDDEN = 512

IN_SIZES = (MLA_Q_RANK, MLA_KV_RANK, MLA_ROPE,
            NA_HEADS * NA_HD, NA_HEADS * NA_HD, NA_HEADS * NA_HD,
            GQA_HEADS * GQA_HD, GQA_KV_HEADS * GQA_HD, GQA_KV_HEADS * GQA_HD,
            D_MODEL, D_MODEL, D_MODEL)
IN_SPLITS = tuple(int(s) for s in np.cumsum(IN_SIZES)[:-1])

N_LAT = BATCH * SEQ
N_CTX = BATCH * CTX_LEN
N_TOK = N_LAT + N_CTX
ROWS = SEQ // GRID_W

V7X_LANES = 128
V7X_VMEM_LIMIT = 56 * 1024 * 1024

_PACK_ORDER = (9, 10, 11, 0, 1, 3, 4, 5, 6, 7, 8, 2)
_PACK_COLS = 9728
_PACK_OFF = {}
_off = 0
for _i in _PACK_ORDER:
    _PACK_OFF[_i] = _off
    _off += IN_SIZES[_i]

ROUTE_COLS = V7X_LANES
MOE_TM = 256
TOK_TM = 256
MM_TM = 512

F32 = jnp.float32
BF16 = jnp.bfloat16


def _cparams(sem):
    return pltpu.CompilerParams(dimension_semantics=sem, vmem_limit_bytes=V7X_VMEM_LIMIT)


def _seg(i, tm):
    return jnp.minimum(i // (SEQ // tm), 2)


def _ada_kernel(c_ref, w_ref, b_ref, o_ref):
    c = c_ref[...]
    a = c * jax.nn.sigmoid(c)
    o_ref[0] = jnp.dot(a, w_ref[0], preferred_element_type=F32,
                       precision=lax.Precision.HIGHEST) + b_ref[0]


def _ada(c_rows, ada_w, ada_b):
    tn = 1024
    n = 6 * D_MODEL
    return pl.pallas_call(
        _ada_kernel,
        grid=(DEPTH, n // tn),
        in_specs=[
            pl.BlockSpec((8, D_MODEL), lambda l, j: (0, 0)),
            pl.BlockSpec((1, D_MODEL, tn), lambda l, j: (l, 0, j)),
            pl.BlockSpec((1, 1, tn), lambda l, j: (l, 0, j)),
        ],
        out_specs=pl.BlockSpec((1, 8, tn), lambda l, j: (l, 0, j)),
        out_shape=jax.ShapeDtypeStruct((DEPTH, 8, n), F32),
        compiler_params=_cparams(("arbitrary", "arbitrary")),
        name="ada",
    )(c_rows, ada_w, ada_b.reshape(DEPTH, 1, n))


def _norm_mod_kernel(x_ref, g_ref, sh_ref, sc_ref, h_ref):
    x = x_ref[...]
    xn = x * lax.rsqrt(jnp.mean(x * x, axis=-1, keepdims=True) + EPS) * g_ref[...]
    h_ref[...] = (xn * (1.0 + sc_ref[0]) + sh_ref[0]).astype(h_ref.dtype)


def _norm_mod_route_kernel(x_ref, g_ref, sh_ref, sc_ref, wr_ref, br_ref, h_ref, lg_ref):
    x = x_ref[...]
    xn = x * lax.rsqrt(jnp.mean(x * x, axis=-1, keepdims=True) + EPS) * g_ref[...]
    h = xn * (1.0 + sc_ref[0]) + sh_ref[0]
    h_ref[...] = h
    lg_ref[...] = jnp.dot(h, wr_ref[...], preferred_element_type=F32,
                          precision=lax.Precision.HIGHEST) + br_ref[...]


def _norm_mod(x, g, mod, k_shift, k_scale, route=None):
    m = x.shape[0]
    tm = TOK_TM
    base = [
        pl.BlockSpec((tm, D_MODEL), lambda i: (i, 0)),
        pl.BlockSpec((1, D_MODEL), lambda i: (0, 0)),
        pl.BlockSpec((1, 1, D_MODEL), lambda i: (_seg(i, tm) * 6 + k_shift, 0, 0)),
        pl.BlockSpec((1, 1, D_MODEL), lambda i: (_seg(i, tm) * 6 + k_scale, 0, 0)),
    ]
    if route is None:
        return pl.pallas_call(
            _norm_mod_kernel,
            grid=(m // tm,),
            in_specs=base,
            out_specs=pl.BlockSpec((tm, D_MODEL), lambda i: (i, 0)),
            out_shape=jax.ShapeDtypeStruct((m, D_MODEL), BF16),
            compiler_params=_cparams(("arbitrary",)),
            name="norm_mod",
        )(x, g.reshape(1, D_MODEL), mod, mod)
    wr, br = route
    return pl.pallas_call(
        _norm_mod_route_kernel,
        grid=(m // tm,),
        in_specs=base + [
            pl.BlockSpec((D_MODEL, ROUTE_COLS), lambda i: (0, 0)),
            pl.BlockSpec((1, ROUTE_COLS), lambda i: (0, 0)),
        ],
        out_specs=[pl.BlockSpec((tm, D_MODEL), lambda i: (i, 0)),
                   pl.BlockSpec((tm, ROUTE_COLS), lambda i: (i, 0))],
        out_shape=[jax.ShapeDtypeStruct((m, D_MODEL), F32),
                   jax.ShapeDtypeStruct((m, ROUTE_COLS), F32)],
        compiler_params=_cparams(("arbitrary",)),
        name="norm_mod_route",
    )(x, g.reshape(1, D_MODEL), mod, mod, wr, br)


def _mm_kernel(x_ref, w_ref, o_ref, wb_ref):
    @pl.when(pl.program_id(1) == 0)
    def _():
        wb_ref[...] = w_ref[...].astype(BF16)

    o_ref[...] = jnp.dot(x_ref[...], wb_ref[...], preferred_element_type=F32).astype(o_ref.dtype)


def _mm_res_kernel(x_ref, w_ref, r_ref, g_ref, o_ref, wb_ref):
    @pl.when(pl.program_id(1) == 0)
    def _():
        wb_ref[...] = w_ref[...].astype(BF16)

    acc = jnp.dot(x_ref[...], wb_ref[...], preferred_element_type=F32)
    o_ref[...] = r_ref[...] + g_ref[0] * acc


def _mm(x, w, out_dtype, tn):
    m, k = x.shape
    n = w.shape[1]
    tm = MM_TM
    return pl.pallas_call(
        _mm_kernel,
        grid=(n // tn, m // tm),
        in_specs=[pl.BlockSpec((tm, k), lambda j, i: (i, 0)),
                  pl.BlockSpec((k, tn), lambda j, i: (0, j))],
        out_specs=pl.BlockSpec((tm, tn), lambda j, i: (i, j)),
        out_shape=jax.ShapeDtypeStruct((m, n), out_dtype),
        scratch_shapes=[pltpu.VMEM((k, tn), BF16)],
        compiler_params=_cparams(("arbitrary", "arbitrary")),
        name="mm",
    )(x, w)


def _mm_res(x, w, res, mod, k_gate, tn):
    m, k = x.shape
    n = w.shape[1]
    tm = MM_TM
    nj = n // tn
    return pl.pallas_call(
        _mm_res_kernel,
        grid=(nj, m // tm),
        in_specs=[pl.BlockSpec((tm, k), lambda j, i: (i, 0)),
                  pl.BlockSpec((k, tn), lambda j, i: (0, j)),
                  pl.BlockSpec((tm, tn), lambda j, i: (i, j)),
                  pl.BlockSpec((1, 1, tn), lambda j, i: (_seg(i, tm) * 6 + k_gate, 0, j))],
        out_specs=pl.BlockSpec((tm, tn), lambda j, i: (i, j)),
        out_shape=jax.ShapeDtypeStruct((m, n), F32),
        scratch_shapes=[pltpu.VMEM((k, tn), BF16)],
        compiler_params=_cparams(("arbitrary", "arbitrary")),
        name="mm_res",
    )(x, w, res, mod)


def _attn_kernel(*refs, has_k2, has_sink, tk, n_chunks):
    refs = list(refs)
    sink_ref = refs.pop(0) if has_sink else None
    q_ref, k1_ref, v1_ref = refs[:3]
    refs = refs[3:]
    if has_k2:
        k2_ref, v2_ref = refs[:2]
        refs = refs[2:]
    o_ref = refs[0]

    q = q_ref[0, 0]
    tq = q.shape[0]
    dv = v1_ref.shape[-1]
    if has_sink:
        snk = sink_ref[pl.program_id(1)]
        m0 = jnp.full((tq, 1), snk, F32)
        l0 = jnp.ones((tq, 1), F32)
    else:
        m0 = jnp.full((tq, 1), NEG_INF, F32)
        l0 = jnp.zeros((tq, 1), F32)
    acc0 = jnp.zeros((tq, dv), F32)

    def step(kc, vc, carry):
        m, l, acc = carry
        s = lax.dot_general(q, kc, (((1,), (1,)), ((), ())), preferred_element_type=F32)
        m_new = jnp.maximum(m, jnp.max(s, axis=-1, keepdims=True))
        a = jnp.exp(m - m_new)
        p = jnp.exp(s - m_new)
        l = a * l + jnp.sum(p, axis=-1, keepdims=True)
        acc = a * acc + jnp.dot(p.astype(BF16), vc, preferred_element_type=F32)
        return m_new, l, acc

    def body(c, carry):
        off = pl.multiple_of(c * tk, tk)
        return step(k1_ref[0, 0, pl.ds(off, tk), :], v1_ref[0, 0, pl.ds(off, tk), :], carry)

    carry = lax.fori_loop(0, n_chunks, body, (m0, l0, acc0))
    if has_k2:
        carry = step(k2_ref[0, 0], v2_ref[0, 0], carry)
    _, l, acc = carry
    o_ref[0, 0] = (acc / l).astype(o_ref.dtype)


def _attn(q, k1, v1, k2=None, v2=None, sink=None, *, tq, tk):
    b, hq, sq, dq = q.shape
    hk, s1 = k1.shape[1], k1.shape[2]
    dv = v1.shape[-1]
    g = hq // hk
    has_k2 = k2 is not None
    has_sink = sink is not None
    nsp = 1 if has_sink else 0

    def qmap(bi, h, i, *_):
        return (bi, h, i, 0)

    def kmap(bi, h, i, *_):
        return (bi, h // g, 0, 0)

    in_specs = [pl.BlockSpec((1, 1, tq, dq), qmap),
                pl.BlockSpec((1, 1, s1, dq), kmap),
                pl.BlockSpec((1, 1, s1, dv), kmap)]
    args = [q, k1, v1]
    if has_k2:
        s2 = k2.shape[2]
        in_specs += [pl.BlockSpec((1, 1, s2, dq), kmap), pl.BlockSpec((1, 1, s2, dv), kmap)]
        args += [k2, v2]
    kern = functools.partial(_attn_kernel, has_k2=has_k2, has_sink=has_sink, tk=tk,
                             n_chunks=s1 // tk)
    gs = pltpu.PrefetchScalarGridSpec(
        num_scalar_prefetch=nsp,
        grid=(b, hq, sq // tq),
        in_specs=in_specs,
        out_specs=pl.BlockSpec((1, 1, tq, dv), qmap),
    )
    pre = [sink.astype(F32)] if has_sink else []
    return pl.pallas_call(
        kern,
        grid_spec=gs,
        out_shape=jax.ShapeDtypeStruct((b, hq, sq, dv), BF16),
        compiler_params=_cparams(("arbitrary", "arbitrary", "arbitrary")),
        name="attn",
    )(*pre, *args)


_LOG2E = 1.4426950408889634


def _mla_kernel(q_ref, k1_ref, vt1_ref, k2_ref, vt2_ref, o_ref, sa_ref, sb_ref, sc_ref, acc_ref,
                *, tk, n_chunks):
    q = q_ref[0, 0]
    tq = q.shape[0]
    nt = (((1,), (1,)), ((), ()))

    def scores(dst_ref, kc):
        dst_ref[...] = lax.dot_general(kc, q, nt, preferred_element_type=F32)

    def k_chunk(c):
        return k1_ref[0, 0, pl.ds(pl.multiple_of(c * tk, tk), tk), :]

    def vt_chunk(c):
        return vt1_ref[0, 0, :, pl.ds(pl.multiple_of(c * tk, tk), tk)]

    def accumulate(s_ref, vtc, m, l):
        st = s_ref[...]
        m_new = jnp.maximum(m, jnp.max(st, axis=0, keepdims=True))
        a = jnp.exp2(m - m_new)
        p = jnp.exp2(st - m_new)
        l = a * l + jnp.sum(p, axis=0, keepdims=True)
        acc_ref[...] = a * acc_ref[...] + jnp.dot(vtc, p.astype(BF16), preferred_element_type=F32)
        return m_new, l

    m = jnp.full((1, tq), NEG_INF, F32)
    l = jnp.zeros((1, tq), F32)
    acc_ref[...] = jnp.zeros_like(acc_ref)
    scores(sc_ref, k2_ref[0, 0])
    scores(sa_ref, k_chunk(0))
    m, l = accumulate(sc_ref, vt2_ref[0, 0], m, l)

    def body(i, carry):
        m, l = carry
        scores(sb_ref, k_chunk(2 * i + 1))
        m, l = accumulate(sa_ref, vt_chunk(2 * i), m, l)
        scores(sa_ref, k_chunk(2 * i + 2))
        m, l = accumulate(sb_ref, vt_chunk(2 * i + 1), m, l)
        return m, l

    m, l = lax.fori_loop(0, n_chunks // 2 - 1, body, (m, l))
    scores(sb_ref, k_chunk(n_chunks - 1))
    m, l = accumulate(sa_ref, vt_chunk(n_chunks - 2), m, l)
    m, l = accumulate(sb_ref, vt_chunk(n_chunks - 1), m, l)
    o_ref[0, 0] = (acc_ref[...] / l).T.astype(o_ref.dtype)


def _mla_attn(q, k1, vt1, k2, vt2, *, tq, tk):
    b, h, sq, dq = q.shape
    s1, s2 = k1.shape[2], k2.shape[2]
    dv = vt1.shape[2]

    def qmap(bi, hi, i):
        return (bi, hi, i, 0)

    def kmap(bi, hi, i):
        return (bi, hi, 0, 0)

    return pl.pallas_call(
        functools.partial(_mla_kernel, tk=tk, n_chunks=s1 // tk),
        grid=(b, h, sq // tq),
        in_specs=[pl.BlockSpec((1, 1, tq, dq), qmap),
                  pl.BlockSpec((1, 1, s1, dq), kmap),
                  pl.BlockSpec((1, 1, dv, s1), kmap),
                  pl.BlockSpec((1, 1, s2, dq), kmap),
                  pl.BlockSpec((1, 1, dv, s2), kmap)],
        out_specs=pl.BlockSpec((1, 1, tq, dv), qmap),
        out_shape=jax.ShapeDtypeStruct((b, h, sq, dv), BF16),
        scratch_shapes=[pltpu.VMEM((tk, tq), F32), pltpu.VMEM((tk, tq), F32),
                        pltpu.VMEM((s2, tq), F32), pltpu.VMEM((dv, tq), F32)],
        compiler_params=_cparams(("arbitrary", "arbitrary", "arbitrary")),
        name="mla_attn",
    )(q, k1, vt1, k2, vt2)


_NA_KEYS = NA_ROWS * GRID_W


def _na_kernel(q_ref, k_ref, v_ref, kc_ref, vc_ref, bias_ref, o_ref):
    r = pl.program_id(2)
    start = pl.multiple_of(jnp.clip(r - NA_ROWS // 2, 0, ROWS - NA_ROWS) * GRID_W, GRID_W)
    q = q_ref[0, 0]
    k = k_ref[0, 0, pl.ds(start, _NA_KEYS), :]
    v = v_ref[0, 0, pl.ds(start, _NA_KEYS), :]
    nt = (((1,), (1,)), ((), ()))
    s = lax.dot_general(q, k, nt, preferred_element_type=F32) + bias_ref[0, 0]
    sc = lax.dot_general(q, kc_ref[0, 0], nt, preferred_element_type=F32)
    m = jnp.maximum(jnp.max(s, axis=-1, keepdims=True), jnp.max(sc, axis=-1, keepdims=True))
    p = jnp.exp(s - m)
    pc = jnp.exp(sc - m)
    l = jnp.sum(p, axis=-1, keepdims=True) + jnp.sum(pc, axis=-1, keepdims=True)
    o = (jnp.dot(p.astype(BF16), v, preferred_element_type=F32)
         + jnp.dot(pc.astype(BF16), vc_ref[0, 0], preferred_element_type=F32))
    o_ref[0] = (o / l).astype(o_ref.dtype)


def _na_pattern(r):
    half = NA_ROWS // 2
    return jnp.where(r < half, r, jnp.where(r <= ROWS - half, half, r - (ROWS - NA_ROWS)))


def _na_attn(q, k, v, kc, vc, bias):
    def kmap(b, h, r):
        return (b, h, 0, 0)

    return pl.pallas_call(
        _na_kernel,
        grid=(BATCH, NA_HEADS, ROWS),
        in_specs=[pl.BlockSpec((1, 1, GRID_W, NA_HD), lambda b, h, r: (b, h, r, 0)),
                  pl.BlockSpec((1, 1, SEQ, NA_HD), kmap),
                  pl.BlockSpec((1, 1, SEQ, NA_HD), kmap),
                  pl.BlockSpec((1, 1, CTX_LEN, NA_HD), kmap),
                  pl.BlockSpec((1, 1, CTX_LEN, NA_HD), kmap),
                  pl.BlockSpec((1, 1, GRID_W, _NA_KEYS), lambda b, h, r: (h, _na_pattern(r), 0, 0))],
        out_specs=pl.BlockSpec((1, GRID_W, NA_HD), lambda b, h, r: (b, r, h)),
        out_shape=jax.ShapeDtypeStruct((BATCH, SEQ, NA_HEADS * NA_HD), BF16),
        compiler_params=_cparams(("arbitrary", "arbitrary", "arbitrary")),
        name="na_attn",
    )(q, k, v, kc, vc, bias)


def _na_bias_table(rpb):
    half = NA_ROWS // 2
    r_rep = np.array(list(range(half)) + [half] + list(range(ROWS - half + 1, ROWS)))
    start = np.clip(r_rep - half, 0, ROWS - NA_ROWS)
    dr = start[:, None] + np.arange(NA_ROWS)[None, :] - r_rep[:, None] + NA_ROWS - 1
    qc = np.arange(GRID_W)
    kcol = np.arange(GRID_W)
    col_start = np.clip(qc - NA_COLS // 2, 0, GRID_W - NA_COLS)
    in_win = (kcol[None, :] >= col_start[:, None]) & (kcol[None, :] < col_start[:, None] + NA_COLS)
    dc = np.clip(kcol[None, :] - qc[:, None], 1 - NA_COLS, NA_COLS - 1) + NA_COLS - 1
    rsel = (dr[:, :, None] == np.arange(2 * NA_ROWS - 1)).astype(np.float32)
    csel = (dc[:, :, None] == np.arange(2 * NA_COLS - 1)).astype(np.float32)
    b = jnp.einsum('pja,hab,qkb->hpqjk', rsel, rpb.astype(F32), csel,
                   precision=lax.Precision.HIGHEST)
    b = jnp.where(in_win[None, None, :, None, :], b.astype(F32), NEG_INF)
    return b.reshape(NA_HEADS, NA_ROWS, GRID_W, _NA_KEYS)


_GQA_G = GQA_HEADS // GQA_KV_HEADS
_GQA_BAND = 3 * GQA_WINDOW


def _gqa_kernel(sink_ref, q_ref, k_ref, v_ref, kc_ref, vc_ref, o_ref):
    hk = pl.program_id(1)
    n = pl.program_id(2)
    w = GQA_WINDOW
    start = pl.multiple_of(jnp.clip((n - 1) * w, 0, SEQ - _GQA_BAND), w)
    q = q_ref[0, 0, 0]
    k = k_ref[0, 0, pl.ds(start, _GQA_BAND), :]
    v = v_ref[0, 0, pl.ds(start, _GQA_BAND), :]
    nt = (((1,), (1,)), ((), ()))
    s = lax.dot_general(q, k, nt, preferred_element_type=F32)
    rows = lax.broadcasted_iota(jnp.int32, s.shape, 0)
    cols = lax.broadcasted_iota(jnp.int32, s.shape, 1)
    qpos = n * w + (rows & (w - 1))
    kpos = start + cols
    s = jnp.where(jnp.abs(kpos - qpos) <= GQA_WINDOW, s, NEG_INF)
    sc = lax.dot_general(q, kc_ref[0, 0], nt, preferred_element_type=F32)
    grow = lax.broadcasted_iota(jnp.int32, (_GQA_G * w, 1), 0) // w
    snk = jnp.full((_GQA_G * w, 1), sink_ref[hk * _GQA_G], F32)
    for g in range(1, _GQA_G):
        snk = jnp.where(grow == g, sink_ref[hk * _GQA_G + g], snk)
    m = jnp.maximum(jnp.maximum(jnp.max(s, axis=-1, keepdims=True),
                                jnp.max(sc, axis=-1, keepdims=True)), snk)
    p = jnp.exp(s - m)
    pc = jnp.exp(sc - m)
    l = jnp.sum(p, axis=-1, keepdims=True) + jnp.sum(pc, axis=-1, keepdims=True) + jnp.exp(snk - m)
    o = (jnp.dot(p.astype(BF16), v, preferred_element_type=F32)
         + jnp.dot(pc.astype(BF16), vc_ref[0, 0], preferred_element_type=F32))
    o_ref[0, 0, 0] = (o / l).astype(o_ref.dtype)


def _gqa_attn(q, k, v, kc, vc, sink):
    nb = SEQ // GQA_WINDOW
    rows = _GQA_G * GQA_WINDOW

    def qmap(b, h, n, *_):
        return (b, h, n, 0, 0)

    def kmap(b, h, n, *_):
        return (b, h, 0, 0)

    gs = pltpu.PrefetchScalarGridSpec(
        num_scalar_prefetch=1,
        grid=(BATCH, GQA_KV_HEADS, nb),
        in_specs=[pl.BlockSpec((1, 1, 1, rows, GQA_HD), qmap),
                  pl.BlockSpec((1, 1, SEQ, GQA_HD), kmap),
                  pl.BlockSpec((1, 1, SEQ, GQA_HD), kmap),
                  pl.BlockSpec((1, 1, CTX_LEN, GQA_HD), kmap),
                  pl.BlockSpec((1, 1, CTX_LEN, GQA_HD), kmap)],
        out_specs=pl.BlockSpec((1, 1, 1, rows, GQA_HD), qmap),
    )
    return pl.pallas_call(
        _gqa_kernel,
        grid_spec=gs,
        out_shape=jax.ShapeDtypeStruct((BATCH, GQA_KV_HEADS, nb, rows, GQA_HD), BF16),
        compiler_params=_cparams(("arbitrary", "arbitrary", "arbitrary")),
        name="gqa_attn",
    )(sink.astype(F32), q, k, v, kc, vc)


def _merge_kernel(ga_ref, gb_ref, gc_ref, oa_ref, ob_ref, oc_ref, wa_ref, wb_ref, wc_ref, y_ref):
    def branch(g_ref, o_ref, w_ref):
        return jax.nn.sigmoid(g_ref[...].astype(F32)) * jnp.dot(
            o_ref[...], w_ref[...], preferred_element_type=F32)

    y = branch(ga_ref, oa_ref, wa_ref) + branch(gb_ref, ob_ref, wb_ref) + branch(gc_ref, oc_ref, wc_ref)
    y_ref[...] = y.astype(y_ref.dtype)


def _merge(proj, oa, ob, oc, wa, wb, wc):
    m = oa.shape[0]
    tm = TOK_TM
    ka, kb, kc = oa.shape[1], ob.shape[1], oc.shape[1]
    return pl.pallas_call(
        _merge_kernel,
        grid=(m // tm,),
        in_specs=[pl.BlockSpec((tm, D_MODEL), lambda i: (i, 0)),
                  pl.BlockSpec((tm, D_MODEL), lambda i: (i, 1)),
                  pl.BlockSpec((tm, D_MODEL), lambda i: (i, 2)),
                  pl.BlockSpec((tm, ka), lambda i: (i, 0)),
                  pl.BlockSpec((tm, kb), lambda i: (i, 0)),
                  pl.BlockSpec((tm, kc), lambda i: (i, 0)),
                  pl.BlockSpec((ka, D_MODEL), lambda i: (0, 0)),
                  pl.BlockSpec((kb, D_MODEL), lambda i: (0, 0)),
                  pl.BlockSpec((kc, D_MODEL), lambda i: (0, 0))],
        out_specs=pl.BlockSpec((tm, D_MODEL), lambda i: (i, 0)),
        out_shape=jax.ShapeDtypeStruct((m, D_MODEL), BF16),
        compiler_params=_cparams(("arbitrary",)),
        name="merge",
    )(proj, proj, proj, oa, ob, oc, wa, wb, wc)


def _moe_ffn_kernel(te_ref, tok_ref, nu_ref, h_hbm, sw_ref, wg_ref, wu_ref, wd_ref, y_ref,
                    xbuf, sem, wgb, wub, wdb):
    i = pl.program_id(0)
    tm = MOE_TM
    slot = i % 2
    n_used = nu_ref[0]

    def row_copy(tok, s, r):
        return pltpu.make_async_copy(h_hbm.at[pl.ds(tok, 1)], xbuf.at[s, pl.ds(r, 1)], sem.at[s])

    def start_gather(tile, s):
        base = tile * tm

        def body(r, c):
            row_copy(tok_ref[base + r], s, r).start()
            return c

        lax.fori_loop(0, tm, body, 0)

    @pl.when(i == 0)
    def _():
        start_gather(0, 0)

    @pl.when(i + 1 < n_used)
    def _():
        start_gather(i + 1, 1 - slot)

    @pl.when(i < n_used)
    def _():
        @pl.when((i == 0) | (te_ref[i] != te_ref[jnp.maximum(i - 1, 0)]))
        def _():
            wgb[...] = wg_ref[0].astype(BF16)
            wub[...] = wu_ref[0].astype(BF16)
            wdb[...] = wd_ref[0].astype(BF16)

        def wbody(r, c):
            row_copy(0, slot, r).wait()
            return c

        lax.fori_loop(0, tm, wbody, 0)
        x = xbuf[slot].astype(BF16)
        hg = jnp.dot(x, wgb[...], preferred_element_type=F32)
        hu = jnp.dot(x, wub[...], preferred_element_type=F32)
        act = (hg * jax.nn.sigmoid(hg)) * hu * sw_ref[...]
        y_ref[...] = jnp.dot(act.astype(BF16), wdb[...], preferred_element_type=F32)

    @pl.when(i >= n_used)
    def _():
        y_ref[...] = jnp.zeros_like(y_ref)


def _moe_ffn(h, tile_expert, slot_token, n_used, slot_w, wg, wu, wd):
    p = slot_token.shape[0]
    tm = MOE_TM
    nt = p // tm
    gs = pltpu.PrefetchScalarGridSpec(
        num_scalar_prefetch=3,
        grid=(nt,),
        in_specs=[pl.BlockSpec(memory_space=pl.ANY),
                  pl.BlockSpec((tm, 1), lambda i, te, tok, nu: (i, 0)),
                  pl.BlockSpec((1, D_MODEL, MOE_HIDDEN), lambda i, te, tok, nu: (te[i], 0, 0)),
                  pl.BlockSpec((1, D_MODEL, MOE_HIDDEN), lambda i, te, tok, nu: (te[i], 0, 0)),
                  pl.BlockSpec((1, MOE_HIDDEN, D_MODEL), lambda i, te, tok, nu: (te[i], 0, 0))],
        out_specs=pl.BlockSpec((tm, D_MODEL), lambda i, te, tok, nu: (i, 0)),
        scratch_shapes=[pltpu.VMEM((2, tm, D_MODEL), F32),
                        pltpu.SemaphoreType.DMA((2,)),
                        pltpu.VMEM((D_MODEL, MOE_HIDDEN), BF16),
                        pltpu.VMEM((D_MODEL, MOE_HIDDEN), BF16),
                        pltpu.VMEM((MOE_HIDDEN, D_MODEL), BF16)],
    )
    return pl.pallas_call(
        _moe_ffn_kernel,
        grid_spec=gs,
        out_shape=jax.ShapeDtypeStruct((p, D_MODEL), F32),
        compiler_params=_cparams(("arbitrary",)),
        name="moe_ffn",
    )(tile_expert, slot_token, n_used, h, slot_w, wg, wu, wd)


def _moe_combine_kernel(pos_ref, y_hbm, x_ref, g_ref, o_ref, ybuf, sem):
    i = pl.program_id(0)
    nt = pl.num_programs(0)
    tm = TOK_TM
    slot = i % 2

    def row_copy(src, s, r):
        return pltpu.make_async_copy(y_hbm.at[pl.ds(src, 1)], ybuf.at[s, pl.ds(r, 1)], sem.at[s])

    def start_gather(tile, s):
        base = tile * tm

        def body(r, c):
            row_copy(pos_ref[2 * (base + r)], s, r).start()
            row_copy(pos_ref[2 * (base + r) + 1], s, tm + r).start()
            return c

        lax.fori_loop(0, tm, body, 0)

    @pl.when(i == 0)
    def _():
        start_gather(0, 0)

    @pl.when(i + 1 < nt)
    def _():
        start_gather(i + 1, 1 - slot)

    def wbody(r, c):
        row_copy(0, slot, r).wait()
        return c

    lax.fori_loop(0, 2 * tm, wbody, 0)
    y = ybuf[slot, pl.ds(0, tm), :] + ybuf[slot, pl.ds(tm, tm), :]
    o_ref[...] = x_ref[...] + g_ref[0] * y


def _moe_combine(y, pos, x, mod, k_gate):
    m = x.shape[0]
    tm = TOK_TM
    gs = pltpu.PrefetchScalarGridSpec(
        num_scalar_prefetch=1,
        grid=(m // tm,),
        in_specs=[pl.BlockSpec(memory_space=pl.ANY),
                  pl.BlockSpec((tm, D_MODEL), lambda i, pos: (i, 0)),
                  pl.BlockSpec((1, 1, D_MODEL), lambda i, pos: (_seg(i, tm) * 6 + k_gate, 0, 0))],
        out_specs=pl.BlockSpec((tm, D_MODEL), lambda i, pos: (i, 0)),
        scratch_shapes=[pltpu.VMEM((2, 2 * tm, D_MODEL), F32),
                        pltpu.SemaphoreType.DMA((2,))],
    )
    return pl.pallas_call(
        _moe_combine_kernel,
        grid_spec=gs,
        out_shape=jax.ShapeDtypeStruct((m, D_MODEL), F32),
        compiler_params=_cparams(("arbitrary",)),
        name="moe_combine",
    )(pos, y, x, mod)


def _route(logits, m):
    tm = MOE_TM
    gp = jax.nn.softmax(logits[:, :MOE_GROUPS], axis=-1)
    g_w, g_idx = lax.top_k(gp, 1)
    el = logits[:, MOE_GROUPS:MOE_GROUPS + MOE_EXPERTS].reshape(m, MOE_GROUPS, MOE_PER_GROUP)
    g_onehot = (g_idx == jnp.arange(MOE_GROUPS, dtype=g_idx.dtype)[None, :]).astype(F32)
    el_g = jnp.sum(el * g_onehot[:, :, None], axis=1)
    top_l, top_i = lax.top_k(el_g, MOE_TOPK)
    w_sel = jax.nn.softmax(top_l, axis=-1) * g_w
    eid = (g_idx * MOE_PER_GROUP + top_i).astype(jnp.int32)

    a = m * MOE_TOPK
    e_flat = eid.reshape(a)
    onehot = (e_flat[:, None] == jnp.arange(MOE_EXPERTS, dtype=jnp.int32)[None, :]).astype(jnp.int32)
    csum = jnp.cumsum(onehot, axis=0)
    rank = jnp.sum(csum * onehot, axis=1) - 1
    counts = csum[-1]
    padded = ((counts + tm - 1) // tm) * tm
    ends = jnp.cumsum(padded)
    starts = ends - padded
    pos = (jnp.sum(onehot * starts[None, :], axis=1) + rank).astype(jnp.int32)
    p = a + MOE_EXPERTS * tm
    slot_token = jnp.zeros((p,), jnp.int32).at[pos].set(jnp.arange(a, dtype=jnp.int32) // MOE_TOPK)
    slot_w = jnp.zeros((p,), F32).at[pos].set(w_sel.reshape(a)).reshape(p, 1)
    n_used = (ends[-1] // tm).astype(jnp.int32).reshape(1)
    tile_start = jnp.arange(p // tm, dtype=jnp.int32) * tm
    last_e = jnp.max(jnp.where(counts > 0, jnp.arange(MOE_EXPERTS, dtype=jnp.int32), 0))
    tile_expert = jnp.minimum(
        jnp.sum((ends[None, :] <= tile_start[:, None]).astype(jnp.int32), axis=1), last_e)
    return tile_expert, slot_token, n_used, slot_w, pos


def _rms(x, g):
    xf = x.astype(F32)
    return xf * lax.rsqrt(jnp.mean(xf * xf, axis=-1, keepdims=True) + EPS) * g.astype(F32)


def _rope_tables(rot_dim):
    t = jnp.arange(SEQ)
    row = (t // GRID_W).astype(F32)
    col = (t % GRID_W).astype(F32)
    n_freq = rot_dim // 4
    inv = ROPE_THETA ** (-jnp.arange(n_freq, dtype=F32) / n_freq)
    ang = jnp.concatenate([row[:, None] * inv, col[:, None] * inv], axis=-1)
    return jnp.cos(ang), jnp.sin(ang)


def _rope(x, cos, sin):
    half = x.shape[-1] // 2
    x1, x2 = x[..., :half], x[..., half:]
    c, s = cos[:, None, :], sin[:, None, :]
    return jnp.concatenate([x1 * c - x2 * s, x1 * s + x2 * c], axis=-1)


def _split_lat_ctx(t):
    rest = t.shape[1:]
    return t[:N_LAT].reshape((BATCH, SEQ) + rest), t[N_LAT:].reshape((BATCH, CTX_LEN) + rest)


def _bhsd(t):
    return jnp.transpose(t, (0, 2, 1, 3)).astype(BF16)


def _col(proj, idx):
    off = _PACK_OFF[idx]
    return proj[:, off:off + IN_SIZES[idx]]


def _pack_w_in(w):
    parts = jnp.split(w, IN_SPLITS, axis=1)
    pad = jnp.zeros((D_MODEL, _PACK_COLS - sum(IN_SIZES)), w.dtype)
    return jnp.concatenate([parts[i] for i in _PACK_ORDER] + [pad], axis=1).astype(BF16)


def _token_mixer(h, p, ctx_out):
    proj = _mm(h, _pack_w_in(p['w_in']), BF16, 512)

    rope_mla = _rope_tables(MLA_ROPE)
    rope_gqa = _rope_tables(GQA_HD)

    cqn = _rms(_col(proj, 0), p['mla_q_norm_g']).astype(BF16)
    q_all = _mm(cqn, p['mla_w_uq'], F32, 512).reshape(N_TOK, MLA_HEADS, MLA_QK)
    q_all = _rms(q_all, p['mla_qn_g']) * (MLA_QK ** -0.5)
    q_l, q_c = _split_lat_ctx(q_all)
    q_l = jnp.concatenate([q_l[..., :MLA_NOPE], _rope(q_l[..., MLA_NOPE:], *rope_mla)], axis=-1)

    ckvn = _rms(_col(proj, 1), p['mla_kv_norm_g']).astype(BF16)
    kv_all = _mm(ckvn, p['mla_w_ukv'], F32, 512).reshape(N_TOK, MLA_HEADS, MLA_NOPE + MLA_V)
    krope = jnp.broadcast_to(_col(proj, 2).astype(F32)[:, None, :], (N_TOK, MLA_HEADS, MLA_ROPE))
    k_all = _rms(jnp.concatenate([kv_all[..., :MLA_NOPE], krope], axis=-1), p['mla_kn_g'])
    k_l, k_c = _split_lat_ctx(k_all)
    k_l = jnp.concatenate([k_l[..., :MLA_NOPE], _rope(k_l[..., MLA_NOPE:], *rope_mla)], axis=-1)
    v_l, v_c = _split_lat_ctx(kv_all[..., MLA_NOPE:])
    ka_l, ka_c, va_c = _bhsd(k_l), _bhsd(k_c), _bhsd(v_c)
    vt_l = jnp.transpose(v_l, (0, 2, 3, 1)).astype(BF16)
    vt_c = jnp.transpose(v_c, (0, 2, 3, 1)).astype(BF16)
    oa_l = _mla_attn(_bhsd(q_l * _LOG2E), ka_l, vt_l, ka_c, vt_c, tq=512, tk=512)

    nq = _rms(_col(proj, 3).reshape(N_TOK, NA_HEADS, NA_HD), p['na_qn_g']) * (NA_HD ** -0.5)
    nk = _rms(_col(proj, 4).reshape(N_TOK, NA_HEADS, NA_HD), p['na_kn_g'])
    nv = _col(proj, 5).reshape(N_TOK, NA_HEADS, NA_HD)
    nq_l, nq_c = _split_lat_ctx(nq)
    nk_l, nk_c = _split_lat_ctx(nk)
    nv_l, nv_c = _split_lat_ctx(nv)
    kb_c, vb_c = _bhsd(nk_c), _bhsd(nv_c)
    ob_l = _na_attn(_bhsd(nq_l), _bhsd(nk_l), _bhsd(nv_l), kb_c, vb_c, _na_bias_table(p['na_rpb']))

    gq = _rms(_col(proj, 6).reshape(N_TOK, GQA_HEADS, GQA_HD), p['gqa_qn_g'])
    gk = _rms(_col(proj, 7).reshape(N_TOK, GQA_KV_HEADS, GQA_HD), p['gqa_kn_g'])
    gv = _col(proj, 8).reshape(N_TOK, GQA_KV_HEADS, GQA_HD)
    gq_l, gq_c = _split_lat_ctx(gq)
    gk_l, gk_c = _split_lat_ctx(gk)
    gv_l, gv_c = _split_lat_ctx(gv)
    gq_l = _rope(gq_l, *rope_gqa) * (GQA_HD ** -0.5)
    gk_l = _rope(gk_l, *rope_gqa)
    kc_c, vc_c = _bhsd(gk_c), _bhsd(gv_c)
    nb = SEQ // GQA_WINDOW
    gq_blk = gq_l.reshape(BATCH, nb, GQA_WINDOW, GQA_KV_HEADS, _GQA_G, GQA_HD)
    gq_blk = jnp.transpose(gq_blk, (0, 3, 1, 4, 2, 5)).reshape(
        BATCH, GQA_KV_HEADS, nb, _GQA_G * GQA_WINDOW, GQA_HD).astype(BF16)
    oc_blk = _gqa_attn(gq_blk, _bhsd(gk_l), _bhsd(gv_l), kc_c, vc_c, p['gqa_sink'])
    oc_l = oc_blk.reshape(BATCH, GQA_KV_HEADS, nb, _GQA_G, GQA_WINDOW, GQA_HD)
    oc_l = jnp.transpose(oc_l, (0, 2, 4, 1, 3, 5)).reshape(N_LAT, GQA_HEADS * GQA_HD)

    def flat(o):
        return jnp.transpose(o, (0, 2, 1, 3)).reshape(o.shape[0] * o.shape[2], -1)

    oa = flat(oa_l)
    ob = ob_l.reshape(N_LAT, NA_HEADS * NA_HD)
    oc = oc_l
    if ctx_out:
        oa_c = _attn(_bhsd(q_c), ka_c, va_c, tq=CTX_LEN, tk=CTX_LEN)
        ob_c = _attn(_bhsd(nq_c), kb_c, vb_c, tq=CTX_LEN, tk=CTX_LEN)
        oc_c = _attn(_bhsd(gq_c * (GQA_HD ** -0.5)), kc_c, vc_c, sink=p['gqa_sink'],
                     tq=CTX_LEN, tk=CTX_LEN)
        oa = jnp.concatenate([oa, flat(oa_c)], axis=0)
        ob = jnp.concatenate([ob, flat(ob_c)], axis=0)
        oc = jnp.concatenate([oc, flat(oc_c)], axis=0)
    return _merge(proj, oa, ob, oc, p['w_o_mla'].astype(BF16), p['w_o_na'].astype(BF16),
                  p['w_o_gqa'].astype(BF16))


def _moe(x, mod, norm_g, p):
    m = x.shape[0]
    wr = jnp.concatenate([p['moe_w_group'], p['moe_w_expert'],
                          jnp.zeros((D_MODEL, ROUTE_COLS - MOE_GROUPS - MOE_EXPERTS), F32)], axis=1)
    br = jnp.concatenate([p['moe_b_group'], p['moe_b_expert'],
                          jnp.zeros((ROUTE_COLS - MOE_GROUPS - MOE_EXPERTS,), F32)]).reshape(1, ROUTE_COLS)
    h, logits = _norm_mod(x, norm_g, mod, 3, 4, route=(wr, br))
    tile_expert, slot_token, n_used, slot_w, pos = _route(logits, m)
    y = _moe_ffn(h, tile_expert, slot_token, n_used, slot_w,
                 p['moe_w_gate'], p['moe_w_up'], p['moe_w_down'])
    return _moe_combine(y, pos, x, mod, 5)


def kernel(x, c, ctx, c_ctx, ada_w, ada_b, norm_mix_g, norm_ffn_g, w_in,
           mla_q_norm_g, mla_w_uq, mla_kv_norm_g, mla_w_ukv, mla_qn_g, mla_kn_g,
           na_qn_g, na_kn_g, na_rpb, gqa_qn_g, gqa_kn_g, gqa_sink,
           w_o_mla, w_o_na, w_o_gqa, w_out,
           moe_w_group, moe_b_group, moe_w_expert, moe_b_expert,
           moe_w_gate, moe_w_up, moe_w_down):
    xt = jnp.concatenate([x.reshape(N_LAT, D_MODEL), ctx.reshape(N_CTX, D_MODEL)], axis=0)
    c_rows = jnp.concatenate([c, c_ctx[None, :], jnp.zeros((8 - BATCH - 1, D_MODEL), F32)], axis=0)
    mod_all = _ada(c_rows, ada_w, ada_b)
    for l in range(DEPTH):
        ctx_out = l < DEPTH - 1
        p = {
            'w_in': w_in[l], 'mla_q_norm_g': mla_q_norm_g[l], 'mla_w_uq': mla_w_uq[l],
            'mla_kv_norm_g': mla_kv_norm_g[l], 'mla_w_ukv': mla_w_ukv[l],
            'mla_qn_g': mla_qn_g[l], 'mla_kn_g': mla_kn_g[l],
            'na_qn_g': na_qn_g[l], 'na_kn_g': na_kn_g[l], 'na_rpb': na_rpb[l],
            'gqa_qn_g': gqa_qn_g[l], 'gqa_kn_g': gqa_kn_g[l], 'gqa_sink': gqa_sink[l],
            'w_o_mla': w_o_mla[l], 'w_o_na': w_o_na[l], 'w_o_gqa': w_o_gqa[l],
            'moe_w_group': moe_w_group[l], 'moe_b_group': moe_b_group[l],
            'moe_w_expert': moe_w_expert[l], 'moe_b_expert': moe_b_expert[l],
            'moe_w_gate': moe_w_gate[l], 'moe_w_up': moe_w_up[l], 'moe_w_down': moe_w_down[l],
        }
        mod = mod_all[l].reshape(8 * 6, 1, D_MODEL)
        h = _norm_mod(xt, norm_mix_g[l], mod, 0, 1)
        y = _token_mixer(h, p, ctx_out)
        m = y.shape[0]
        x_mid = _mm_res(y, w_out[l], xt[:m], mod, 2, 512)
        xt = _moe(x_mid, mod, norm_ffn_g[l], p)
    return xt[:N_LAT].reshape(BATCH, SEQ, D_MODEL)
```

```python
import functools

import numpy as np
import jax
import jax.numpy as jnp
from jax import lax
from jax.experimental import pallas as pl
from jax.experimental.pallas import tpu as pltpu

D_MODEL = 2048
BATCH = 2
SEQ = 4096
DEPTH = 2
GRID_W = 64
CTX_LEN = 256
EPS = 1e-6
ROPE_THETA = 10000.0
NEG_INF = -1e30

MLA_HEADS = 8
MLA_Q_RANK = 512
MLA_KV_RANK = 512
MLA_NOPE = 128
MLA_ROPE = 64
MLA_QK = MLA_NOPE + MLA_ROPE
MLA_V = 128
NA_HEADS = 4
NA_HD = 128
NA_ROWS = 8
NA_COLS = 16
GQA_HEADS = 8
GQA_KV_HEADS = 2
GQA_HD = 64
GQA_WINDOW = 128
MOE_GROUPS = 4
MOE_PER_GROUP = 8
MOE_EXPERTS = MOE_GROUPS * MOE_PER_GROUP
MOE_TOPK = 2
MOE_HIDDEN = 512

IN_SIZES = (MLA_Q_RANK, MLA_KV_RANK, MLA_ROPE,
            NA_HEADS * NA_HD, NA_HEADS * NA_HD, NA_HEADS * NA_HD,
            GQA_HEADS * GQA_HD, GQA_KV_HEADS * GQA_HD, GQA_KV_HEADS * GQA_HD,
            D_MODEL, D_MODEL, D_MODEL)
IN_SPLITS = tuple(int(s) for s in np.cumsum(IN_SIZES)[:-1])

N_LAT = BATCH * SEQ
N_CTX = BATCH * CTX_LEN
N_TOK = N_LAT + N_CTX
ROWS = SEQ // GRID_W

V7X_LANES = 128
V7X_VMEM_LIMIT = 56 * 1024 * 1024

_PACK_ORDER = (9, 10, 11, 0, 1, 3, 4, 5, 6, 7, 8, 2)
_PACK_COLS = 9728
_PACK_OFF = {}
_off = 0
for _i in _PACK_ORDER:
    _PACK_OFF[_i] = _off
    _off += IN_SIZES[_i]

ROUTE_COLS = V7X_LANES
MOE_TM = 256
TOK_TM = 256
MM_TM = 512
CTX_BLK = N_LAT // CTX_LEN

F32 = jnp.float32
BF16 = jnp.bfloat16
_LOG2E = 1.4426950408889634
_NT = (((1,), (1,)), ((), ()))


def _cparams(sem):
    return pltpu.CompilerParams(dimension_semantics=sem, vmem_limit_bytes=V7X_VMEM_LIMIT)


def _seg(i, tm):
    return jnp.minimum(i // (SEQ // tm), 2)


def _pcol(idx, width):
    assert _PACK_OFF[idx] % width == 0
    return _PACK_OFF[idx] // width


def _ada_kernel(c_ref, w_ref, b_ref, o_ref):
    c = c_ref[...]
    a = c * jax.nn.sigmoid(c)
    o_ref[0] = jnp.dot(a, w_ref[0], preferred_element_type=F32,
                       precision=lax.Precision.HIGHEST) + b_ref[0]


def _ada(c_rows, ada_w, ada_b):
    tn = 1024
    n = 6 * D_MODEL
    return pl.pallas_call(
        _ada_kernel,
        grid=(DEPTH, n // tn),
        in_specs=[
            pl.BlockSpec((8, D_MODEL), lambda l, j: (0, 0)),
            pl.BlockSpec((1, D_MODEL, tn), lambda l, j: (l, 0, j)),
            pl.BlockSpec((1, 1, tn), lambda l, j: (l, 0, j)),
        ],
        out_specs=pl.BlockSpec((1, 8, tn), lambda l, j: (l, 0, j)),
        out_shape=jax.ShapeDtypeStruct((DEPTH, 8, n), F32),
        compiler_params=_cparams(("arbitrary", "arbitrary")),
        name="ada",
    )(c_rows, ada_w, ada_b.reshape(DEPTH, 1, n))


def _norm_mod_kernel(x_ref, g_ref, sh_ref, sc_ref, h_ref):
    x = x_ref[...]
    xn = x * lax.rsqrt(jnp.mean(x * x, axis=-1, keepdims=True) + EPS) * g_ref[...]
    h_ref[...] = (xn * (1.0 + sc_ref[0]) + sh_ref[0]).astype(h_ref.dtype)


def _norm_mod_route_kernel(x_ref, g_ref, sh_ref, sc_ref, wr_ref, br_ref, h_ref, lg_ref):
    x = x_ref[...]
    xn = x * lax.rsqrt(jnp.mean(x * x, axis=-1, keepdims=True) + EPS) * g_ref[...]
    h = xn * (1.0 + sc_ref[0]) + sh_ref[0]
    h_ref[...] = h
    lg_ref[...] = jnp.dot(h, wr_ref[...], preferred_element_type=F32,
                          precision=lax.Precision.HIGHEST) + br_ref[...]


def _norm_mod(x, g, mod, k_shift, k_scale, route=None):
    m = x.shape[0]
    tm = TOK_TM
    base = [
        pl.BlockSpec((tm, D_MODEL), lambda i: (i, 0)),
        pl.BlockSpec((1, D_MODEL), lambda i: (0, 0)),
        pl.BlockSpec((1, 1, D_MODEL), lambda i: (_seg(i, tm) * 6 + k_shift, 0, 0)),
        pl.BlockSpec((1, 1, D_MODEL), lambda i: (_seg(i, tm) * 6 + k_scale, 0, 0)),
    ]
    if route is None:
        return pl.pallas_call(
            _norm_mod_kernel,
            grid=(m // tm,),
            in_specs=base,
            out_specs=pl.BlockSpec((tm, D_MODEL), lambda i: (i, 0)),
            out_shape=jax.ShapeDtypeStruct((m, D_MODEL), BF16),
            compiler_params=_cparams(("arbitrary",)),
            name="norm_mod",
        )(x, g.reshape(1, D_MODEL), mod, mod)
    wr, br = route
    return pl.pallas_call(
        _norm_mod_route_kernel,
        grid=(m // tm,),
        in_specs=base + [
            pl.BlockSpec((D_MODEL, ROUTE_COLS), lambda i: (0, 0)),
            pl.BlockSpec((1, ROUTE_COLS), lambda i: (0, 0)),
        ],
        out_specs=[pl.BlockSpec((tm, D_MODEL), lambda i: (i, 0)),
                   pl.BlockSpec((tm, ROUTE_COLS), lambda i: (i, 0))],
        out_shape=[jax.ShapeDtypeStruct((m, D_MODEL), F32),
                   jax.ShapeDtypeStruct((m, ROUTE_COLS), F32)],
        compiler_params=_cparams(("arbitrary",)),
        name="norm_mod_route",
    )(x, g.reshape(1, D_MODEL), mod, mod, wr, br)


def _mm_kernel(x_ref, w_ref, o_ref, wb_ref):
    @pl.when(pl.program_id(1) == 0)
    def _():
        wb_ref[...] = w_ref[...].astype(BF16)

    o_ref[...] = jnp.dot(x_ref[...], wb_ref[...], preferred_element_type=F32).astype(o_ref.dtype)


def _mm_res_kernel(x_ref, w_ref, r_ref, g_ref, o_ref, wb_ref):
    @pl.when(pl.program_id(1) == 0)
    def _():
        wb_ref[...] = w_ref[...].astype(BF16)

    acc = jnp.dot(x_ref[...], wb_ref[...], preferred_element_type=F32)
    o_ref[...] = r_ref[...] + g_ref[0] * acc


def _mm(x, w, out_dtype, tn):
    m, k = x.shape
    n = w.shape[1]
    tm = MM_TM
    return pl.pallas_call(
        _mm_kernel,
        grid=(n // tn, m // tm),
        in_specs=[pl.BlockSpec((tm, k), lambda j, i: (i, 0)),
                  pl.BlockSpec((k, tn), lambda j, i: (0, j))],
        out_specs=pl.BlockSpec((tm, tn), lambda j, i: (i, j)),
        out_shape=jax.ShapeDtypeStruct((m, n), out_dtype),
        scratch_shapes=[pltpu.VMEM((k, tn), BF16)],
        compiler_params=_cparams(("arbitrary", "arbitrary")),
        name="mm",
    )(x, w)


def _mm_res(x, w, res, mod, k_gate, tn):
    m, k = x.shape
    n = w.shape[1]
    tm = MM_TM
    nj = n // tn
    return pl.pallas_call(
        _mm_res_kernel,
        grid=(nj, m // tm),
        in_specs=[pl.BlockSpec((tm, k), lambda j, i: (i, 0)),
                  pl.BlockSpec((k, tn), lambda j, i: (0, j)),
                  pl.BlockSpec((tm, tn), lambda j, i: (i, j)),
                  pl.BlockSpec((1, 1, tn), lambda j, i: (_seg(i, tm) * 6 + k_gate, 0, j))],
        out_specs=pl.BlockSpec((tm, tn), lambda j, i: (i, j)),
        out_shape=jax.ShapeDtypeStruct((m, n), F32),
        scratch_shapes=[pltpu.VMEM((k, tn), BF16)],
        compiler_params=_cparams(("arbitrary", "arbitrary")),
        name="mm_res",
    )(x, w, res, mod)


def _row_rms(x, g):
    return x * lax.rsqrt(jnp.mean(x * x, axis=-1, keepdims=True) + EPS) * g


_MLA_PAD = 2 * V7X_LANES


def _mla_prep_kernel(cq_ref, ckv_ref, kr_ref, wq_ref, wkv_ref, gqi_ref, gkvi_ref,
                     gq_ref, gkn_ref, gkr_ref, c_ref, s_ref, q_ref, k_ref, vt_ref):
    c = c_ref[...]
    s = s_ref[...]
    hw = MLA_HEADS * MLA_NOPE

    def rot(t):
        return t * c + (pltpu.roll(t, 32, 1) + pltpu.roll(t, 96, 1)) * s

    cqn = _row_rms(cq_ref[...].astype(F32), gqi_ref[...]).astype(BF16)
    qf = jnp.dot(cqn, wq_ref[...], preferred_element_type=F32)
    gq = gq_ref[...]
    inv = 1.0 / MLA_QK
    for h in range(MLA_HEADS):
        nope = qf[:, h * 128:(h + 1) * 128]
        t = qf[:, hw + h * 128:hw + (h + 1) * 128]
        ss = jnp.sum(nope * nope, axis=-1, keepdims=True) + jnp.sum(t * t, axis=-1, keepdims=True)
        r = lax.rsqrt(ss * inv + EPS)
        q_ref[h, :, 0:128] = (nope * r * gq[:, 0:128]).astype(BF16)
        q_ref[h, :, 128:256] = rot(t * r * gq[:, 128:256]).astype(BF16)

    ckvn = _row_rms(ckv_ref[...].astype(F32), gkvi_ref[...]).astype(BF16)
    kvf = jnp.dot(ckvn, wkv_ref[...], preferred_element_type=F32)
    kr = kr_ref[...].astype(F32)
    ssr = jnp.sum(kr * kr, axis=-1, keepdims=True)
    yrot = rot(kr * gkr_ref[...])
    gkn = gkn_ref[...]
    for h in range(MLA_HEADS):
        nope = kvf[:, h * 128:(h + 1) * 128]
        r = lax.rsqrt((jnp.sum(nope * nope, axis=-1, keepdims=True) + ssr) * inv + EPS)
        k_ref[h, :, 0:128] = (nope * r * gkn).astype(BF16)
        k_ref[h, :, 128:256] = (yrot * r).astype(BF16)
        vt_ref[h] = kvf[:, hw + h * 128:hw + (h + 1) * 128].T.astype(BF16)


def _mla_prep(proj, p, tabs):
    tm = TOK_TM
    hw = MLA_HEADS * MLA_NOPE
    wq = p['mla_w_uq'].reshape(MLA_Q_RANK, MLA_HEADS, MLA_QK)
    wq_rope = jnp.pad(wq[:, :, MLA_NOPE:], ((0, 0), (0, 0), (0, 128 - MLA_ROPE)))
    wq = jnp.concatenate([wq[:, :, :MLA_NOPE].reshape(MLA_Q_RANK, hw),
                          wq_rope.reshape(MLA_Q_RANK, hw)], axis=1).astype(BF16)
    wkv = p['mla_w_ukv'].reshape(MLA_KV_RANK, MLA_HEADS, MLA_NOPE + MLA_V)
    wkv = jnp.concatenate([wkv[:, :, :MLA_NOPE].reshape(MLA_KV_RANK, hw),
                           wkv[:, :, MLA_NOPE:].reshape(MLA_KV_RANK, hw)], axis=1).astype(BF16)
    zpad = jnp.zeros((128 - MLA_ROPE,), F32)
    gq = (jnp.concatenate([p['mla_qn_g'], zpad]) * (MLA_QK ** -0.5 * _LOG2E)).reshape(1, _MLA_PAD)
    gkn = p['mla_kn_g'][:MLA_NOPE].reshape(1, 128)
    gkr = jnp.concatenate([p['mla_kn_g'][MLA_NOPE:], zpad]).reshape(1, 128)

    def const(shape):
        return pl.BlockSpec(shape, lambda i: (0,) * len(shape))

    return pl.pallas_call(
        _mla_prep_kernel,
        grid=(N_TOK // tm,),
        in_specs=[pl.BlockSpec((tm, MLA_Q_RANK), lambda i: (i, _pcol(0, MLA_Q_RANK))),
                  pl.BlockSpec((tm, MLA_KV_RANK), lambda i: (i, _pcol(1, MLA_KV_RANK))),
                  pl.BlockSpec((tm, 128), lambda i: (i, _pcol(2, 128))),
                  const((MLA_Q_RANK, 2 * hw)), const((MLA_KV_RANK, 2 * hw)),
                  const((1, MLA_Q_RANK)), const((1, MLA_KV_RANK)),
                  const((1, _MLA_PAD)), const((1, 128)), const((1, 128)),
                  pl.BlockSpec((tm, 128), lambda i: (i, 0)),
                  pl.BlockSpec((tm, 128), lambda i: (i, 0))],
        out_specs=[pl.BlockSpec((MLA_HEADS, tm, _MLA_PAD), lambda i: (0, i, 0)),
                   pl.BlockSpec((MLA_HEADS, tm, _MLA_PAD), lambda i: (0, i, 0)),
                   pl.BlockSpec((MLA_HEADS, MLA_V, tm), lambda i: (0, 0, i))],
        out_shape=[jax.ShapeDtypeStruct((MLA_HEADS, N_TOK, _MLA_PAD), BF16),
                   jax.ShapeDtypeStruct((MLA_HEADS, N_TOK, _MLA_PAD), BF16),
                   jax.ShapeDtypeStruct((MLA_HEADS, MLA_V, N_TOK), BF16)],
        compiler_params=_cparams(("arbitrary",)),
        name="mla_prep",
    )(proj, proj, proj, wq, wkv, p['mla_q_norm_g'].reshape(1, -1), p['mla_kv_norm_g'].reshape(1, -1),
      gq, gkn, gkr, tabs['mla_c'], tabs['mla_s'])


def _mla_kernel(q_ref, k1_ref, vt1_ref, k2_ref, vt2_ref, o_ref, sa_ref, sb_ref, sc_ref, acc_ref,
                *, tk, n_chunks):
    q = q_ref[0]
    tq = q.shape[0]

    def scores(dst_ref, kc):
        dst_ref[...] = lax.dot_general(kc, q, _NT, preferred_element_type=F32)

    def k_chunk(c):
        return k1_ref[0, pl.ds(pl.multiple_of(c * tk, tk), tk), :]

    def vt_chunk(c):
        return vt1_ref[0, :, pl.ds(pl.multiple_of(c * tk, tk), tk)]

    def accumulate(s_ref, vtc, m, l):
        st = s_ref[...]
        m_new = jnp.maximum(m, jnp.max(st, axis=0, keepdims=True))
        a = jnp.exp2(m - m_new)
        p = jnp.exp2(st - m_new)
        l = a * l + jnp.sum(p, axis=0, keepdims=True)
        acc_ref[...] = a * acc_ref[...] + jnp.dot(vtc, p.astype(BF16), preferred_element_type=F32)
        return m_new, l

    m = jnp.full((1, tq), NEG_INF, F32)
    l = jnp.zeros((1, tq), F32)
    acc_ref[...] = jnp.zeros_like(acc_ref)
    scores(sc_ref, k2_ref[0])
    scores(sa_ref, k_chunk(0))
    m, l = accumulate(sc_ref, vt2_ref[0], m, l)

    def body(i, carry):
        m, l = carry
        scores(sb_ref, k_chunk(2 * i + 1))
        m, l = accumulate(sa_ref, vt_chunk(2 * i), m, l)
        scores(sa_ref, k_chunk(2 * i + 2))
        m, l = accumulate(sb_ref, vt_chunk(2 * i + 1), m, l)
        return m, l

    m, l = lax.fori_loop(0, n_chunks // 2 - 1, body, (m, l))
    scores(sb_ref, k_chunk(n_chunks - 1))
    m, l = accumulate(sa_ref, vt_chunk(n_chunks - 2), m, l)
    m, l = accumulate(sb_ref, vt_chunk(n_chunks - 1), m, l)
    o_ref[...] = (acc_ref[...] / l).T.astype(o_ref.dtype)


def _mla_attn(q, k, vt, *, tq, tk):
    nq = SEQ // tq
    return pl.pallas_call(
        functools.partial(_mla_kernel, tk=tk, n_chunks=SEQ // tk),
        grid=(BATCH, MLA_HEADS, nq),
        in_specs=[pl.BlockSpec((1, tq, _MLA_PAD), lambda b, h, i: (h, b * nq + i, 0)),
                  pl.BlockSpec((1, SEQ, _MLA_PAD), lambda b, h, i: (h, b, 0)),
                  pl.BlockSpec((1, MLA_V, SEQ), lambda b, h, i: (h, 0, b)),
                  pl.BlockSpec((1, CTX_LEN, _MLA_PAD), lambda b, h, i: (h, CTX_BLK + b, 0)),
                  pl.BlockSpec((1, MLA_V, CTX_LEN), lambda b, h, i: (h, 0, CTX_BLK + b))],
        out_specs=pl.BlockSpec((tq, MLA_V), lambda b, h, i: (b * nq + i, h)),
        out_shape=jax.ShapeDtypeStruct((N_LAT, MLA_HEADS * MLA_V), BF16),
        scratch_shapes=[pltpu.VMEM((tk, tq), F32), pltpu.VMEM((tk, tq), F32),
                        pltpu.VMEM((CTX_LEN, tq), F32), pltpu.VMEM((MLA_V, tq), F32)],
        compiler_params=_cparams(("arbitrary", "arbitrary", "arbitrary")),
        name="mla_attn",
    )(q, k, vt, k, vt)


_NA_KEYS = NA_ROWS * GRID_W
NA_RB = 8


def _na_pattern(r):
    half = NA_ROWS // 2
    return jnp.where(r < half, r, jnp.where(r <= ROWS - half, half, r - (ROWS - NA_ROWS)))


def _na_kernel(q_ref, k_ref, v_ref, kc_ref, vc_ref, bias_ref, gq_ref, gk_ref, o_ref, kn_ref, kcn_ref):
    rb = pl.program_id(2)

    @pl.when(rb == 0)
    def _():
        kn_ref[...] = _row_rms(k_ref[...].astype(F32), gk_ref[...]).astype(BF16)
        kcn_ref[...] = _row_rms(kc_ref[...].astype(F32), gk_ref[...]).astype(BF16)

    q_all = _row_rms(q_ref[...].astype(F32), gq_ref[...]).astype(BF16)
    vc = vc_ref[...]
    sc_all = lax.dot_general(q_all, kcn_ref[...], _NT, preferred_element_type=F32)
    ms, ls, os_ = [], [], []
    for j in range(NA_RB):
        r = rb * NA_RB + j
        start = pl.multiple_of(jnp.clip(r - NA_ROWS // 2, 0, ROWS - NA_ROWS) * GRID_W, GRID_W)
        q = q_all[j * GRID_W:(j + 1) * GRID_W]
        k = kn_ref[pl.ds(start, _NA_KEYS), :]
        v = v_ref[pl.ds(start, _NA_KEYS), :]
        s = lax.dot_general(q, k, _NT, preferred_element_type=F32) + bias_ref[0, _na_pattern(r)]
        sc = sc_all[j * GRID_W:(j + 1) * GRID_W]
        m = jnp.maximum(jnp.max(s, axis=-1, keepdims=True), jnp.max(sc, axis=-1, keepdims=True))
        p = jnp.exp(s - m)
        ms.append(m)
        ls.append(jnp.sum(p, axis=-1, keepdims=True))
        os_.append(jnp.dot(p.astype(BF16), v, preferred_element_type=F32))
    m_all = jnp.concatenate(ms, axis=0)
    pc = jnp.exp(sc_all - m_all)
    l_all = jnp.concatenate(ls, axis=0) + jnp.sum(pc, axis=-1, keepdims=True)
    o = jnp.concatenate(os_, axis=0) + jnp.dot(pc.astype(BF16), vc, preferred_element_type=F32)
    o_ref[...] = (o / l_all).astype(o_ref.dtype)


def _na_attn(proj, bias, gq, gk):
    qrows = NA_RB * GRID_W
    nrb = ROWS // NA_RB
    cq, ck, cv = _pcol(3, NA_HD), _pcol(4, NA_HD), _pcol(5, NA_HD)
    return pl.pallas_call(
        _na_kernel,
        grid=(BATCH, NA_HEADS, nrb),
        in_specs=[pl.BlockSpec((qrows, NA_HD), lambda b, h, r: (b * nrb + r, cq + h)),
                  pl.BlockSpec((SEQ, NA_HD), lambda b, h, r: (b, ck + h)),
                  pl.BlockSpec((SEQ, NA_HD), lambda b, h, r: (b, cv + h)),
                  pl.BlockSpec((CTX_LEN, NA_HD), lambda b, h, r: (CTX_BLK + b, ck + h)),
                  pl.BlockSpec((CTX_LEN, NA_HD), lambda b, h, r: (CTX_BLK + b, cv + h)),
                  pl.BlockSpec((1, NA_ROWS, GRID_W, _NA_KEYS), lambda b, h, r: (h, 0, 0, 0)),
                  pl.BlockSpec((1, NA_HD), lambda b, h, r: (0, 0)),
                  pl.BlockSpec((1, NA_HD), lambda b, h, r: (0, 0))],
        out_specs=pl.BlockSpec((qrows, NA_HD), lambda b, h, r: (b * nrb + r, h)),
        out_shape=jax.ShapeDtypeStruct((N_LAT, NA_HEADS * NA_HD), BF16),
        scratch_shapes=[pltpu.VMEM((SEQ, NA_HD), BF16), pltpu.VMEM((CTX_LEN, NA_HD), BF16)],
        compiler_params=_cparams(("arbitrary", "arbitrary", "arbitrary")),
        name="na_attn",
    )(proj, proj, proj, proj, proj, bias, gq, gk)


def _na_bias_table(rpb):
    half = NA_ROWS // 2
    r_rep = np.array(list(range(half)) + [half] + list(range(ROWS - half + 1, ROWS)))
    start = np.clip(r_rep - half, 0, ROWS - NA_ROWS)
    dr = start[:, None] + np.arange(NA_ROWS)[None, :] - r_rep[:, None] + NA_ROWS - 1
    qc = np.arange(GRID_W)
    kcol = np.arange(GRID_W)
    col_start = np.clip(qc - NA_COLS // 2, 0, GRID_W - NA_COLS)
    in_win = (kcol[None, :] >= col_start[:, None]) & (kcol[None, :] < col_start[:, None] + NA_COLS)
    dc = np.clip(kcol[None, :] - qc[:, None], 1 - NA_COLS, NA_COLS - 1) + NA_COLS - 1
    rsel = (dr[:, :, None] == np.arange(2 * NA_ROWS - 1)).astype(np.float32)
    csel = (dc[:, :, None] == np.arange(2 * NA_COLS - 1)).astype(np.float32)
    b = jnp.einsum('pja,hab,qkb->hpqjk', rsel, rpb.astype(F32), csel,
                   precision=lax.Precision.HIGHEST)
    b = jnp.where(in_win[None, None, :, None, :], b.astype(F32), NEG_INF)
    return b.reshape(NA_HEADS, NA_ROWS, GRID_W, _NA_KEYS)


_GQA_G = GQA_HEADS // GQA_KV_HEADS
_GQA_BAND = 3 * GQA_WINDOW
_GQA_QW = GQA_HEADS * V7X_LANES


def _gqa_prep_kernel(q_ref, k_ref, gq_ref, gk_ref, c_ref, s1_ref, s2_ref, qd_ref, kn_ref):
    c = c_ref[...]
    s1 = s1_ref[...]
    s2 = s2_ref[...]
    lo = lax.broadcasted_iota(jnp.int32, (1, 128), 1) < GQA_HD

    def head_rms(x, g):
        x2 = x * x
        s_lo = jnp.sum(jnp.where(lo, x2, 0.0), axis=-1, keepdims=True)
        s_hi = jnp.sum(jnp.where(lo, 0.0, x2), axis=-1, keepdims=True)
        inv = 1.0 / GQA_HD
        r = jnp.where(lo, lax.rsqrt(s_lo * inv + EPS), lax.rsqrt(s_hi * inv + EPS))
        return x * r * g

    def rot(x):
        return x * c + pltpu.roll(x, 96, 1) * s1 + pltpu.roll(x, 32, 1) * s2

    gq = gq_ref[...]
    for j in range(GQA_HEADS // 2):
        y = rot(head_rms(q_ref[:, j * 128:(j + 1) * 128].astype(F32), gq))
        sw = pltpu.roll(y, 64, 1)
        hk = (2 * j) // _GQA_G
        if hk == 0:
            even, odd = jnp.where(lo, y, 0.0), jnp.where(lo, sw, 0.0)
        else:
            even, odd = jnp.where(lo, 0.0, sw), jnp.where(lo, 0.0, y)
        qd_ref[:, (2 * j) * 128:(2 * j + 1) * 128] = even.astype(BF16)
        qd_ref[:, (2 * j + 1) * 128:(2 * j + 2) * 128] = odd.astype(BF16)
    kn_ref[...] = rot(head_rms(k_ref[...].astype(F32), gk_ref[...])).astype(BF16)


def _gqa_prep(proj, p, tabs):
    tm = TOK_TM
    gq = (jnp.tile(p['gqa_qn_g'], 2) * (GQA_HD ** -0.5)).reshape(1, 128)
    gk = jnp.tile(p['gqa_kn_g'], 2).reshape(1, 128)
    row = pl.BlockSpec((tm, 128), lambda i: (i, 0))
    vec = pl.BlockSpec((1, 128), lambda i: (0, 0))
    return pl.pallas_call(
        _gqa_prep_kernel,
        grid=(N_TOK // tm,),
        in_specs=[pl.BlockSpec((tm, GQA_HEADS * GQA_HD), lambda i: (i, _pcol(6, GQA_HEADS * GQA_HD))),
                  pl.BlockSpec((tm, 128), lambda i: (i, _pcol(7, 128))),
                  vec, vec, row, row, row],
        out_specs=[pl.BlockSpec((tm, _GQA_QW), lambda i: (i, 0)), row],
        out_shape=[jax.ShapeDtypeStruct((N_TOK, _GQA_QW), BF16),
                   jax.ShapeDtypeStruct((N_TOK, 128), BF16)],
        compiler_params=_cparams(("arbitrary",)),
        name="gqa_prep",
    )(proj, proj, gq, gk, tabs['gqa_c'], tabs['gqa_s1'], tabs['gqa_s2'])


def _gqa_kernel(sink_ref, q_ref, k_ref, v_ref, kc_ref, vc_ref, o_ref):
    hk = pl.program_id(1)
    n = pl.program_id(2)
    w = GQA_WINDOW
    start = pl.multiple_of(jnp.clip((n - 1) * w, 0, SEQ - _GQA_BAND), w)
    q = jnp.concatenate([q_ref[:, g * 128:(g + 1) * 128] for g in range(_GQA_G)], axis=0)
    k = k_ref[pl.ds(start, _GQA_BAND), :]
    v = v_ref[pl.ds(start, _GQA_BAND), :]
    s = lax.dot_general(q, k, _NT, preferred_element_type=F32)
    rows = lax.broadcasted_iota(jnp.int32, s.shape, 0)
    cols = lax.broadcasted_iota(jnp.int32, s.shape, 1)
    qpos = n * w + (rows & (w - 1))
    kpos = start + cols
    s = jnp.where(jnp.abs(kpos - qpos) <= GQA_WINDOW, s, NEG_INF)
    sc = lax.dot_general(q, kc_ref[...], _NT, preferred_element_type=F32)
    grow = lax.broadcasted_iota(jnp.int32, (_GQA_G * w, 1), 0) // w
    snk = jnp.full((_GQA_G * w, 1), sink_ref[hk * _GQA_G], F32)
    for g in range(1, _GQA_G):
        snk = jnp.where(grow == g, sink_ref[hk * _GQA_G + g], snk)
    m = jnp.maximum(jnp.maximum(jnp.max(s, axis=-1, keepdims=True),
                                jnp.max(sc, axis=-1, keepdims=True)), snk)
    p = jnp.exp(s - m)
    pc = jnp.exp(sc - m)
    l = jnp.sum(p, axis=-1, keepdims=True) + jnp.sum(pc, axis=-1, keepdims=True) + jnp.exp(snk - m)
    o = (jnp.dot(p.astype(BF16), v, preferred_element_type=F32)
         + jnp.dot(pc.astype(BF16), vc_ref[...], preferred_element_type=F32)) / l
    half = lax.broadcasted_iota(jnp.int32, (1, 128), 1) // GQA_HD
    o = jnp.where(half == hk, o, 0.0).astype(o_ref.dtype)
    for g in range(_GQA_G):
        o_ref[:, g * 128:(g + 1) * 128] = o[g * w:(g + 1) * w]


def _gqa_attn(qd, kn, proj, sink):
    nb = SEQ // GQA_WINDOW
    qw = _GQA_G * 128
    cv = _pcol(8, 128)
    gs = pltpu.PrefetchScalarGridSpec(
        num_scalar_prefetch=1,
        grid=(BATCH, GQA_KV_HEADS, nb),
        in_specs=[pl.BlockSpec((GQA_WINDOW, qw), lambda b, h, n, *_: (b * nb + n, h)),
                  pl.BlockSpec((SEQ, 128), lambda b, h, n, *_: (b, 0)),
                  pl.BlockSpec((SEQ, 128), lambda b, h, n, *_: (b, cv)),
                  pl.BlockSpec((CTX_LEN, 128), lambda b, h, n, *_: (CTX_BLK + b, 0)),
                  pl.BlockSpec((CTX_LEN, 128), lambda b, h, n, *_: (CTX_BLK + b, cv))],
        out_specs=pl.BlockSpec((GQA_WINDOW, qw), lambda b, h, n, *_: (b * nb + n, h)),
    )
    return pl.pallas_call(
        _gqa_kernel,
        grid_spec=gs,
        out_shape=jax.ShapeDtypeStruct((N_LAT, _GQA_QW), BF16),
        compiler_params=_cparams(("arbitrary", "arbitrary", "arbitrary")),
        name="gqa_attn",
    )(sink.astype(F32), qd, kn, proj, kn, proj)


def _ctx_kernel(sink_ref, mq_ref, mk_ref, mvt_ref, nq_ref, nk_ref, nv_ref, gnq_ref, gnk_ref,
                gq_ref, gk_ref, gv_ref, oa_ref, ob_ref, oc_ref):
    for h in range(MLA_HEADS):
        st = lax.dot_general(mk_ref[h], mq_ref[h], _NT, preferred_element_type=F32)
        p = jnp.exp2(st - jnp.max(st, axis=0, keepdims=True))
        l = jnp.sum(p, axis=0, keepdims=True)
        ot = jnp.dot(mvt_ref[h], p.astype(BF16), preferred_element_type=F32) / l
        oa_ref[:, h * MLA_V:(h + 1) * MLA_V] = ot.T.astype(oa_ref.dtype)
    for h in range(NA_HEADS):
        sl = slice(h * NA_HD, (h + 1) * NA_HD)
        q = _row_rms(nq_ref[:, sl].astype(F32), gnq_ref[...]).astype(BF16)
        k = _row_rms(nk_ref[:, sl].astype(F32), gnk_ref[...]).astype(BF16)
        s = lax.dot_general(q, k, _NT, preferred_element_type=F32)
        p = jnp.exp(s - jnp.max(s, axis=-1, keepdims=True))
        l = jnp.sum(p, axis=-1, keepdims=True)
        o = jnp.dot(p.astype(BF16), nv_ref[:, sl], preferred_element_type=F32) / l
        ob_ref[:, sl] = o.astype(ob_ref.dtype)
    half = lax.broadcasted_iota(jnp.int32, (1, 128), 1) // GQA_HD
    k = gk_ref[...]
    v = gv_ref[...]
    for h in range(GQA_HEADS):
        sl = slice(h * 128, (h + 1) * 128)
        s = lax.dot_general(gq_ref[:, sl], k, _NT, preferred_element_type=F32)
        snk = sink_ref[h]
        m = jnp.maximum(jnp.max(s, axis=-1, keepdims=True), snk)
        p = jnp.exp(s - m)
        l = jnp.sum(p, axis=-1, keepdims=True) + jnp.exp(snk - m)
        o = jnp.dot(p.astype(BF16), v, preferred_element_type=F32) / l
        oc_ref[:, sl] = jnp.where(half == h // _GQA_G, o, 0.0).astype(oc_ref.dtype)


def _ctx_attn(sink, mq, mk, mvt, proj, gnq, gnk, qd, kn):
    c = CTX_LEN
    nwid = NA_HEADS * NA_HD

    def row(width, col=0):
        return pl.BlockSpec((c, width), lambda b, *_: (CTX_BLK + b, col))

    def out(width):
        return pl.BlockSpec((c, width), lambda b, *_: (b, 0))

    vec = pl.BlockSpec((1, NA_HD), lambda b, *_: (0, 0))
    gs = pltpu.PrefetchScalarGridSpec(
        num_scalar_prefetch=1,
        grid=(BATCH,),
        in_specs=[pl.BlockSpec((MLA_HEADS, c, _MLA_PAD), lambda b, *_: (0, CTX_BLK + b, 0)),
                  pl.BlockSpec((MLA_HEADS, c, _MLA_PAD), lambda b, *_: (0, CTX_BLK + b, 0)),
                  pl.BlockSpec((MLA_HEADS, MLA_V, c), lambda b, *_: (0, 0, CTX_BLK + b)),
                  row(nwid, _pcol(3, nwid)), row(nwid, _pcol(4, nwid)), row(nwid, _pcol(5, nwid)),
                  vec, vec,
                  row(_GQA_QW), row(128), row(128, _pcol(8, 128))],
        out_specs=[out(MLA_HEADS * MLA_V), out(nwid), out(_GQA_QW)],
    )
    return pl.pallas_call(
        _ctx_kernel,
        grid_spec=gs,
        out_shape=[jax.ShapeDtypeStruct((N_CTX, MLA_HEADS * MLA_V), BF16),
                   jax.ShapeDtypeStruct((N_CTX, nwid), BF16),
                   jax.ShapeDtypeStruct((N_CTX, _GQA_QW), BF16)],
        compiler_params=_cparams(("arbitrary",)),
        name="ctx_attn",
    )(sink.astype(F32), mq, mk, mvt, proj, proj, proj, gnq, gnk, qd, kn, proj)


def _merge_kernel(*refs, has_ctx, n_lat_tiles):
    ga_ref, gb_ref, gc_ref, oa_ref, ob_ref, oc_ref = refs[:6]
    refs = refs[6:]
    if has_ctx:
        ca_ref, cb_ref, cc_ref = refs[:3]
        refs = refs[3:]
    wa_ref, wb_ref, wc_ref, y_ref = refs
    is_ctx = pl.program_id(0) >= n_lat_tiles

    def branch(g_ref, o_ref, c_ref, w_ref):
        o = o_ref[...]
        if has_ctx:
            o = jnp.where(is_ctx, c_ref[...], o)
        return jax.nn.sigmoid(g_ref[...].astype(F32)) * jnp.dot(o, w_ref[...], preferred_element_type=F32)

    y = (branch(ga_ref, oa_ref, ca_ref if has_ctx else None, wa_ref)
         + branch(gb_ref, ob_ref, cb_ref if has_ctx else None, wb_ref)
         + branch(gc_ref, oc_ref, cc_ref if has_ctx else None, wc_ref))
    y_ref[...] = y.astype(y_ref.dtype)


def _merge(proj, lat, ctx, wa, wb, wc):
    tm = TOK_TM
    has_ctx = ctx is not None
    m = N_TOK if has_ctx else N_LAT
    nl = N_LAT // tm
    ks = [o.shape[1] for o in lat]
    in_specs = [pl.BlockSpec((tm, D_MODEL), lambda i: (i, 0)),
                pl.BlockSpec((tm, D_MODEL), lambda i: (i, 1)),
                pl.BlockSpec((tm, D_MODEL), lambda i: (i, 2))]
    in_specs += [pl.BlockSpec((tm, k), lambda i: (jnp.minimum(i, nl - 1), 0)) for k in ks]
    args = [proj, proj, proj] + list(lat)
    if has_ctx:
        in_specs += [pl.BlockSpec((tm, k), lambda i: (jnp.maximum(i - nl, 0), 0)) for k in ks]
        args += list(ctx)
    in_specs += [pl.BlockSpec((k, D_MODEL), lambda i: (0, 0)) for k in ks]
    args += [wa, wb, wc]
    return pl.pallas_call(
        functools.partial(_merge_kernel, has_ctx=has_ctx, n_lat_tiles=nl),
        grid=(m // tm,),
        in_specs=in_specs,
        out_specs=pl.BlockSpec((tm, D_MODEL), lambda i: (i, 0)),
        out_shape=jax.ShapeDtypeStruct((m, D_MODEL), BF16),
        compiler_params=_cparams(("arbitrary",)),
        name="merge",
    )(*args)


def _moe_ffn_kernel(te_ref, tok_ref, nu_ref, h_hbm, sw_ref, wg_ref, wu_ref, wd_ref, y_ref,
                    xbuf, sem, wgb, wub, wdb):
    i = pl.program_id(0)
    tm = MOE_TM
    slot = i % 2
    n_used = nu_ref[0]

    def row_copy(tok, s, r):
        return pltpu.make_async_copy(h_hbm.at[pl.ds(tok, 1)], xbuf.at[s, pl.ds(r, 1)], sem.at[s])

    def start_gather(tile, s):
        base = tile * tm

        def body(r, c):
            row_copy(tok_ref[base + r], s, r).start()
            return c

        lax.fori_loop(0, tm, body, 0)

    @pl.when(i == 0)
    def _():
        start_gather(0, 0)

    @pl.when(i + 1 < n_used)
    def _():
        start_gather(i + 1, 1 - slot)

    @pl.when(i < n_used)
    def _():
        @pl.when((i == 0) | (te_ref[i] != te_ref[jnp.maximum(i - 1, 0)]))
        def _():
            wgb[...] = wg_ref[0].astype(BF16)
            wub[...] = wu_ref[0].astype(BF16)
            wdb[...] = wd_ref[0].astype(BF16)

        def wbody(r, c):
            row_copy(0, slot, r).wait()
            return c

        lax.fori_loop(0, tm, wbody, 0)
        x = xbuf[slot].astype(BF16)
        hg = jnp.dot(x, wgb[...], preferred_element_type=F32)
        hu = jnp.dot(x, wub[...], preferred_element_type=F32)
        act = (hg * jax.nn.sigmoid(hg)) * hu * sw_ref[...]
        y_ref[...] = jnp.dot(act.astype(BF16), wdb[...], preferred_element_type=F32)

    @pl.when(i >= n_used)
    def _():
        y_ref[...] = jnp.zeros_like(y_ref)


def _moe_ffn(h, tile_expert, slot_token, n_used, slot_w, wg, wu, wd):
    p = slot_token.shape[0]
    tm = MOE_TM
    nt = p // tm
    gs = pltpu.PrefetchScalarGridSpec(
        num_scalar_prefetch=3,
        grid=(nt,),
        in_specs=[pl.BlockSpec(memory_space=pl.ANY),
                  pl.BlockSpec((tm, 1), lambda i, te, tok, nu: (i, 0)),
                  pl.BlockSpec((1, D_MODEL, MOE_HIDDEN), lambda i, te, tok, nu: (te[i], 0, 0)),
                  pl.BlockSpec((1, D_MODEL, MOE_HIDDEN), lambda i, te, tok, nu: (te[i], 0, 0)),
                  pl.BlockSpec((1, MOE_HIDDEN, D_MODEL), lambda i, te, tok, nu: (te[i], 0, 0))],
        out_specs=pl.BlockSpec((tm, D_MODEL), lambda i, te, tok, nu: (i, 0)),
        scratch_shapes=[pltpu.VMEM((2, tm, D_MODEL), F32),
                        pltpu.SemaphoreType.DMA((2,)),
                        pltpu.VMEM((D_MODEL, MOE_HIDDEN), BF16),
                        pltpu.VMEM((D_MODEL, MOE_HIDDEN), BF16),
                        pltpu.VMEM((MOE_HIDDEN, D_MODEL), BF16)],
    )
    return pl.pallas_call(
        _moe_ffn_kernel,
        grid_spec=gs,
        out_shape=jax.ShapeDtypeStruct((p, D_MODEL), F32),
        compiler_params=_cparams(("arbitrary",)),
        name="moe_ffn",
    )(tile_expert, slot_token, n_used, h, slot_w, wg, wu, wd)


def _moe_combine_kernel(pos_ref, y_hbm, x_ref, g_ref, o_ref, ybuf, sem):
    i = pl.program_id(0)
    nt = pl.num_programs(0)
    tm = TOK_TM
    slot = i % 2

    def row_copy(src, s, r):
        return pltpu.make_async_copy(y_hbm.at[pl.ds(src, 1)], ybuf.at[s, pl.ds(r, 1)], sem.at[s])

    def start_gather(tile, s):
        base = tile * tm

        def body(r, c):
            row_copy(pos_ref[2 * (base + r)], s, r).start()
            row_copy(pos_ref[2 * (base + r) + 1], s, tm + r).start()
            return c

        lax.fori_loop(0, tm, body, 0)

    @pl.when(i == 0)
    def _():
        start_gather(0, 0)

    @pl.when(i + 1 < nt)
    def _():
        start_gather(i + 1, 1 - slot)

    def wbody(r, c):
        row_copy(0, slot, r).wait()
        return c

    lax.fori_loop(0, 2 * tm, wbody, 0)
    y = ybuf[slot, pl.ds(0, tm), :] + ybuf[slot, pl.ds(tm, tm), :]
    o_ref[...] = x_ref[...] + g_ref[0] * y


def _moe_combine(y, pos, x, mod, k_gate):
    m = x.shape[0]
    tm = TOK_TM
    gs = pltpu.PrefetchScalarGridSpec(
        num_scalar_prefetch=1,
        grid=(m // tm,),
        in_specs=[pl.BlockSpec(memory_space=pl.ANY),
                  pl.BlockSpec((tm, D_MODEL), lambda i, pos: (i, 0)),
                  pl.BlockSpec((1, 1, D_MODEL), lambda i, pos: (_seg(i, tm) * 6 + k_gate, 0, 0))],
        out_specs=pl.BlockSpec((tm, D_MODEL), lambda i, pos: (i, 0)),
        scratch_shapes=[pltpu.VMEM((2, 2 * tm, D_MODEL), F32),
                        pltpu.SemaphoreType.DMA((2,))],
    )
    return pl.pallas_call(
        _moe_combine_kernel,
        grid_spec=gs,
        out_shape=jax.ShapeDtypeStruct((m, D_MODEL), F32),
        compiler_params=_cparams(("arbitrary",)),
        name="moe_combine",
    )(pos, y, x, mod)


def _route(logits, m):
    tm = MOE_TM
    gp = jax.nn.softmax(logits[:, :MOE_GROUPS], axis=-1)
    g_w, g_idx = lax.top_k(gp, 1)
    el = logits[:, MOE_GROUPS:MOE_GROUPS + MOE_EXPERTS].reshape(m, MOE_GROUPS, MOE_PER_GROUP)
    g_onehot = (g_idx == jnp.arange(MOE_GROUPS, dtype=g_idx.dtype)[None, :]).astype(F32)
    el_g = jnp.sum(el * g_onehot[:, :, None], axis=1)
    top_l, top_i = lax.top_k(el_g, MOE_TOPK)
    w_sel = jax.nn.softmax(top_l, axis=-1) * g_w
    eid = (g_idx * MOE_PER_GROUP + top_i).astype(jnp.int32)

    a = m * MOE_TOPK
    e_flat = eid.reshape(a)
    onehot = (e_flat[:, None] == jnp.arange(MOE_EXPERTS, dtype=jnp.int32)[None, :]).astype(jnp.int32)
    csum = jnp.cumsum(onehot, axis=0)
    rank = jnp.sum(csum * onehot, axis=1) - 1
    counts = csum[-1]
    padded = ((counts + tm - 1) // tm) * tm
    ends = jnp.cumsum(padded)
    starts = ends - padded
    pos = (jnp.sum(onehot * starts[None, :], axis=1) + rank).astype(jnp.int32)
    p = a + MOE_EXPERTS * tm
    slot_token = jnp.zeros((p,), jnp.int32).at[pos].set(jnp.arange(a, dtype=jnp.int32) // MOE_TOPK)
    slot_w = jnp.zeros((p,), F32).at[pos].set(w_sel.reshape(a)).reshape(p, 1)
    n_used = (ends[-1] // tm).astype(jnp.int32).reshape(1)
    tile_start = jnp.arange(p // tm, dtype=jnp.int32) * tm
    last_e = jnp.max(jnp.where(counts > 0, jnp.arange(MOE_EXPERTS, dtype=jnp.int32), 0))
    tile_expert = jnp.minimum(
        jnp.sum((ends[None, :] <= tile_start[:, None]).astype(jnp.int32), axis=1), last_e)
    return tile_expert, slot_token, n_used, slot_w, pos


def _rope_angles(rot_dim):
    t = jnp.arange(SEQ)
    row = (t // GRID_W).astype(F32)
    col = (t % GRID_W).astype(F32)
    n_freq = rot_dim // 4
    inv = ROPE_THETA ** (-jnp.arange(n_freq, dtype=F32) / n_freq)
    ang = jnp.concatenate([row[:, None] * inv, col[:, None] * inv], axis=-1)
    return jnp.cos(ang), jnp.sin(ang)


def _rope_tables():
    def rows(lat, ctx_val):
        ctx = jnp.full((N_CTX, 128), 0.0, F32) + ctx_val
        return jnp.concatenate([lat, lat, ctx], axis=0)

    z32 = jnp.zeros((SEQ, 32), F32)
    z64 = jnp.zeros((SEQ, 64), F32)
    lane = np.arange(128)
    cm, sm = _rope_angles(MLA_ROPE)
    mla_c = rows(jnp.concatenate([cm, cm, z64], axis=1), jnp.asarray((lane < 64).astype(np.float32)))
    mla_s = rows(jnp.concatenate([-sm, sm, z64], axis=1), 0.0)
    cg, sg = _rope_angles(GQA_HD)
    gqa_c = rows(jnp.concatenate([cg, cg, cg, cg], axis=1), 1.0)
    gqa_s1 = rows(jnp.concatenate([-sg, z32, -sg, z32], axis=1), 0.0)
    gqa_s2 = rows(jnp.concatenate([z32, sg, z32, sg], axis=1), 0.0)
    return {'mla_c': mla_c, 'mla_s': mla_s, 'gqa_c': gqa_c, 'gqa_s1': gqa_s1, 'gqa_s2': gqa_s2}


def _pack_w_in(w):
    parts = jnp.split(w, IN_SPLITS, axis=1)
    pad = jnp.zeros((D_MODEL, _PACK_COLS - sum(IN_SIZES)), w.dtype)
    return jnp.concatenate([parts[i] for i in _PACK_ORDER] + [pad], axis=1).astype(BF16)


def _pad_w_o_gqa(w):
    w4 = w.reshape(GQA_KV_HEADS, _GQA_G, GQA_HD, D_MODEL)
    z = jnp.zeros_like(w4[0:1])
    halves = [jnp.concatenate([w4[hk:hk + 1] if hk == half else z for hk in range(GQA_KV_HEADS)], axis=0)
              for half in range(GQA_KV_HEADS)]
    return jnp.stack(halves, axis=2).reshape(_GQA_QW, D_MODEL).astype(BF16)


def _token_mixer(h, p, tabs, ctx_out):
    proj = _mm(h, _pack_w_in(p['w_in']), BF16, 512)
    mq, mk, mvt = _mla_prep(proj, p, tabs)
    oa = _mla_attn(mq, mk, mvt, tq=512, tk=512)
    gnq = (p['na_qn_g'] * (NA_HD ** -0.5)).reshape(1, NA_HD)
    gnk = p['na_kn_g'].reshape(1, NA_HD)
    ob = _na_attn(proj, _na_bias_table(p['na_rpb']), gnq, gnk)
    qd, kn = _gqa_prep(proj, p, tabs)
    oc = _gqa_attn(qd, kn, proj, p['gqa_sink'])
    ctx = _ctx_attn(p['gqa_sink'], mq, mk, mvt, proj, gnq, gnk, qd, kn) if ctx_out else None
    return _merge(proj, (oa, ob, oc), ctx, p['w_o_mla'].astype(BF16), p['w_o_na'].astype(BF16),
                  _pad_w_o_gqa(p['w_o_gqa']))


def _moe(x, mod, norm_g, p):
    m = x.shape[0]
    wr = jnp.concatenate([p['moe_w_group'], p['moe_w_expert'],
                          jnp.zeros((D_MODEL, ROUTE_COLS - MOE_GROUPS - MOE_EXPERTS), F32)], axis=1)
    br = jnp.concatenate([p['moe_b_group'], p['moe_b_expert'],
                          jnp.zeros((ROUTE_COLS - MOE_GROUPS - MOE_EXPERTS,), F32)]).reshape(1, ROUTE_COLS)
    h, logits = _norm_mod(x, norm_g, mod, 3, 4, route=(wr, br))
    tile_expert, slot_token, n_used, slot_w, pos = _route(logits, m)
    y = _moe_ffn(h, tile_expert, slot_token, n_used, slot_w,
                 p['moe_w_gate'], p['moe_w_up'], p['moe_w_down'])
    return _moe_combine(y, pos, x, mod, 5)


def kernel(x, c, ctx, c_ctx, ada_w, ada_b, norm_mix_g, norm_ffn_g, w_in,
           mla_q_norm_g, mla_w_uq, mla_kv_norm_g, mla_w_ukv, mla_qn_g, mla_kn_g,
           na_qn_g, na_kn_g, na_rpb, gqa_qn_g, gqa_kn_g, gqa_sink,
           w_o_mla, w_o_na, w_o_gqa, w_out,
           moe_w_group, moe_b_group, moe_w_expert, moe_b_expert,
           moe_w_gate, moe_w_up, moe_w_down):
    xt = jnp.concatenate([x.reshape(N_LAT, D_MODEL), ctx.reshape(N_CTX, D_MODEL)], axis=0)
    c_rows = jnp.concatenate([c, c_ctx[None, :], jnp.zeros((8 - BATCH - 1, D_MODEL), F32)], axis=0)
    mod_all = _ada(c_rows, ada_w, ada_b)
    tabs = _rope_tables()
    for l in range(DEPTH):
        ctx_out = l < DEPTH - 1
        p = {
            'w_in': w_in[l], 'mla_q_norm_g': mla_q_norm_g[l], 'mla_w_uq': mla_w_uq[l],
            'mla_kv_norm_g': mla_kv_norm_g[l], 'mla_w_ukv': mla_w_ukv[l],
            'mla_qn_g': mla_qn_g[l], 'mla_kn_g': mla_kn_g[l],
            'na_qn_g': na_qn_g[l], 'na_kn_g': na_kn_g[l], 'na_rpb': na_rpb[l],
            'gqa_qn_g': gqa_qn_g[l], 'gqa_kn_g': gqa_kn_g[l], 'gqa_sink': gqa_sink[l],
            'w_o_mla': w_o_mla[l], 'w_o_na': w_o_na[l], 'w_o_gqa': w_o_gqa[l],
            'moe_w_group': moe_w_group[l], 'moe_b_group': moe_b_group[l],
            'moe_w_expert': moe_w_expert[l], 'moe_b_expert': moe_b_expert[l],
            'moe_w_gate': moe_w_gate[l], 'moe_w_up': moe_w_up[l], 'moe_w_down': moe_w_down[l],
        }
        mod = mod_all[l].reshape(8 * 6, 1, D_MODEL)
        h = _norm_mod(xt, norm_mix_g[l], mod, 0, 1)
        y = _token_mixer(h, p, tabs, ctx_out)
        m = y.shape[0]
        x_mid = _mm_res(y, w_out[l], xt[:m], mod, 2, 512)
        xt = _moe(x_mid, mod, norm_ffn_g[l], p)
    return xt[:N_LAT].reshape(BATCH, SEQ, D_MODEL)
```

```python
import functools

import numpy as np
import jax
import jax.numpy as jnp
from jax import lax
from jax.experimental import pallas as pl
from jax.experimental.pallas import tpu as pltpu

D_MODEL = 2048
BATCH = 2
SEQ = 4096
DEPTH = 2
GRID_W = 64
CTX_LEN = 256
EPS = 1e-6
ROPE_THETA = 10000.0
NEG_INF = -1e30

MLA_HEADS = 8
MLA_Q_RANK = 512
MLA_KV_RANK = 512
MLA_NOPE = 128
MLA_ROPE = 64
MLA_QK = MLA_NOPE + MLA_ROPE
MLA_V = 128
NA_HEADS = 4
NA_HD = 128
NA_ROWS = 8
NA_COLS = 16
GQA_HEADS = 8
GQA_KV_HEADS = 2
GQA_HD = 64
GQA_WINDOW = 128
MOE_GROUPS = 4
MOE_PER_GROUP = 8
MOE_EXPERTS = MOE_GROUPS * MOE_PER_GROUP
MOE_TOPK = 2
MOE_HIDDEN = 512

IN_SIZES = (MLA_Q_RANK, MLA_KV_RANK, MLA_ROPE,
            NA_HEADS * NA_HD, NA_HEADS * NA_HD, NA_HEADS * NA_HD,
            GQA_HEADS * GQA_HD, GQA_KV_HEADS * GQA_HD, GQA_KV_HEADS * GQA_HD,
            D_MODEL, D_MODEL, D_MODEL)
IN_SPLITS = tuple(int(s) for s in np.cumsum(IN_SIZES)[:-1])

N_LAT = BATCH * SEQ
N_CTX = BATCH * CTX_LEN
N_TOK = N_LAT + N_CTX
ROWS = SEQ // GRID_W

V7X_LANES = 128
V7X_VMEM_LIMIT = 56 * 1024 * 1024

_PACK_ORDER = (9, 10, 11, 0, 1, 3, 4, 5, 6, 7, 8, 2)
_PACK_COLS = 9728
_PACK_OFF = {}
_off = 0
for _i in _PACK_ORDER:
    _PACK_OFF[_i] = _off
    _off += IN_SIZES[_i]

ROUTE_COLS = V7X_LANES
MOE_TM = 256
TOK_TM = 256
MM_TM = 512
CTX_BLK = N_LAT // CTX_LEN

F32 = jnp.float32
BF16 = jnp.bfloat16
_LOG2E = 1.4426950408889634
_NT = (((1,), (1,)), ((), ()))


def _cparams(sem):
    return pltpu.CompilerParams(dimension_semantics=sem, vmem_limit_bytes=V7X_VMEM_LIMIT)


def _seg(i, tm):
    return jnp.minimum(i // (SEQ // tm), 2)


def _pcol(idx, width):
    assert _PACK_OFF[idx] % width == 0
    return _PACK_OFF[idx] // width


def _ada_kernel(c_ref, w_ref, b_ref, o_ref):
    c = c_ref[...]
    a = c * jax.nn.sigmoid(c)
    o_ref[0] = jnp.dot(a, w_ref[0], preferred_element_type=F32,
                       precision=lax.Precision.HIGHEST) + b_ref[0]


def _ada(c_rows, ada_w, ada_b):
    tn = 1024
    n = 6 * D_MODEL
    return pl.pallas_call(
        _ada_kernel,
        grid=(DEPTH, n // tn),
        in_specs=[
            pl.BlockSpec((8, D_MODEL), lambda l, j: (0, 0)),
            pl.BlockSpec((1, D_MODEL, tn), lambda l, j: (l, 0, j)),
            pl.BlockSpec((1, 1, tn), lambda l, j: (l, 0, j)),
        ],
        out_specs=pl.BlockSpec((1, 8, tn), lambda l, j: (l, 0, j)),
        out_shape=jax.ShapeDtypeStruct((DEPTH, 8, n), F32),
        compiler_params=_cparams(("arbitrary", "arbitrary")),
        name="ada",
    )(c_rows, ada_w, ada_b.reshape(DEPTH, 1, n))


def _norm_mod_kernel(x_ref, g_ref, sh_ref, sc_ref, h_ref):
    x = x_ref[...]
    xn = x * lax.rsqrt(jnp.mean(x * x, axis=-1, keepdims=True) + EPS) * g_ref[...]
    h_ref[...] = (xn * (1.0 + sc_ref[0]) + sh_ref[0]).astype(h_ref.dtype)


def _norm_mod_route_kernel(x_ref, g_ref, sh_ref, sc_ref, wr_ref, br_ref, h_ref, lg_ref):
    x = x_ref[...]
    xn = x * lax.rsqrt(jnp.mean(x * x, axis=-1, keepdims=True) + EPS) * g_ref[...]
    h = xn * (1.0 + sc_ref[0]) + sh_ref[0]
    h_ref[...] = h
    lg_ref[...] = jnp.dot(h, wr_ref[...], preferred_element_type=F32,
                          precision=lax.Precision.HIGHEST) + br_ref[...]


def _norm_mod(x, g, mod, k_shift, k_scale, route=None):
    m = x.shape[0]
    tm = TOK_TM
    base = [
        pl.BlockSpec((tm, D_MODEL), lambda i: (i, 0)),
        pl.BlockSpec((1, D_MODEL), lambda i: (0, 0)),
        pl.BlockSpec((1, 1, D_MODEL), lambda i: (_seg(i, tm) * 6 + k_shift, 0, 0)),
        pl.BlockSpec((1, 1, D_MODEL), lambda i: (_seg(i, tm) * 6 + k_scale, 0, 0)),
    ]
    if route is None:
        return pl.pallas_call(
            _norm_mod_kernel,
            grid=(m // tm,),
            in_specs=base,
            out_specs=pl.BlockSpec((tm, D_MODEL), lambda i: (i, 0)),
            out_shape=jax.ShapeDtypeStruct((m, D_MODEL), BF16),
            compiler_params=_cparams(("arbitrary",)),
            name="norm_mod",
        )(x, g.reshape(1, D_MODEL), mod, mod)
    wr, br = route
    return pl.pallas_call(
        _norm_mod_route_kernel,
        grid=(m // tm,),
        in_specs=base + [
            pl.BlockSpec((D_MODEL, ROUTE_COLS), lambda i: (0, 0)),
            pl.BlockSpec((1, ROUTE_COLS), lambda i: (0, 0)),
        ],
        out_specs=[pl.BlockSpec((tm, D_MODEL), lambda i: (i, 0)),
                   pl.BlockSpec((tm, ROUTE_COLS), lambda i: (i, 0))],
        out_shape=[jax.ShapeDtypeStruct((m, D_MODEL), F32),
                   jax.ShapeDtypeStruct((m, ROUTE_COLS), F32)],
        compiler_params=_cparams(("arbitrary",)),
        name="norm_mod_route",
    )(x, g.reshape(1, D_MODEL), mod, mod, wr, br)


def _mm_kernel(x_ref, w_ref, o_ref, wb_ref):
    @pl.when(pl.program_id(1) == 0)
    def _():
        wb_ref[...] = w_ref[...].astype(BF16)

    o_ref[...] = jnp.dot(x_ref[...], wb_ref[...], preferred_element_type=F32).astype(o_ref.dtype)


def _mm_res_kernel(x_ref, w_ref, r_ref, g_ref, o_ref, wb_ref):
    @pl.when(pl.program_id(1) == 0)
    def _():
        wb_ref[...] = w_ref[0].astype(BF16)

    acc = jnp.dot(x_ref[...], wb_ref[...], preferred_element_type=F32)
    o_ref[...] = r_ref[...] + g_ref[0] * acc


def _mm(x, w, out_dtype, tn):
    m, k = x.shape
    n = w.shape[1]
    tm = MM_TM
    return pl.pallas_call(
        _mm_kernel,
        grid=(n // tn, m // tm),
        in_specs=[pl.BlockSpec((tm, k), lambda j, i: (i, 0)),
                  pl.BlockSpec((k, tn), lambda j, i: (0, j))],
        out_specs=pl.BlockSpec((tm, tn), lambda j, i: (i, j)),
        out_shape=jax.ShapeDtypeStruct((m, n), out_dtype),
        scratch_shapes=[pltpu.VMEM((k, tn), BF16)],
        compiler_params=_cparams(("arbitrary", "arbitrary")),
        name="mm",
    )(x, w)


def _mm_res(x, w, layer, res, mod, k_gate, tn):
    m, k = x.shape
    n = w.shape[2]
    tm = MM_TM
    nj = n // tn
    return pl.pallas_call(
        _mm_res_kernel,
        grid=(nj, m // tm),
        in_specs=[pl.BlockSpec((tm, k), lambda j, i: (i, 0)),
                  pl.BlockSpec((1, k, tn), lambda j, i: (layer, 0, j)),
                  pl.BlockSpec((tm, tn), lambda j, i: (i, j)),
                  pl.BlockSpec((1, 1, tn), lambda j, i: (_seg(i, tm) * 6 + k_gate, 0, j))],
        out_specs=pl.BlockSpec((tm, tn), lambda j, i: (i, j)),
        out_shape=jax.ShapeDtypeStruct((m, n), F32),
        scratch_shapes=[pltpu.VMEM((k, tn), BF16)],
        compiler_params=_cparams(("arbitrary", "arbitrary")),
        name="mm_res",
    )(x, w, res, mod)


def _row_rms(x, g):
    return x * lax.rsqrt(jnp.mean(x * x, axis=-1, keepdims=True) + EPS) * g


_MLA_PAD = 2 * V7X_LANES


def _mla_prep_kernel(cq_ref, ckv_ref, kr_ref, wq_ref, wkv_ref, gqi_ref, gkvi_ref,
                     gq_ref, gkn_ref, gkr_ref, c_ref, s_ref, q_ref, k_ref, vt_ref):
    c = c_ref[...]
    s = s_ref[...]
    hw = MLA_HEADS * MLA_NOPE

    def rot(t):
        return t * c + (pltpu.roll(t, 32, 1) + pltpu.roll(t, 96, 1)) * s

    cqn = _row_rms(cq_ref[...].astype(F32), gqi_ref[...]).astype(BF16)
    qf = jnp.dot(cqn, wq_ref[...], preferred_element_type=F32)
    gq = gq_ref[...]
    inv = 1.0 / MLA_QK
    for h in range(MLA_HEADS):
        nope = qf[:, h * 128:(h + 1) * 128]
        t = qf[:, hw + h * 128:hw + (h + 1) * 128]
        ss = jnp.sum(nope * nope, axis=-1, keepdims=True) + jnp.sum(t * t, axis=-1, keepdims=True)
        r = lax.rsqrt(ss * inv + EPS)
        q_ref[h, :, 0:128] = (nope * r * gq[:, 0:128]).astype(BF16)
        q_ref[h, :, 128:256] = rot(t * r * gq[:, 128:256]).astype(BF16)

    ckvn = _row_rms(ckv_ref[...].astype(F32), gkvi_ref[...]).astype(BF16)
    kvf = jnp.dot(ckvn, wkv_ref[...], preferred_element_type=F32)
    kr = kr_ref[...].astype(F32)
    ssr = jnp.sum(kr * kr, axis=-1, keepdims=True)
    yrot = rot(kr * gkr_ref[...])
    gkn = gkn_ref[...]
    for h in range(MLA_HEADS):
        nope = kvf[:, h * 128:(h + 1) * 128]
        r = lax.rsqrt((jnp.sum(nope * nope, axis=-1, keepdims=True) + ssr) * inv + EPS)
        k_ref[h, :, 0:128] = (nope * r * gkn).astype(BF16)
        k_ref[h, :, 128:256] = (yrot * r).astype(BF16)
        vt_ref[h] = kvf[:, hw + h * 128:hw + (h + 1) * 128].T.astype(BF16)


def _mla_prep(proj, p, tabs):
    tm = TOK_TM
    hw = MLA_HEADS * MLA_NOPE
    wq = p['mla_w_uq'].reshape(MLA_Q_RANK, MLA_HEADS, MLA_QK)
    wq_rope = jnp.pad(wq[:, :, MLA_NOPE:], ((0, 0), (0, 0), (0, 128 - MLA_ROPE)))
    wq = jnp.concatenate([wq[:, :, :MLA_NOPE].reshape(MLA_Q_RANK, hw),
                          wq_rope.reshape(MLA_Q_RANK, hw)], axis=1).astype(BF16)
    wkv = p['mla_w_ukv'].reshape(MLA_KV_RANK, MLA_HEADS, MLA_NOPE + MLA_V)
    wkv = jnp.concatenate([wkv[:, :, :MLA_NOPE].reshape(MLA_KV_RANK, hw),
                           wkv[:, :, MLA_NOPE:].reshape(MLA_KV_RANK, hw)], axis=1).astype(BF16)
    zpad = jnp.zeros((128 - MLA_ROPE,), F32)
    gq = (jnp.concatenate([p['mla_qn_g'], zpad]) * (MLA_QK ** -0.5 * _LOG2E)).reshape(1, _MLA_PAD)
    gkn = p['mla_kn_g'][:MLA_NOPE].reshape(1, 128)
    gkr = jnp.concatenate([p['mla_kn_g'][MLA_NOPE:], zpad]).reshape(1, 128)

    def const(shape):
        return pl.BlockSpec(shape, lambda i: (0,) * len(shape))

    return pl.pallas_call(
        _mla_prep_kernel,
        grid=(N_TOK // tm,),
        in_specs=[pl.BlockSpec((tm, MLA_Q_RANK), lambda i: (i, _pcol(0, MLA_Q_RANK))),
                  pl.BlockSpec((tm, MLA_KV_RANK), lambda i: (i, _pcol(1, MLA_KV_RANK))),
                  pl.BlockSpec((tm, 128), lambda i: (i, _pcol(2, 128))),
                  const((MLA_Q_RANK, 2 * hw)), const((MLA_KV_RANK, 2 * hw)),
                  const((1, MLA_Q_RANK)), const((1, MLA_KV_RANK)),
                  const((1, _MLA_PAD)), const((1, 128)), const((1, 128)),
                  pl.BlockSpec((tm, 128), lambda i: (i, 0)),
                  pl.BlockSpec((tm, 128), lambda i: (i, 0))],
        out_specs=[pl.BlockSpec((MLA_HEADS, tm, _MLA_PAD), lambda i: (0, i, 0)),
                   pl.BlockSpec((MLA_HEADS, tm, _MLA_PAD), lambda i: (0, i, 0)),
                   pl.BlockSpec((MLA_HEADS, MLA_V, tm), lambda i: (0, 0, i))],
        out_shape=[jax.ShapeDtypeStruct((MLA_HEADS, N_TOK, _MLA_PAD), BF16),
                   jax.ShapeDtypeStruct((MLA_HEADS, N_TOK, _MLA_PAD), BF16),
                   jax.ShapeDtypeStruct((MLA_HEADS, MLA_V, N_TOK), BF16)],
        compiler_params=_cparams(("arbitrary",)),
        name="mla_prep",
    )(proj, proj, proj, wq, wkv, p['mla_q_norm_g'].reshape(1, -1), p['mla_kv_norm_g'].reshape(1, -1),
      gq, gkn, gkr, tabs['mla_c'], tabs['mla_s'])


def _mla_kernel(q_ref, k1_ref, vt1_ref, k2_ref, vt2_ref, o_ref, sa_ref, sb_ref, sc_ref, acc_ref,
                *, tk, n_chunks):
    q = q_ref[0]
    tq = q.shape[0]

    def scores(dst_ref, kc):
        dst_ref[...] = lax.dot_general(kc, q, _NT, preferred_element_type=F32)

    def k_chunk(c):
        return k1_ref[0, pl.ds(pl.multiple_of(c * tk, tk), tk), :]

    def vt_chunk(c):
        return vt1_ref[0, :, pl.ds(pl.multiple_of(c * tk, tk), tk)]

    def accumulate(s_ref, vtc, m, l):
        st = s_ref[...]
        m_new = jnp.maximum(m, jnp.max(st, axis=0, keepdims=True))
        a = jnp.exp2(m - m_new)
        p = jnp.exp2(st - m_new)
        l = a * l + jnp.sum(p, axis=0, keepdims=True)
        acc_ref[...] = a * acc_ref[...] + jnp.dot(vtc, p.astype(BF16), preferred_element_type=F32)
        return m_new, l

    m = jnp.full((1, tq), NEG_INF, F32)
    l = jnp.zeros((1, tq), F32)
    acc_ref[...] = jnp.zeros_like(acc_ref)
    scores(sc_ref, k2_ref[0])
    scores(sa_ref, k_chunk(0))
    m, l = accumulate(sc_ref, vt2_ref[0], m, l)

    def body(i, carry):
        m, l = carry
        scores(sb_ref, k_chunk(2 * i + 1))
        m, l = accumulate(sa_ref, vt_chunk(2 * i), m, l)
        scores(sa_ref, k_chunk(2 * i + 2))
        m, l = accumulate(sb_ref, vt_chunk(2 * i + 1), m, l)
        return m, l

    m, l = lax.fori_loop(0, n_chunks // 2 - 1, body, (m, l))
    scores(sb_ref, k_chunk(n_chunks - 1))
    m, l = accumulate(sa_ref, vt_chunk(n_chunks - 2), m, l)
    m, l = accumulate(sb_ref, vt_chunk(n_chunks - 1), m, l)
    o_ref[...] = (acc_ref[...] / l).T.astype(o_ref.dtype)


def _mla_attn(q, k, vt, *, tq, tk):
    nq = SEQ // tq
    return pl.pallas_call(
        functools.partial(_mla_kernel, tk=tk, n_chunks=SEQ // tk),
        grid=(BATCH, MLA_HEADS, nq),
        in_specs=[pl.BlockSpec((1, tq, _MLA_PAD), lambda b, h, i: (h, b * nq + i, 0)),
                  pl.BlockSpec((1, SEQ, _MLA_PAD), lambda b, h, i: (h, b, 0)),
                  pl.BlockSpec((1, MLA_V, SEQ), lambda b, h, i: (h, 0, b)),
                  pl.BlockSpec((1, CTX_LEN, _MLA_PAD), lambda b, h, i: (h, CTX_BLK + b, 0)),
                  pl.BlockSpec((1, MLA_V, CTX_LEN), lambda b, h, i: (h, 0, CTX_BLK + b))],
        out_specs=pl.BlockSpec((tq, MLA_V), lambda b, h, i: (b * nq + i, h)),
        out_shape=jax.ShapeDtypeStruct((N_LAT, MLA_HEADS * MLA_V), BF16),
        scratch_shapes=[pltpu.VMEM((tk, tq), F32), pltpu.VMEM((tk, tq), F32),
                        pltpu.VMEM((CTX_LEN, tq), F32), pltpu.VMEM((MLA_V, tq), F32)],
        compiler_params=_cparams(("arbitrary", "arbitrary", "arbitrary")),
        name="mla_attn",
    )(q, k, vt, k, vt)


_NA_KEYS = NA_ROWS * GRID_W
NA_RB = 8


def _na_pattern(r):
    half = NA_ROWS // 2
    return jnp.where(r < half, r, jnp.where(r <= ROWS - half, half, r - (ROWS - NA_ROWS)))


def _na_kernel(q_ref, k_ref, v_ref, kc_ref, vc_ref, bias_ref, gq_ref, gk_ref, o_ref, kn_ref, kcn_ref):
    rb = pl.program_id(2)

    @pl.when(rb == 0)
    def _():
        kn_ref[...] = _row_rms(k_ref[...].astype(F32), gk_ref[...]).astype(BF16)
        kcn_ref[...] = _row_rms(kc_ref[...].astype(F32), gk_ref[...]).astype(BF16)

    q_all = _row_rms(q_ref[...].astype(F32), gq_ref[...]).astype(BF16)
    vc = vc_ref[...]
    sc_all = lax.dot_general(q_all, kcn_ref[...], _NT, preferred_element_type=F32)
    ms, ls, os_ = [], [], []
    for j in range(NA_RB):
        r = rb * NA_RB + j
        start = pl.multiple_of(jnp.clip(r - NA_ROWS // 2, 0, ROWS - NA_ROWS) * GRID_W, GRID_W)
        q = q_all[j * GRID_W:(j + 1) * GRID_W]
        k = kn_ref[pl.ds(start, _NA_KEYS), :]
        v = v_ref[pl.ds(start, _NA_KEYS), :]
        s = lax.dot_general(q, k, _NT, preferred_element_type=F32) + bias_ref[0, _na_pattern(r)]
        sc = sc_all[j * GRID_W:(j + 1) * GRID_W]
        m = jnp.maximum(jnp.max(s, axis=-1, keepdims=True), jnp.max(sc, axis=-1, keepdims=True))
        p = jnp.exp(s - m)
        ms.append(m)
        ls.append(jnp.sum(p, axis=-1, keepdims=True))
        os_.append(jnp.dot(p.astype(BF16), v, preferred_element_type=F32))
    m_all = jnp.concatenate(ms, axis=0)
    pc = jnp.exp(sc_all - m_all)
    l_all = jnp.concatenate(ls, axis=0) + jnp.sum(pc, axis=-1, keepdims=True)
    o = jnp.concatenate(os_, axis=0) + jnp.dot(pc.astype(BF16), vc, preferred_element_type=F32)
    o_ref[...] = (o / l_all).astype(o_ref.dtype)


def _na_attn(proj, bias, gq, gk):
    qrows = NA_RB * GRID_W
    nrb = ROWS // NA_RB
    cq, ck, cv = _pcol(3, NA_HD), _pcol(4, NA_HD), _pcol(5, NA_HD)
    return pl.pallas_call(
        _na_kernel,
        grid=(BATCH, NA_HEADS, nrb),
        in_specs=[pl.BlockSpec((qrows, NA_HD), lambda b, h, r: (b * nrb + r, cq + h)),
                  pl.BlockSpec((SEQ, NA_HD), lambda b, h, r: (b, ck + h)),
                  pl.BlockSpec((SEQ, NA_HD), lambda b, h, r: (b, cv + h)),
                  pl.BlockSpec((CTX_LEN, NA_HD), lambda b, h, r: (CTX_BLK + b, ck + h)),
                  pl.BlockSpec((CTX_LEN, NA_HD), lambda b, h, r: (CTX_BLK + b, cv + h)),
                  pl.BlockSpec((1, NA_ROWS, GRID_W, _NA_KEYS), lambda b, h, r: (h, 0, 0, 0)),
                  pl.BlockSpec((1, NA_HD), lambda b, h, r: (0, 0)),
                  pl.BlockSpec((1, NA_HD), lambda b, h, r: (0, 0))],
        out_specs=pl.BlockSpec((qrows, NA_HD), lambda b, h, r: (b * nrb + r, h)),
        out_shape=jax.ShapeDtypeStruct((N_LAT, NA_HEADS * NA_HD), BF16),
        scratch_shapes=[pltpu.VMEM((SEQ, NA_HD), BF16), pltpu.VMEM((CTX_LEN, NA_HD), BF16)],
        compiler_params=_cparams(("arbitrary", "arbitrary", "arbitrary")),
        name="na_attn",
    )(proj, proj, proj, proj, proj, bias, gq, gk)


def _na_bias_table(rpb):
    half = NA_ROWS // 2
    r_rep = np.array(list(range(half)) + [half] + list(range(ROWS - half + 1, ROWS)))
    start = np.clip(r_rep - half, 0, ROWS - NA_ROWS)
    dr = start[:, None] + np.arange(NA_ROWS)[None, :] - r_rep[:, None] + NA_ROWS - 1
    qc = np.arange(GRID_W)
    kcol = np.arange(GRID_W)
    col_start = np.clip(qc - NA_COLS // 2, 0, GRID_W - NA_COLS)
    in_win = (kcol[None, :] >= col_start[:, None]) & (kcol[None, :] < col_start[:, None] + NA_COLS)
    dc = np.clip(kcol[None, :] - qc[:, None], 1 - NA_COLS, NA_COLS - 1) + NA_COLS - 1
    rsel = (dr[:, :, None] == np.arange(2 * NA_ROWS - 1)).astype(np.float32)
    csel = (dc[:, :, None] == np.arange(2 * NA_COLS - 1)).astype(np.float32)
    b = jnp.einsum('pja,hab,qkb->hpqjk', rsel, rpb.astype(F32), csel,
                   precision=lax.Precision.HIGHEST)
    b = jnp.where(in_win[None, None, :, None, :], b.astype(F32), NEG_INF)
    return b.reshape(NA_HEADS, NA_ROWS, GRID_W, _NA_KEYS)


_GQA_G = GQA_HEADS // GQA_KV_HEADS
_GQA_BAND = 3 * GQA_WINDOW
_GQA_QW = GQA_HEADS * V7X_LANES


def _gqa_prep_kernel(q_ref, k_ref, gq_ref, gk_ref, c_ref, s1_ref, s2_ref, qd_ref, kn_ref):
    c = c_ref[...]
    s1 = s1_ref[...]
    s2 = s2_ref[...]
    lo = lax.broadcasted_iota(jnp.int32, (1, 128), 1) < GQA_HD

    def head_rms(x, g):
        x2 = x * x
        s_lo = jnp.sum(jnp.where(lo, x2, 0.0), axis=-1, keepdims=True)
        s_hi = jnp.sum(jnp.where(lo, 0.0, x2), axis=-1, keepdims=True)
        inv = 1.0 / GQA_HD
        r = jnp.where(lo, lax.rsqrt(s_lo * inv + EPS), lax.rsqrt(s_hi * inv + EPS))
        return x * r * g

    def rot(x):
        return x * c + pltpu.roll(x, 96, 1) * s1 + pltpu.roll(x, 32, 1) * s2

    gq = gq_ref[...]
    for j in range(GQA_HEADS // 2):
        y = rot(head_rms(q_ref[:, j * 128:(j + 1) * 128].astype(F32), gq))
        sw = pltpu.roll(y, 64, 1)
        hk = (2 * j) // _GQA_G
        if hk == 0:
            even, odd = jnp.where(lo, y, 0.0), jnp.where(lo, sw, 0.0)
        else:
            even, odd = jnp.where(lo, 0.0, sw), jnp.where(lo, 0.0, y)
        qd_ref[:, (2 * j) * 128:(2 * j + 1) * 128] = even.astype(BF16)
        qd_ref[:, (2 * j + 1) * 128:(2 * j + 2) * 128] = odd.astype(BF16)
    kn_ref[...] = rot(head_rms(k_ref[...].astype(F32), gk_ref[...])).astype(BF16)


def _gqa_prep(proj, p, tabs):
    tm = TOK_TM
    gq = (jnp.tile(p['gqa_qn_g'], 2) * (GQA_HD ** -0.5)).reshape(1, 128)
    gk = jnp.tile(p['gqa_kn_g'], 2).reshape(1, 128)
    row = pl.BlockSpec((tm, 128), lambda i: (i, 0))
    vec = pl.BlockSpec((1, 128), lambda i: (0, 0))
    return pl.pallas_call(
        _gqa_prep_kernel,
        grid=(N_TOK // tm,),
        in_specs=[pl.BlockSpec((tm, GQA_HEADS * GQA_HD), lambda i: (i, _pcol(6, GQA_HEADS * GQA_HD))),
                  pl.BlockSpec((tm, 128), lambda i: (i, _pcol(7, 128))),
                  vec, vec, row, row, row],
        out_specs=[pl.BlockSpec((tm, _GQA_QW), lambda i: (i, 0)), row],
        out_shape=[jax.ShapeDtypeStruct((N_TOK, _GQA_QW), BF16),
                   jax.ShapeDtypeStruct((N_TOK, 128), BF16)],
        compiler_params=_cparams(("arbitrary",)),
        name="gqa_prep",
    )(proj, proj, gq, gk, tabs['gqa_c'], tabs['gqa_s1'], tabs['gqa_s2'])


def _gqa_kernel(sink_ref, q_ref, k_ref, v_ref, kc_ref, vc_ref, o_ref):
    hk = pl.program_id(1)
    n = pl.program_id(2)
    w = GQA_WINDOW
    start = pl.multiple_of(jnp.clip((n - 1) * w, 0, SEQ - _GQA_BAND), w)
    q = jnp.concatenate([q_ref[:, g * 128:(g + 1) * 128] for g in range(_GQA_G)], axis=0)
    k = k_ref[pl.ds(start, _GQA_BAND), :]
    v = v_ref[pl.ds(start, _GQA_BAND), :]
    s = lax.dot_general(q, k, _NT, preferred_element_type=F32)
    rows = lax.broadcasted_iota(jnp.int32, s.shape, 0)
    cols = lax.broadcasted_iota(jnp.int32, s.shape, 1)
    qpos = n * w + (rows & (w - 1))
    kpos = start + cols
    s = jnp.where(jnp.abs(kpos - qpos) <= GQA_WINDOW, s, NEG_INF)
    sc = lax.dot_general(q, kc_ref[...], _NT, preferred_element_type=F32)
    grow = lax.broadcasted_iota(jnp.int32, (_GQA_G * w, 1), 0) // w
    snk = jnp.full((_GQA_G * w, 1), sink_ref[hk * _GQA_G], F32)
    for g in range(1, _GQA_G):
        snk = jnp.where(grow == g, sink_ref[hk * _GQA_G + g], snk)
    m = jnp.maximum(jnp.maximum(jnp.max(s, axis=-1, keepdims=True),
                                jnp.max(sc, axis=-1, keepdims=True)), snk)
    p = jnp.exp(s - m)
    pc = jnp.exp(sc - m)
    l = jnp.sum(p, axis=-1, keepdims=True) + jnp.sum(pc, axis=-1, keepdims=True) + jnp.exp(snk - m)
    o = (jnp.dot(p.astype(BF16), v, preferred_element_type=F32)
         + jnp.dot(pc.astype(BF16), vc_ref[...], preferred_element_type=F32)) / l
    half = lax.broadcasted_iota(jnp.int32, (1, 128), 1) // GQA_HD
    o = jnp.where(half == hk, o, 0.0).astype(o_ref.dtype)
    for g in range(_GQA_G):
        o_ref[:, g * 128:(g + 1) * 128] = o[g * w:(g + 1) * w]


def _gqa_attn(qd, kn, proj, sink):
    nb = SEQ // GQA_WINDOW
    qw = _GQA_G * 128
    cv = _pcol(8, 128)
    gs = pltpu.PrefetchScalarGridSpec(
        num_scalar_prefetch=1,
        grid=(BATCH, GQA_KV_HEADS, nb),
        in_specs=[pl.BlockSpec((GQA_WINDOW, qw), lambda b, h, n, *_: (b * nb + n, h)),
                  pl.BlockSpec((SEQ, 128), lambda b, h, n, *_: (b, 0)),
                  pl.BlockSpec((SEQ, 128), lambda b, h, n, *_: (b, cv)),
                  pl.BlockSpec((CTX_LEN, 128), lambda b, h, n, *_: (CTX_BLK + b, 0)),
                  pl.BlockSpec((CTX_LEN, 128), lambda b, h, n, *_: (CTX_BLK + b, cv))],
        out_specs=pl.BlockSpec((GQA_WINDOW, qw), lambda b, h, n, *_: (b * nb + n, h)),
    )
    return pl.pallas_call(
        _gqa_kernel,
        grid_spec=gs,
        out_shape=jax.ShapeDtypeStruct((N_LAT, _GQA_QW), BF16),
        compiler_params=_cparams(("arbitrary", "arbitrary", "arbitrary")),
        name="gqa_attn",
    )(sink.astype(F32), qd, kn, proj, kn, proj)


def _ctx_kernel(sink_ref, mq_ref, mk_ref, mvt_ref, nq_ref, nk_ref, nv_ref, gnq_ref, gnk_ref,
                gq_ref, gk_ref, gv_ref, oa_ref, ob_ref, oc_ref):
    for h in range(MLA_HEADS):
        st = lax.dot_general(mk_ref[h], mq_ref[h], _NT, preferred_element_type=F32)
        p = jnp.exp2(st - jnp.max(st, axis=0, keepdims=True))
        l = jnp.sum(p, axis=0, keepdims=True)
        ot = jnp.dot(mvt_ref[h], p.astype(BF16), preferred_element_type=F32) / l
        oa_ref[:, h * MLA_V:(h + 1) * MLA_V] = ot.T.astype(oa_ref.dtype)
    for h in range(NA_HEADS):
        sl = slice(h * NA_HD, (h + 1) * NA_HD)
        q = _row_rms(nq_ref[:, sl].astype(F32), gnq_ref[...]).astype(BF16)
        k = _row_rms(nk_ref[:, sl].astype(F32), gnk_ref[...]).astype(BF16)
        s = lax.dot_general(q, k, _NT, preferred_element_type=F32)
        p = jnp.exp(s - jnp.max(s, axis=-1, keepdims=True))
        l = jnp.sum(p, axis=-1, keepdims=True)
        o = jnp.dot(p.astype(BF16), nv_ref[:, sl], preferred_element_type=F32) / l
        ob_ref[:, sl] = o.astype(ob_ref.dtype)
    half = lax.broadcasted_iota(jnp.int32, (1, 128), 1) // GQA_HD
    k = gk_ref[...]
    v = gv_ref[...]
    for h in range(GQA_HEADS):
        sl = slice(h * 128, (h + 1) * 128)
        s = lax.dot_general(gq_ref[:, sl], k, _NT, preferred_element_type=F32)
        snk = sink_ref[h]
        m = jnp.maximum(jnp.max(s, axis=-1, keepdims=True), snk)
        p = jnp.exp(s - m)
        l = jnp.sum(p, axis=-1, keepdims=True) + jnp.exp(snk - m)
        o = jnp.dot(p.astype(BF16), v, preferred_element_type=F32) / l
        oc_ref[:, sl] = jnp.where(half == h // _GQA_G, o, 0.0).astype(oc_ref.dtype)


def _ctx_attn(sink, mq, mk, mvt, proj, gnq, gnk, qd, kn):
    c = CTX_LEN
    nwid = NA_HEADS * NA_HD

    def row(width, col=0):
        return pl.BlockSpec((c, width), lambda b, *_: (CTX_BLK + b, col))

    def out(width):
        return pl.BlockSpec((c, width), lambda b, *_: (b, 0))

    vec = pl.BlockSpec((1, NA_HD), lambda b, *_: (0, 0))
    gs = pltpu.PrefetchScalarGridSpec(
        num_scalar_prefetch=1,
        grid=(BATCH,),
        in_specs=[pl.BlockSpec((MLA_HEADS, c, _MLA_PAD), lambda b, *_: (0, CTX_BLK + b, 0)),
                  pl.BlockSpec((MLA_HEADS, c, _MLA_PAD), lambda b, *_: (0, CTX_BLK + b, 0)),
                  pl.BlockSpec((MLA_HEADS, MLA_V, c), lambda b, *_: (0, 0, CTX_BLK + b)),
                  row(nwid, _pcol(3, nwid)), row(nwid, _pcol(4, nwid)), row(nwid, _pcol(5, nwid)),
                  vec, vec,
                  row(_GQA_QW), row(128), row(128, _pcol(8, 128))],
        out_specs=[out(MLA_HEADS * MLA_V), out(nwid), out(_GQA_QW)],
    )
    return pl.pallas_call(
        _ctx_kernel,
        grid_spec=gs,
        out_shape=[jax.ShapeDtypeStruct((N_CTX, MLA_HEADS * MLA_V), BF16),
                   jax.ShapeDtypeStruct((N_CTX, nwid), BF16),
                   jax.ShapeDtypeStruct((N_CTX, _GQA_QW), BF16)],
        compiler_params=_cparams(("arbitrary",)),
        name="ctx_attn",
    )(sink.astype(F32), mq, mk, mvt, proj, proj, proj, gnq, gnk, qd, kn, proj)


def _merge_kernel(*refs, has_ctx, n_lat_tiles):
    ga_ref, gb_ref, gc_ref, oa_ref, ob_ref, oc_ref = refs[:6]
    refs = refs[6:]
    if has_ctx:
        ca_ref, cb_ref, cc_ref = refs[:3]
        refs = refs[3:]
    wa_ref, wb_ref, wc_ref, y_ref = refs
    is_ctx = pl.program_id(0) >= n_lat_tiles

    def branch(g_ref, o_ref, c_ref, w_ref):
        o = o_ref[...]
        if has_ctx:
            o = jnp.where(is_ctx, c_ref[...], o)
        return jax.nn.sigmoid(g_ref[...].astype(F32)) * jnp.dot(o, w_ref[...], preferred_element_type=F32)

    y = (branch(ga_ref, oa_ref, ca_ref if has_ctx else None, wa_ref)
         + branch(gb_ref, ob_ref, cb_ref if has_ctx else None, wb_ref)
         + branch(gc_ref, oc_ref, cc_ref if has_ctx else None, wc_ref))
    y_ref[...] = y.astype(y_ref.dtype)


def _merge(proj, lat, ctx, wa, wb, wc):
    tm = TOK_TM
    has_ctx = ctx is not None
    m = N_TOK if has_ctx else N_LAT
    nl = N_LAT // tm
    ks = [o.shape[1] for o in lat]
    in_specs = [pl.BlockSpec((tm, D_MODEL), lambda i: (i, 0)),
                pl.BlockSpec((tm, D_MODEL), lambda i: (i, 1)),
                pl.BlockSpec((tm, D_MODEL), lambda i: (i, 2))]
    in_specs += [pl.BlockSpec((tm, k), lambda i: (jnp.minimum(i, nl - 1), 0)) for k in ks]
    args = [proj, proj, proj] + list(lat)
    if has_ctx:
        in_specs += [pl.BlockSpec((tm, k), lambda i: (jnp.maximum(i - nl, 0), 0)) for k in ks]
        args += list(ctx)
    in_specs += [pl.BlockSpec((k, D_MODEL), lambda i: (0, 0)) for k in ks]
    args += [wa, wb, wc]
    return pl.pallas_call(
        functools.partial(_merge_kernel, has_ctx=has_ctx, n_lat_tiles=nl),
        grid=(m // tm,),
        in_specs=in_specs,
        out_specs=pl.BlockSpec((tm, D_MODEL), lambda i: (i, 0)),
        out_shape=jax.ShapeDtypeStruct((m, D_MODEL), BF16),
        compiler_params=_cparams(("arbitrary",)),
        name="merge",
    )(*args)


def _moe_ffn_kernel(te_ref, tok_ref, nu_ref, h_hbm, wg_ref, wu_ref, wd_ref, y_ref,
                    xbuf, sem, wgb, wub, wdb):
    i = pl.program_id(0)
    tm = MOE_TM
    slot = i % 2
    n_used = nu_ref[0]

    def row_copy(tok, s, r):
        return pltpu.make_async_copy(h_hbm.at[pl.ds(tok, 1)], xbuf.at[s, pl.ds(r, 1)], sem.at[s])

    def start_gather(tile, s):
        base = tile * tm

        def body(r, c):
            row_copy(tok_ref[base + r], s, r).start()
            return c

        lax.fori_loop(0, tm, body, 0, unroll=8)

    @pl.when(i == 0)
    def _():
        start_gather(0, 0)

    @pl.when(i + 1 < n_used)
    def _():
        start_gather(i + 1, 1 - slot)

    @pl.when(i < n_used)
    def _():
        @pl.when((i == 0) | (te_ref[i] != te_ref[jnp.maximum(i - 1, 0)]))
        def _():
            wgb[...] = wg_ref[0, 0].astype(BF16)
            wub[...] = wu_ref[0, 0].astype(BF16)
            wdb[...] = wd_ref[0, 0].astype(BF16)

        pltpu.make_async_copy(h_hbm.at[pl.ds(0, tm)], xbuf.at[slot], sem.at[slot]).wait()
        x = xbuf[slot].astype(BF16)
        hg = jnp.dot(x, wgb[...], preferred_element_type=F32)
        hu = jnp.dot(x, wub[...], preferred_element_type=F32)
        act = (hg * jax.nn.sigmoid(hg)) * hu
        y_ref[...] = jnp.dot(act.astype(BF16), wdb[...], preferred_element_type=F32)

    @pl.when(i >= n_used)
    def _():
        y_ref[...] = jnp.zeros_like(y_ref)


def _moe_ffn(h, tile_expert, slot_token, n_used, wg, wu, wd, layer):
    p = slot_token.shape[0]
    tm = MOE_TM
    nt = p // tm
    gs = pltpu.PrefetchScalarGridSpec(
        num_scalar_prefetch=3,
        grid=(nt,),
        in_specs=[pl.BlockSpec(memory_space=pl.ANY),
                  pl.BlockSpec((1, 1, D_MODEL, MOE_HIDDEN), lambda i, te, tok, nu: (layer, te[i], 0, 0)),
                  pl.BlockSpec((1, 1, D_MODEL, MOE_HIDDEN), lambda i, te, tok, nu: (layer, te[i], 0, 0)),
                  pl.BlockSpec((1, 1, MOE_HIDDEN, D_MODEL), lambda i, te, tok, nu: (layer, te[i], 0, 0))],
        out_specs=pl.BlockSpec((tm, D_MODEL), lambda i, te, tok, nu: (i, 0)),
        scratch_shapes=[pltpu.VMEM((2, tm, D_MODEL), F32),
                        pltpu.SemaphoreType.DMA((2,)),
                        pltpu.VMEM((D_MODEL, MOE_HIDDEN), BF16),
                        pltpu.VMEM((D_MODEL, MOE_HIDDEN), BF16),
                        pltpu.VMEM((MOE_HIDDEN, D_MODEL), BF16)],
    )
    return pl.pallas_call(
        _moe_ffn_kernel,
        grid_spec=gs,
        out_shape=jax.ShapeDtypeStruct((p, D_MODEL), F32),
        compiler_params=_cparams(("arbitrary",)),
        name="moe_ffn",
    )(tile_expert, slot_token, n_used, h, wg, wu, wd)


def _moe_combine_kernel(pos_ref, y_hbm, x_ref, w_ref, g_ref, o_ref, ybuf, sem):
    i = pl.program_id(0)
    nt = pl.num_programs(0)
    tm = TOK_TM
    slot = i % 2

    def row_copy(src, s, r):
        return pltpu.make_async_copy(y_hbm.at[pl.ds(src, 1)], ybuf.at[s, pl.ds(r, 1)], sem.at[s])

    def start_gather(tile, s):
        base = tile * tm

        def body(r, c):
            row_copy(pos_ref[2 * (base + r)], s, r).start()
            row_copy(pos_ref[2 * (base + r) + 1], s, tm + r).start()
            return c

        lax.fori_loop(0, tm, body, 0, unroll=4)

    @pl.when(i == 0)
    def _():
        start_gather(0, 0)

    @pl.when(i + 1 < nt)
    def _():
        start_gather(i + 1, 1 - slot)

    pltpu.make_async_copy(y_hbm.at[pl.ds(0, 2 * tm)], ybuf.at[slot], sem.at[slot]).wait()
    w = w_ref[...]
    y = ybuf[slot, pl.ds(0, tm), :] * w[:, 0:1] + ybuf[slot, pl.ds(tm, tm), :] * w[:, 1:2]
    o_ref[...] = x_ref[...] + g_ref[0] * y


def _moe_combine(y, pos, w_sel, x, mod, k_gate):
    m = x.shape[0]
    tm = TOK_TM
    gs = pltpu.PrefetchScalarGridSpec(
        num_scalar_prefetch=1,
        grid=(m // tm,),
        in_specs=[pl.BlockSpec(memory_space=pl.ANY),
                  pl.BlockSpec((tm, D_MODEL), lambda i, pos: (i, 0)),
                  pl.BlockSpec((tm, MOE_TOPK), lambda i, pos: (i, 0)),
                  pl.BlockSpec((1, 1, D_MODEL), lambda i, pos: (_seg(i, tm) * 6 + k_gate, 0, 0))],
        out_specs=pl.BlockSpec((tm, D_MODEL), lambda i, pos: (i, 0)),
        scratch_shapes=[pltpu.VMEM((2, 2 * tm, D_MODEL), F32),
                        pltpu.SemaphoreType.DMA((2,))],
    )
    return pl.pallas_call(
        _moe_combine_kernel,
        grid_spec=gs,
        out_shape=jax.ShapeDtypeStruct((m, D_MODEL), F32),
        compiler_params=_cparams(("arbitrary",)),
        name="moe_combine",
    )(pos, y, x, w_sel, mod)


def _route(logits, m):
    tm = MOE_TM
    gp = jax.nn.softmax(logits[:, :MOE_GROUPS], axis=-1)
    g_idx = jnp.argmax(gp, axis=-1).astype(jnp.int32)[:, None]
    g_w = jnp.max(gp, axis=-1, keepdims=True)
    el = logits[:, MOE_GROUPS:MOE_GROUPS + MOE_EXPERTS].reshape(m, MOE_GROUPS, MOE_PER_GROUP)
    g_onehot = (g_idx == jnp.arange(MOE_GROUPS, dtype=jnp.int32)[None, :]).astype(F32)
    el_g = jnp.sum(el * g_onehot[:, :, None], axis=1)
    i0 = jnp.argmax(el_g, axis=-1).astype(jnp.int32)[:, None]
    l0 = jnp.max(el_g, axis=-1, keepdims=True)
    rest = jnp.where(jnp.arange(MOE_PER_GROUP, dtype=jnp.int32)[None, :] == i0, -jnp.inf, el_g)
    i1 = jnp.argmax(rest, axis=-1).astype(jnp.int32)[:, None]
    l1 = jnp.max(rest, axis=-1, keepdims=True)
    top_l = jnp.concatenate([l0, l1], axis=-1)
    top_i = jnp.concatenate([i0, i1], axis=-1)
    w_sel = jax.nn.softmax(top_l, axis=-1) * g_w
    eid = (g_idx * MOE_PER_GROUP + top_i).astype(jnp.int32)

    a = m * MOE_TOPK
    e_flat = eid.reshape(a)
    onehot = (e_flat[:, None] == jnp.arange(MOE_EXPERTS, dtype=jnp.int32)[None, :]).astype(jnp.int32)
    csum = jnp.cumsum(onehot, axis=0)
    rank = jnp.sum(csum * onehot, axis=1) - 1
    counts = csum[-1]
    padded = ((counts + tm - 1) // tm) * tm
    ends = jnp.cumsum(padded)
    starts = ends - padded
    pos = (jnp.sum(onehot * starts[None, :], axis=1) + rank).astype(jnp.int32)
    p = a + MOE_EXPERTS * tm
    slot_token = jnp.zeros((p,), jnp.int32).at[pos].set(jnp.arange(a, dtype=jnp.int32) // MOE_TOPK)
    n_used = (ends[-1] // tm).astype(jnp.int32).reshape(1)
    tile_start = jnp.arange(p // tm, dtype=jnp.int32) * tm
    last_e = jnp.max(jnp.where(counts > 0, jnp.arange(MOE_EXPERTS, dtype=jnp.int32), 0))
    tile_expert = jnp.minimum(
        jnp.sum((ends[None, :] <= tile_start[:, None]).astype(jnp.int32), axis=1), last_e)
    return tile_expert, slot_token, n_used, w_sel, pos


def _rope_angles(rot_dim):
    t = jnp.arange(SEQ)
    row = (t // GRID_W).astype(F32)
    col = (t % GRID_W).astype(F32)
    n_freq = rot_dim // 4
    inv = ROPE_THETA ** (-jnp.arange(n_freq, dtype=F32) / n_freq)
    ang = jnp.concatenate([row[:, None] * inv, col[:, None] * inv], axis=-1)
    return jnp.cos(ang), jnp.sin(ang)


def _rope_tables():
    def rows(lat, ctx_val):
        ctx = jnp.full((N_CTX, 128), 0.0, F32) + ctx_val
        return jnp.concatenate([lat, lat, ctx], axis=0)

    z32 = jnp.zeros((SEQ, 32), F32)
    z64 = jnp.zeros((SEQ, 64), F32)
    lane = np.arange(128)
    cm, sm = _rope_angles(MLA_ROPE)
    mla_c = rows(jnp.concatenate([cm, cm, z64], axis=1), jnp.asarray((lane < 64).astype(np.float32)))
    mla_s = rows(jnp.concatenate([-sm, sm, z64], axis=1), 0.0)
    cg, sg = _rope_angles(GQA_HD)
    gqa_c = rows(jnp.concatenate([cg, cg, cg, cg], axis=1), 1.0)
    gqa_s1 = rows(jnp.concatenate([-sg, z32, -sg, z32], axis=1), 0.0)
    gqa_s2 = rows(jnp.concatenate([z32, sg, z32, sg], axis=1), 0.0)
    return {'mla_c': mla_c, 'mla_s': mla_s, 'gqa_c': gqa_c, 'gqa_s1': gqa_s1, 'gqa_s2': gqa_s2}


def _pack_w_in(w):
    parts = jnp.split(w, IN_SPLITS, axis=1)
    pad = jnp.zeros((D_MODEL, _PACK_COLS - sum(IN_SIZES)), w.dtype)
    return jnp.concatenate([parts[i] for i in _PACK_ORDER] + [pad], axis=1).astype(BF16)


def _pad_w_o_gqa(w):
    w4 = w.reshape(GQA_KV_HEADS, _GQA_G, GQA_HD, D_MODEL)
    z = jnp.zeros_like(w4[0:1])
    halves = [jnp.concatenate([w4[hk:hk + 1] if hk == half else z for hk in range(GQA_KV_HEADS)], axis=0)
              for half in range(GQA_KV_HEADS)]
    return jnp.stack(halves, axis=2).reshape(_GQA_QW, D_MODEL).astype(BF16)


def _token_mixer(h, p, tabs, ctx_out):
    proj = _mm(h, _pack_w_in(p['w_in']), BF16, 512)
    mq, mk, mvt = _mla_prep(proj, p, tabs)
    oa = _mla_attn(mq, mk, mvt, tq=512, tk=512)
    gnq = (p['na_qn_g'] * (NA_HD ** -0.5)).reshape(1, NA_HD)
    gnk = p['na_kn_g'].reshape(1, NA_HD)
    ob = _na_attn(proj, _na_bias_table(p['na_rpb']), gnq, gnk)
    qd, kn = _gqa_prep(proj, p, tabs)
    oc = _gqa_attn(qd, kn, proj, p['gqa_sink'])
    ctx = _ctx_attn(p['gqa_sink'], mq, mk, mvt, proj, gnq, gnk, qd, kn) if ctx_out else None
    return _merge(proj, (oa, ob, oc), ctx, p['w_o_mla'].astype(BF16), p['w_o_na'].astype(BF16),
                  _pad_w_o_gqa(p['w_o_gqa']))


def _moe(x, mod, norm_g, p):
    m = x.shape[0]
    wr = jnp.concatenate([p['moe_w_group'], p['moe_w_expert'],
                          jnp.zeros((D_MODEL, ROUTE_COLS - MOE_GROUPS - MOE_EXPERTS), F32)], axis=1)
    br = jnp.concatenate([p['moe_b_group'], p['moe_b_expert'],
                          jnp.zeros((ROUTE_COLS - MOE_GROUPS - MOE_EXPERTS,), F32)]).reshape(1, ROUTE_COLS)
    h, logits = _norm_mod(x, norm_g, mod, 3, 4, route=(wr, br))
    tile_expert, slot_token, n_used, w_sel, pos = _route(logits, m)
    y = _moe_ffn(h, tile_expert, slot_token, n_used,
                 p['moe_w_gate'], p['moe_w_up'], p['moe_w_down'], p['layer'])
    return _moe_combine(y, pos, w_sel, x, mod, 5)


def kernel(x, c, ctx, c_ctx, ada_w, ada_b, norm_mix_g, norm_ffn_g, w_in,
           mla_q_norm_g, mla_w_uq, mla_kv_norm_g, mla_w_ukv, mla_qn_g, mla_kn_g,
           na_qn_g, na_kn_g, na_rpb, gqa_qn_g, gqa_kn_g, gqa_sink,
           w_o_mla, w_o_na, w_o_gqa, w_out,
           moe_w_group, moe_b_group, moe_w_expert, moe_b_expert,
           moe_w_gate, moe_w_up, moe_w_down):
    xt = jnp.concatenate([x.reshape(N_LAT, D_MODEL), ctx.reshape(N_CTX, D_MODEL)], axis=0)
    c_rows = jnp.concatenate([c, c_ctx[None, :], jnp.zeros((8 - BATCH - 1, D_MODEL), F32)], axis=0)
    mod_all = _ada(c_rows, ada_w, ada_b)
    tabs = _rope_tables()
    for l in range(DEPTH):
        ctx_out = l < DEPTH - 1
        p = {
            'w_in': w_in[l], 'mla_q_norm_g': mla_q_norm_g[l], 'mla_w_uq': mla_w_uq[l],
            'mla_kv_norm_g': mla_kv_norm_g[l], 'mla_w_ukv': mla_w_ukv[l],
            'mla_qn_g': mla_qn_g[l], 'mla_kn_g': mla_kn_g[l],
            'na_qn_g': na_qn_g[l], 'na_kn_g': na_kn_g[l], 'na_rpb': na_rpb[l],
            'gqa_qn_g': gqa_qn_g[l], 'gqa_kn_g': gqa_kn_g[l], 'gqa_sink': gqa_sink[l],
            'w_o_mla': w_o_mla[l], 'w_o_na': w_o_na[l], 'w_o_gqa': w_o_gqa[l],
            'moe_w_group': moe_w_group[l], 'moe_b_group': moe_b_group[l],
            'moe_w_expert': moe_w_expert[l], 'moe_b_expert': moe_b_expert[l],
            'moe_w_gate': moe_w_gate, 'moe_w_up': moe_w_up, 'moe_w_down': moe_w_down, 'layer': l,
        }
        mod = mod_all[l].reshape(8 * 6, 1, D_MODEL)
        h = _norm_mod(xt, norm_mix_g[l], mod, 0, 1)
        y = _token_mixer(h, p, tabs, ctx_out)
        m = y.shape[0]
        x_mid = _mm_res(y, w_out, l, xt, mod, 2, 512)
        xt = _moe(x_mid, mod, norm_ffn_g[l], p)
    return xt[:N_LAT].reshape(BATCH, SEQ, D_MODEL)
```

```python
import functools

import numpy as np
import jax
import jax.numpy as jnp
from jax import lax
from jax.experimental import pallas as pl
from jax.experimental.pallas import tpu as pltpu

D_MODEL = 2048
BATCH = 2
SEQ = 4096
DEPTH = 2
GRID_W = 64
CTX_LEN = 256
EPS = 1e-6
ROPE_THETA = 10000.0
NEG_INF = -1e30

MLA_HEADS = 8
MLA_Q_RANK = 512
MLA_KV_RANK = 512
MLA_NOPE = 128
MLA_ROPE = 64
MLA_QK = MLA_NOPE + MLA_ROPE
MLA_V = 128
NA_HEADS = 4
NA_HD = 128
NA_ROWS = 8
NA_COLS = 16
GQA_HEADS = 8
GQA_KV_HEADS = 2
GQA_HD = 64
GQA_WINDOW = 128
MOE_GROUPS = 4
MOE_PER_GROUP = 8
MOE_EXPERTS = MOE_GROUPS * MOE_PER_GROUP
MOE_TOPK = 2
MOE_HIDDEN = 512

IN_SIZES = (MLA_Q_RANK, MLA_KV_RANK, MLA_ROPE,
            NA_HEADS * NA_HD, NA_HEADS * NA_HD, NA_HEADS * NA_HD,
            GQA_HEADS * GQA_HD, GQA_KV_HEADS * GQA_HD, GQA_KV_HEADS * GQA_HD,
            D_MODEL, D_MODEL, D_MODEL)
IN_SPLITS = tuple(int(s) for s in np.cumsum(IN_SIZES)[:-1])

N_LAT = BATCH * SEQ
N_CTX = BATCH * CTX_LEN
N_TOK = N_LAT + N_CTX
ROWS = SEQ // GRID_W

V7X_LANES = 128
V7X_VMEM_LIMIT = 56 * 1024 * 1024

_PACK_ORDER = (9, 10, 11, 0, 1, 3, 4, 5, 6, 7, 8, 2)
_PACK_COLS = 9728
_PACK_OFF = {}
_off = 0
for _i in _PACK_ORDER:
    _PACK_OFF[_i] = _off
    _off += IN_SIZES[_i]

ROUTE_COLS = V7X_LANES
MOE_TM = 256
TOK_TM = 256
MM_TM = 512
CTX_BLK = N_LAT // CTX_LEN

F32 = jnp.float32
BF16 = jnp.bfloat16
_LOG2E = 1.4426950408889634
_NT = (((1,), (1,)), ((), ()))


def _cparams(sem):
    return pltpu.CompilerParams(dimension_semantics=sem, vmem_limit_bytes=V7X_VMEM_LIMIT)


def _seg(i, tm):
    return jnp.minimum(i // (SEQ // tm), 2)


def _pcol(idx, width):
    assert _PACK_OFF[idx] % width == 0
    return _PACK_OFF[idx] // width


def _ada_kernel(c_ref, w_ref, b_ref, o_ref):
    c = c_ref[...]
    a = c * jax.nn.sigmoid(c)
    o_ref[0] = jnp.dot(a, w_ref[0], preferred_element_type=F32,
                       precision=lax.Precision.HIGHEST) + b_ref[0]


def _ada(c_rows, ada_w, ada_b):
    tn = 1024
    n = 6 * D_MODEL
    return pl.pallas_call(
        _ada_kernel,
        grid=(DEPTH, n // tn),
        in_specs=[
            pl.BlockSpec((8, D_MODEL), lambda l, j: (0, 0)),
            pl.BlockSpec((1, D_MODEL, tn), lambda l, j: (l, 0, j)),
            pl.BlockSpec((1, 1, tn), lambda l, j: (l, 0, j)),
        ],
        out_specs=pl.BlockSpec((1, 8, tn), lambda l, j: (l, 0, j)),
        out_shape=jax.ShapeDtypeStruct((DEPTH, 8, n), F32),
        compiler_params=_cparams(("arbitrary", "arbitrary")),
        name="ada",
    )(c_rows, ada_w, ada_b.reshape(DEPTH, 1, n))


def _norm_mod_kernel(x_ref, g_ref, sh_ref, sc_ref, h_ref):
    x = x_ref[...]
    xn = x * lax.rsqrt(jnp.mean(x * x, axis=-1, keepdims=True) + EPS) * g_ref[...]
    h_ref[...] = (xn * (1.0 + sc_ref[0]) + sh_ref[0]).astype(h_ref.dtype)


def _norm_mod_route_kernel(x_ref, g_ref, sh_ref, sc_ref, wr_ref, br_ref, h_ref, lg_ref):
    x = x_ref[...]
    xn = x * lax.rsqrt(jnp.mean(x * x, axis=-1, keepdims=True) + EPS) * g_ref[...]
    h = xn * (1.0 + sc_ref[0]) + sh_ref[0]
    h_ref[...] = h
    lg_ref[...] = jnp.dot(h, wr_ref[...], preferred_element_type=F32,
                          precision=lax.Precision.HIGHEST) + br_ref[...]


def _norm_mod(x, g, mod, k_shift, k_scale, route=None):
    m = x.shape[0]
    tm = TOK_TM
    base = [
        pl.BlockSpec((tm, D_MODEL), lambda i: (i, 0)),
        pl.BlockSpec((1, D_MODEL), lambda i: (0, 0)),
        pl.BlockSpec((1, 1, D_MODEL), lambda i: (_seg(i, tm) * 6 + k_shift, 0, 0)),
        pl.BlockSpec((1, 1, D_MODEL), lambda i: (_seg(i, tm) * 6 + k_scale, 0, 0)),
    ]
    if route is None:
        return pl.pallas_call(
            _norm_mod_kernel,
            grid=(m // tm,),
            in_specs=base,
            out_specs=pl.BlockSpec((tm, D_MODEL), lambda i: (i, 0)),
            out_shape=jax.ShapeDtypeStruct((m, D_MODEL), BF16),
            compiler_params=_cparams(("arbitrary",)),
            name="norm_mod",
        )(x, g.reshape(1, D_MODEL), mod, mod)
    wr, br = route
    return pl.pallas_call(
        _norm_mod_route_kernel,
        grid=(m // tm,),
        in_specs=base + [
            pl.BlockSpec((D_MODEL, ROUTE_COLS), lambda i: (0, 0)),
            pl.BlockSpec((1, ROUTE_COLS), lambda i: (0, 0)),
        ],
        out_specs=[pl.BlockSpec((tm, D_MODEL), lambda i: (i, 0)),
                   pl.BlockSpec((tm, ROUTE_COLS), lambda i: (i, 0))],
        out_shape=[jax.ShapeDtypeStruct((m, D_MODEL), F32),
                   jax.ShapeDtypeStruct((m, ROUTE_COLS), F32)],
        compiler_params=_cparams(("arbitrary",)),
        name="norm_mod_route",
    )(x, g.reshape(1, D_MODEL), mod, mod, wr, br)


def _mm_kernel(x_ref, w_ref, o_ref, wb_ref):
    @pl.when(pl.program_id(1) == 0)
    def _():
        wb_ref[...] = w_ref[...].astype(BF16)

    o_ref[...] = jnp.dot(x_ref[...], wb_ref[...], preferred_element_type=F32).astype(o_ref.dtype)


def _mm_res_kernel(x_ref, w_ref, r_ref, g_ref, o_ref, wb_ref):
    @pl.when(pl.program_id(1) == 0)
    def _():
        wb_ref[...] = w_ref[0].astype(BF16)

    acc = jnp.dot(x_ref[...], wb_ref[...], preferred_element_type=F32)
    o_ref[...] = r_ref[...] + g_ref[0] * acc


def _mm_bf16_kernel(x_ref, w_ref, o_ref):
    o_ref[...] = jnp.dot(x_ref[...], w_ref[...], preferred_element_type=F32).astype(o_ref.dtype)


def _mm(x, w, out_dtype, tn):
    m, k = x.shape
    n = w.shape[1]
    tm = MM_TM
    if w.dtype == BF16:
        return pl.pallas_call(
            _mm_bf16_kernel,
            grid=(n // tn, m // tm),
            in_specs=[pl.BlockSpec((tm, k), lambda j, i: (i, 0)),
                      pl.BlockSpec((k, tn), lambda j, i: (0, j))],
            out_specs=pl.BlockSpec((tm, tn), lambda j, i: (i, j)),
            out_shape=jax.ShapeDtypeStruct((m, n), out_dtype),
            compiler_params=_cparams(("arbitrary", "arbitrary")),
            name="mm_bf16",
        )(x, w)
    return pl.pallas_call(
        _mm_kernel,
        grid=(n // tn, m // tm),
        in_specs=[pl.BlockSpec((tm, k), lambda j, i: (i, 0)),
                  pl.BlockSpec((k, tn), lambda j, i: (0, j))],
        out_specs=pl.BlockSpec((tm, tn), lambda j, i: (i, j)),
        out_shape=jax.ShapeDtypeStruct((m, n), out_dtype),
        scratch_shapes=[pltpu.VMEM((k, tn), BF16)],
        compiler_params=_cparams(("arbitrary", "arbitrary")),
        name="mm",
    )(x, w)


def _mm_res(x, w, layer, res, mod, k_gate, tn):
    m, k = x.shape
    n = w.shape[2]
    tm = MM_TM
    nj = n // tn
    return pl.pallas_call(
        _mm_res_kernel,
        grid=(nj, m // tm),
        in_specs=[pl.BlockSpec((tm, k), lambda j, i: (i, 0)),
                  pl.BlockSpec((1, k, tn), lambda j, i: (layer, 0, j)),
                  pl.BlockSpec((tm, tn), lambda j, i: (i, j)),
                  pl.BlockSpec((1, 1, tn), lambda j, i: (_seg(i, tm) * 6 + k_gate, 0, j))],
        out_specs=pl.BlockSpec((tm, tn), lambda j, i: (i, j)),
        out_shape=jax.ShapeDtypeStruct((m, n), F32),
        scratch_shapes=[pltpu.VMEM((k, tn), BF16)],
        compiler_params=_cparams(("arbitrary", "arbitrary")),
        name="mm_res",
    )(x, w, res, mod)


def _row_rms(x, g):
    return x * lax.rsqrt(jnp.mean(x * x, axis=-1, keepdims=True) + EPS) * g


_MLA_PAD = 2 * V7X_LANES


def _mla_prep_kernel(cq_ref, ckv_ref, kr_ref, wq_ref, wkv_ref, gqi_ref, gkvi_ref,
                     gq_ref, gkn_ref, gkr_ref, c_ref, s_ref, q_ref, k_ref, vt_ref):
    c = c_ref[...]
    s = s_ref[...]
    hw = MLA_HEADS * MLA_NOPE

    def rot(t):
        return t * c + (pltpu.roll(t, 32, 1) + pltpu.roll(t, 96, 1)) * s

    cqn = _row_rms(cq_ref[...].astype(F32), gqi_ref[...]).astype(BF16)
    qf = jnp.dot(cqn, wq_ref[...], preferred_element_type=F32)
    gq = gq_ref[...]
    inv = 1.0 / MLA_QK
    for h in range(MLA_HEADS):
        nope = qf[:, h * 128:(h + 1) * 128]
        t = qf[:, hw + h * 128:hw + (h + 1) * 128]
        ss = jnp.sum(nope * nope, axis=-1, keepdims=True) + jnp.sum(t * t, axis=-1, keepdims=True)
        r = lax.rsqrt(ss * inv + EPS)
        q_ref[h, :, 0:128] = (nope * r * gq[:, 0:128]).astype(BF16)
        q_ref[h, :, 128:256] = rot(t * r * gq[:, 128:256]).astype(BF16)

    ckvn = _row_rms(ckv_ref[...].astype(F32), gkvi_ref[...]).astype(BF16)
    kvf = jnp.dot(ckvn, wkv_ref[...], preferred_element_type=F32)
    kr = kr_ref[...].astype(F32)
    ssr = jnp.sum(kr * kr, axis=-1, keepdims=True)
    yrot = rot(kr * gkr_ref[...])
    gkn = gkn_ref[...]
    for h in range(MLA_HEADS):
        nope = kvf[:, h * 128:(h + 1) * 128]
        r = lax.rsqrt((jnp.sum(nope * nope, axis=-1, keepdims=True) + ssr) * inv + EPS)
        k_ref[h, :, 0:128] = (nope * r * gkn).astype(BF16)
        k_ref[h, :, 128:256] = (yrot * r).astype(BF16)
        vt_ref[h] = kvf[:, hw + h * 128:hw + (h + 1) * 128].T.astype(BF16)


def _mla_prep(proj, p, tabs):
    tm = TOK_TM
    hw = MLA_HEADS * MLA_NOPE
    wq = p['mla_w_uq'].reshape(MLA_Q_RANK, MLA_HEADS, MLA_QK)
    wq_rope = jnp.pad(wq[:, :, MLA_NOPE:], ((0, 0), (0, 0), (0, 128 - MLA_ROPE)))
    wq = jnp.concatenate([wq[:, :, :MLA_NOPE].reshape(MLA_Q_RANK, hw),
                          wq_rope.reshape(MLA_Q_RANK, hw)], axis=1).astype(BF16)
    wkv = p['mla_w_ukv'].reshape(MLA_KV_RANK, MLA_HEADS, MLA_NOPE + MLA_V)
    wkv = jnp.concatenate([wkv[:, :, :MLA_NOPE].reshape(MLA_KV_RANK, hw),
                           wkv[:, :, MLA_NOPE:].reshape(MLA_KV_RANK, hw)], axis=1).astype(BF16)
    zpad = jnp.zeros((128 - MLA_ROPE,), F32)
    gq = (jnp.concatenate([p['mla_qn_g'], zpad]) * (MLA_QK ** -0.5 * _LOG2E)).reshape(1, _MLA_PAD)
    gkn = p['mla_kn_g'][:MLA_NOPE].reshape(1, 128)
    gkr = jnp.concatenate([p['mla_kn_g'][MLA_NOPE:], zpad]).reshape(1, 128)

    def const(shape):
        return pl.BlockSpec(shape, lambda i: (0,) * len(shape))

    return pl.pallas_call(
        _mla_prep_kernel,
        grid=(N_TOK // tm,),
        in_specs=[pl.BlockSpec((tm, MLA_Q_RANK), lambda i: (i, _pcol(0, MLA_Q_RANK))),
                  pl.BlockSpec((tm, MLA_KV_RANK), lambda i: (i, _pcol(1, MLA_KV_RANK))),
                  pl.BlockSpec((tm, 128), lambda i: (i, _pcol(2, 128))),
                  const((MLA_Q_RANK, 2 * hw)), const((MLA_KV_RANK, 2 * hw)),
                  const((1, MLA_Q_RANK)), const((1, MLA_KV_RANK)),
                  const((1, _MLA_PAD)), const((1, 128)), const((1, 128)),
                  pl.BlockSpec((tm, 128), lambda i: (i, 0)),
                  pl.BlockSpec((tm, 128), lambda i: (i, 0))],
        out_specs=[pl.BlockSpec((MLA_HEADS, tm, _MLA_PAD), lambda i: (0, i, 0)),
                   pl.BlockSpec((MLA_HEADS, tm, _MLA_PAD), lambda i: (0, i, 0)),
                   pl.BlockSpec((MLA_HEADS, MLA_V, tm), lambda i: (0, 0, i))],
        out_shape=[jax.ShapeDtypeStruct((MLA_HEADS, N_TOK, _MLA_PAD), BF16),
                   jax.ShapeDtypeStruct((MLA_HEADS, N_TOK, _MLA_PAD), BF16),
                   jax.ShapeDtypeStruct((MLA_HEADS, MLA_V, N_TOK), BF16)],
        compiler_params=_cparams(("arbitrary",)),
        name="mla_prep",
    )(proj, proj, proj, wq, wkv, p['mla_q_norm_g'].reshape(1, -1), p['mla_kv_norm_g'].reshape(1, -1),
      gq, gkn, gkr, tabs['mla_c'], tabs['mla_s'])


def _mla_kernel(q_ref, k1_ref, vt1_ref, k2_ref, vt2_ref, o_ref, sa_ref, sb_ref, sc_ref, acc_ref,
                *, tk, n_chunks):
    q = q_ref[0]
    tq = q.shape[0]

    def scores(dst_ref, kc):
        dst_ref[...] = lax.dot_general(kc, q, _NT, preferred_element_type=F32)

    def k_chunk(c):
        return k1_ref[0, pl.ds(pl.multiple_of(c * tk, tk), tk), :]

    def vt_chunk(c):
        return vt1_ref[0, :, pl.ds(pl.multiple_of(c * tk, tk), tk)]

    def accumulate(s_ref, vtc, m, l):
        st = s_ref[...]
        m_new = jnp.maximum(m, jnp.max(st, axis=0, keepdims=True))
        a = jnp.exp2(m - m_new)
        p = jnp.exp2(st - m_new)
        l = a * l + jnp.sum(p, axis=0, keepdims=True)
        acc_ref[...] = a * acc_ref[...] + jnp.dot(vtc, p.astype(BF16), preferred_element_type=F32)
        return m_new, l

    m = jnp.full((1, tq), NEG_INF, F32)
    l = jnp.zeros((1, tq), F32)
    acc_ref[...] = jnp.zeros_like(acc_ref)
    scores(sc_ref, k2_ref[0])
    scores(sa_ref, k_chunk(0))
    m, l = accumulate(sc_ref, vt2_ref[0], m, l)

    for i in range(n_chunks // 2 - 1):
        scores(sb_ref, k_chunk(2 * i + 1))
        m, l = accumulate(sa_ref, vt_chunk(2 * i), m, l)
        scores(sa_ref, k_chunk(2 * i + 2))
        m, l = accumulate(sb_ref, vt_chunk(2 * i + 1), m, l)
    scores(sb_ref, k_chunk(n_chunks - 1))
    m, l = accumulate(sa_ref, vt_chunk(n_chunks - 2), m, l)
    m, l = accumulate(sb_ref, vt_chunk(n_chunks - 1), m, l)
    o_ref[...] = (acc_ref[...] / l).T.astype(o_ref.dtype)


def _mla_attn(q, k, vt, *, tq, tk):
    nq = SEQ // tq
    return pl.pallas_call(
        functools.partial(_mla_kernel, tk=tk, n_chunks=SEQ // tk),
        grid=(BATCH, MLA_HEADS, nq),
        in_specs=[pl.BlockSpec((1, tq, _MLA_PAD), lambda b, h, i: (h, b * nq + i, 0)),
                  pl.BlockSpec((1, SEQ, _MLA_PAD), lambda b, h, i: (h, b, 0)),
                  pl.BlockSpec((1, MLA_V, SEQ), lambda b, h, i: (h, 0, b)),
                  pl.BlockSpec((1, CTX_LEN, _MLA_PAD), lambda b, h, i: (h, CTX_BLK + b, 0)),
                  pl.BlockSpec((1, MLA_V, CTX_LEN), lambda b, h, i: (h, 0, CTX_BLK + b))],
        out_specs=pl.BlockSpec((tq, MLA_V), lambda b, h, i: (b * nq + i, h)),
        out_shape=jax.ShapeDtypeStruct((N_LAT, MLA_HEADS * MLA_V), BF16),
        scratch_shapes=[pltpu.VMEM((tk, tq), F32), pltpu.VMEM((tk, tq), F32),
                        pltpu.VMEM((CTX_LEN, tq), F32), pltpu.VMEM((MLA_V, tq), F32)],
        compiler_params=_cparams(("arbitrary", "arbitrary", "arbitrary")),
        name="mla_attn",
    )(q, k, vt, k, vt)


_NA_KEYS = NA_ROWS * GRID_W
NA_RB = 8


def _na_pattern(r):
    half = NA_ROWS // 2
    return jnp.where(r < half, r, jnp.where(r <= ROWS - half, half, r - (ROWS - NA_ROWS)))


def _na_kernel(q_ref, k_ref, v_ref, kc_ref, vc_ref, bias_ref, gq_ref, gk_ref, o_ref, kn_ref, kcn_ref):
    rb = pl.program_id(2)

    @pl.when(rb == 0)
    def _():
        kn_ref[...] = _row_rms(k_ref[...].astype(F32), gk_ref[...]).astype(BF16)
        kcn_ref[...] = _row_rms(kc_ref[...].astype(F32), gk_ref[...]).astype(BF16)

    q_all = _row_rms(q_ref[...].astype(F32), gq_ref[...]).astype(BF16)
    vc = vc_ref[...]
    sc_all = lax.dot_general(q_all, kcn_ref[...], _NT, preferred_element_type=F32)
    ms, ls, os_ = [], [], []
    for j in range(NA_RB):
        r = rb * NA_RB + j
        start = pl.multiple_of(jnp.clip(r - NA_ROWS // 2, 0, ROWS - NA_ROWS) * GRID_W, GRID_W)
        q = q_all[j * GRID_W:(j + 1) * GRID_W]
        k = kn_ref[pl.ds(start, _NA_KEYS), :]
        v = v_ref[pl.ds(start, _NA_KEYS), :]
        s = lax.dot_general(q, k, _NT, preferred_element_type=F32) + bias_ref[0, _na_pattern(r)]
        sc = sc_all[j * GRID_W:(j + 1) * GRID_W]
        m = jnp.maximum(jnp.max(s, axis=-1, keepdims=True), jnp.max(sc, axis=-1, keepdims=True))
        p = jnp.exp(s - m)
        ms.append(m)
        ls.append(jnp.sum(p, axis=-1, keepdims=True))
        os_.append(jnp.dot(p.astype(BF16), v, preferred_element_type=F32))
    m_all = jnp.concatenate(ms, axis=0)
    pc = jnp.exp(sc_all - m_all)
    l_all = jnp.concatenate(ls, axis=0) + jnp.sum(pc, axis=-1, keepdims=True)
    o = jnp.concatenate(os_, axis=0) + jnp.dot(pc.astype(BF16), vc, preferred_element_type=F32)
    o_ref[...] = (o / l_all).astype(o_ref.dtype)


def _na_attn(proj, bias, gq, gk):
    qrows = NA_RB * GRID_W
    nrb = ROWS // NA_RB
    cq, ck, cv = _pcol(3, NA_HD), _pcol(4, NA_HD), _pcol(5, NA_HD)
    return pl.pallas_call(
        _na_kernel,
        grid=(BATCH, NA_HEADS, nrb),
        in_specs=[pl.BlockSpec((qrows, NA_HD), lambda b, h, r: (b * nrb + r, cq + h)),
                  pl.BlockSpec((SEQ, NA_HD), lambda b, h, r: (b, ck + h)),
                  pl.BlockSpec((SEQ, NA_HD), lambda b, h, r: (b, cv + h)),
                  pl.BlockSpec((CTX_LEN, NA_HD), lambda b, h, r: (CTX_BLK + b, ck + h)),
                  pl.BlockSpec((CTX_LEN, NA_HD), lambda b, h, r: (CTX_BLK + b, cv + h)),
                  pl.BlockSpec((1, NA_ROWS, GRID_W, _NA_KEYS), lambda b, h, r: (h, 0, 0, 0)),
                  pl.BlockSpec((1, NA_HD), lambda b, h, r: (0, 0)),
                  pl.BlockSpec((1, NA_HD), lambda b, h, r: (0, 0))],
        out_specs=pl.BlockSpec((qrows, NA_HD), lambda b, h, r: (b * nrb + r, h)),
        out_shape=jax.ShapeDtypeStruct((N_LAT, NA_HEADS * NA_HD), BF16),
        scratch_shapes=[pltpu.VMEM((SEQ, NA_HD), BF16), pltpu.VMEM((CTX_LEN, NA_HD), BF16)],
        compiler_params=_cparams(("arbitrary", "arbitrary", "arbitrary")),
        name="na_attn",
    )(proj, proj, proj, proj, proj, bias, gq, gk)


def _na_bias_table(rpb):
    half = NA_ROWS // 2
    r_rep = np.array(list(range(half)) + [half] + list(range(ROWS - half + 1, ROWS)))
    start = np.clip(r_rep - half, 0, ROWS - NA_ROWS)
    dr = start[:, None] + np.arange(NA_ROWS)[None, :] - r_rep[:, None] + NA_ROWS - 1
    qc = np.arange(GRID_W)
    kcol = np.arange(GRID_W)
    col_start = np.clip(qc - NA_COLS // 2, 0, GRID_W - NA_COLS)
    in_win = (kcol[None, :] >= col_start[:, None]) & (kcol[None, :] < col_start[:, None] + NA_COLS)
    dc = np.clip(kcol[None, :] - qc[:, None], 1 - NA_COLS, NA_COLS - 1) + NA_COLS - 1
    rsel = (dr[:, :, None] == np.arange(2 * NA_ROWS - 1)).astype(np.float32)
    csel = (dc[:, :, None] == np.arange(2 * NA_COLS - 1)).astype(np.float32)
    b = jnp.einsum('pja,hab,qkb->hpqjk', rsel, rpb.astype(F32), csel,
                   precision=lax.Precision.HIGHEST)
    b = jnp.where(in_win[None, None, :, None, :], b.astype(F32), NEG_INF)
    return b.reshape(NA_HEADS, NA_ROWS, GRID_W, _NA_KEYS)


_GQA_G = GQA_HEADS // GQA_KV_HEADS
_GQA_BAND = 3 * GQA_WINDOW
_GQA_QW = GQA_HEADS * V7X_LANES


def _gqa_prep_kernel(q_ref, k_ref, gq_ref, gk_ref, c_ref, s1_ref, s2_ref, qd_ref, kn_ref):
    c = c_ref[...]
    s1 = s1_ref[...]
    s2 = s2_ref[...]
    lo = lax.broadcasted_iota(jnp.int32, (1, 128), 1) < GQA_HD

    def head_rms(x, g):
        x2 = x * x
        s_lo = jnp.sum(jnp.where(lo, x2, 0.0), axis=-1, keepdims=True)
        s_hi = jnp.sum(jnp.where(lo, 0.0, x2), axis=-1, keepdims=True)
        inv = 1.0 / GQA_HD
        r = jnp.where(lo, lax.rsqrt(s_lo * inv + EPS), lax.rsqrt(s_hi * inv + EPS))
        return x * r * g

    def rot(x):
        return x * c + pltpu.roll(x, 96, 1) * s1 + pltpu.roll(x, 32, 1) * s2

    gq = gq_ref[...]
    for j in range(GQA_HEADS // 2):
        y = rot(head_rms(q_ref[:, j * 128:(j + 1) * 128].astype(F32), gq))
        sw = pltpu.roll(y, 64, 1)
        hk = (2 * j) // _GQA_G
        if hk == 0:
            even, odd = jnp.where(lo, y, 0.0), jnp.where(lo, sw, 0.0)
        else:
            even, odd = jnp.where(lo, 0.0, sw), jnp.where(lo, 0.0, y)
        qd_ref[:, (2 * j) * 128:(2 * j + 1) * 128] = even.astype(BF16)
        qd_ref[:, (2 * j + 1) * 128:(2 * j + 2) * 128] = odd.astype(BF16)
    kn_ref[...] = rot(head_rms(k_ref[...].astype(F32), gk_ref[...])).astype(BF16)


def _gqa_prep(proj, p, tabs):
    tm = TOK_TM
    gq = (jnp.tile(p['gqa_qn_g'], 2) * (GQA_HD ** -0.5)).reshape(1, 128)
    gk = jnp.tile(p['gqa_kn_g'], 2).reshape(1, 128)
    row = pl.BlockSpec((tm, 128), lambda i: (i, 0))
    vec = pl.BlockSpec((1, 128), lambda i: (0, 0))
    return pl.pallas_call(
        _gqa_prep_kernel,
        grid=(N_TOK // tm,),
        in_specs=[pl.BlockSpec((tm, GQA_HEADS * GQA_HD), lambda i: (i, _pcol(6, GQA_HEADS * GQA_HD))),
                  pl.BlockSpec((tm, 128), lambda i: (i, _pcol(7, 128))),
                  vec, vec, row, row, row],
        out_specs=[pl.BlockSpec((tm, _GQA_QW), lambda i: (i, 0)), row],
        out_shape=[jax.ShapeDtypeStruct((N_TOK, _GQA_QW), BF16),
                   jax.ShapeDtypeStruct((N_TOK, 128), BF16)],
        compiler_params=_cparams(("arbitrary",)),
        name="gqa_prep",
    )(proj, proj, gq, gk, tabs['gqa_c'], tabs['gqa_s1'], tabs['gqa_s2'])


def _gqa_kernel(sink_ref, q_ref, k_ref, v_ref, kc_ref, vc_ref, o_ref):
    hk = pl.program_id(1)
    n = pl.program_id(2)
    w = GQA_WINDOW
    start = pl.multiple_of(jnp.clip((n - 1) * w, 0, SEQ - _GQA_BAND), w)
    q = jnp.concatenate([q_ref[:, g * 128:(g + 1) * 128] for g in range(_GQA_G)], axis=0)
    k = k_ref[pl.ds(start, _GQA_BAND), :]
    v = v_ref[pl.ds(start, _GQA_BAND), :]
    s = lax.dot_general(q, k, _NT, preferred_element_type=F32)
    rows = lax.broadcasted_iota(jnp.int32, s.shape, 0)
    cols = lax.broadcasted_iota(jnp.int32, s.shape, 1)
    qpos = n * w + (rows & (w - 1))
    kpos = start + cols
    s = jnp.where(jnp.abs(kpos - qpos) <= GQA_WINDOW, s, NEG_INF)
    sc = lax.dot_general(q, kc_ref[...], _NT, preferred_element_type=F32)
    grow = lax.broadcasted_iota(jnp.int32, (_GQA_G * w, 1), 0) // w
    snk = jnp.full((_GQA_G * w, 1), sink_ref[hk * _GQA_G], F32)
    for g in range(1, _GQA_G):
        snk = jnp.where(grow == g, sink_ref[hk * _GQA_G + g], snk)
    m = jnp.maximum(jnp.maximum(jnp.max(s, axis=-1, keepdims=True),
                                jnp.max(sc, axis=-1, keepdims=True)), snk)
    p = jnp.exp(s - m)
    pc = jnp.exp(sc - m)
    l = jnp.sum(p, axis=-1, keepdims=True) + jnp.sum(pc, axis=-1, keepdims=True) + jnp.exp(snk - m)
    o = (jnp.dot(p.astype(BF16), v, preferred_element_type=F32)
         + jnp.dot(pc.astype(BF16), vc_ref[...], preferred_element_type=F32)) / l
    half = lax.broadcasted_iota(jnp.int32, (1, 128), 1) // GQA_HD
    o = jnp.where(half == hk, o, 0.0).astype(o_ref.dtype)
    for g in range(_GQA_G):
        o_ref[:, g * 128:(g + 1) * 128] = o[g * w:(g + 1) * w]


def _gqa_attn(qd, kn, proj, sink):
    nb = SEQ // GQA_WINDOW
    qw = _GQA_G * 128
    cv = _pcol(8, 128)
    gs = pltpu.PrefetchScalarGridSpec(
        num_scalar_prefetch=1,
        grid=(BATCH, GQA_KV_HEADS, nb),
        in_specs=[pl.BlockSpec((GQA_WINDOW, qw), lambda b, h, n, *_: (b * nb + n, h)),
                  pl.BlockSpec((SEQ, 128), lambda b, h, n, *_: (b, 0)),
                  pl.BlockSpec((SEQ, 128), lambda b, h, n, *_: (b, cv)),
                  pl.BlockSpec((CTX_LEN, 128), lambda b, h, n, *_: (CTX_BLK + b, 0)),
                  pl.BlockSpec((CTX_LEN, 128), lambda b, h, n, *_: (CTX_BLK + b, cv))],
        out_specs=pl.BlockSpec((GQA_WINDOW, qw), lambda b, h, n, *_: (b * nb + n, h)),
    )
    return pl.pallas_call(
        _gqa_kernel,
        grid_spec=gs,
        out_shape=jax.ShapeDtypeStruct((N_LAT, _GQA_QW), BF16),
        compiler_params=_cparams(("arbitrary", "arbitrary", "arbitrary")),
        name="gqa_attn",
    )(sink.astype(F32), qd, kn, proj, kn, proj)


def _ctx_kernel(sink_ref, mq_ref, mk_ref, mvt_ref, nq_ref, nk_ref, nv_ref, gnq_ref, gnk_ref,
                gq_ref, gk_ref, gv_ref, oa_ref, ob_ref, oc_ref):
    for h in range(MLA_HEADS):
        st = lax.dot_general(mk_ref[h], mq_ref[h], _NT, preferred_element_type=F32)
        p = jnp.exp2(st - jnp.max(st, axis=0, keepdims=True))
        l = jnp.sum(p, axis=0, keepdims=True)
        ot = jnp.dot(mvt_ref[h], p.astype(BF16), preferred_element_type=F32) / l
        oa_ref[:, h * MLA_V:(h + 1) * MLA_V] = ot.T.astype(oa_ref.dtype)
    for h in range(NA_HEADS):
        sl = slice(h * NA_HD, (h + 1) * NA_HD)
        q = _row_rms(nq_ref[:, sl].astype(F32), gnq_ref[...]).astype(BF16)
        k = _row_rms(nk_ref[:, sl].astype(F32), gnk_ref[...]).astype(BF16)
        s = lax.dot_general(q, k, _NT, preferred_element_type=F32)
        p = jnp.exp(s - jnp.max(s, axis=-1, keepdims=True))
        l = jnp.sum(p, axis=-1, keepdims=True)
        o = jnp.dot(p.astype(BF16), nv_ref[:, sl], preferred_element_type=F32) / l
        ob_ref[:, sl] = o.astype(ob_ref.dtype)
    half = lax.broadcasted_iota(jnp.int32, (1, 128), 1) // GQA_HD
    k = gk_ref[...]
    v = gv_ref[...]
    for h in range(GQA_HEADS):
        sl = slice(h * 128, (h + 1) * 128)
        s = lax.dot_general(gq_ref[:, sl], k, _NT, preferred_element_type=F32)
        snk = sink_ref[h]
        m = jnp.maximum(jnp.max(s, axis=-1, keepdims=True), snk)
        p = jnp.exp(s - m)
        l = jnp.sum(p, axis=-1, keepdims=True) + jnp.exp(snk - m)
        o = jnp.dot(p.astype(BF16), v, preferred_element_type=F32) / l
        oc_ref[:, sl] = jnp.where(half == h // _GQA_G, o, 0.0).astype(oc_ref.dtype)


def _ctx_attn(sink, mq, mk, mvt, proj, gnq, gnk, qd, kn):
    c = CTX_LEN
    nwid = NA_HEADS * NA_HD

    def row(width, col=0):
        return pl.BlockSpec((c, width), lambda b, *_: (CTX_BLK + b, col))

    def out(width):
        return pl.BlockSpec((c, width), lambda b, *_: (b, 0))

    vec = pl.BlockSpec((1, NA_HD), lambda b, *_: (0, 0))
    gs = pltpu.PrefetchScalarGridSpec(
        num_scalar_prefetch=1,
        grid=(BATCH,),
        in_specs=[pl.BlockSpec((MLA_HEADS, c, _MLA_PAD), lambda b, *_: (0, CTX_BLK + b, 0)),
                  pl.BlockSpec((MLA_HEADS, c, _MLA_PAD), lambda b, *_: (0, CTX_BLK + b, 0)),
                  pl.BlockSpec((MLA_HEADS, MLA_V, c), lambda b, *_: (0, 0, CTX_BLK + b)),
                  row(nwid, _pcol(3, nwid)), row(nwid, _pcol(4, nwid)), row(nwid, _pcol(5, nwid)),
                  vec, vec,
                  row(_GQA_QW), row(128), row(128, _pcol(8, 128))],
        out_specs=[out(MLA_HEADS * MLA_V), out(nwid), out(_GQA_QW)],
    )
    return pl.pallas_call(
        _ctx_kernel,
        grid_spec=gs,
        out_shape=[jax.ShapeDtypeStruct((N_CTX, MLA_HEADS * MLA_V), BF16),
                   jax.ShapeDtypeStruct((N_CTX, nwid), BF16),
                   jax.ShapeDtypeStruct((N_CTX, _GQA_QW), BF16)],
        compiler_params=_cparams(("arbitrary",)),
        name="ctx_attn",
    )(sink.astype(F32), mq, mk, mvt, proj, proj, proj, gnq, gnk, qd, kn, proj)


def _merge_kernel(*refs, has_ctx, n_lat_tiles):
    ga_ref, gb_ref, gc_ref, oa_ref, ob_ref, oc_ref = refs[:6]
    refs = refs[6:]
    if has_ctx:
        ca_ref, cb_ref, cc_ref = refs[:3]
        refs = refs[3:]
    wa_ref, wb_ref, wc_ref, y_ref = refs
    is_ctx = pl.program_id(0) >= n_lat_tiles

    def branch(g_ref, o_ref, c_ref, w_ref):
        o = o_ref[...]
        if has_ctx:
            o = jnp.where(is_ctx, c_ref[...], o)
        return jax.nn.sigmoid(g_ref[...].astype(F32)) * jnp.dot(o, w_ref[...], preferred_element_type=F32)

    y = (branch(ga_ref, oa_ref, ca_ref if has_ctx else None, wa_ref)
         + branch(gb_ref, ob_ref, cb_ref if has_ctx else None, wb_ref)
         + branch(gc_ref, oc_ref, cc_ref if has_ctx else None, wc_ref))
    y_ref[...] = y.astype(y_ref.dtype)


def _merge(proj, lat, ctx, wa, wb, wc):
    tm = TOK_TM
    has_ctx = ctx is not None
    m = N_TOK if has_ctx else N_LAT
    nl = N_LAT // tm
    ks = [o.shape[1] for o in lat]
    in_specs = [pl.BlockSpec((tm, D_MODEL), lambda i: (i, 0)),
                pl.BlockSpec((tm, D_MODEL), lambda i: (i, 1)),
                pl.BlockSpec((tm, D_MODEL), lambda i: (i, 2))]
    in_specs += [pl.BlockSpec((tm, k), lambda i: (jnp.minimum(i, nl - 1), 0)) for k in ks]
    args = [proj, proj, proj] + list(lat)
    if has_ctx:
        in_specs += [pl.BlockSpec((tm, k), lambda i: (jnp.maximum(i - nl, 0), 0)) for k in ks]
        args += list(ctx)
    in_specs += [pl.BlockSpec((k, D_MODEL), lambda i: (0, 0)) for k in ks]
    args += [wa, wb, wc]
    return pl.pallas_call(
        functools.partial(_merge_kernel, has_ctx=has_ctx, n_lat_tiles=nl),
        grid=(m // tm,),
        in_specs=in_specs,
        out_specs=pl.BlockSpec((tm, D_MODEL), lambda i: (i, 0)),
        out_shape=jax.ShapeDtypeStruct((m, D_MODEL), BF16),
        compiler_params=_cparams(("arbitrary",)),
        name="merge",
    )(*args)


def _moe_ffn_kernel(te_ref, tok_ref, nu_ref, h_hbm, wg_ref, wu_ref, wd_ref, y_ref,
                    xbuf, sem, wgb, wub, wdb):
    i = pl.program_id(0)
    tm = MOE_TM
    slot = i % 2
    n_used = nu_ref[0]

    def row_copy(tok, s, r):
        return pltpu.make_async_copy(h_hbm.at[pl.ds(tok, 1)], xbuf.at[s, pl.ds(r, 1)], sem.at[s])

    def start_gather(tile, s):
        base = tile * tm

        def body(r, c):
            row_copy(tok_ref[base + r], s, r).start()
            return c

        lax.fori_loop(0, tm, body, 0, unroll=8)

    @pl.when(i == 0)
    def _():
        start_gather(0, 0)

    @pl.when(i + 1 < n_used)
    def _():
        start_gather(i + 1, 1 - slot)

    @pl.when(i < n_used)
    def _():
        @pl.when((i == 0) | (te_ref[i] != te_ref[jnp.maximum(i - 1, 0)]))
        def _():
            wgb[...] = wg_ref[0, 0].astype(BF16)
            wub[...] = wu_ref[0, 0].astype(BF16)
            wdb[...] = wd_ref[0, 0].astype(BF16)

        pltpu.make_async_copy(h_hbm.at[pl.ds(0, tm)], xbuf.at[slot], sem.at[slot]).wait()
        x = xbuf[slot].astype(BF16)
        hg = jnp.dot(x, wgb[...], preferred_element_type=F32)
        hu = jnp.dot(x, wub[...], preferred_element_type=F32)
        act = (hg * jax.nn.sigmoid(hg)) * hu
        y_ref[...] = jnp.dot(act.astype(BF16), wdb[...], preferred_element_type=F32)

    @pl.when(i >= n_used)
    def _():
        y_ref[...] = jnp.zeros_like(y_ref)


def _moe_ffn(h, tile_expert, slot_token, n_used, wg, wu, wd, layer):
    p = slot_token.shape[0]
    tm = MOE_TM
    nt = p // tm
    gs = pltpu.PrefetchScalarGridSpec(
        num_scalar_prefetch=3,
        grid=(nt,),
        in_specs=[pl.BlockSpec(memory_space=pl.ANY),
                  pl.BlockSpec((1, 1, D_MODEL, MOE_HIDDEN), lambda i, te, tok, nu: (layer, te[i], 0, 0)),
                  pl.BlockSpec((1, 1, D_MODEL, MOE_HIDDEN), lambda i, te, tok, nu: (layer, te[i], 0, 0)),
                  pl.BlockSpec((1, 1, MOE_HIDDEN, D_MODEL), lambda i, te, tok, nu: (layer, te[i], 0, 0))],
        out_specs=pl.BlockSpec((tm, D_MODEL), lambda i, te, tok, nu: (i, 0)),
        scratch_shapes=[pltpu.VMEM((2, tm, D_MODEL), F32),
                        pltpu.SemaphoreType.DMA((2,)),
                        pltpu.VMEM((D_MODEL, MOE_HIDDEN), BF16),
                        pltpu.VMEM((D_MODEL, MOE_HIDDEN), BF16),
                        pltpu.VMEM((MOE_HIDDEN, D_MODEL), BF16)],
    )
    return pl.pallas_call(
        _moe_ffn_kernel,
        grid_spec=gs,
        out_shape=jax.ShapeDtypeStruct((p, D_MODEL), F32),
        compiler_params=_cparams(("arbitrary",)),
        name="moe_ffn",
    )(tile_expert, slot_token, n_used, h, wg, wu, wd)


def _moe_combine_kernel(pos_ref, y_hbm, x_ref, w_ref, g_ref, o_ref, ybuf, sem):
    i = pl.program_id(0)
    nt = pl.num_programs(0)
    tm = TOK_TM
    slot = i % 2

    def row_copy(src, s, r):
        return pltpu.make_async_copy(y_hbm.at[pl.ds(src, 1)], ybuf.at[s, pl.ds(r, 1)], sem.at[s])

    def start_gather(tile, s):
        base = tile * tm

        def body(r, c):
            row_copy(pos_ref[2 * (base + r)], s, r).start()
            row_copy(pos_ref[2 * (base + r) + 1], s, tm + r).start()
            return c

        lax.fori_loop(0, tm, body, 0, unroll=4)

    @pl.when(i == 0)
    def _():
        start_gather(0, 0)

    @pl.when(i + 1 < nt)
    def _():
        start_gather(i + 1, 1 - slot)

    pltpu.make_async_copy(y_hbm.at[pl.ds(0, 2 * tm)], ybuf.at[slot], sem.at[slot]).wait()
    w = w_ref[...]
    y = ybuf[slot, pl.ds(0, tm), :] * w[:, 0:1] + ybuf[slot, pl.ds(tm, tm), :] * w[:, 1:2]
    o_ref[...] = x_ref[...] + g_ref[0] * y


def _moe_combine(y, pos, w_sel, x, mod, k_gate):
    m = x.shape[0]
    tm = TOK_TM
    gs = pltpu.PrefetchScalarGridSpec(
        num_scalar_prefetch=1,
        grid=(m // tm,),
        in_specs=[pl.BlockSpec(memory_space=pl.ANY),
                  pl.BlockSpec((tm, D_MODEL), lambda i, pos: (i, 0)),
                  pl.BlockSpec((tm, MOE_TOPK), lambda i, pos: (i, 0)),
                  pl.BlockSpec((1, 1, D_MODEL), lambda i, pos: (_seg(i, tm) * 6 + k_gate, 0, 0))],
        out_specs=pl.BlockSpec((tm, D_MODEL), lambda i, pos: (i, 0)),
        scratch_shapes=[pltpu.VMEM((2, 2 * tm, D_MODEL), F32),
                        pltpu.SemaphoreType.DMA((2,))],
    )
    return pl.pallas_call(
        _moe_combine_kernel,
        grid_spec=gs,
        out_shape=jax.ShapeDtypeStruct((m, D_MODEL), F32),
        compiler_params=_cparams(("arbitrary",)),
        name="moe_combine",
    )(pos, y, x, w_sel, mod)


def _route(logits, m):
    tm = MOE_TM
    gp = jax.nn.softmax(logits[:, :MOE_GROUPS], axis=-1)
    g_idx = jnp.argmax(gp, axis=-1).astype(jnp.int32)[:, None]
    g_w = jnp.max(gp, axis=-1, keepdims=True)
    el = logits[:, MOE_GROUPS:MOE_GROUPS + MOE_EXPERTS].reshape(m, MOE_GROUPS, MOE_PER_GROUP)
    g_onehot = (g_idx == jnp.arange(MOE_GROUPS, dtype=jnp.int32)[None, :]).astype(F32)
    el_g = jnp.sum(el * g_onehot[:, :, None], axis=1)
    i0 = jnp.argmax(el_g, axis=-1).astype(jnp.int32)[:, None]
    l0 = jnp.max(el_g, axis=-1, keepdims=True)
    rest = jnp.where(jnp.arange(MOE_PER_GROUP, dtype=jnp.int32)[None, :] == i0, -jnp.inf, el_g)
    i1 = jnp.argmax(rest, axis=-1).astype(jnp.int32)[:, None]
    l1 = jnp.max(rest, axis=-1, keepdims=True)
    top_l = jnp.concatenate([l0, l1], axis=-1)
    top_i = jnp.concatenate([i0, i1], axis=-1)
    w_sel = jax.nn.softmax(top_l, axis=-1) * g_w
    eid = (g_idx * MOE_PER_GROUP + top_i).astype(jnp.int32)

    a = m * MOE_TOPK
    e_flat = eid.reshape(a)
    onehot = (e_flat[:, None] == jnp.arange(MOE_EXPERTS, dtype=jnp.int32)[None, :]).astype(jnp.int32)
    csum = jnp.cumsum(onehot, axis=0)
    rank = jnp.sum(csum * onehot, axis=1) - 1
    counts = csum[-1]
    padded = ((counts + tm - 1) // tm) * tm
    ends = jnp.cumsum(padded)
    starts = ends - padded
    pos = (jnp.sum(onehot * starts[None, :], axis=1) + rank).astype(jnp.int32)
    p = a + MOE_EXPERTS * tm
    slot_token = jnp.zeros((p,), jnp.int32).at[pos].set(jnp.arange(a, dtype=jnp.int32) // MOE_TOPK)
    n_used = (ends[-1] // tm).astype(jnp.int32).reshape(1)
    tile_start = jnp.arange(p // tm, dtype=jnp.int32) * tm
    last_e = jnp.max(jnp.where(counts > 0, jnp.arange(MOE_EXPERTS, dtype=jnp.int32), 0))
    tile_expert = jnp.minimum(
        jnp.sum((ends[None, :] <= tile_start[:, None]).astype(jnp.int32), axis=1), last_e)
    return tile_expert, slot_token, n_used, w_sel, pos


def _rope_angles(rot_dim):
    t = jnp.arange(SEQ)
    row = (t // GRID_W).astype(F32)
    col = (t % GRID_W).astype(F32)
    n_freq = rot_dim // 4
    inv = ROPE_THETA ** (-jnp.arange(n_freq, dtype=F32) / n_freq)
    ang = jnp.concatenate([row[:, None] * inv, col[:, None] * inv], axis=-1)
    return jnp.cos(ang), jnp.sin(ang)


def _rope_tables():
    def rows(lat, ctx_val):
        ctx = jnp.full((N_CTX, 128), 0.0, F32) + ctx_val
        return jnp.concatenate([lat, lat, ctx], axis=0)

    z32 = jnp.zeros((SEQ, 32), F32)
    z64 = jnp.zeros((SEQ, 64), F32)
    lane = np.arange(128)
    cm, sm = _rope_angles(MLA_ROPE)
    mla_c = rows(jnp.concatenate([cm, cm, z64], axis=1), jnp.asarray((lane < 64).astype(np.float32)))
    mla_s = rows(jnp.concatenate([-sm, sm, z64], axis=1), 0.0)
    cg, sg = _rope_angles(GQA_HD)
    gqa_c = rows(jnp.concatenate([cg, cg, cg, cg], axis=1), 1.0)
    gqa_s1 = rows(jnp.concatenate([-sg, z32, -sg, z32], axis=1), 0.0)
    gqa_s2 = rows(jnp.concatenate([z32, sg, z32, sg], axis=1), 0.0)
    return {'mla_c': mla_c, 'mla_s': mla_s, 'gqa_c': gqa_c, 'gqa_s1': gqa_s1, 'gqa_s2': gqa_s2}


def _pack_w_in(w):
    parts = jnp.split(w, IN_SPLITS, axis=1)
    pad = jnp.zeros((D_MODEL, _PACK_COLS - sum(IN_SIZES)), w.dtype)
    return jnp.concatenate([parts[i] for i in _PACK_ORDER] + [pad], axis=1).astype(BF16)


def _pad_w_o_gqa(w):
    w4 = w.reshape(GQA_KV_HEADS, _GQA_G, GQA_HD, D_MODEL)
    z = jnp.zeros_like(w4[0:1])
    halves = [jnp.concatenate([w4[hk:hk + 1] if hk == half else z for hk in range(GQA_KV_HEADS)], axis=0)
              for half in range(GQA_KV_HEADS)]
    return jnp.stack(halves, axis=2).reshape(_GQA_QW, D_MODEL).astype(BF16)


def _token_mixer(h, p, tabs, ctx_out):
    proj = _mm(h, _pack_w_in(p['w_in']), BF16, _PACK_COLS // 4)
    mq, mk, mvt = _mla_prep(proj, p, tabs)
    oa = _mla_attn(mq, mk, mvt, tq=512, tk=1024)
    gnq = (p['na_qn_g'] * (NA_HD ** -0.5)).reshape(1, NA_HD)
    gnk = p['na_kn_g'].reshape(1, NA_HD)
    ob = _na_attn(proj, _na_bias_table(p['na_rpb']), gnq, gnk)
    qd, kn = _gqa_prep(proj, p, tabs)
    oc = _gqa_attn(qd, kn, proj, p['gqa_sink'])
    ctx = _ctx_attn(p['gqa_sink'], mq, mk, mvt, proj, gnq, gnk, qd, kn) if ctx_out else None
    return _merge(proj, (oa, ob, oc), ctx, p['w_o_mla'].astype(BF16), p['w_o_na'].astype(BF16),
                  _pad_w_o_gqa(p['w_o_gqa']))


def _moe(x, mod, norm_g, p):
    m = x.shape[0]
    wr = jnp.concatenate([p['moe_w_group'], p['moe_w_expert'],
                          jnp.zeros((D_MODEL, ROUTE_COLS - MOE_GROUPS - MOE_EXPERTS), F32)], axis=1)
    br = jnp.concatenate([p['moe_b_group'], p['moe_b_expert'],
                          jnp.zeros((ROUTE_COLS - MOE_GROUPS - MOE_EXPERTS,), F32)]).reshape(1, ROUTE_COLS)
    h, logits = _norm_mod(x, norm_g, mod, 3, 4, route=(wr, br))
    tile_expert, slot_token, n_used, w_sel, pos = _route(logits, m)
    y = _moe_ffn(h, tile_expert, slot_token, n_used,
                 p['moe_w_gate'], p['moe_w_up'], p['moe_w_down'], p['layer'])
    return _moe_combine(y, pos, w_sel, x, mod, 5)


def kernel(x, c, ctx, c_ctx, ada_w, ada_b, norm_mix_g, norm_ffn_g, w_in,
           mla_q_norm_g, mla_w_uq, mla_kv_norm_g, mla_w_ukv, mla_qn_g, mla_kn_g,
           na_qn_g, na_kn_g, na_rpb, gqa_qn_g, gqa_kn_g, gqa_sink,
           w_o_mla, w_o_na, w_o_gqa, w_out,
           moe_w_group, moe_b_group, moe_w_expert, moe_b_expert,
           moe_w_gate, moe_w_up, moe_w_down):
    xt = jnp.concatenate([x.reshape(N_LAT, D_MODEL), ctx.reshape(N_CTX, D_MODEL)], axis=0)
    c_rows = jnp.concatenate([c, c_ctx[None, :], jnp.zeros((8 - BATCH - 1, D_MODEL), F32)], axis=0)
    mod_all = _ada(c_rows, ada_w, ada_b)
    tabs = _rope_tables()
    for l in range(DEPTH):
        ctx_out = l < DEPTH - 1
        p = {
            'w_in': w_in[l], 'mla_q_norm_g': mla_q_norm_g[l], 'mla_w_uq': mla_w_uq[l],
            'mla_kv_norm_g': mla_kv_norm_g[l], 'mla_w_ukv': mla_w_ukv[l],
            'mla_qn_g': mla_qn_g[l], 'mla_kn_g': mla_kn_g[l],
            'na_qn_g': na_qn_g[l], 'na_kn_g': na_kn_g[l], 'na_rpb': na_rpb[l],
            'gqa_qn_g': gqa_qn_g[l], 'gqa_kn_g': gqa_kn_g[l], 'gqa_sink': gqa_sink[l],
            'w_o_mla': w_o_mla[l], 'w_o_na': w_o_na[l], 'w_o_gqa': w_o_gqa[l],
            'moe_w_group': moe_w_group[l], 'moe_b_group': moe_b_group[l],
            'moe_w_expert': moe_w_expert[l], 'moe_b_expert': moe_b_expert[l],
            'moe_w_gate': moe_w_gate, 'moe_w_up': moe_w_up, 'moe_w_down': moe_w_down, 'layer': l,
        }
        mod = mod_all[l].reshape(8 * 6, 1, D_MODEL)
        h = _norm_mod(xt, norm_mix_g[l], mod, 0, 1)
        y = _token_mixer(h, p, tabs, ctx_out)
        m = y.shape[0]
        x_mid = _mm_res(y, w_out, l, xt, mod, 2, 1024)
        xt = _moe(x_mid, mod, norm_ffn_g[l], p)
    return xt[:N_LAT].reshape(BATCH, SEQ, D_MODEL)
```

```python
import functools

import numpy as np
import jax
import jax.numpy as jnp
from jax import lax
from jax.experimental import pallas as pl
from jax.experimental.pallas import tpu as pltpu

D_MODEL = 2048
BATCH = 2
SEQ = 4096
DEPTH = 2
GRID_W = 64
CTX_LEN = 256
EPS = 1e-6
ROPE_THETA = 10000.0
NEG_INF = -1e30

MLA_HEADS = 8
MLA_Q_RANK = 512
MLA_KV_RANK = 512
MLA_NOPE = 128
MLA_ROPE = 64
MLA_QK = MLA_NOPE + MLA_ROPE
MLA_V = 128
NA_HEADS = 4
NA_HD = 128
NA_ROWS = 8
NA_COLS = 16
GQA_HEADS = 8
GQA_KV_HEADS = 2
GQA_HD = 64
GQA_WINDOW = 128
MOE_GROUPS = 4
MOE_PER_GROUP = 8
MOE_EXPERTS = MOE_GROUPS * MOE_PER_GROUP
MOE_TOPK = 2
MOE_HIDDEN = 512

IN_SIZES = (MLA_Q_RANK, MLA_KV_RANK, MLA_ROPE,
            NA_HEADS * NA_HD, NA_HEADS * NA_HD, NA_HEADS * NA_HD,
            GQA_HEADS * GQA_HD, GQA_KV_HEADS * GQA_HD, GQA_KV_HEADS * GQA_HD,
            D_MODEL, D_MODEL, D_MODEL)
IN_SPLITS = tuple(int(s) for s in np.cumsum(IN_SIZES)[:-1])

N_LAT = BATCH * SEQ
N_CTX = BATCH * CTX_LEN
N_TOK = N_LAT + N_CTX
ROWS = SEQ // GRID_W

V7X_LANES = 128
V7X_VMEM_LIMIT = 56 * 1024 * 1024

_PACK_ORDER = (9, 10, 11, 0, 1, 3, 4, 5, 6, 7, 8, 2)
_PACK_COLS = 9728
_PACK_OFF = {}
_off = 0
for _i in _PACK_ORDER:
    _PACK_OFF[_i] = _off
    _off += IN_SIZES[_i]

ROUTE_COLS = V7X_LANES
MOE_TM = 256
TOK_TM = 256
MM_TM = 512
CTX_BLK = N_LAT // CTX_LEN

F32 = jnp.float32
BF16 = jnp.bfloat16
_LOG2E = 1.4426950408889634
_NT = (((1,), (1,)), ((), ()))


def _cparams(sem):
    return pltpu.CompilerParams(dimension_semantics=sem, vmem_limit_bytes=V7X_VMEM_LIMIT)


def _seg(i, tm):
    return jnp.minimum(i // (SEQ // tm), 2)


def _pcol(idx, width):
    assert _PACK_OFF[idx] % width == 0
    return _PACK_OFF[idx] // width


def _ada_kernel(c_ref, w_ref, b_ref, o_ref):
    c = c_ref[...]
    a = c * jax.nn.sigmoid(c)
    o_ref[0] = jnp.dot(a, w_ref[0], preferred_element_type=F32,
                       precision=lax.Precision.HIGHEST) + b_ref[0]


def _ada(c_rows, ada_w, ada_b):
    tn = 1024
    n = 6 * D_MODEL
    return pl.pallas_call(
        _ada_kernel,
        grid=(DEPTH, n // tn),
        in_specs=[
            pl.BlockSpec((8, D_MODEL), lambda l, j: (0, 0)),
            pl.BlockSpec((1, D_MODEL, tn), lambda l, j: (l, 0, j)),
            pl.BlockSpec((1, 1, tn), lambda l, j: (l, 0, j)),
        ],
        out_specs=pl.BlockSpec((1, 8, tn), lambda l, j: (l, 0, j)),
        out_shape=jax.ShapeDtypeStruct((DEPTH, 8, n), F32),
        compiler_params=_cparams(("arbitrary", "arbitrary")),
        name="ada",
    )(c_rows, ada_w, ada_b.reshape(DEPTH, 1, n))


def _norm_mod_kernel(x_ref, g_ref, sh_ref, sc_ref, h_ref):
    x = x_ref[...]
    xn = x * lax.rsqrt(jnp.mean(x * x, axis=-1, keepdims=True) + EPS) * g_ref[...]
    h_ref[...] = (xn * (1.0 + sc_ref[0]) + sh_ref[0]).astype(h_ref.dtype)


def _norm_mod_route_kernel(x_ref, g_ref, sh_ref, sc_ref, wr_ref, br_ref, h_ref, lg_ref):
    x = x_ref[...]
    xn = x * lax.rsqrt(jnp.mean(x * x, axis=-1, keepdims=True) + EPS) * g_ref[...]
    h = xn * (1.0 + sc_ref[0]) + sh_ref[0]
    h_ref[...] = h
    lg_ref[...] = jnp.dot(h, wr_ref[...], preferred_element_type=F32,
                          precision=lax.Precision.HIGHEST) + br_ref[...]


def _norm_mod(x, g, mod, k_shift, k_scale, route=None):
    m = x.shape[0]
    tm = TOK_TM
    base = [
        pl.BlockSpec((tm, D_MODEL), lambda i: (i, 0)),
        pl.BlockSpec((1, D_MODEL), lambda i: (0, 0)),
        pl.BlockSpec((1, 1, D_MODEL), lambda i: (_seg(i, tm) * 6 + k_shift, 0, 0)),
        pl.BlockSpec((1, 1, D_MODEL), lambda i: (_seg(i, tm) * 6 + k_scale, 0, 0)),
    ]
    if route is None:
        return pl.pallas_call(
            _norm_mod_kernel,
            grid=(m // tm,),
            in_specs=base,
            out_specs=pl.BlockSpec((tm, D_MODEL), lambda i: (i, 0)),
            out_shape=jax.ShapeDtypeStruct((m, D_MODEL), BF16),
            compiler_params=_cparams(("arbitrary",)),
            name="norm_mod",
        )(x, g.reshape(1, D_MODEL), mod, mod)
    wr, br = route
    return pl.pallas_call(
        _norm_mod_route_kernel,
        grid=(m // tm,),
        in_specs=base + [
            pl.BlockSpec((D_MODEL, ROUTE_COLS), lambda i: (0, 0)),
            pl.BlockSpec((1, ROUTE_COLS), lambda i: (0, 0)),
        ],
        out_specs=[pl.BlockSpec((tm, D_MODEL), lambda i: (i, 0)),
                   pl.BlockSpec((tm, ROUTE_COLS), lambda i: (i, 0))],
        out_shape=[jax.ShapeDtypeStruct((m, D_MODEL), F32),
                   jax.ShapeDtypeStruct((m, ROUTE_COLS), F32)],
        compiler_params=_cparams(("arbitrary",)),
        name="norm_mod_route",
    )(x, g.reshape(1, D_MODEL), mod, mod, wr, br)


def _mm_kernel(x_ref, w_ref, o_ref, wb_ref):
    @pl.when(pl.program_id(1) == 0)
    def _():
        wb_ref[...] = w_ref[...].astype(BF16)

    o_ref[...] = jnp.dot(x_ref[...], wb_ref[...], preferred_element_type=F32).astype(o_ref.dtype)


def _mm_res_kernel(x_ref, w_ref, r_ref, g_ref, o_ref, wb_ref):
    @pl.when(pl.program_id(1) == 0)
    def _():
        wb_ref[...] = w_ref[0].astype(BF16)

    acc = jnp.dot(x_ref[...], wb_ref[...], preferred_element_type=F32)
    o_ref[...] = r_ref[...] + g_ref[0] * acc


def _mm_bf16_kernel(x_ref, w_ref, o_ref):
    o_ref[...] = jnp.dot(x_ref[...], w_ref[...], preferred_element_type=F32).astype(o_ref.dtype)


def _mm(x, w, out_dtype, tn):
    m, k = x.shape
    n = w.shape[1]
    tm = MM_TM
    if w.dtype == BF16:
        return pl.pallas_call(
            _mm_bf16_kernel,
            grid=(n // tn, m // tm),
            in_specs=[pl.BlockSpec((tm, k), lambda j, i: (i, 0)),
                      pl.BlockSpec((k, tn), lambda j, i: (0, j))],
            out_specs=pl.BlockSpec((tm, tn), lambda j, i: (i, j)),
            out_shape=jax.ShapeDtypeStruct((m, n), out_dtype),
            compiler_params=_cparams(("arbitrary", "arbitrary")),
            name="mm_bf16",
        )(x, w)
    return pl.pallas_call(
        _mm_kernel,
        grid=(n // tn, m // tm),
        in_specs=[pl.BlockSpec((tm, k), lambda j, i: (i, 0)),
                  pl.BlockSpec((k, tn), lambda j, i: (0, j))],
        out_specs=pl.BlockSpec((tm, tn), lambda j, i: (i, j)),
        out_shape=jax.ShapeDtypeStruct((m, n), out_dtype),
        scratch_shapes=[pltpu.VMEM((k, tn), BF16)],
        compiler_params=_cparams(("arbitrary", "arbitrary")),
        name="mm",
    )(x, w)


def _mm_res(x, w, layer, res, mod, k_gate, tn):
    m, k = x.shape
    n = w.shape[2]
    tm = MM_TM
    nj = n // tn
    return pl.pallas_call(
        _mm_res_kernel,
        grid=(nj, m // tm),
        in_specs=[pl.BlockSpec((tm, k), lambda j, i: (i, 0)),
                  pl.BlockSpec((1, k, tn), lambda j, i: (layer, 0, j)),
                  pl.BlockSpec((tm, tn), lambda j, i: (i, j)),
                  pl.BlockSpec((1, 1, tn), lambda j, i: (_seg(i, tm) * 6 + k_gate, 0, j))],
        out_specs=pl.BlockSpec((tm, tn), lambda j, i: (i, j)),
        out_shape=jax.ShapeDtypeStruct((m, n), F32),
        scratch_shapes=[pltpu.VMEM((k, tn), BF16)],
        compiler_params=_cparams(("arbitrary", "arbitrary")),
        name="mm_res",
    )(x, w, res, mod)


def _row_rms(x, g):
    return x * lax.rsqrt(jnp.mean(x * x, axis=-1, keepdims=True) + EPS) * g


_MLA_PAD = 2 * V7X_LANES


def _mla_prep_kernel(cq_ref, ckv_ref, kr_ref, wq_ref, wkv_ref, gqi_ref, gkvi_ref,
                     gq_ref, gkn_ref, gkr_ref, c_ref, s_ref, q_ref, k_ref, vt_ref):
    c = c_ref[...]
    s = s_ref[...]
    hw = MLA_HEADS * MLA_NOPE

    def rot(t):
        return t * c + (pltpu.roll(t, 32, 1) + pltpu.roll(t, 96, 1)) * s

    cqn = _row_rms(cq_ref[...].astype(F32), gqi_ref[...]).astype(BF16)
    qf = jnp.dot(cqn, wq_ref[...], preferred_element_type=F32)
    gq = gq_ref[...]
    inv = 1.0 / MLA_QK
    for h in range(MLA_HEADS):
        nope = qf[:, h * 128:(h + 1) * 128]
        t = qf[:, hw + h * 128:hw + (h + 1) * 128]
        ss = jnp.sum(nope * nope, axis=-1, keepdims=True) + jnp.sum(t * t, axis=-1, keepdims=True)
        r = lax.rsqrt(ss * inv + EPS)
        q_ref[h, :, 0:128] = (nope * r * gq[:, 0:128]).astype(BF16)
        q_ref[h, :, 128:256] = rot(t * r * gq[:, 128:256]).astype(BF16)

    ckvn = _row_rms(ckv_ref[...].astype(F32), gkvi_ref[...]).astype(BF16)
    kvf = jnp.dot(ckvn, wkv_ref[...], preferred_element_type=F32)
    kr = kr_ref[...].astype(F32)
    ssr = jnp.sum(kr * kr, axis=-1, keepdims=True)
    yrot = rot(kr * gkr_ref[...])
    gkn = gkn_ref[...]
    for h in range(MLA_HEADS):
        nope = kvf[:, h * 128:(h + 1) * 128]
        r = lax.rsqrt((jnp.sum(nope * nope, axis=-1, keepdims=True) + ssr) * inv + EPS)
        k_ref[h, :, 0:128] = (nope * r * gkn).astype(BF16)
        k_ref[h, :, 128:256] = (yrot * r).astype(BF16)
        vt_ref[h] = kvf[:, hw + h * 128:hw + (h + 1) * 128].T.astype(BF16)


def _mla_prep(proj, p, tabs):
    tm = TOK_TM
    hw = MLA_HEADS * MLA_NOPE
    wq = p['mla_w_uq'].reshape(MLA_Q_RANK, MLA_HEADS, MLA_QK)
    wq_rope = jnp.pad(wq[:, :, MLA_NOPE:], ((0, 0), (0, 0), (0, 128 - MLA_ROPE)))
    wq = jnp.concatenate([wq[:, :, :MLA_NOPE].reshape(MLA_Q_RANK, hw),
                          wq_rope.reshape(MLA_Q_RANK, hw)], axis=1).astype(BF16)
    wkv = p['mla_w_ukv'].reshape(MLA_KV_RANK, MLA_HEADS, MLA_NOPE + MLA_V)
    wkv = jnp.concatenate([wkv[:, :, :MLA_NOPE].reshape(MLA_KV_RANK, hw),
                           wkv[:, :, MLA_NOPE:].reshape(MLA_KV_RANK, hw)], axis=1).astype(BF16)
    zpad = jnp.zeros((128 - MLA_ROPE,), F32)
    gq = (jnp.concatenate([p['mla_qn_g'], zpad]) * (MLA_QK ** -0.5 * _LOG2E)).reshape(1, _MLA_PAD)
    gkn = p['mla_kn_g'][:MLA_NOPE].reshape(1, 128)
    gkr = jnp.concatenate([p['mla_kn_g'][MLA_NOPE:], zpad]).reshape(1, 128)

    def const(shape):
        return pl.BlockSpec(shape, lambda i: (0,) * len(shape))

    return pl.pallas_call(
        _mla_prep_kernel,
        grid=(N_TOK // tm,),
        in_specs=[pl.BlockSpec((tm, MLA_Q_RANK), lambda i: (i, _pcol(0, MLA_Q_RANK))),
                  pl.BlockSpec((tm, MLA_KV_RANK), lambda i: (i, _pcol(1, MLA_KV_RANK))),
                  pl.BlockSpec((tm, 128), lambda i: (i, _pcol(2, 128))),
                  const((MLA_Q_RANK, 2 * hw)), const((MLA_KV_RANK, 2 * hw)),
                  const((1, MLA_Q_RANK)), const((1, MLA_KV_RANK)),
                  const((1, _MLA_PAD)), const((1, 128)), const((1, 128)),
                  pl.BlockSpec((tm, 128), lambda i: (i, 0)),
                  pl.BlockSpec((tm, 128), lambda i: (i, 0))],
        out_specs=[pl.BlockSpec((MLA_HEADS, tm, _MLA_PAD), lambda i: (0, i, 0)),
                   pl.BlockSpec((MLA_HEADS, tm, _MLA_PAD), lambda i: (0, i, 0)),
                   pl.BlockSpec((MLA_HEADS, MLA_V, tm), lambda i: (0, 0, i))],
        out_shape=[jax.ShapeDtypeStruct((MLA_HEADS, N_TOK, _MLA_PAD), BF16),
                   jax.ShapeDtypeStruct((MLA_HEADS, N_TOK, _MLA_PAD), BF16),
                   jax.ShapeDtypeStruct((MLA_HEADS, MLA_V, N_TOK), BF16)],
        compiler_params=_cparams(("arbitrary",)),
        name="mla_prep",
    )(proj, proj, proj, wq, wkv, p['mla_q_norm_g'].reshape(1, -1), p['mla_kv_norm_g'].reshape(1, -1),
      gq, gkn, gkr, tabs['mla_c'], tabs['mla_s'])


def _mla_kernel(q_ref, k1_ref, vt1_ref, k2_ref, vt2_ref, o_ref, sa_ref, sb_ref, sc_ref, acc_ref,
                *, tk, n_chunks):
    q = q_ref[0]
    tq = q.shape[0]

    def scores(dst_ref, kc):
        dst_ref[...] = lax.dot_general(kc, q, _NT, preferred_element_type=F32)

    def k_chunk(c):
        return k1_ref[0, pl.ds(pl.multiple_of(c * tk, tk), tk), :]

    def vt_chunk(c):
        return vt1_ref[0, :, pl.ds(pl.multiple_of(c * tk, tk), tk)]

    def accumulate(s_ref, vtc, m, l):
        st = s_ref[...]
        m_new = jnp.maximum(m, jnp.max(st, axis=0, keepdims=True))
        a = jnp.exp2(m - m_new)
        p = jnp.exp2(st - m_new)
        l = a * l + jnp.sum(p, axis=0, keepdims=True)
        acc_ref[...] = a * acc_ref[...] + jnp.dot(vtc, p.astype(BF16), preferred_element_type=F32)
        return m_new, l

    m = jnp.full((1, tq), NEG_INF, F32)
    l = jnp.zeros((1, tq), F32)
    acc_ref[...] = jnp.zeros_like(acc_ref)
    scores(sc_ref, k2_ref[0])
    scores(sa_ref, k_chunk(0))
    m, l = accumulate(sc_ref, vt2_ref[0], m, l)

    for i in range(n_chunks // 2 - 1):
        scores(sb_ref, k_chunk(2 * i + 1))
        m, l = accumulate(sa_ref, vt_chunk(2 * i), m, l)
        scores(sa_ref, k_chunk(2 * i + 2))
        m, l = accumulate(sb_ref, vt_chunk(2 * i + 1), m, l)
    scores(sb_ref, k_chunk(n_chunks - 1))
    m, l = accumulate(sa_ref, vt_chunk(n_chunks - 2), m, l)
    m, l = accumulate(sb_ref, vt_chunk(n_chunks - 1), m, l)
    o_ref[...] = (acc_ref[...] / l).T.astype(o_ref.dtype)


def _mla_attn(q, k, vt, *, tq, tk):
    nq = SEQ // tq
    return pl.pallas_call(
        functools.partial(_mla_kernel, tk=tk, n_chunks=SEQ // tk),
        grid=(BATCH, MLA_HEADS, nq),
        in_specs=[pl.BlockSpec((1, tq, _MLA_PAD), lambda b, h, i: (h, b * nq + i, 0)),
                  pl.BlockSpec((1, SEQ, _MLA_PAD), lambda b, h, i: (h, b, 0)),
                  pl.BlockSpec((1, MLA_V, SEQ), lambda b, h, i: (h, 0, b)),
                  pl.BlockSpec((1, CTX_LEN, _MLA_PAD), lambda b, h, i: (h, CTX_BLK + b, 0)),
                  pl.BlockSpec((1, MLA_V, CTX_LEN), lambda b, h, i: (h, 0, CTX_BLK + b))],
        out_specs=pl.BlockSpec((tq, MLA_V), lambda b, h, i: (b * nq + i, h)),
        out_shape=jax.ShapeDtypeStruct((N_LAT, MLA_HEADS * MLA_V), BF16),
        scratch_shapes=[pltpu.VMEM((tk, tq), F32), pltpu.VMEM((tk, tq), F32),
                        pltpu.VMEM((CTX_LEN, tq), F32), pltpu.VMEM((MLA_V, tq), F32)],
        compiler_params=_cparams(("arbitrary", "arbitrary", "arbitrary")),
        name="mla_attn",
    )(q, k, vt, k, vt)


_NA_KEYS = NA_ROWS * GRID_W
NA_RB = 8


def _na_pattern(r):
    half = NA_ROWS // 2
    return jnp.where(r < half, r, jnp.where(r <= ROWS - half, half, r - (ROWS - NA_ROWS)))


def _na_kernel(q_ref, k_ref, v_ref, kc_ref, vc_ref, bias_ref, gq_ref, gk_ref, o_ref, kn_ref, kcn_ref):
    rb = pl.program_id(2)

    @pl.when(rb == 0)
    def _():
        kn_ref[...] = _row_rms(k_ref[...].astype(F32), gk_ref[...]).astype(BF16)
        kcn_ref[...] = _row_rms(kc_ref[...].astype(F32), gk_ref[...]).astype(BF16)

    q_all = _row_rms(q_ref[...].astype(F32), gq_ref[...]).astype(BF16)
    vc = vc_ref[...]
    sc_all = lax.dot_general(q_all, kcn_ref[...], _NT, preferred_element_type=F32)
    ms, ls, os_ = [], [], []
    for j in range(NA_RB):
        r = rb * NA_RB + j
        start = pl.multiple_of(jnp.clip(r - NA_ROWS // 2, 0, ROWS - NA_ROWS) * GRID_W, GRID_W)
        q = q_all[j * GRID_W:(j + 1) * GRID_W]
        k = kn_ref[pl.ds(start, _NA_KEYS), :]
        v = v_ref[pl.ds(start, _NA_KEYS), :]
        s = lax.dot_general(q, k, _NT, preferred_element_type=F32) + bias_ref[0, _na_pattern(r)]
        sc = sc_all[j * GRID_W:(j + 1) * GRID_W]
        m = jnp.maximum(jnp.max(s, axis=-1, keepdims=True), jnp.max(sc, axis=-1, keepdims=True))
        p = jnp.exp(s - m)
        ms.append(m)
        ls.append(jnp.sum(p, axis=-1, keepdims=True))
        os_.append(jnp.dot(p.astype(BF16), v, preferred_element_type=F32))
    m_all = jnp.concatenate(ms, axis=0)
    pc = jnp.exp(sc_all - m_all)
    l_all = jnp.concatenate(ls, axis=0) + jnp.sum(pc, axis=-1, keepdims=True)
    o = jnp.concatenate(os_, axis=0) + jnp.dot(pc.astype(BF16), vc, preferred_element_type=F32)
    o_ref[...] = (o / l_all).astype(o_ref.dtype)


def _na_attn(proj, bias, gq, gk):
    qrows = NA_RB * GRID_W
    nrb = ROWS // NA_RB
    cq, ck, cv = _pcol(3, NA_HD), _pcol(4, NA_HD), _pcol(5, NA_HD)
    return pl.pallas_call(
        _na_kernel,
        grid=(BATCH, NA_HEADS, nrb),
        in_specs=[pl.BlockSpec((qrows, NA_HD), lambda b, h, r: (b * nrb + r, cq + h)),
                  pl.BlockSpec((SEQ, NA_HD), lambda b, h, r: (b, ck + h)),
                  pl.BlockSpec((SEQ, NA_HD), lambda b, h, r: (b, cv + h)),
                  pl.BlockSpec((CTX_LEN, NA_HD), lambda b, h, r: (CTX_BLK + b, ck + h)),
                  pl.BlockSpec((CTX_LEN, NA_HD), lambda b, h, r: (CTX_BLK + b, cv + h)),
                  pl.BlockSpec((1, NA_ROWS, GRID_W, _NA_KEYS), lambda b, h, r: (h, 0, 0, 0)),
                  pl.BlockSpec((1, NA_HD), lambda b, h, r: (0, 0)),
                  pl.BlockSpec((1, NA_HD), lambda b, h, r: (0, 0))],
        out_specs=pl.BlockSpec((qrows, NA_HD), lambda b, h, r: (b * nrb + r, h)),
        out_shape=jax.ShapeDtypeStruct((N_LAT, NA_HEADS * NA_HD), BF16),
        scratch_shapes=[pltpu.VMEM((SEQ, NA_HD), BF16), pltpu.VMEM((CTX_LEN, NA_HD), BF16)],
        compiler_params=_cparams(("arbitrary", "arbitrary", "arbitrary")),
        name="na_attn",
    )(proj, proj, proj, proj, proj, bias, gq, gk)


def _na_bias_table(rpb):
    half = NA_ROWS // 2
    r_rep = np.array(list(range(half)) + [half] + list(range(ROWS - half + 1, ROWS)))
    start = np.clip(r_rep - half, 0, ROWS - NA_ROWS)
    dr = start[:, None] + np.arange(NA_ROWS)[None, :] - r_rep[:, None] + NA_ROWS - 1
    qc = np.arange(GRID_W)
    kcol = np.arange(GRID_W)
    col_start = np.clip(qc - NA_COLS // 2, 0, GRID_W - NA_COLS)
    in_win = (kcol[None, :] >= col_start[:, None]) & (kcol[None, :] < col_start[:, None] + NA_COLS)
    dc = np.clip(kcol[None, :] - qc[:, None], 1 - NA_COLS, NA_COLS - 1) + NA_COLS - 1
    rsel = (dr[:, :, None] == np.arange(2 * NA_ROWS - 1)).astype(np.float32)
    csel = (dc[:, :, None] == np.arange(2 * NA_COLS - 1)).astype(np.float32)
    b = jnp.einsum('pja,hab,qkb->hpqjk', rsel, rpb.astype(F32), csel,
                   precision=lax.Precision.HIGHEST)
    b = jnp.where(in_win[None, None, :, None, :], b.astype(F32), NEG_INF)
    return b.reshape(NA_HEADS, NA_ROWS, GRID_W, _NA_KEYS)


_GQA_G = GQA_HEADS // GQA_KV_HEADS
_GQA_BAND = 3 * GQA_WINDOW
_GQA_QW = GQA_HEADS * V7X_LANES


def _gqa_prep_kernel(q_ref, k_ref, gq_ref, gk_ref, c_ref, s1_ref, s2_ref, qd_ref, kn_ref):
    c = c_ref[...]
    s1 = s1_ref[...]
    s2 = s2_ref[...]
    lo = lax.broadcasted_iota(jnp.int32, (1, 128), 1) < GQA_HD

    def head_rms(x, g):
        x2 = x * x
        s_lo = jnp.sum(jnp.where(lo, x2, 0.0), axis=-1, keepdims=True)
        s_hi = jnp.sum(jnp.where(lo, 0.0, x2), axis=-1, keepdims=True)
        inv = 1.0 / GQA_HD
        r = jnp.where(lo, lax.rsqrt(s_lo * inv + EPS), lax.rsqrt(s_hi * inv + EPS))
        return x * r * g

    def rot(x):
        return x * c + pltpu.roll(x, 96, 1) * s1 + pltpu.roll(x, 32, 1) * s2

    gq = gq_ref[...]
    for j in range(GQA_HEADS // 2):
        y = rot(head_rms(q_ref[:, j * 128:(j + 1) * 128].astype(F32), gq))
        sw = pltpu.roll(y, 64, 1)
        hk = (2 * j) // _GQA_G
        if hk == 0:
            even, odd = jnp.where(lo, y, 0.0), jnp.where(lo, sw, 0.0)
        else:
            even, odd = jnp.where(lo, 0.0, sw), jnp.where(lo, 0.0, y)
        qd_ref[:, (2 * j) * 128:(2 * j + 1) * 128] = even.astype(BF16)
        qd_ref[:, (2 * j + 1) * 128:(2 * j + 2) * 128] = odd.astype(BF16)
    kn_ref[...] = rot(head_rms(k_ref[...].astype(F32), gk_ref[...])).astype(BF16)


def _gqa_prep(proj, p, tabs):
    tm = TOK_TM
    gq = (jnp.tile(p['gqa_qn_g'], 2) * (GQA_HD ** -0.5)).reshape(1, 128)
    gk = jnp.tile(p['gqa_kn_g'], 2).reshape(1, 128)
    row = pl.BlockSpec((tm, 128), lambda i: (i, 0))
    vec = pl.BlockSpec((1, 128), lambda i: (0, 0))
    return pl.pallas_call(
        _gqa_prep_kernel,
        grid=(N_TOK // tm,),
        in_specs=[pl.BlockSpec((tm, GQA_HEADS * GQA_HD), lambda i: (i, _pcol(6, GQA_HEADS * GQA_HD))),
                  pl.BlockSpec((tm, 128), lambda i: (i, _pcol(7, 128))),
                  vec, vec, row, row, row],
        out_specs=[pl.BlockSpec((tm, _GQA_QW), lambda i: (i, 0)), row],
        out_shape=[jax.ShapeDtypeStruct((N_TOK, _GQA_QW), BF16),
                   jax.ShapeDtypeStruct((N_TOK, 128), BF16)],
        compiler_params=_cparams(("arbitrary",)),
        name="gqa_prep",
    )(proj, proj, gq, gk, tabs['gqa_c'], tabs['gqa_s1'], tabs['gqa_s2'])


def _gqa_kernel(sink_ref, q_ref, k_ref, v_ref, kc_ref, vc_ref, o_ref):
    hk = pl.program_id(1)
    n = pl.program_id(2)
    w = GQA_WINDOW
    start = pl.multiple_of(jnp.clip((n - 1) * w, 0, SEQ - _GQA_BAND), w)
    q = jnp.concatenate([q_ref[:, g * 128:(g + 1) * 128] for g in range(_GQA_G)], axis=0)
    k = k_ref[pl.ds(start, _GQA_BAND), :]
    v = v_ref[pl.ds(start, _GQA_BAND), :]
    s = lax.dot_general(q, k, _NT, preferred_element_type=F32)
    rows = lax.broadcasted_iota(jnp.int32, s.shape, 0)
    cols = lax.broadcasted_iota(jnp.int32, s.shape, 1)
    qpos = n * w + (rows & (w - 1))
    kpos = start + cols
    s = jnp.where(jnp.abs(kpos - qpos) <= GQA_WINDOW, s, NEG_INF)
    sc = lax.dot_general(q, kc_ref[...], _NT, preferred_element_type=F32)
    grow = lax.broadcasted_iota(jnp.int32, (_GQA_G * w, 1), 0) // w
    snk = jnp.full((_GQA_G * w, 1), sink_ref[hk * _GQA_G], F32)
    for g in range(1, _GQA_G):
        snk = jnp.where(grow == g, sink_ref[hk * _GQA_G + g], snk)
    m = jnp.maximum(jnp.maximum(jnp.max(s, axis=-1, keepdims=True),
                                jnp.max(sc, axis=-1, keepdims=True)), snk)
    p = jnp.exp(s - m)
    pc = jnp.exp(sc - m)
    l = jnp.sum(p, axis=-1, keepdims=True) + jnp.sum(pc, axis=-1, keepdims=True) + jnp.exp(snk - m)
    o = (jnp.dot(p.astype(BF16), v, preferred_element_type=F32)
         + jnp.dot(pc.astype(BF16), vc_ref[...], preferred_element_type=F32)) / l
    half = lax.broadcasted_iota(jnp.int32, (1, 128), 1) // GQA_HD
    o = jnp.where(half == hk, o, 0.0).astype(o_ref.dtype)
    for g in range(_GQA_G):
        o_ref[:, g * 128:(g + 1) * 128] = o[g * w:(g + 1) * w]


def _gqa_attn(qd, kn, proj, sink):
    nb = SEQ // GQA_WINDOW
    qw = _GQA_G * 128
    cv = _pcol(8, 128)
    gs = pltpu.PrefetchScalarGridSpec(
        num_scalar_prefetch=1,
        grid=(BATCH, GQA_KV_HEADS, nb),
        in_specs=[pl.BlockSpec((GQA_WINDOW, qw), lambda b, h, n, *_: (b * nb + n, h)),
                  pl.BlockSpec((SEQ, 128), lambda b, h, n, *_: (b, 0)),
                  pl.BlockSpec((SEQ, 128), lambda b, h, n, *_: (b, cv)),
                  pl.BlockSpec((CTX_LEN, 128), lambda b, h, n, *_: (CTX_BLK + b, 0)),
                  pl.BlockSpec((CTX_LEN, 128), lambda b, h, n, *_: (CTX_BLK + b, cv))],
        out_specs=pl.BlockSpec((GQA_WINDOW, qw), lambda b, h, n, *_: (b * nb + n, h)),
    )
    return pl.pallas_call(
        _gqa_kernel,
        grid_spec=gs,
        out_shape=jax.ShapeDtypeStruct((N_LAT, _GQA_QW), BF16),
        compiler_params=_cparams(("arbitrary", "arbitrary", "arbitrary")),
        name="gqa_attn",
    )(sink.astype(F32), qd, kn, proj, kn, proj)


def _ctx_kernel(sink_ref, mq_ref, mk_ref, mvt_ref, nq_ref, nk_ref, nv_ref, gnq_ref, gnk_ref,
                gq_ref, gk_ref, gv_ref, oa_ref, ob_ref, oc_ref):
    for h in range(MLA_HEADS):
        st = lax.dot_general(mk_ref[h], mq_ref[h], _NT, preferred_element_type=F32)
        p = jnp.exp2(st - jnp.max(st, axis=0, keepdims=True))
        l = jnp.sum(p, axis=0, keepdims=True)
        ot = jnp.dot(mvt_ref[h], p.astype(BF16), preferred_element_type=F32) / l
        oa_ref[:, h * MLA_V:(h + 1) * MLA_V] = ot.T.astype(oa_ref.dtype)
    for h in range(NA_HEADS):
        sl = slice(h * NA_HD, (h + 1) * NA_HD)
        q = _row_rms(nq_ref[:, sl].astype(F32), gnq_ref[...]).astype(BF16)
        k = _row_rms(nk_ref[:, sl].astype(F32), gnk_ref[...]).astype(BF16)
        s = lax.dot_general(q, k, _NT, preferred_element_type=F32)
        p = jnp.exp(s - jnp.max(s, axis=-1, keepdims=True))
        l = jnp.sum(p, axis=-1, keepdims=True)
        o = jnp.dot(p.astype(BF16), nv_ref[:, sl], preferred_element_type=F32) / l
        ob_ref[:, sl] = o.astype(ob_ref.dtype)
    half = lax.broadcasted_iota(jnp.int32, (1, 128), 1) // GQA_HD
    k = gk_ref[...]
    v = gv_ref[...]
    for h in range(GQA_HEADS):
        sl = slice(h * 128, (h + 1) * 128)
        s = lax.dot_general(gq_ref[:, sl], k, _NT, preferred_element_type=F32)
        snk = sink_ref[h]
        m = jnp.maximum(jnp.max(s, axis=-1, keepdims=True), snk)
        p = jnp.exp(s - m)
        l = jnp.sum(p, axis=-1, keepdims=True) + jnp.exp(snk - m)
        o = jnp.dot(p.astype(BF16), v, preferred_element_type=F32) / l
        oc_ref[:, sl] = jnp.where(half == h // _GQA_G, o, 0.0).astype(oc_ref.dtype)


def _ctx_attn(sink, mq, mk, mvt, proj, gnq, gnk, qd, kn):
    c = CTX_LEN
    nwid = NA_HEADS * NA_HD

    def row(width, col=0):
        return pl.BlockSpec((c, width), lambda b, *_: (CTX_BLK + b, col))

    def out(width):
        return pl.BlockSpec((c, width), lambda b, *_: (b, 0))

    vec = pl.BlockSpec((1, NA_HD), lambda b, *_: (0, 0))
    gs = pltpu.PrefetchScalarGridSpec(
        num_scalar_prefetch=1,
        grid=(BATCH,),
        in_specs=[pl.BlockSpec((MLA_HEADS, c, _MLA_PAD), lambda b, *_: (0, CTX_BLK + b, 0)),
                  pl.BlockSpec((MLA_HEADS, c, _MLA_PAD), lambda b, *_: (0, CTX_BLK + b, 0)),
                  pl.BlockSpec((MLA_HEADS, MLA_V, c), lambda b, *_: (0, 0, CTX_BLK + b)),
                  row(nwid, _pcol(3, nwid)), row(nwid, _pcol(4, nwid)), row(nwid, _pcol(5, nwid)),
                  vec, vec,
                  row(_GQA_QW), row(128), row(128, _pcol(8, 128))],
        out_specs=[out(MLA_HEADS * MLA_V), out(nwid), out(_GQA_QW)],
    )
    return pl.pallas_call(
        _ctx_kernel,
        grid_spec=gs,
        out_shape=[jax.ShapeDtypeStruct((N_CTX, MLA_HEADS * MLA_V), BF16),
                   jax.ShapeDtypeStruct((N_CTX, nwid), BF16),
                   jax.ShapeDtypeStruct((N_CTX, _GQA_QW), BF16)],
        compiler_params=_cparams(("arbitrary",)),
        name="ctx_attn",
    )(sink.astype(F32), mq, mk, mvt, proj, proj, proj, gnq, gnk, qd, kn, proj)


def _merge_kernel(*refs, has_ctx, n_lat_tiles):
    ga_ref, gb_ref, gc_ref, oa_ref, ob_ref, oc_ref = refs[:6]
    refs = refs[6:]
    if has_ctx:
        ca_ref, cb_ref, cc_ref = refs[:3]
        refs = refs[3:]
    wa_ref, wb_ref, wc_ref, y_ref = refs
    is_ctx = pl.program_id(0) >= n_lat_tiles

    def branch(g_ref, o_ref, c_ref, w_ref):
        o = o_ref[...]
        if has_ctx:
            o = jnp.where(is_ctx, c_ref[...], o)
        return jax.nn.sigmoid(g_ref[...].astype(F32)) * jnp.dot(o, w_ref[...], preferred_element_type=F32)

    y = (branch(ga_ref, oa_ref, ca_ref if has_ctx else None, wa_ref)
         + branch(gb_ref, ob_ref, cb_ref if has_ctx else None, wb_ref)
         + branch(gc_ref, oc_ref, cc_ref if has_ctx else None, wc_ref))
    y_ref[...] = y.astype(y_ref.dtype)


def _merge(proj, lat, ctx, wa, wb, wc):
    tm = TOK_TM
    has_ctx = ctx is not None
    m = N_TOK if has_ctx else N_LAT
    nl = N_LAT // tm
    ks = [o.shape[1] for o in lat]
    in_specs = [pl.BlockSpec((tm, D_MODEL), lambda i: (i, 0)),
                pl.BlockSpec((tm, D_MODEL), lambda i: (i, 1)),
                pl.BlockSpec((tm, D_MODEL), lambda i: (i, 2))]
    in_specs += [pl.BlockSpec((tm, k), lambda i: (jnp.minimum(i, nl - 1), 0)) for k in ks]
    args = [proj, proj, proj] + list(lat)
    if has_ctx:
        in_specs += [pl.BlockSpec((tm, k), lambda i: (jnp.maximum(i - nl, 0), 0)) for k in ks]
        args += list(ctx)
    in_specs += [pl.BlockSpec((k, D_MODEL), lambda i: (0, 0)) for k in ks]
    args += [wa, wb, wc]
    return pl.pallas_call(
        functools.partial(_merge_kernel, has_ctx=has_ctx, n_lat_tiles=nl),
        grid=(m // tm,),
        in_specs=in_specs,
        out_specs=pl.BlockSpec((tm, D_MODEL), lambda i: (i, 0)),
        out_shape=jax.ShapeDtypeStruct((m, D_MODEL), BF16),
        compiler_params=_cparams(("arbitrary",)),
        name="merge",
    )(*args)


def _moe_ffn_kernel(te_ref, tok_ref, nu_ref, en_ref, es_ref, h_hbm, wg_hbm, wu_hbm, wd_hbm, y_ref,
                    xbuf, sem, wgs, wus, wds, wsem, wgb, wub, wdb, *, layer):
    i = pl.program_id(0)
    tm = MOE_TM
    slot = i % 2
    n_used = nu_ref[0]

    def row_copy(tok, s, r):
        return pltpu.make_async_copy(h_hbm.at[pl.ds(tok, 1)], xbuf.at[s, pl.ds(r, 1)], sem.at[s])

    def start_gather(tile, s):
        base = tile * tm

        def body(r, c):
            row_copy(tok_ref[base + r], s, r).start()
            return c

        lax.fori_loop(0, tm, body, 0, unroll=8)

    def weight_copies(e, s):
        return (pltpu.make_async_copy(wg_hbm.at[layer, e], wgs.at[s], wsem.at[s]),
                pltpu.make_async_copy(wu_hbm.at[layer, e], wus.at[s], wsem.at[s]),
                pltpu.make_async_copy(wd_hbm.at[layer, e], wds.at[s], wsem.at[s]))

    @pl.when(i == 0)
    def _():
        for cp in weight_copies(te_ref[0], es_ref[0]):
            cp.start(priority=1)
        start_gather(0, 0)

    @pl.when(i + 1 < n_used)
    def _():
        start_gather(i + 1, 1 - slot)

    @pl.when(i < n_used)
    def _():
        @pl.when((i == 0) | (te_ref[i] != te_ref[jnp.maximum(i - 1, 0)]))
        def _():
            s = es_ref[i]

            @pl.when(en_ref[i] >= 0)
            def _():
                for cp in weight_copies(en_ref[i], 1 - s):
                    cp.start(priority=1)

            for cp in weight_copies(te_ref[i], s):
                cp.wait()
            wgb[...] = wgs[s].astype(BF16)
            wub[...] = wus[s].astype(BF16)
            wdb[...] = wds[s].astype(BF16)

        pltpu.make_async_copy(h_hbm.at[pl.ds(0, tm)], xbuf.at[slot], sem.at[slot]).wait()
        x = xbuf[slot].astype(BF16)
        hg = jnp.dot(x, wgb[...], preferred_element_type=F32)
        hu = jnp.dot(x, wub[...], preferred_element_type=F32)
        act = (hg * jax.nn.sigmoid(hg)) * hu
        y_ref[...] = jnp.dot(act.astype(BF16), wdb[...], preferred_element_type=F32)

    @pl.when(i >= n_used)
    def _():
        y_ref[...] = jnp.zeros_like(y_ref)


def _moe_ffn(h, plan, wg, wu, wd, layer):
    tile_expert, slot_token, n_used, next_expert, expert_slot = plan
    p = slot_token.shape[0]
    tm = MOE_TM
    nt = p // tm
    anyspec = pl.BlockSpec(memory_space=pl.ANY)
    gs = pltpu.PrefetchScalarGridSpec(
        num_scalar_prefetch=5,
        grid=(nt,),
        in_specs=[anyspec, anyspec, anyspec, anyspec],
        out_specs=pl.BlockSpec((tm, D_MODEL), lambda i, *_: (i, 0)),
        scratch_shapes=[pltpu.VMEM((2, tm, D_MODEL), F32),
                        pltpu.SemaphoreType.DMA((2,)),
                        pltpu.VMEM((2, D_MODEL, MOE_HIDDEN), F32),
                        pltpu.VMEM((2, D_MODEL, MOE_HIDDEN), F32),
                        pltpu.VMEM((2, MOE_HIDDEN, D_MODEL), F32),
                        pltpu.SemaphoreType.DMA((2,)),
                        pltpu.VMEM((D_MODEL, MOE_HIDDEN), BF16),
                        pltpu.VMEM((D_MODEL, MOE_HIDDEN), BF16),
                        pltpu.VMEM((MOE_HIDDEN, D_MODEL), BF16)],
    )
    return pl.pallas_call(
        functools.partial(_moe_ffn_kernel, layer=layer),
        grid_spec=gs,
        out_shape=jax.ShapeDtypeStruct((p, D_MODEL), F32),
        compiler_params=_cparams(("arbitrary",)),
        name="moe_ffn",
    )(tile_expert, slot_token, n_used, next_expert, expert_slot, h, wg, wu, wd)


def _moe_combine_kernel(pos_ref, y_hbm, x_ref, w_ref, g_ref, o_ref, ybuf, sem):
    i = pl.program_id(0)
    nt = pl.num_programs(0)
    tm = TOK_TM
    slot = i % 2

    def row_copy(src, s, r):
        return pltpu.make_async_copy(y_hbm.at[pl.ds(src, 1)], ybuf.at[s, pl.ds(r, 1)], sem.at[s])

    def start_gather(tile, s):
        base = tile * tm

        def body(r, c):
            row_copy(pos_ref[2 * (base + r)], s, r).start(priority=0)
            row_copy(pos_ref[2 * (base + r) + 1], s, tm + r).start(priority=1)
            return c

        lax.fori_loop(0, tm, body, 0, unroll=4)

    @pl.when(i == 0)
    def _():
        start_gather(0, 0)

    @pl.when(i + 1 < nt)
    def _():
        start_gather(i + 1, 1 - slot)

    pltpu.make_async_copy(y_hbm.at[pl.ds(0, 2 * tm)], ybuf.at[slot], sem.at[slot]).wait()
    w = w_ref[...]
    y = ybuf[slot, pl.ds(0, tm), :] * w[:, 0:1] + ybuf[slot, pl.ds(tm, tm), :] * w[:, 1:2]
    o_ref[...] = x_ref[...] + g_ref[0] * y


def _moe_combine(y, pos, w_sel, x, mod, k_gate):
    m = x.shape[0]
    tm = TOK_TM
    gs = pltpu.PrefetchScalarGridSpec(
        num_scalar_prefetch=1,
        grid=(m // tm,),
        in_specs=[pl.BlockSpec(memory_space=pl.ANY),
                  pl.BlockSpec((tm, D_MODEL), lambda i, pos: (i, 0)),
                  pl.BlockSpec((tm, MOE_TOPK), lambda i, pos: (i, 0)),
                  pl.BlockSpec((1, 1, D_MODEL), lambda i, pos: (_seg(i, tm) * 6 + k_gate, 0, 0))],
        out_specs=pl.BlockSpec((tm, D_MODEL), lambda i, pos: (i, 0)),
        scratch_shapes=[pltpu.VMEM((2, 2 * tm, D_MODEL), F32),
                        pltpu.SemaphoreType.DMA((2,))],
    )
    return pl.pallas_call(
        _moe_combine_kernel,
        grid_spec=gs,
        out_shape=jax.ShapeDtypeStruct((m, D_MODEL), F32),
        compiler_params=_cparams(("arbitrary",)),
        name="moe_combine",
    )(pos, y, x, w_sel, mod)


def _route(logits, m):
    tm = MOE_TM
    gp = jax.nn.softmax(logits[:, :MOE_GROUPS], axis=-1)
    g_idx = jnp.argmax(gp, axis=-1).astype(jnp.int32)[:, None]
    g_w = jnp.max(gp, axis=-1, keepdims=True)
    el = logits[:, MOE_GROUPS:MOE_GROUPS + MOE_EXPERTS].reshape(m, MOE_GROUPS, MOE_PER_GROUP)
    g_onehot = (g_idx == jnp.arange(MOE_GROUPS, dtype=jnp.int32)[None, :]).astype(F32)
    el_g = jnp.sum(el * g_onehot[:, :, None], axis=1)
    i0 = jnp.argmax(el_g, axis=-1).astype(jnp.int32)[:, None]
    l0 = jnp.max(el_g, axis=-1, keepdims=True)
    rest = jnp.where(jnp.arange(MOE_PER_GROUP, dtype=jnp.int32)[None, :] == i0, -jnp.inf, el_g)
    i1 = jnp.argmax(rest, axis=-1).astype(jnp.int32)[:, None]
    l1 = jnp.max(rest, axis=-1, keepdims=True)
    top_l = jnp.concatenate([l0, l1], axis=-1)
    top_i = jnp.concatenate([i0, i1], axis=-1)
    w_sel = jax.nn.softmax(top_l, axis=-1) * g_w
    eid = (g_idx * MOE_PER_GROUP + top_i).astype(jnp.int32)

    a = m * MOE_TOPK
    e_flat = eid.reshape(a)
    onehot = (e_flat[:, None] == jnp.arange(MOE_EXPERTS, dtype=jnp.int32)[None, :]).astype(jnp.int32)
    csum = jnp.cumsum(onehot, axis=0)
    rank = jnp.sum(csum * onehot, axis=1) - 1
    counts = csum[-1]
    padded = ((counts + tm - 1) // tm) * tm
    ends = jnp.cumsum(padded)
    starts = ends - padded
    pos = (jnp.sum(onehot * starts[None, :], axis=1) + rank).astype(jnp.int32)
    p = a + MOE_EXPERTS * tm
    slot_token = jnp.zeros((p,), jnp.int32).at[pos].set(jnp.arange(a, dtype=jnp.int32) // MOE_TOPK)
    n_used = (ends[-1] // tm).astype(jnp.int32).reshape(1)
    tile_start = jnp.arange(p // tm, dtype=jnp.int32) * tm
    last_e = jnp.max(jnp.where(counts > 0, jnp.arange(MOE_EXPERTS, dtype=jnp.int32), 0))
    tile_expert = jnp.minimum(
        jnp.sum((ends[None, :] <= tile_start[:, None]).astype(jnp.int32), axis=1), last_e)
    eidx = jnp.arange(MOE_EXPERTS, dtype=jnp.int32)
    used = counts > 0
    later = used[None, :] & (eidx[None, :] > eidx[:, None])
    next_used = jnp.min(jnp.where(later, eidx[None, :], MOE_EXPERTS), axis=1)
    next_used = jnp.where(next_used == MOE_EXPERTS, -1, next_used).astype(jnp.int32)
    ordinal = (jnp.cumsum(used.astype(jnp.int32)) - 1) % 2
    t_onehot = (tile_expert[:, None] == eidx[None, :]).astype(jnp.int32)
    next_expert = jnp.sum(t_onehot * next_used[None, :], axis=1).astype(jnp.int32)
    expert_slot = jnp.sum(t_onehot * ordinal[None, :], axis=1).astype(jnp.int32)
    return (tile_expert, slot_token, n_used, next_expert, expert_slot), w_sel, pos


def _rope_angles(rot_dim):
    t = jnp.arange(SEQ)
    row = (t // GRID_W).astype(F32)
    col = (t % GRID_W).astype(F32)
    n_freq = rot_dim // 4
    inv = ROPE_THETA ** (-jnp.arange(n_freq, dtype=F32) / n_freq)
    ang = jnp.concatenate([row[:, None] * inv, col[:, None] * inv], axis=-1)
    return jnp.cos(ang), jnp.sin(ang)


def _rope_tables():
    def rows(lat, ctx_val):
        ctx = jnp.full((N_CTX, 128), 0.0, F32) + ctx_val
        return jnp.concatenate([lat, lat, ctx], axis=0)

    z32 = jnp.zeros((SEQ, 32), F32)
    z64 = jnp.zeros((SEQ, 64), F32)
    lane = np.arange(128)
    cm, sm = _rope_angles(MLA_ROPE)
    mla_c = rows(jnp.concatenate([cm, cm, z64], axis=1), jnp.asarray((lane < 64).astype(np.float32)))
    mla_s = rows(jnp.concatenate([-sm, sm, z64], axis=1), 0.0)
    cg, sg = _rope_angles(GQA_HD)
    gqa_c = rows(jnp.concatenate([cg, cg, cg, cg], axis=1), 1.0)
    gqa_s1 = rows(jnp.concatenate([-sg, z32, -sg, z32], axis=1), 0.0)
    gqa_s2 = rows(jnp.concatenate([z32, sg, z32, sg], axis=1), 0.0)
    return {'mla_c': mla_c, 'mla_s': mla_s, 'gqa_c': gqa_c, 'gqa_s1': gqa_s1, 'gqa_s2': gqa_s2}


def _pack_w_in(w):
    parts = jnp.split(w, IN_SPLITS, axis=1)
    pad = jnp.zeros((D_MODEL, _PACK_COLS - sum(IN_SIZES)), w.dtype)
    return jnp.concatenate([parts[i] for i in _PACK_ORDER] + [pad], axis=1).astype(BF16)


def _pad_w_o_gqa(w):
    w4 = w.reshape(GQA_KV_HEADS, _GQA_G, GQA_HD, D_MODEL)
    z = jnp.zeros_like(w4[0:1])
    halves = [jnp.concatenate([w4[hk:hk + 1] if hk == half else z for hk in range(GQA_KV_HEADS)], axis=0)
              for half in range(GQA_KV_HEADS)]
    return jnp.stack(halves, axis=2).reshape(_GQA_QW, D_MODEL).astype(BF16)


def _token_mixer(h, p, tabs, ctx_out):
    proj = _mm(h, _pack_w_in(p['w_in']), BF16, _PACK_COLS // 4)
    mq, mk, mvt = _mla_prep(proj, p, tabs)
    oa = _mla_attn(mq, mk, mvt, tq=512, tk=1024)
    gnq = (p['na_qn_g'] * (NA_HD ** -0.5)).reshape(1, NA_HD)
    gnk = p['na_kn_g'].reshape(1, NA_HD)
    ob = _na_attn(proj, _na_bias_table(p['na_rpb']), gnq, gnk)
    qd, kn = _gqa_prep(proj, p, tabs)
    oc = _gqa_attn(qd, kn, proj, p['gqa_sink'])
    ctx = _ctx_attn(p['gqa_sink'], mq, mk, mvt, proj, gnq, gnk, qd, kn) if ctx_out else None
    return _merge(proj, (oa, ob, oc), ctx, p['w_o_mla'].astype(BF16), p['w_o_na'].astype(BF16),
                  _pad_w_o_gqa(p['w_o_gqa']))


def _moe(x, mod, norm_g, p):
    m = x.shape[0]
    wr = jnp.concatenate([p['moe_w_group'], p['moe_w_expert'],
                          jnp.zeros((D_MODEL, ROUTE_COLS - MOE_GROUPS - MOE_EXPERTS), F32)], axis=1)
    br = jnp.concatenate([p['moe_b_group'], p['moe_b_expert'],
                          jnp.zeros((ROUTE_COLS - MOE_GROUPS - MOE_EXPERTS,), F32)]).reshape(1, ROUTE_COLS)
    h, logits = _norm_mod(x, norm_g, mod, 3, 4, route=(wr, br))
    plan, w_sel, pos = _route(logits, m)
    y = _moe_ffn(h, plan, p['moe_w_gate'], p['moe_w_up'], p['moe_w_down'], p['layer'])
    return _moe_combine(y, pos, w_sel, x, mod, 5)


def kernel(x, c, ctx, c_ctx, ada_w, ada_b, norm_mix_g, norm_ffn_g, w_in,
           mla_q_norm_g, mla_w_uq, mla_kv_norm_g, mla_w_ukv, mla_qn_g, mla_kn_g,
           na_qn_g, na_kn_g, na_rpb, gqa_qn_g, gqa_kn_g, gqa_sink,
           w_o_mla, w_o_na, w_o_gqa, w_out,
           moe_w_group, moe_b_group, moe_w_expert, moe_b_expert,
           moe_w_gate, moe_w_up, moe_w_down):
    xt = jnp.concatenate([x.reshape(N_LAT, D_MODEL), ctx.reshape(N_CTX, D_MODEL)], axis=0)
    c_rows = jnp.concatenate([c, c_ctx[None, :], jnp.zeros((8 - BATCH - 1, D_MODEL), F32)], axis=0)
    mod_all = _ada(c_rows, ada_w, ada_b)
    tabs = _rope_tables()
    for l in range(DEPTH):
        ctx_out = l < DEPTH - 1
        p = {
            'w_in': w_in[l], 'mla_q_norm_g': mla_q_norm_g[l], 'mla_w_uq': mla_w_uq[l],
            'mla_kv_norm_g': mla_kv_norm_g[l], 'mla_w_ukv': mla_w_ukv[l],
            'mla_qn_g': mla_qn_g[l], 'mla_kn_g': mla_kn_g[l],
            'na_qn_g': na_qn_g[l], 'na_kn_g': na_kn_g[l], 'na_rpb': na_rpb[l],
            'gqa_qn_g': gqa_qn_g[l], 'gqa_kn_g': gqa_kn_g[l], 'gqa_sink': gqa_sink[l],
            'w_o_mla': w_o_mla[l], 'w_o_na': w_o_na[l], 'w_o_gqa': w_o_gqa[l],
            'moe_w_group': moe_w_group[l], 'moe_b_group': moe_b_group[l],
            'moe_w_expert': moe_w_expert[l], 'moe_b_expert': moe_b_expert[l],
            'moe_w_gate': moe_w_gate, 'moe_w_up': moe_w_up, 'moe_w_down': moe_w_down, 'layer': l,
        }
        mod = mod_all[l].reshape(8 * 6, 1, D_MODEL)
        h = _norm_mod(xt, norm_mix_g[l], mod, 0, 1)
        y = _token_mixer(h, p, tabs, ctx_out)
        m = y.shape[0]
        x_mid = _mm_res(y, w_out, l, xt, mod, 2, 1024)
        xt = _moe(x_mid, mod, norm_ffn_g[l], p)
    return xt[:N_LAT].reshape(BATCH, SEQ, D_MODEL)
```

```python
import functools

import numpy as np
import jax
import jax.numpy as jnp
from jax import lax
from jax.experimental import pallas as pl
from jax.experimental.pallas import tpu as pltpu

D_MODEL = 2048
BATCH = 2
SEQ = 4096
DEPTH = 2
GRID_W = 64
CTX_LEN = 256
EPS = 1e-6
ROPE_THETA = 10000.0
NEG_INF = -1e30

MLA_HEADS = 8
MLA_Q_RANK = 512
MLA_KV_RANK = 512
MLA_NOPE = 128
MLA_ROPE = 64
MLA_QK = MLA_NOPE + MLA_ROPE
MLA_V = 128
NA_HEADS = 4
NA_HD = 128
NA_ROWS = 8
NA_COLS = 16
GQA_HEADS = 8
GQA_KV_HEADS = 2
GQA_HD = 64
GQA_WINDOW = 128
MOE_GROUPS = 4
MOE_PER_GROUP = 8
MOE_EXPERTS = MOE_GROUPS * MOE_PER_GROUP
MOE_TOPK = 2
MOE_HIDDEN = 512

IN_SIZES = (MLA_Q_RANK, MLA_KV_RANK, MLA_ROPE,
            NA_HEADS * NA_HD, NA_HEADS * NA_HD, NA_HEADS * NA_HD,
            GQA_HEADS * GQA_HD, GQA_KV_HEADS * GQA_HD, GQA_KV_HEADS * GQA_HD,
            D_MODEL, D_MODEL, D_MODEL)
IN_SPLITS = tuple(int(s) for s in np.cumsum(IN_SIZES)[:-1])

N_LAT = BATCH * SEQ
N_CTX = BATCH * CTX_LEN
N_TOK = N_LAT + N_CTX
ROWS = SEQ // GRID_W

V7X_LANES = 128
V7X_VMEM_LIMIT = 56 * 1024 * 1024

_PACK_ORDER = (9, 10, 11, 0, 1, 3, 4, 5, 6, 7, 8, 2)
_PACK_COLS = 9728
_PACK_OFF = {}
_off = 0
for _i in _PACK_ORDER:
    _PACK_OFF[_i] = _off
    _off += IN_SIZES[_i]

ROUTE_COLS = V7X_LANES
MOE_TM = 256
TOK_TM = 256
MM_TM = 512
CTX_BLK = N_LAT // CTX_LEN

F32 = jnp.float32
BF16 = jnp.bfloat16
_LOG2E = 1.4426950408889634
_NT = (((1,), (1,)), ((), ()))


def _cparams(sem):
    return pltpu.CompilerParams(dimension_semantics=sem, vmem_limit_bytes=V7X_VMEM_LIMIT)


def _seg(i, tm):
    return jnp.minimum(i // (SEQ // tm), 2)


def _pcol(idx, width):
    assert _PACK_OFF[idx] % width == 0
    return _PACK_OFF[idx] // width


def _ada_kernel(c_ref, w_ref, b_ref, o_ref):
    c = c_ref[...]
    a = c * jax.nn.sigmoid(c)
    o_ref[0] = jnp.dot(a, w_ref[0], preferred_element_type=F32,
                       precision=lax.Precision.HIGHEST) + b_ref[0]


def _ada(c_rows, ada_w, ada_b):
    tn = 1024
    n = 6 * D_MODEL
    return pl.pallas_call(
        _ada_kernel,
        grid=(DEPTH, n // tn),
        in_specs=[
            pl.BlockSpec((8, D_MODEL), lambda l, j: (0, 0)),
            pl.BlockSpec((1, D_MODEL, tn), lambda l, j: (l, 0, j)),
            pl.BlockSpec((1, 1, tn), lambda l, j: (l, 0, j)),
        ],
        out_specs=pl.BlockSpec((1, 8, tn), lambda l, j: (l, 0, j)),
        out_shape=jax.ShapeDtypeStruct((DEPTH, 8, n), F32),
        compiler_params=_cparams(("arbitrary", "arbitrary")),
        name="ada",
    )(c_rows, ada_w, ada_b.reshape(DEPTH, 1, n))


def _norm_mod_kernel(x_ref, g_ref, sh_ref, sc_ref, h_ref):
    x = x_ref[...]
    xn = x * lax.rsqrt(jnp.mean(x * x, axis=-1, keepdims=True) + EPS) * g_ref[...]
    h_ref[...] = (xn * (1.0 + sc_ref[0]) + sh_ref[0]).astype(h_ref.dtype)


def _norm_mod(x, g, mod, k_shift, k_scale):
    m = x.shape[0]
    tm = TOK_TM
    return pl.pallas_call(
        _norm_mod_kernel,
        grid=(m // tm,),
        in_specs=[pl.BlockSpec((tm, D_MODEL), lambda i: (i, 0)),
                  pl.BlockSpec((1, D_MODEL), lambda i: (0, 0)),
                  pl.BlockSpec((1, 1, D_MODEL), lambda i: (_seg(i, tm) * 6 + k_shift, 0, 0)),
                  pl.BlockSpec((1, 1, D_MODEL), lambda i: (_seg(i, tm) * 6 + k_scale, 0, 0))],
        out_specs=pl.BlockSpec((tm, D_MODEL), lambda i: (i, 0)),
        out_shape=jax.ShapeDtypeStruct((m, D_MODEL), BF16),
        compiler_params=_cparams(("arbitrary",)),
        name="norm_mod",
    )(x, g.reshape(1, D_MODEL), mod, mod)


def _out_route_kernel(y_ref, w_ref, r_ref, g1_ref, gn_ref, sh_ref, sc_ref, wr_ref, br_ref,
                      xo_ref, h_ref, lg_ref):
    acc = jnp.dot(y_ref[...], w_ref[...], preferred_element_type=F32)
    x = r_ref[...] + g1_ref[0] * acc
    xo_ref[...] = x
    xn = x * lax.rsqrt(jnp.mean(x * x, axis=-1, keepdims=True) + EPS) * gn_ref[...]
    h = xn * (1.0 + sc_ref[0]) + sh_ref[0]
    h_ref[...] = h
    hi = h.astype(BF16)
    lo = (h - hi.astype(F32)).astype(BF16)
    wr = wr_ref[...]
    w_hi = wr.astype(BF16)
    w_lo = (wr - w_hi.astype(F32)).astype(BF16)
    lg_ref[...] = (jnp.dot(hi, w_hi, preferred_element_type=F32)
                   + jnp.dot(hi, w_lo, preferred_element_type=F32)
                   + jnp.dot(lo, w_hi, preferred_element_type=F32) + br_ref[...])


def _out_route(y, w_out_bf16, res, mod, norm_g, wr, br):
    m = y.shape[0]
    tm = TOK_TM

    def modrow(k):
        return pl.BlockSpec((1, 1, D_MODEL), lambda i: (_seg(i, tm) * 6 + k, 0, 0))

    row = pl.BlockSpec((tm, D_MODEL), lambda i: (i, 0))
    return pl.pallas_call(
        _out_route_kernel,
        grid=(m // tm,),
        in_specs=[row,
                  pl.BlockSpec((D_MODEL, D_MODEL), lambda i: (0, 0)),
                  row, modrow(2),
                  pl.BlockSpec((1, D_MODEL), lambda i: (0, 0)),
                  modrow(3), modrow(4),
                  pl.BlockSpec((D_MODEL, ROUTE_COLS), lambda i: (0, 0)),
                  pl.BlockSpec((1, ROUTE_COLS), lambda i: (0, 0))],
        out_specs=[row, row, pl.BlockSpec((tm, ROUTE_COLS), lambda i: (i, 0))],
        out_shape=[jax.ShapeDtypeStruct((m, D_MODEL), F32),
                   jax.ShapeDtypeStruct((m, D_MODEL), F32),
                   jax.ShapeDtypeStruct((m, ROUTE_COLS), F32)],
        compiler_params=_cparams(("arbitrary",)),
        name="out_route",
    )(y, w_out_bf16, res, mod, norm_g.reshape(1, D_MODEL), mod, mod, wr, br)


def _mm_bf16_kernel(x_ref, w_ref, o_ref):
    o_ref[...] = jnp.dot(x_ref[...], w_ref[...], preferred_element_type=F32).astype(o_ref.dtype)


def _mm(x, w, out_dtype, tn):
    m, k = x.shape
    n = w.shape[1]
    tm = MM_TM
    assert x.dtype == BF16 and w.dtype == BF16
    return pl.pallas_call(
        _mm_bf16_kernel,
        grid=(n // tn, m // tm),
        in_specs=[pl.BlockSpec((tm, k), lambda j, i: (i, 0)),
                  pl.BlockSpec((k, tn), lambda j, i: (0, j))],
        out_specs=pl.BlockSpec((tm, tn), lambda j, i: (i, j)),
        out_shape=jax.ShapeDtypeStruct((m, n), out_dtype),
        compiler_params=_cparams(("arbitrary", "arbitrary")),
        name="mm_bf16",
    )(x, w)


def _row_rms(x, g):
    return x * lax.rsqrt(jnp.mean(x * x, axis=-1, keepdims=True) + EPS) * g


_MLA_PAD = 2 * V7X_LANES


def _mla_prep_kernel(cq_ref, ckv_ref, kr_ref, wq_ref, wkv_ref, gqi_ref, gkvi_ref,
                     gq_ref, gkn_ref, gkr_ref, c_ref, s_ref, q_ref, k_ref, vt_ref):
    c = c_ref[...]
    s = s_ref[...]
    hw = MLA_HEADS * MLA_NOPE

    def rot(t):
        return t * c + (pltpu.roll(t, 32, 1) + pltpu.roll(t, 96, 1)) * s

    cqn = _row_rms(cq_ref[...].astype(F32), gqi_ref[...]).astype(BF16)
    qf = jnp.dot(cqn, wq_ref[...], preferred_element_type=F32)
    gq = gq_ref[...]
    inv = 1.0 / MLA_QK
    for h in range(MLA_HEADS):
        nope = qf[:, h * 128:(h + 1) * 128]
        t = qf[:, hw + h * 128:hw + (h + 1) * 128]
        ss = jnp.sum(nope * nope, axis=-1, keepdims=True) + jnp.sum(t * t, axis=-1, keepdims=True)
        r = lax.rsqrt(ss * inv + EPS)
        q_ref[h, :, 0:128] = (nope * r * gq[:, 0:128]).astype(BF16)
        q_ref[h, :, 128:256] = rot(t * r * gq[:, 128:256]).astype(BF16)

    ckvn = _row_rms(ckv_ref[...].astype(F32), gkvi_ref[...]).astype(BF16)
    kvf = jnp.dot(ckvn, wkv_ref[...], preferred_element_type=F32)
    kr = kr_ref[...].astype(F32)
    ssr = jnp.sum(kr * kr, axis=-1, keepdims=True)
    yrot = rot(kr * gkr_ref[...])
    gkn = gkn_ref[...]
    for h in range(MLA_HEADS):
        nope = kvf[:, h * 128:(h + 1) * 128]
        r = lax.rsqrt((jnp.sum(nope * nope, axis=-1, keepdims=True) + ssr) * inv + EPS)
        k_ref[h, :, 0:128] = (nope * r * gkn).astype(BF16)
        k_ref[h, :, 128:256] = (yrot * r).astype(BF16)
        vt_ref[h] = kvf[:, hw + h * 128:hw + (h + 1) * 128].T.astype(BF16)


def _mla_prep(proj, p, tabs):
    tm = TOK_TM
    hw = MLA_HEADS * MLA_NOPE
    wq = p['mla_w_uq'].reshape(MLA_Q_RANK, MLA_HEADS, MLA_QK)
    wq_rope = jnp.pad(wq[:, :, MLA_NOPE:], ((0, 0), (0, 0), (0, 128 - MLA_ROPE)))
    wq = jnp.concatenate([wq[:, :, :MLA_NOPE].reshape(MLA_Q_RANK, hw),
                          wq_rope.reshape(MLA_Q_RANK, hw)], axis=1).astype(BF16)
    wkv = p['mla_w_ukv'].reshape(MLA_KV_RANK, MLA_HEADS, MLA_NOPE + MLA_V)
    wkv = jnp.concatenate([wkv[:, :, :MLA_NOPE].reshape(MLA_KV_RANK, hw),
                           wkv[:, :, MLA_NOPE:].reshape(MLA_KV_RANK, hw)], axis=1).astype(BF16)
    zpad = jnp.zeros((128 - MLA_ROPE,), F32)
    gq = (jnp.concatenate([p['mla_qn_g'], zpad]) * (MLA_QK ** -0.5 * _LOG2E)).reshape(1, _MLA_PAD)
    gkn = p['mla_kn_g'][:MLA_NOPE].reshape(1, 128)
    gkr = jnp.concatenate([p['mla_kn_g'][MLA_NOPE:], zpad]).reshape(1, 128)

    def const(shape):
        return pl.BlockSpec(shape, lambda i: (0,) * len(shape))

    return pl.pallas_call(
        _mla_prep_kernel,
        grid=(N_TOK // tm,),
        in_specs=[pl.BlockSpec((tm, MLA_Q_RANK), lambda i: (i, _pcol(0, MLA_Q_RANK))),
                  pl.BlockSpec((tm, MLA_KV_RANK), lambda i: (i, _pcol(1, MLA_KV_RANK))),
                  pl.BlockSpec((tm, 128), lambda i: (i, _pcol(2, 128))),
                  const((MLA_Q_RANK, 2 * hw)), const((MLA_KV_RANK, 2 * hw)),
                  const((1, MLA_Q_RANK)), const((1, MLA_KV_RANK)),
                  const((1, _MLA_PAD)), const((1, 128)), const((1, 128)),
                  pl.BlockSpec((tm, 128), lambda i: (i, 0)),
                  pl.BlockSpec((tm, 128), lambda i: (i, 0))],
        out_specs=[pl.BlockSpec((MLA_HEADS, tm, _MLA_PAD), lambda i: (0, i, 0)),
                   pl.BlockSpec((MLA_HEADS, tm, _MLA_PAD), lambda i: (0, i, 0)),
                   pl.BlockSpec((MLA_HEADS, MLA_V, tm), lambda i: (0, 0, i))],
        out_shape=[jax.ShapeDtypeStruct((MLA_HEADS, N_TOK, _MLA_PAD), BF16),
                   jax.ShapeDtypeStruct((MLA_HEADS, N_TOK, _MLA_PAD), BF16),
                   jax.ShapeDtypeStruct((MLA_HEADS, MLA_V, N_TOK), BF16)],
        compiler_params=_cparams(("arbitrary",)),
        name="mla_prep",
    )(proj, proj, proj, wq, wkv, p['mla_q_norm_g'].reshape(1, -1), p['mla_kv_norm_g'].reshape(1, -1),
      gq, gkn, gkr, tabs['mla_c'], tabs['mla_s'])


def _mla_kernel(q_ref, k1_ref, vt1_ref, k2_ref, vt2_ref, o_ref, sa_ref, sb_ref, sc_ref, acc_ref,
                *, tk, n_chunks):
    q = q_ref[0]
    tq = q.shape[0]
    hq = tq // 2
    qs = (q[:hq], q[hq:])

    def scores(dst_ref, kc):
        out = []
        for j in range(2):
            st = lax.dot_general(kc, qs[j], _NT, preferred_element_type=F32)
            dst_ref[j] = st
            out.append(jnp.max(st, axis=0, keepdims=True))
        return out

    def k_chunk(c):
        return k1_ref[0, pl.ds(pl.multiple_of(c * tk, tk), tk), :]

    def vt_chunk(c):
        return vt1_ref[0, :, pl.ds(pl.multiple_of(c * tk, tk), tk)]

    def accumulate(s_ref, smax, vtc, m, l):
        m_out, l_out = [], []
        for j in range(2):
            m_new = jnp.maximum(m[j], smax[j])
            a = jnp.exp2(m[j] - m_new)
            p = jnp.exp2(s_ref[j] - m_new)
            l_out.append(a * l[j] + jnp.sum(p, axis=0, keepdims=True))
            acc_ref[j] = a * acc_ref[j] + jnp.dot(vtc, p.astype(BF16), preferred_element_type=F32)
            m_out.append(m_new)
        return m_out, l_out

    m = [jnp.full((1, hq), NEG_INF, F32)] * 2
    l = [jnp.zeros((1, hq), F32)] * 2
    acc_ref[...] = jnp.zeros_like(acc_ref)
    mx_c = scores(sc_ref, k2_ref[0])
    mx_a = scores(sa_ref, k_chunk(0))
    m, l = accumulate(sc_ref, mx_c, vt2_ref[0], m, l)

    for i in range(n_chunks // 2 - 1):
        mx_b = scores(sb_ref, k_chunk(2 * i + 1))
        m, l = accumulate(sa_ref, mx_a, vt_chunk(2 * i), m, l)
        mx_a = scores(sa_ref, k_chunk(2 * i + 2))
        m, l = accumulate(sb_ref, mx_b, vt_chunk(2 * i + 1), m, l)
    mx_b = scores(sb_ref, k_chunk(n_chunks - 1))
    m, l = accumulate(sa_ref, mx_a, vt_chunk(n_chunks - 2), m, l)
    m, l = accumulate(sb_ref, mx_b, vt_chunk(n_chunks - 1), m, l)
    for j in range(2):
        o_ref[j * hq:(j + 1) * hq, :] = (acc_ref[j] / l[j]).T.astype(o_ref.dtype)


def _mla_attn(q, k, vt, *, tq, tk):
    nq = SEQ // tq
    return pl.pallas_call(
        functools.partial(_mla_kernel, tk=tk, n_chunks=SEQ // tk),
        grid=(BATCH, MLA_HEADS, nq),
        in_specs=[pl.BlockSpec((1, tq, _MLA_PAD), lambda b, h, i: (h, b * nq + i, 0)),
                  pl.BlockSpec((1, SEQ, _MLA_PAD), lambda b, h, i: (h, b, 0)),
                  pl.BlockSpec((1, MLA_V, SEQ), lambda b, h, i: (h, 0, b)),
                  pl.BlockSpec((1, CTX_LEN, _MLA_PAD), lambda b, h, i: (h, CTX_BLK + b, 0)),
                  pl.BlockSpec((1, MLA_V, CTX_LEN), lambda b, h, i: (h, 0, CTX_BLK + b))],
        out_specs=pl.BlockSpec((tq, MLA_V), lambda b, h, i: (b * nq + i, h)),
        out_shape=jax.ShapeDtypeStruct((N_LAT, MLA_HEADS * MLA_V), BF16),
        scratch_shapes=[pltpu.VMEM((2, tk, tq // 2), F32), pltpu.VMEM((2, tk, tq // 2), F32),
                        pltpu.VMEM((2, CTX_LEN, tq // 2), F32), pltpu.VMEM((2, MLA_V, tq // 2), F32)],
        compiler_params=_cparams(("arbitrary", "arbitrary", "arbitrary")),
        name="mla_attn",
    )(q, k, vt, k, vt)


_NA_KEYS = NA_ROWS * GRID_W
NA_RB = 8


def _na_pattern(r):
    half = NA_ROWS // 2
    return jnp.where(r < half, r, jnp.where(r <= ROWS - half, half, r - (ROWS - NA_ROWS)))


def _na_kernel(q_ref, k_ref, v_ref, kc_ref, vc_ref, bias_ref, gq_ref, gk_ref, o_ref, kn_ref, kcn_ref):
    rb = pl.program_id(2)

    @pl.when(rb == 0)
    def _():
        kn_ref[...] = _row_rms(k_ref[...].astype(F32), gk_ref[...]).astype(BF16)
        kcn_ref[...] = _row_rms(kc_ref[...].astype(F32), gk_ref[...]).astype(BF16)

    q_all = _row_rms(q_ref[...].astype(F32), gq_ref[...]).astype(BF16)
    vc = vc_ref[...]
    sc_all = lax.dot_general(q_all, kcn_ref[...], _NT, preferred_element_type=F32)
    ms, ls, os_ = [], [], []
    for j in range(NA_RB):
        r = rb * NA_RB + j
        start = pl.multiple_of(jnp.clip(r - NA_ROWS // 2, 0, ROWS - NA_ROWS) * GRID_W, GRID_W)
        q = q_all[j * GRID_W:(j + 1) * GRID_W]
        k = kn_ref[pl.ds(start, _NA_KEYS), :]
        v = v_ref[pl.ds(start, _NA_KEYS), :]
        s = lax.dot_general(q, k, _NT, preferred_element_type=F32) + bias_ref[0, _na_pattern(r)]
        sc = sc_all[j * GRID_W:(j + 1) * GRID_W]
        m = jnp.maximum(jnp.max(s, axis=-1, keepdims=True), jnp.max(sc, axis=-1, keepdims=True))
        p = jnp.exp(s - m)
        ms.append(m)
        ls.append(jnp.sum(p, axis=-1, keepdims=True))
        os_.append(jnp.dot(p.astype(BF16), v, preferred_element_type=F32))
    m_all = jnp.concatenate(ms, axis=0)
    pc = jnp.exp(sc_all - m_all)
    l_all = jnp.concatenate(ls, axis=0) + jnp.sum(pc, axis=-1, keepdims=True)
    o = jnp.concatenate(os_, axis=0) + jnp.dot(pc.astype(BF16), vc, preferred_element_type=F32)
    o_ref[...] = (o / l_all).astype(o_ref.dtype)


def _na_attn(proj, bias, gq, gk):
    qrows = NA_RB * GRID_W
    nrb = ROWS // NA_RB
    cq, ck, cv = _pcol(3, NA_HD), _pcol(4, NA_HD), _pcol(5, NA_HD)
    return pl.pallas_call(
        _na_kernel,
        grid=(BATCH, NA_HEADS, nrb),
        in_specs=[pl.BlockSpec((qrows, NA_HD), lambda b, h, r: (b * nrb + r, cq + h)),
                  pl.BlockSpec((SEQ, NA_HD), lambda b, h, r: (b, ck + h)),
                  pl.BlockSpec((SEQ, NA_HD), lambda b, h, r: (b, cv + h)),
                  pl.BlockSpec((CTX_LEN, NA_HD), lambda b, h, r: (CTX_BLK + b, ck + h)),
                  pl.BlockSpec((CTX_LEN, NA_HD), lambda b, h, r: (CTX_BLK + b, cv + h)),
                  pl.BlockSpec((1, NA_ROWS, GRID_W, _NA_KEYS), lambda b, h, r: (h, 0, 0, 0)),
                  pl.BlockSpec((1, NA_HD), lambda b, h, r: (0, 0)),
                  pl.BlockSpec((1, NA_HD), lambda b, h, r: (0, 0))],
        out_specs=pl.BlockSpec((qrows, NA_HD), lambda b, h, r: (b * nrb + r, h)),
        out_shape=jax.ShapeDtypeStruct((N_LAT, NA_HEADS * NA_HD), BF16),
        scratch_shapes=[pltpu.VMEM((SEQ, NA_HD), BF16), pltpu.VMEM((CTX_LEN, NA_HD), BF16)],
        compiler_params=_cparams(("arbitrary", "arbitrary", "arbitrary")),
        name="na_attn",
    )(proj, proj, proj, proj, proj, bias, gq, gk)


def _na_bias_table(rpb):
    half = NA_ROWS // 2
    r_rep = np.array(list(range(half)) + [half] + list(range(ROWS - half + 1, ROWS)))
    start = np.clip(r_rep - half, 0, ROWS - NA_ROWS)
    dr = start[:, None] + np.arange(NA_ROWS)[None, :] - r_rep[:, None] + NA_ROWS - 1
    qc = np.arange(GRID_W)
    kcol = np.arange(GRID_W)
    col_start = np.clip(qc - NA_COLS // 2, 0, GRID_W - NA_COLS)
    in_win = (kcol[None, :] >= col_start[:, None]) & (kcol[None, :] < col_start[:, None] + NA_COLS)
    dc = np.clip(kcol[None, :] - qc[:, None], 1 - NA_COLS, NA_COLS - 1) + NA_COLS - 1
    rsel = (dr[:, :, None] == np.arange(2 * NA_ROWS - 1)).astype(np.float32)
    csel = (dc[:, :, None] == np.arange(2 * NA_COLS - 1)).astype(np.float32)
    b = jnp.einsum('pja,hab,qkb->hpqjk', rsel, rpb.astype(F32), csel,
                   precision=lax.Precision.HIGHEST)
    b = jnp.where(in_win[None, None, :, None, :], b.astype(F32), NEG_INF)
    return b.reshape(NA_HEADS, NA_ROWS, GRID_W, _NA_KEYS)


_GQA_G = GQA_HEADS // GQA_KV_HEADS
_GQA_BAND = 3 * GQA_WINDOW
_GQA_QW = GQA_HEADS * V7X_LANES


def _gqa_prep_kernel(q_ref, k_ref, gq_ref, gk_ref, c_ref, s1_ref, s2_ref, qd_ref, kn_ref):
    c = c_ref[...]
    s1 = s1_ref[...]
    s2 = s2_ref[...]
    lo = lax.broadcasted_iota(jnp.int32, (1, 128), 1) < GQA_HD

    def head_rms(x, g):
        x2 = x * x
        s_lo = jnp.sum(jnp.where(lo, x2, 0.0), axis=-1, keepdims=True)
        s_hi = jnp.sum(jnp.where(lo, 0.0, x2), axis=-1, keepdims=True)
        inv = 1.0 / GQA_HD
        r = jnp.where(lo, lax.rsqrt(s_lo * inv + EPS), lax.rsqrt(s_hi * inv + EPS))
        return x * r * g

    def rot(x):
        return x * c + pltpu.roll(x, 96, 1) * s1 + pltpu.roll(x, 32, 1) * s2

    gq = gq_ref[...]
    for j in range(GQA_HEADS // 2):
        y = rot(head_rms(q_ref[:, j * 128:(j + 1) * 128].astype(F32), gq))
        sw = pltpu.roll(y, 64, 1)
        hk = (2 * j) // _GQA_G
        if hk == 0:
            even, odd = jnp.where(lo, y, 0.0), jnp.where(lo, sw, 0.0)
        else:
            even, odd = jnp.where(lo, 0.0, sw), jnp.where(lo, 0.0, y)
        qd_ref[:, (2 * j) * 128:(2 * j + 1) * 128] = even.astype(BF16)
        qd_ref[:, (2 * j + 1) * 128:(2 * j + 2) * 128] = odd.astype(BF16)
    kn_ref[...] = rot(head_rms(k_ref[...].astype(F32), gk_ref[...])).astype(BF16)


def _gqa_prep(proj, p, tabs):
    tm = TOK_TM
    gq = (jnp.tile(p['gqa_qn_g'], 2) * (GQA_HD ** -0.5)).reshape(1, 128)
    gk = jnp.tile(p['gqa_kn_g'], 2).reshape(1, 128)
    row = pl.BlockSpec((tm, 128), lambda i: (i, 0))
    vec = pl.BlockSpec((1, 128), lambda i: (0, 0))
    return pl.pallas_call(
        _gqa_prep_kernel,
        grid=(N_TOK // tm,),
        in_specs=[pl.BlockSpec((tm, GQA_HEADS * GQA_HD), lambda i: (i, _pcol(6, GQA_HEADS * GQA_HD))),
                  pl.BlockSpec((tm, 128), lambda i: (i, _pcol(7, 128))),
                  vec, vec, row, row, row],
        out_specs=[pl.BlockSpec((tm, _GQA_QW), lambda i: (i, 0)), row],
        out_shape=[jax.ShapeDtypeStruct((N_TOK, _GQA_QW), BF16),
                   jax.ShapeDtypeStruct((N_TOK, 128), BF16)],
        compiler_params=_cparams(("arbitrary",)),
        name="gqa_prep",
    )(proj, proj, gq, gk, tabs['gqa_c'], tabs['gqa_s1'], tabs['gqa_s2'])


def _gqa_kernel(sink_ref, q_ref, k_ref, v_ref, kc_ref, vc_ref, o_ref):
    hk = pl.program_id(1)
    n = pl.program_id(2)
    w = GQA_WINDOW
    start = pl.multiple_of(jnp.clip((n - 1) * w, 0, SEQ - _GQA_BAND), w)
    q = jnp.concatenate([q_ref[:, g * 128:(g + 1) * 128] for g in range(_GQA_G)], axis=0)
    k = k_ref[pl.ds(start, _GQA_BAND), :]
    v = v_ref[pl.ds(start, _GQA_BAND), :]
    s = lax.dot_general(q, k, _NT, preferred_element_type=F32)
    rows = lax.broadcasted_iota(jnp.int32, s.shape, 0)
    cols = lax.broadcasted_iota(jnp.int32, s.shape, 1)
    qpos = n * w + (rows & (w - 1))
    kpos = start + cols
    s = jnp.where(jnp.abs(kpos - qpos) <= GQA_WINDOW, s, NEG_INF)
    sc = lax.dot_general(q, kc_ref[...], _NT, preferred_element_type=F32)
    grow = lax.broadcasted_iota(jnp.int32, (_GQA_G * w, 1), 0) // w
    snk = jnp.full((_GQA_G * w, 1), sink_ref[hk * _GQA_G], F32)
    for g in range(1, _GQA_G):
        snk = jnp.where(grow == g, sink_ref[hk * _GQA_G + g], snk)
    m = jnp.maximum(jnp.maximum(jnp.max(s, axis=-1, keepdims=True),
                                jnp.max(sc, axis=-1, keepdims=True)), snk)
    p = jnp.exp(s - m)
    pc = jnp.exp(sc - m)
    l = jnp.sum(p, axis=-1, keepdims=True) + jnp.sum(pc, axis=-1, keepdims=True) + jnp.exp(snk - m)
    o = (jnp.dot(p.astype(BF16), v, preferred_element_type=F32)
         + jnp.dot(pc.astype(BF16), vc_ref[...], preferred_element_type=F32)) / l
    half = lax.broadcasted_iota(jnp.int32, (1, 128), 1) // GQA_HD
    o = jnp.where(half == hk, o, 0.0).astype(o_ref.dtype)
    for g in range(_GQA_G):
        o_ref[:, g * 128:(g + 1) * 128] = o[g * w:(g + 1) * w]


def _gqa_attn(qd, kn, proj, sink):
    nb = SEQ // GQA_WINDOW
    qw = _GQA_G * 128
    cv = _pcol(8, 128)
    gs = pltpu.PrefetchScalarGridSpec(
        num_scalar_prefetch=1,
        grid=(BATCH, GQA_KV_HEADS, nb),
        in_specs=[pl.BlockSpec((GQA_WINDOW, qw), lambda b, h, n, *_: (b * nb + n, h)),
                  pl.BlockSpec((SEQ, 128), lambda b, h, n, *_: (b, 0)),
                  pl.BlockSpec((SEQ, 128), lambda b, h, n, *_: (b, cv)),
                  pl.BlockSpec((CTX_LEN, 128), lambda b, h, n, *_: (CTX_BLK + b, 0)),
                  pl.BlockSpec((CTX_LEN, 128), lambda b, h, n, *_: (CTX_BLK + b, cv))],
        out_specs=pl.BlockSpec((GQA_WINDOW, qw), lambda b, h, n, *_: (b * nb + n, h)),
    )
    return pl.pallas_call(
        _gqa_kernel,
        grid_spec=gs,
        out_shape=jax.ShapeDtypeStruct((N_LAT, _GQA_QW), BF16),
        compiler_params=_cparams(("arbitrary", "arbitrary", "arbitrary")),
        name="gqa_attn",
    )(sink.astype(F32), qd, kn, proj, kn, proj)


def _ctx_kernel(sink_ref, mq_ref, mk_ref, mvt_ref, nq_ref, nk_ref, nv_ref, gnq_ref, gnk_ref,
                gq_ref, gk_ref, gv_ref, oa_ref, ob_ref, oc_ref):
    for h in range(MLA_HEADS):
        st = lax.dot_general(mk_ref[h], mq_ref[h], _NT, preferred_element_type=F32)
        p = jnp.exp2(st - jnp.max(st, axis=0, keepdims=True))
        l = jnp.sum(p, axis=0, keepdims=True)
        ot = jnp.dot(mvt_ref[h], p.astype(BF16), preferred_element_type=F32) / l
        oa_ref[:, h * MLA_V:(h + 1) * MLA_V] = ot.T.astype(oa_ref.dtype)
    for h in range(NA_HEADS):
        sl = slice(h * NA_HD, (h + 1) * NA_HD)
        q = _row_rms(nq_ref[:, sl].astype(F32), gnq_ref[...]).astype(BF16)
        k = _row_rms(nk_ref[:, sl].astype(F32), gnk_ref[...]).astype(BF16)
        s = lax.dot_general(q, k, _NT, preferred_element_type=F32)
        p = jnp.exp(s - jnp.max(s, axis=-1, keepdims=True))
        l = jnp.sum(p, axis=-1, keepdims=True)
        o = jnp.dot(p.astype(BF16), nv_ref[:, sl], preferred_element_type=F32) / l
        ob_ref[:, sl] = o.astype(ob_ref.dtype)
    half = lax.broadcasted_iota(jnp.int32, (1, 128), 1) // GQA_HD
    k = gk_ref[...]
    v = gv_ref[...]
    for h in range(GQA_HEADS):
        sl = slice(h * 128, (h + 1) * 128)
        s = lax.dot_general(gq_ref[:, sl], k, _NT, preferred_element_type=F32)
        snk = sink_ref[h]
        m = jnp.maximum(jnp.max(s, axis=-1, keepdims=True), snk)
        p = jnp.exp(s - m)
        l = jnp.sum(p, axis=-1, keepdims=True) + jnp.exp(snk - m)
        o = jnp.dot(p.astype(BF16), v, preferred_element_type=F32) / l
        oc_ref[:, sl] = jnp.where(half == h // _GQA_G, o, 0.0).astype(oc_ref.dtype)


def _ctx_attn(sink, mq, mk, mvt, proj, gnq, gnk, qd, kn):
    c = CTX_LEN
    nwid = NA_HEADS * NA_HD

    def row(width, col=0):
        return pl.BlockSpec((c, width), lambda b, *_: (CTX_BLK + b, col))

    def out(width):
        return pl.BlockSpec((c, width), lambda b, *_: (b, 0))

    vec = pl.BlockSpec((1, NA_HD), lambda b, *_: (0, 0))
    gs = pltpu.PrefetchScalarGridSpec(
        num_scalar_prefetch=1,
        grid=(BATCH,),
        in_specs=[pl.BlockSpec((MLA_HEADS, c, _MLA_PAD), lambda b, *_: (0, CTX_BLK + b, 0)),
                  pl.BlockSpec((MLA_HEADS, c, _MLA_PAD), lambda b, *_: (0, CTX_BLK + b, 0)),
                  pl.BlockSpec((MLA_HEADS, MLA_V, c), lambda b, *_: (0, 0, CTX_BLK + b)),
                  row(nwid, _pcol(3, nwid)), row(nwid, _pcol(4, nwid)), row(nwid, _pcol(5, nwid)),
                  vec, vec,
                  row(_GQA_QW), row(128), row(128, _pcol(8, 128))],
        out_specs=[out(MLA_HEADS * MLA_V), out(nwid), out(_GQA_QW)],
    )
    return pl.pallas_call(
        _ctx_kernel,
        grid_spec=gs,
        out_shape=[jax.ShapeDtypeStruct((N_CTX, MLA_HEADS * MLA_V), BF16),
                   jax.ShapeDtypeStruct((N_CTX, nwid), BF16),
                   jax.ShapeDtypeStruct((N_CTX, _GQA_QW), BF16)],
        compiler_params=_cparams(("arbitrary",)),
        name="ctx_attn",
    )(sink.astype(F32), mq, mk, mvt, proj, proj, proj, gnq, gnk, qd, kn, proj)


def _merge_kernel(*refs, has_ctx, n_lat_tiles):
    ga_ref, gb_ref, gc_ref, oa_ref, ob_ref, oc_ref = refs[:6]
    refs = refs[6:]
    if has_ctx:
        ca_ref, cb_ref, cc_ref = refs[:3]
        refs = refs[3:]
    wa_ref, wb_ref, wc_ref, y_ref = refs
    is_ctx = pl.program_id(0) >= n_lat_tiles

    def branch(g_ref, o_ref, c_ref, w_ref):
        o = o_ref[...]
        if has_ctx:
            o = jnp.where(is_ctx, c_ref[...], o)
        return jax.nn.sigmoid(g_ref[...].astype(F32)) * jnp.dot(o, w_ref[...], preferred_element_type=F32)

    y = (branch(ga_ref, oa_ref, ca_ref if has_ctx else None, wa_ref)
         + branch(gb_ref, ob_ref, cb_ref if has_ctx else None, wb_ref)
         + branch(gc_ref, oc_ref, cc_ref if has_ctx else None, wc_ref))
    y_ref[...] = y.astype(y_ref.dtype)


def _merge(proj, lat, ctx, wa, wb, wc):
    tm = TOK_TM
    has_ctx = ctx is not None
    m = N_TOK if has_ctx else N_LAT
    nl = N_LAT // tm
    ks = [o.shape[1] for o in lat]
    in_specs = [pl.BlockSpec((tm, D_MODEL), lambda i: (i, 0)),
                pl.BlockSpec((tm, D_MODEL), lambda i: (i, 1)),
                pl.BlockSpec((tm, D_MODEL), lambda i: (i, 2))]
    in_specs += [pl.BlockSpec((tm, k), lambda i: (jnp.minimum(i, nl - 1), 0)) for k in ks]
    args = [proj, proj, proj] + list(lat)
    if has_ctx:
        in_specs += [pl.BlockSpec((tm, k), lambda i: (jnp.maximum(i - nl, 0), 0)) for k in ks]
        args += list(ctx)
    in_specs += [pl.BlockSpec((k, D_MODEL), lambda i: (0, 0)) for k in ks]
    args += [wa, wb, wc]
    return pl.pallas_call(
        functools.partial(_merge_kernel, has_ctx=has_ctx, n_lat_tiles=nl),
        grid=(m // tm,),
        in_specs=in_specs,
        out_specs=pl.BlockSpec((tm, D_MODEL), lambda i: (i, 0)),
        out_shape=jax.ShapeDtypeStruct((m, D_MODEL), BF16),
        compiler_params=_cparams(("arbitrary",)),
        name="merge",
    )(*args)


def _moe_ffn_kernel(te_ref, tok_ref, nu_ref, en_ref, es_ref, h_hbm, wg_hbm, wu_hbm, wd_hbm, y_ref,
                    xbuf, sem, wgs, wus, wds, wsem, wgb, wub, wdb, *, layer):
    i = pl.program_id(0)
    tm = MOE_TM
    slot = i % 2
    n_used = nu_ref[0]

    def row_copy(tok, s, r):
        return pltpu.make_async_copy(h_hbm.at[pl.ds(tok, 1)], xbuf.at[s, pl.ds(r, 1)], sem.at[s])

    def start_gather(tile, s):
        base = tile * tm

        def body(r, c):
            row_copy(tok_ref[base + r], s, r).start()
            return c

        lax.fori_loop(0, tm, body, 0, unroll=8)

    def weight_copies(e, s):
        return (pltpu.make_async_copy(wg_hbm.at[layer, e], wgs.at[s], wsem.at[s]),
                pltpu.make_async_copy(wu_hbm.at[layer, e], wus.at[s], wsem.at[s]),
                pltpu.make_async_copy(wd_hbm.at[layer, e], wds.at[s], wsem.at[s]))

    @pl.when(i == 0)
    def _():
        for cp in weight_copies(te_ref[0], es_ref[0]):
            cp.start(priority=1)
        start_gather(0, 0)

    @pl.when(i + 1 < n_used)
    def _():
        start_gather(i + 1, 1 - slot)

    @pl.when(i < n_used)
    def _():
        @pl.when((i == 0) | (te_ref[i] != te_ref[jnp.maximum(i - 1, 0)]))
        def _():
            s = es_ref[i]

            @pl.when(en_ref[i] >= 0)
            def _():
                for cp in weight_copies(en_ref[i], 1 - s):
                    cp.start(priority=1)

            for cp in weight_copies(te_ref[i], s):
                cp.wait()
            wgb[...] = wgs[s].astype(BF16)
            wub[...] = wus[s].astype(BF16)
            wdb[...] = wds[s].astype(BF16)

        pltpu.make_async_copy(h_hbm.at[pl.ds(0, tm)], xbuf.at[slot], sem.at[slot]).wait()
        x = xbuf[slot].astype(BF16)
        hg = jnp.dot(x, wgb[...], preferred_element_type=F32)
        hu = jnp.dot(x, wub[...], preferred_element_type=F32)
        act = (hg * jax.nn.sigmoid(hg)) * hu
        y_ref[...] = jnp.dot(act.astype(BF16), wdb[...], preferred_element_type=F32)

    @pl.when(i >= n_used)
    def _():
        y_ref[...] = jnp.zeros_like(y_ref)


def _moe_ffn(h, plan, wg, wu, wd, layer):
    tile_expert, slot_token, n_used, next_expert, expert_slot = plan
    p = slot_token.shape[0]
    tm = MOE_TM
    nt = p // tm
    anyspec = pl.BlockSpec(memory_space=pl.ANY)
    gs = pltpu.PrefetchScalarGridSpec(
        num_scalar_prefetch=5,
        grid=(nt,),
        in_specs=[anyspec, anyspec, anyspec, anyspec],
        out_specs=pl.BlockSpec((tm, D_MODEL), lambda i, *_: (i, 0)),
        scratch_shapes=[pltpu.VMEM((2, tm, D_MODEL), F32),
                        pltpu.SemaphoreType.DMA((2,)),
                        pltpu.VMEM((2, D_MODEL, MOE_HIDDEN), F32),
                        pltpu.VMEM((2, D_MODEL, MOE_HIDDEN), F32),
                        pltpu.VMEM((2, MOE_HIDDEN, D_MODEL), F32),
                        pltpu.SemaphoreType.DMA((2,)),
                        pltpu.VMEM((D_MODEL, MOE_HIDDEN), BF16),
                        pltpu.VMEM((D_MODEL, MOE_HIDDEN), BF16),
                        pltpu.VMEM((MOE_HIDDEN, D_MODEL), BF16)],
    )
    return pl.pallas_call(
        functools.partial(_moe_ffn_kernel, layer=layer),
        grid_spec=gs,
        out_shape=jax.ShapeDtypeStruct((p, D_MODEL), F32),
        compiler_params=_cparams(("arbitrary",)),
        name="moe_ffn",
    )(tile_expert, slot_token, n_used, next_expert, expert_slot, h, wg, wu, wd)


def _moe_combine_kernel(pos_ref, y_hbm, x_ref, w_ref, g_ref, o_ref, ybuf, sem):
    i = pl.program_id(0)
    nt = pl.num_programs(0)
    tm = TOK_TM
    slot = i % 2

    def row_copy(src, s, r):
        return pltpu.make_async_copy(y_hbm.at[pl.ds(src, 1)], ybuf.at[s, pl.ds(r, 1)], sem.at[s])

    def start_gather(tile, s):
        base = tile * tm

        def body(r, c):
            row_copy(pos_ref[2 * (base + r)], s, r).start(priority=0)
            row_copy(pos_ref[2 * (base + r) + 1], s, tm + r).start(priority=1)
            return c

        lax.fori_loop(0, tm, body, 0, unroll=4)

    @pl.when(i == 0)
    def _():
        start_gather(0, 0)

    @pl.when(i + 1 < nt)
    def _():
        start_gather(i + 1, 1 - slot)

    pltpu.make_async_copy(y_hbm.at[pl.ds(0, 2 * tm)], ybuf.at[slot], sem.at[slot]).wait()
    w = w_ref[...]
    y = ybuf[slot, pl.ds(0, tm), :] * w[:, 0:1] + ybuf[slot, pl.ds(tm, tm), :] * w[:, 1:2]
    o_ref[...] = x_ref[...] + g_ref[0] * y


def _moe_combine(y, pos, w_sel, x, mod, k_gate):
    m = x.shape[0]
    tm = TOK_TM
    gs = pltpu.PrefetchScalarGridSpec(
        num_scalar_prefetch=1,
        grid=(m // tm,),
        in_specs=[pl.BlockSpec(memory_space=pl.ANY),
                  pl.BlockSpec((tm, D_MODEL), lambda i, pos: (i, 0)),
                  pl.BlockSpec((tm, MOE_TOPK), lambda i, pos: (i, 0)),
                  pl.BlockSpec((1, 1, D_MODEL), lambda i, pos: (_seg(i, tm) * 6 + k_gate, 0, 0))],
        out_specs=pl.BlockSpec((tm, D_MODEL), lambda i, pos: (i, 0)),
        scratch_shapes=[pltpu.VMEM((2, 2 * tm, D_MODEL), F32),
                        pltpu.SemaphoreType.DMA((2,))],
    )
    return pl.pallas_call(
        _moe_combine_kernel,
        grid_spec=gs,
        out_shape=jax.ShapeDtypeStruct((m, D_MODEL), F32),
        compiler_params=_cparams(("arbitrary",)),
        name="moe_combine",
    )(pos, y, x, w_sel, mod)


def _route(logits, m):
    tm = MOE_TM
    gp = jax.nn.softmax(logits[:, :MOE_GROUPS], axis=-1)
    g_idx = jnp.argmax(gp, axis=-1).astype(jnp.int32)[:, None]
    g_w = jnp.max(gp, axis=-1, keepdims=True)
    el = logits[:, MOE_GROUPS:MOE_GROUPS + MOE_EXPERTS].reshape(m, MOE_GROUPS, MOE_PER_GROUP)
    g_onehot = (g_idx == jnp.arange(MOE_GROUPS, dtype=jnp.int32)[None, :]).astype(F32)
    el_g = jnp.sum(el * g_onehot[:, :, None], axis=1)
    i0 = jnp.argmax(el_g, axis=-1).astype(jnp.int32)[:, None]
    l0 = jnp.max(el_g, axis=-1, keepdims=True)
    rest = jnp.where(jnp.arange(MOE_PER_GROUP, dtype=jnp.int32)[None, :] == i0, -jnp.inf, el_g)
    i1 = jnp.argmax(rest, axis=-1).astype(jnp.int32)[:, None]
    l1 = jnp.max(rest, axis=-1, keepdims=True)
    top_l = jnp.concatenate([l0, l1], axis=-1)
    top_i = jnp.concatenate([i0, i1], axis=-1)
    w_sel = jax.nn.softmax(top_l, axis=-1) * g_w
    eid = (g_idx * MOE_PER_GROUP + top_i).astype(jnp.int32)

    a = m * MOE_TOPK
    e_flat = eid.reshape(a)
    onehot = (e_flat[:, None] == jnp.arange(MOE_EXPERTS, dtype=jnp.int32)[None, :]).astype(jnp.int32)
    csum = jnp.cumsum(onehot, axis=0)
    rank = jnp.sum(csum * onehot, axis=1) - 1
    counts = csum[-1]
    padded = ((counts + tm - 1) // tm) * tm
    ends = jnp.cumsum(padded)
    starts = ends - padded
    pos = (jnp.sum(onehot * starts[None, :], axis=1) + rank).astype(jnp.int32)
    p = a + MOE_EXPERTS * tm
    slot_token = jnp.zeros((p,), jnp.int32).at[pos].set(jnp.arange(a, dtype=jnp.int32) // MOE_TOPK)
    n_used = (ends[-1] // tm).astype(jnp.int32).reshape(1)
    tile_start = jnp.arange(p // tm, dtype=jnp.int32) * tm
    last_e = jnp.max(jnp.where(counts > 0, jnp.arange(MOE_EXPERTS, dtype=jnp.int32), 0))
    tile_expert = jnp.minimum(
        jnp.sum((ends[None, :] <= tile_start[:, None]).astype(jnp.int32), axis=1), last_e)
    eidx = jnp.arange(MOE_EXPERTS, dtype=jnp.int32)
    used = counts > 0
    later = used[None, :] & (eidx[None, :] > eidx[:, None])
    next_used = jnp.min(jnp.where(later, eidx[None, :], MOE_EXPERTS), axis=1)
    next_used = jnp.where(next_used == MOE_EXPERTS, -1, next_used).astype(jnp.int32)
    ordinal = (jnp.cumsum(used.astype(jnp.int32)) - 1) % 2
    t_onehot = (tile_expert[:, None] == eidx[None, :]).astype(jnp.int32)
    next_expert = jnp.sum(t_onehot * next_used[None, :], axis=1).astype(jnp.int32)
    expert_slot = jnp.sum(t_onehot * ordinal[None, :], axis=1).astype(jnp.int32)
    return (tile_expert, slot_token, n_used, next_expert, expert_slot), w_sel, pos


def _rope_angles(rot_dim):
    t = jnp.arange(SEQ)
    row = (t // GRID_W).astype(F32)
    col = (t % GRID_W).astype(F32)
    n_freq = rot_dim // 4
    inv = ROPE_THETA ** (-jnp.arange(n_freq, dtype=F32) / n_freq)
    ang = jnp.concatenate([row[:, None] * inv, col[:, None] * inv], axis=-1)
    return jnp.cos(ang), jnp.sin(ang)


def _rope_tables():
    def rows(lat, ctx_val):
        ctx = jnp.full((N_CTX, 128), 0.0, F32) + ctx_val
        return jnp.concatenate([lat, lat, ctx], axis=0)

    z32 = jnp.zeros((SEQ, 32), F32)
    z64 = jnp.zeros((SEQ, 64), F32)
    lane = np.arange(128)
    cm, sm = _rope_angles(MLA_ROPE)
    mla_c = rows(jnp.concatenate([cm, cm, z64], axis=1), jnp.asarray((lane < 64).astype(np.float32)))
    mla_s = rows(jnp.concatenate([-sm, sm, z64], axis=1), 0.0)
    cg, sg = _rope_angles(GQA_HD)
    gqa_c = rows(jnp.concatenate([cg, cg, cg, cg], axis=1), 1.0)
    gqa_s1 = rows(jnp.concatenate([-sg, z32, -sg, z32], axis=1), 0.0)
    gqa_s2 = rows(jnp.concatenate([z32, sg, z32, sg], axis=1), 0.0)
    return {'mla_c': mla_c, 'mla_s': mla_s, 'gqa_c': gqa_c, 'gqa_s1': gqa_s1, 'gqa_s2': gqa_s2}


def _pack_w_in(w):
    parts = jnp.split(w, IN_SPLITS, axis=1)
    pad = jnp.zeros((D_MODEL, _PACK_COLS - sum(IN_SIZES)), w.dtype)
    return jnp.concatenate([parts[i] for i in _PACK_ORDER] + [pad], axis=1).astype(BF16)


def _pad_w_o_gqa(w):
    w4 = w.reshape(GQA_KV_HEADS, _GQA_G, GQA_HD, D_MODEL)
    z = jnp.zeros_like(w4[0:1])
    halves = [jnp.concatenate([w4[hk:hk + 1] if hk == half else z for hk in range(GQA_KV_HEADS)], axis=0)
              for half in range(GQA_KV_HEADS)]
    return jnp.stack(halves, axis=2).reshape(_GQA_QW, D_MODEL).astype(BF16)


def _token_mixer(h, p, tabs, ctx_out):
    proj = _mm(h, _pack_w_in(p['w_in']), BF16, _PACK_COLS // 4)
    mq, mk, mvt = _mla_prep(proj, p, tabs)
    oa = _mla_attn(mq, mk, mvt, tq=512, tk=1024)
    gnq = (p['na_qn_g'] * (NA_HD ** -0.5)).reshape(1, NA_HD)
    gnk = p['na_kn_g'].reshape(1, NA_HD)
    ob = _na_attn(proj, _na_bias_table(p['na_rpb']), gnq, gnk)
    qd, kn = _gqa_prep(proj, p, tabs)
    oc = _gqa_attn(qd, kn, proj, p['gqa_sink'])
    ctx = _ctx_attn(p['gqa_sink'], mq, mk, mvt, proj, gnq, gnk, qd, kn) if ctx_out else None
    return _merge(proj, (oa, ob, oc), ctx, p['w_o_mla'].astype(BF16), p['w_o_na'].astype(BF16),
                  _pad_w_o_gqa(p['w_o_gqa']))


def _post_mixer(y, xt, mod, norm_g, w_out_l, p):
    m = y.shape[0]
    wr = jnp.concatenate([p['moe_w_group'], p['moe_w_expert'],
                          jnp.zeros((D_MODEL, ROUTE_COLS - MOE_GROUPS - MOE_EXPERTS), F32)], axis=1)
    br = jnp.concatenate([p['moe_b_group'], p['moe_b_expert'],
                          jnp.zeros((ROUTE_COLS - MOE_GROUPS - MOE_EXPERTS,), F32)]).reshape(1, ROUTE_COLS)
    x_mid, h, logits = _out_route(y, w_out_l.astype(BF16), xt, mod, norm_g, wr, br)
    plan, w_sel, pos = _route(logits, m)
    yy = _moe_ffn(h, plan, p['moe_w_gate'], p['moe_w_up'], p['moe_w_down'], p['layer'])
    return _moe_combine(yy, pos, w_sel, x_mid, mod, 5)


def kernel(x, c, ctx, c_ctx, ada_w, ada_b, norm_mix_g, norm_ffn_g, w_in,
           mla_q_norm_g, mla_w_uq, mla_kv_norm_g, mla_w_ukv, mla_qn_g, mla_kn_g,
           na_qn_g, na_kn_g, na_rpb, gqa_qn_g, gqa_kn_g, gqa_sink,
           w_o_mla, w_o_na, w_o_gqa, w_out,
           moe_w_group, moe_b_group, moe_w_expert, moe_b_expert,
           moe_w_gate, moe_w_up, moe_w_down):
    xt = jnp.concatenate([x.reshape(N_LAT, D_MODEL), ctx.reshape(N_CTX, D_MODEL)], axis=0)
    c_rows = jnp.concatenate([c, c_ctx[None, :], jnp.zeros((8 - BATCH - 1, D_MODEL), F32)], axis=0)
    mod_all = _ada(c_rows, ada_w, ada_b)
    tabs = _rope_tables()
    for l in range(DEPTH):
        ctx_out = l < DEPTH - 1
        p = {
            'w_in': w_in[l], 'mla_q_norm_g': mla_q_norm_g[l], 'mla_w_uq': mla_w_uq[l],
            'mla_kv_norm_g': mla_kv_norm_g[l], 'mla_w_ukv': mla_w_ukv[l],
            'mla_qn_g': mla_qn_g[l], 'mla_kn_g': mla_kn_g[l],
            'na_qn_g': na_qn_g[l], 'na_kn_g': na_kn_g[l], 'na_rpb': na_rpb[l],
            'gqa_qn_g': gqa_qn_g[l], 'gqa_kn_g': gqa_kn_g[l], 'gqa_sink': gqa_sink[l],
            'w_o_mla': w_o_mla[l], 'w_o_na': w_o_na[l], 'w_o_gqa': w_o_gqa[l],
            'moe_w_group': moe_w_group[l], 'moe_b_group': moe_b_group[l],
            'moe_w_expert': moe_w_expert[l], 'moe_b_expert': moe_b_expert[l],
            'moe_w_gate': moe_w_gate, 'moe_w_up': moe_w_up, 'moe_w_down': moe_w_down, 'layer': l,
        }
        mod = mod_all[l].reshape(8 * 6, 1, D_MODEL)
        h = _norm_mod(xt, norm_mix_g[l], mod, 0, 1)
        y = _token_mixer(h, p, tabs, ctx_out)
        xt = _post_mixer(y, xt, mod, norm_ffn_g[l], w_out[l], p)
    return xt[:N_LAT].reshape(BATCH, SEQ, D_MODEL)
```

```python
import functools

import numpy as np
import jax
import jax.numpy as jnp
from jax import lax
from jax.experimental import pallas as pl
from jax.experimental.pallas import tpu as pltpu

D_MODEL = 2048
BATCH = 2
SEQ = 4096
DEPTH = 2
GRID_W = 64
CTX_LEN = 256
EPS = 1e-6
ROPE_THETA = 10000.0
NEG_INF = -1e30

MLA_HEADS = 8
MLA_Q_RANK = 512
MLA_KV_RANK = 512
MLA_NOPE = 128
MLA_ROPE = 64
MLA_QK = MLA_NOPE + MLA_ROPE
MLA_V = 128
NA_HEADS = 4
NA_HD = 128
NA_ROWS = 8
NA_COLS = 16
GQA_HEADS = 8
GQA_KV_HEADS = 2
GQA_HD = 64
GQA_WINDOW = 128
MOE_GROUPS = 4
MOE_PER_GROUP = 8
MOE_EXPERTS = MOE_GROUPS * MOE_PER_GROUP
MOE_TOPK = 2
MOE_HIDDEN = 512

IN_SIZES = (MLA_Q_RANK, MLA_KV_RANK, MLA_ROPE,
            NA_HEADS * NA_HD, NA_HEADS * NA_HD, NA_HEADS * NA_HD,
            GQA_HEADS * GQA_HD, GQA_KV_HEADS * GQA_HD, GQA_KV_HEADS * GQA_HD,
            D_MODEL, D_MODEL, D_MODEL)
IN_SPLITS = tuple(int(s) for s in np.cumsum(IN_SIZES)[:-1])

N_LAT = BATCH * SEQ
N_CTX = BATCH * CTX_LEN
N_TOK = N_LAT + N_CTX
ROWS = SEQ // GRID_W

V7X_LANES = 128
V7X_VMEM_LIMIT = 56 * 1024 * 1024

_PACK_ORDER = (9, 10, 11, 0, 1, 3, 4, 5, 6, 7, 8, 2)
_PACK_COLS = 9728
_PACK_OFF = {}
_off = 0
for _i in _PACK_ORDER:
    _PACK_OFF[_i] = _off
    _off += IN_SIZES[_i]

ROUTE_COLS = V7X_LANES
MOE_TM = 256
TOK_TM = 256
MM_TM = 512
CTX_BLK = N_LAT // CTX_LEN

F32 = jnp.float32
BF16 = jnp.bfloat16
_LOG2E = 1.4426950408889634
_NT = (((1,), (1,)), ((), ()))


def _cparams(sem):
    return pltpu.CompilerParams(dimension_semantics=sem, vmem_limit_bytes=V7X_VMEM_LIMIT)


def _seg(i, tm):
    return jnp.minimum(i // (SEQ // tm), 2)


def _pcol(idx, width):
    assert _PACK_OFF[idx] % width == 0
    return _PACK_OFF[idx] // width


def _ada_kernel(c_ref, w_ref, b_ref, o_ref):
    c = c_ref[...]
    a = c * jax.nn.sigmoid(c)
    o_ref[0] = jnp.dot(a, w_ref[0], preferred_element_type=F32,
                       precision=lax.Precision.HIGHEST) + b_ref[0]


def _ada(c_rows, ada_w, ada_b):
    tn = 1024
    n = 6 * D_MODEL
    return pl.pallas_call(
        _ada_kernel,
        grid=(DEPTH, n // tn),
        in_specs=[
            pl.BlockSpec((8, D_MODEL), lambda l, j: (0, 0)),
            pl.BlockSpec((1, D_MODEL, tn), lambda l, j: (l, 0, j)),
            pl.BlockSpec((1, 1, tn), lambda l, j: (l, 0, j)),
        ],
        out_specs=pl.BlockSpec((1, 8, tn), lambda l, j: (l, 0, j)),
        out_shape=jax.ShapeDtypeStruct((DEPTH, 8, n), F32),
        compiler_params=_cparams(("arbitrary", "arbitrary")),
        name="ada",
    )(c_rows, ada_w, ada_b.reshape(DEPTH, 1, n))


def _norm_mod_kernel(x_ref, g_ref, sh_ref, sc_ref, h_ref):
    x = x_ref[...]
    xn = x * lax.rsqrt(jnp.mean(x * x, axis=-1, keepdims=True) + EPS) * g_ref[...]
    h_ref[...] = (xn * (1.0 + sc_ref[0]) + sh_ref[0]).astype(h_ref.dtype)


def _norm_mod(x, g, mod, k_shift, k_scale):
    m = x.shape[0]
    tm = TOK_TM
    return pl.pallas_call(
        _norm_mod_kernel,
        grid=(m // tm,),
        in_specs=[pl.BlockSpec((tm, D_MODEL), lambda i: (i, 0)),
                  pl.BlockSpec((1, D_MODEL), lambda i: (0, 0)),
                  pl.BlockSpec((1, 1, D_MODEL), lambda i: (_seg(i, tm) * 6 + k_shift, 0, 0)),
                  pl.BlockSpec((1, 1, D_MODEL), lambda i: (_seg(i, tm) * 6 + k_scale, 0, 0))],
        out_specs=pl.BlockSpec((tm, D_MODEL), lambda i: (i, 0)),
        out_shape=jax.ShapeDtypeStruct((m, D_MODEL), BF16),
        compiler_params=_cparams(("arbitrary",)),
        name="norm_mod",
    )(x, g.reshape(1, D_MODEL), mod, mod)


def _out_route_kernel(y_ref, w_ref, r_ref, g1_ref, gn_ref, sh_ref, sc_ref, wr_ref, br_ref,
                      xo_ref, h_ref, lg_ref):
    acc = jnp.dot(y_ref[...], w_ref[...], preferred_element_type=F32)
    x = r_ref[...] + g1_ref[0] * acc
    xo_ref[...] = x
    xn = x * lax.rsqrt(jnp.mean(x * x, axis=-1, keepdims=True) + EPS) * gn_ref[...]
    h = xn * (1.0 + sc_ref[0]) + sh_ref[0]
    h_ref[...] = h
    hi = h.astype(BF16)
    lo = (h - hi.astype(F32)).astype(BF16)
    wr = wr_ref[...]
    w_hi = wr.astype(BF16)
    w_lo = (wr - w_hi.astype(F32)).astype(BF16)
    lg_ref[...] = (jnp.dot(hi, w_hi, preferred_element_type=F32)
                   + jnp.dot(hi, w_lo, preferred_element_type=F32)
                   + jnp.dot(lo, w_hi, preferred_element_type=F32) + br_ref[...])


def _out_route(y, w_out_bf16, res, mod, norm_g, wr, br):
    m = y.shape[0]
    tm = TOK_TM

    def modrow(k):
        return pl.BlockSpec((1, 1, D_MODEL), lambda i: (_seg(i, tm) * 6 + k, 0, 0))

    row = pl.BlockSpec((tm, D_MODEL), lambda i: (i, 0))
    return pl.pallas_call(
        _out_route_kernel,
        grid=(m // tm,),
        in_specs=[row,
                  pl.BlockSpec((D_MODEL, D_MODEL), lambda i: (0, 0)),
                  row, modrow(2),
                  pl.BlockSpec((1, D_MODEL), lambda i: (0, 0)),
                  modrow(3), modrow(4),
                  pl.BlockSpec((D_MODEL, ROUTE_COLS), lambda i: (0, 0)),
                  pl.BlockSpec((1, ROUTE_COLS), lambda i: (0, 0))],
        out_specs=[row, row, pl.BlockSpec((tm, ROUTE_COLS), lambda i: (i, 0))],
        out_shape=[jax.ShapeDtypeStruct((m, D_MODEL), F32),
                   jax.ShapeDtypeStruct((m, D_MODEL), F32),
                   jax.ShapeDtypeStruct((m, ROUTE_COLS), F32)],
        compiler_params=_cparams(("arbitrary",)),
        name="out_route",
    )(y, w_out_bf16, res, mod, norm_g.reshape(1, D_MODEL), mod, mod, wr, br)


def _mm_bf16_kernel(x_ref, w_ref, o_ref):
    o_ref[...] = jnp.dot(x_ref[...], w_ref[0], preferred_element_type=F32).astype(o_ref.dtype)


def _mm(x, w, layer, out_dtype, tn):
    m, k = x.shape
    n = w.shape[2]
    tm = MM_TM
    assert x.dtype == BF16 and w.dtype == BF16
    return pl.pallas_call(
        _mm_bf16_kernel,
        grid=(n // tn, m // tm),
        in_specs=[pl.BlockSpec((tm, k), lambda j, i: (i, 0)),
                  pl.BlockSpec((1, k, tn), lambda j, i: (layer, 0, j))],
        out_specs=pl.BlockSpec((tm, tn), lambda j, i: (i, j)),
        out_shape=jax.ShapeDtypeStruct((m, n), out_dtype),
        compiler_params=_cparams(("arbitrary", "arbitrary")),
        name="mm_bf16",
    )(x, w)


def _row_rms(x, g):
    return x * lax.rsqrt(jnp.mean(x * x, axis=-1, keepdims=True) + EPS) * g


_MLA_PAD = 2 * V7X_LANES


def _mla_prep_kernel(cq_ref, ckv_ref, kr_ref, wq_ref, wkv_ref, gqi_ref, gkvi_ref,
                     gq_ref, gkn_ref, gkr_ref, c_ref, s_ref, q_ref, k_ref, vt_ref):
    c = c_ref[...]
    s = s_ref[...]
    hw = MLA_HEADS * MLA_NOPE

    def rot(t):
        return t * c + (pltpu.roll(t, 32, 1) + pltpu.roll(t, 96, 1)) * s

    cqn = _row_rms(cq_ref[...].astype(F32), gqi_ref[...]).astype(BF16)
    qf = jnp.dot(cqn, wq_ref[...], preferred_element_type=F32)
    gq = gq_ref[...]
    inv = 1.0 / MLA_QK
    for h in range(MLA_HEADS):
        nope = qf[:, h * 128:(h + 1) * 128]
        t = qf[:, hw + h * 128:hw + (h + 1) * 128]
        ss = jnp.sum(nope * nope, axis=-1, keepdims=True) + jnp.sum(t * t, axis=-1, keepdims=True)
        r = lax.rsqrt(ss * inv + EPS)
        q_ref[h, :, 0:128] = (nope * r * gq[:, 0:128]).astype(BF16)
        q_ref[h, :, 128:256] = rot(t * r * gq[:, 128:256]).astype(BF16)

    ckvn = _row_rms(ckv_ref[...].astype(F32), gkvi_ref[...]).astype(BF16)
    kvf = jnp.dot(ckvn, wkv_ref[...], preferred_element_type=F32)
    kr = kr_ref[...].astype(F32)
    ssr = jnp.sum(kr * kr, axis=-1, keepdims=True)
    yrot = rot(kr * gkr_ref[...])
    gkn = gkn_ref[...]
    for h in range(MLA_HEADS):
        nope = kvf[:, h * 128:(h + 1) * 128]
        r = lax.rsqrt((jnp.sum(nope * nope, axis=-1, keepdims=True) + ssr) * inv + EPS)
        k_ref[h, :, 0:128] = (nope * r * gkn).astype(BF16)
        k_ref[h, :, 128:256] = (yrot * r).astype(BF16)
        vt_ref[h] = kvf[:, hw + h * 128:hw + (h + 1) * 128].T.astype(BF16)


def _mla_prep(proj, p, tabs):
    tm = TOK_TM
    hw = MLA_HEADS * MLA_NOPE
    wq = p['mla_w_uq'].reshape(MLA_Q_RANK, MLA_HEADS, MLA_QK)
    wq_rope = jnp.pad(wq[:, :, MLA_NOPE:], ((0, 0), (0, 0), (0, 128 - MLA_ROPE)))
    wq = jnp.concatenate([wq[:, :, :MLA_NOPE].reshape(MLA_Q_RANK, hw),
                          wq_rope.reshape(MLA_Q_RANK, hw)], axis=1).astype(BF16)
    wkv = p['mla_w_ukv'].reshape(MLA_KV_RANK, MLA_HEADS, MLA_NOPE + MLA_V)
    wkv = jnp.concatenate([wkv[:, :, :MLA_NOPE].reshape(MLA_KV_RANK, hw),
                           wkv[:, :, MLA_NOPE:].reshape(MLA_KV_RANK, hw)], axis=1).astype(BF16)
    zpad = jnp.zeros((128 - MLA_ROPE,), F32)
    gq = (jnp.concatenate([p['mla_qn_g'], zpad]) * (MLA_QK ** -0.5 * _LOG2E)).reshape(1, _MLA_PAD)
    gkn = p['mla_kn_g'][:MLA_NOPE].reshape(1, 128)
    gkr = jnp.concatenate([p['mla_kn_g'][MLA_NOPE:], zpad]).reshape(1, 128)

    def const(shape):
        return pl.BlockSpec(shape, lambda i: (0,) * len(shape))

    return pl.pallas_call(
        _mla_prep_kernel,
        grid=(N_TOK // tm,),
        in_specs=[pl.BlockSpec((tm, MLA_Q_RANK), lambda i: (i, _pcol(0, MLA_Q_RANK))),
                  pl.BlockSpec((tm, MLA_KV_RANK), lambda i: (i, _pcol(1, MLA_KV_RANK))),
                  pl.BlockSpec((tm, 128), lambda i: (i, _pcol(2, 128))),
                  const((MLA_Q_RANK, 2 * hw)), const((MLA_KV_RANK, 2 * hw)),
                  const((1, MLA_Q_RANK)), const((1, MLA_KV_RANK)),
                  const((1, _MLA_PAD)), const((1, 128)), const((1, 128)),
                  pl.BlockSpec((tm, 128), lambda i: (_rope_block(i), 0)),
                  pl.BlockSpec((tm, 128), lambda i: (_rope_block(i), 0))],
        out_specs=[pl.BlockSpec((MLA_HEADS, tm, _MLA_PAD), lambda i: (0, i, 0)),
                   pl.BlockSpec((MLA_HEADS, tm, _MLA_PAD), lambda i: (0, i, 0)),
                   pl.BlockSpec((MLA_HEADS, MLA_V, tm), lambda i: (0, 0, i))],
        out_shape=[jax.ShapeDtypeStruct((MLA_HEADS, N_TOK, _MLA_PAD), BF16),
                   jax.ShapeDtypeStruct((MLA_HEADS, N_TOK, _MLA_PAD), BF16),
                   jax.ShapeDtypeStruct((MLA_HEADS, MLA_V, N_TOK), BF16)],
        compiler_params=_cparams(("arbitrary",)),
        name="mla_prep",
    )(proj, proj, proj, wq, wkv, p['mla_q_norm_g'].reshape(1, -1), p['mla_kv_norm_g'].reshape(1, -1),
      gq, gkn, gkr, tabs['mla_c'], tabs['mla_s'])


def _mla_kernel(q_ref, k1_ref, vt1_ref, k2_ref, vt2_ref, o_ref, sa_ref, sb_ref, sc_ref, acc_ref,
                *, tk, n_chunks):
    q = q_ref[0]
    tq = q.shape[0]
    hq = tq // 2
    qs = (q[:hq], q[hq:])

    def scores(dst_ref, kc):
        out = []
        for j in range(2):
            st = lax.dot_general(kc, qs[j], _NT, preferred_element_type=F32)
            dst_ref[j] = st
            out.append(jnp.max(st, axis=0, keepdims=True))
        return out

    def k_chunk(c):
        return k1_ref[0, pl.ds(pl.multiple_of(c * tk, tk), tk), :]

    def vt_chunk(c):
        return vt1_ref[0, :, pl.ds(pl.multiple_of(c * tk, tk), tk)]

    def accumulate(s_ref, smax, vtc, m, l):
        m_out, l_out = [], []
        for j in range(2):
            m_new = jnp.maximum(m[j], smax[j])
            a = jnp.exp2(m[j] - m_new)
            p = jnp.exp2(s_ref[j] - m_new)
            l_out.append(a * l[j] + jnp.sum(p, axis=0, keepdims=True))
            acc_ref[j] = a * acc_ref[j] + jnp.dot(vtc, p.astype(BF16), preferred_element_type=F32)
            m_out.append(m_new)
        return m_out, l_out

    m = [jnp.full((1, hq), NEG_INF, F32)] * 2
    l = [jnp.zeros((1, hq), F32)] * 2
    acc_ref[...] = jnp.zeros_like(acc_ref)
    mx_c = scores(sc_ref, k2_ref[0])
    mx_a = scores(sa_ref, k_chunk(0))
    m, l = accumulate(sc_ref, mx_c, vt2_ref[0], m, l)

    for i in range(n_chunks // 2 - 1):
        mx_b = scores(sb_ref, k_chunk(2 * i + 1))
        m, l = accumulate(sa_ref, mx_a, vt_chunk(2 * i), m, l)
        mx_a = scores(sa_ref, k_chunk(2 * i + 2))
        m, l = accumulate(sb_ref, mx_b, vt_chunk(2 * i + 1), m, l)
    mx_b = scores(sb_ref, k_chunk(n_chunks - 1))
    m, l = accumulate(sa_ref, mx_a, vt_chunk(n_chunks - 2), m, l)
    m, l = accumulate(sb_ref, mx_b, vt_chunk(n_chunks - 1), m, l)
    for j in range(2):
        o_ref[j * hq:(j + 1) * hq, :] = (acc_ref[j] / l[j]).T.astype(o_ref.dtype)


def _mla_attn(q, k, vt, *, tq, tk):
    nq = SEQ // tq
    return pl.pallas_call(
        functools.partial(_mla_kernel, tk=tk, n_chunks=SEQ // tk),
        grid=(BATCH, MLA_HEADS, nq),
        in_specs=[pl.BlockSpec((1, tq, _MLA_PAD), lambda b, h, i: (h, b * nq + i, 0)),
                  pl.BlockSpec((1, SEQ, _MLA_PAD), lambda b, h, i: (h, b, 0)),
                  pl.BlockSpec((1, MLA_V, SEQ), lambda b, h, i: (h, 0, b)),
                  pl.BlockSpec((1, CTX_LEN, _MLA_PAD), lambda b, h, i: (h, CTX_BLK + b, 0)),
                  pl.BlockSpec((1, MLA_V, CTX_LEN), lambda b, h, i: (h, 0, CTX_BLK + b))],
        out_specs=pl.BlockSpec((tq, MLA_V), lambda b, h, i: (b * nq + i, h)),
        out_shape=jax.ShapeDtypeStruct((N_LAT, MLA_HEADS * MLA_V), BF16),
        scratch_shapes=[pltpu.VMEM((2, tk, tq // 2), F32), pltpu.VMEM((2, tk, tq // 2), F32),
                        pltpu.VMEM((2, CTX_LEN, tq // 2), F32), pltpu.VMEM((2, MLA_V, tq // 2), F32)],
        compiler_params=_cparams(("arbitrary", "arbitrary", "arbitrary")),
        name="mla_attn",
    )(q, k, vt, k, vt)


_NA_KEYS = NA_ROWS * GRID_W
NA_RB = 8


def _na_pattern(r):
    half = NA_ROWS // 2
    return jnp.where(r < half, r, jnp.where(r <= ROWS - half, half, r - (ROWS - NA_ROWS)))


def _na_kernel(q_ref, k_ref, v_ref, kc_ref, vc_ref, bias_ref, gq_ref, gk_ref, o_ref, kn_ref, kcn_ref):
    rb = pl.program_id(2)

    @pl.when(rb == 0)
    def _():
        kn_ref[...] = _row_rms(k_ref[...].astype(F32), gk_ref[...]).astype(BF16)
        kcn_ref[...] = _row_rms(kc_ref[...].astype(F32), gk_ref[...]).astype(BF16)

    q_all = _row_rms(q_ref[...].astype(F32), gq_ref[...]).astype(BF16)
    vc = vc_ref[...]
    sc_all = lax.dot_general(q_all, kcn_ref[...], _NT, preferred_element_type=F32)
    ms, ls, os_ = [], [], []
    for j in range(NA_RB):
        r = rb * NA_RB + j
        start = pl.multiple_of(jnp.clip(r - NA_ROWS // 2, 0, ROWS - NA_ROWS) * GRID_W, GRID_W)
        q = q_all[j * GRID_W:(j + 1) * GRID_W]
        k = kn_ref[pl.ds(start, _NA_KEYS), :]
        v = v_ref[pl.ds(start, _NA_KEYS), :]
        s = lax.dot_general(q, k, _NT, preferred_element_type=F32) + bias_ref[0, _na_pattern(r)]
        sc = sc_all[j * GRID_W:(j + 1) * GRID_W]
        m = jnp.maximum(jnp.max(s, axis=-1, keepdims=True), jnp.max(sc, axis=-1, keepdims=True))
        p = jnp.exp(s - m)
        ms.append(m)
        ls.append(jnp.sum(p, axis=-1, keepdims=True))
        os_.append(jnp.dot(p.astype(BF16), v, preferred_element_type=F32))
    m_all = jnp.concatenate(ms, axis=0)
    pc = jnp.exp(sc_all - m_all)
    l_all = jnp.concatenate(ls, axis=0) + jnp.sum(pc, axis=-1, keepdims=True)
    o = jnp.concatenate(os_, axis=0) + jnp.dot(pc.astype(BF16), vc, preferred_element_type=F32)
    o_ref[...] = (o / l_all).astype(o_ref.dtype)


def _na_attn(proj, bias, gq, gk):
    qrows = NA_RB * GRID_W
    nrb = ROWS // NA_RB
    cq, ck, cv = _pcol(3, NA_HD), _pcol(4, NA_HD), _pcol(5, NA_HD)
    return pl.pallas_call(
        _na_kernel,
        grid=(BATCH, NA_HEADS, nrb),
        in_specs=[pl.BlockSpec((qrows, NA_HD), lambda b, h, r: (b * nrb + r, cq + h)),
                  pl.BlockSpec((SEQ, NA_HD), lambda b, h, r: (b, ck + h)),
                  pl.BlockSpec((SEQ, NA_HD), lambda b, h, r: (b, cv + h)),
                  pl.BlockSpec((CTX_LEN, NA_HD), lambda b, h, r: (CTX_BLK + b, ck + h)),
                  pl.BlockSpec((CTX_LEN, NA_HD), lambda b, h, r: (CTX_BLK + b, cv + h)),
                  pl.BlockSpec((1, NA_ROWS, GRID_W, _NA_KEYS), lambda b, h, r: (h, 0, 0, 0)),
                  pl.BlockSpec((1, NA_HD), lambda b, h, r: (0, 0)),
                  pl.BlockSpec((1, NA_HD), lambda b, h, r: (0, 0))],
        out_specs=pl.BlockSpec((qrows, NA_HD), lambda b, h, r: (b * nrb + r, h)),
        out_shape=jax.ShapeDtypeStruct((N_LAT, NA_HEADS * NA_HD), BF16),
        scratch_shapes=[pltpu.VMEM((SEQ, NA_HD), BF16), pltpu.VMEM((CTX_LEN, NA_HD), BF16)],
        compiler_params=_cparams(("arbitrary", "arbitrary", "arbitrary")),
        name="na_attn",
    )(proj, proj, proj, proj, proj, bias, gq, gk)


def _na_bias_table(rpb):
    half = NA_ROWS // 2
    r_rep = np.array(list(range(half)) + [half] + list(range(ROWS - half + 1, ROWS)))
    start = np.clip(r_rep - half, 0, ROWS - NA_ROWS)
    dr = start[:, None] + np.arange(NA_ROWS)[None, :] - r_rep[:, None] + NA_ROWS - 1
    qc = np.arange(GRID_W)
    kcol = np.arange(GRID_W)
    col_start = np.clip(qc - NA_COLS // 2, 0, GRID_W - NA_COLS)
    in_win = (kcol[None, :] >= col_start[:, None]) & (kcol[None, :] < col_start[:, None] + NA_COLS)
    dc = np.clip(kcol[None, :] - qc[:, None], 1 - NA_COLS, NA_COLS - 1) + NA_COLS - 1
    rsel = (dr[:, :, None] == np.arange(2 * NA_ROWS - 1)).astype(np.float32)
    csel = (dc[:, :, None] == np.arange(2 * NA_COLS - 1)).astype(np.float32)
    b = jnp.einsum('pja,hab,qkb->hpqjk', rsel, rpb.astype(F32), csel,
                   precision=lax.Precision.HIGHEST)
    b = jnp.where(in_win[None, None, :, None, :], b.astype(F32), NEG_INF)
    return b.reshape(NA_HEADS, NA_ROWS, GRID_W, _NA_KEYS)


_GQA_G = GQA_HEADS // GQA_KV_HEADS
_GQA_BAND = 3 * GQA_WINDOW
_GQA_QW = GQA_HEADS * V7X_LANES


def _gqa_prep_kernel(q_ref, k_ref, gq_ref, gk_ref, c_ref, s1_ref, s2_ref, qd_ref, kn_ref):
    c = c_ref[...]
    s1 = s1_ref[...]
    s2 = s2_ref[...]
    lo = lax.broadcasted_iota(jnp.int32, (1, 128), 1) < GQA_HD

    def head_rms(x, g):
        x2 = x * x
        s_lo = jnp.sum(jnp.where(lo, x2, 0.0), axis=-1, keepdims=True)
        s_hi = jnp.sum(jnp.where(lo, 0.0, x2), axis=-1, keepdims=True)
        inv = 1.0 / GQA_HD
        r = jnp.where(lo, lax.rsqrt(s_lo * inv + EPS), lax.rsqrt(s_hi * inv + EPS))
        return x * r * g

    def rot(x):
        return x * c + pltpu.roll(x, 96, 1) * s1 + pltpu.roll(x, 32, 1) * s2

    gq = gq_ref[...]
    for j in range(GQA_HEADS // 2):
        y = rot(head_rms(q_ref[:, j * 128:(j + 1) * 128].astype(F32), gq))
        sw = pltpu.roll(y, 64, 1)
        hk = (2 * j) // _GQA_G
        if hk == 0:
            even, odd = jnp.where(lo, y, 0.0), jnp.where(lo, sw, 0.0)
        else:
            even, odd = jnp.where(lo, 0.0, sw), jnp.where(lo, 0.0, y)
        qd_ref[:, (2 * j) * 128:(2 * j + 1) * 128] = even.astype(BF16)
        qd_ref[:, (2 * j + 1) * 128:(2 * j + 2) * 128] = odd.astype(BF16)
    kn_ref[...] = rot(head_rms(k_ref[...].astype(F32), gk_ref[...])).astype(BF16)


def _gqa_prep(proj, p, tabs):
    tm = TOK_TM
    gq = (jnp.tile(p['gqa_qn_g'], 2) * (GQA_HD ** -0.5)).reshape(1, 128)
    gk = jnp.tile(p['gqa_kn_g'], 2).reshape(1, 128)
    row = pl.BlockSpec((tm, 128), lambda i: (i, 0))
    tab = pl.BlockSpec((tm, 128), lambda i: (_rope_block(i), 0))
    vec = pl.BlockSpec((1, 128), lambda i: (0, 0))
    return pl.pallas_call(
        _gqa_prep_kernel,
        grid=(N_TOK // tm,),
        in_specs=[pl.BlockSpec((tm, GQA_HEADS * GQA_HD), lambda i: (i, _pcol(6, GQA_HEADS * GQA_HD))),
                  pl.BlockSpec((tm, 128), lambda i: (i, _pcol(7, 128))),
                  vec, vec, tab, tab, tab],
        out_specs=[pl.BlockSpec((tm, _GQA_QW), lambda i: (i, 0)), row],
        out_shape=[jax.ShapeDtypeStruct((N_TOK, _GQA_QW), BF16),
                   jax.ShapeDtypeStruct((N_TOK, 128), BF16)],
        compiler_params=_cparams(("arbitrary",)),
        name="gqa_prep",
    )(proj, proj, gq, gk, tabs['gqa_c'], tabs['gqa_s1'], tabs['gqa_s2'])


def _gqa_kernel(sink_ref, q_ref, k_ref, v_ref, kc_ref, vc_ref, o_ref):
    hk = pl.program_id(1)
    n = pl.program_id(2)
    w = GQA_WINDOW
    start = pl.multiple_of(jnp.clip((n - 1) * w, 0, SEQ - _GQA_BAND), w)
    q = jnp.concatenate([q_ref[:, g * 128:(g + 1) * 128] for g in range(_GQA_G)], axis=0)
    k = k_ref[pl.ds(start, _GQA_BAND), :]
    v = v_ref[pl.ds(start, _GQA_BAND), :]
    s = lax.dot_general(q, k, _NT, preferred_element_type=F32)
    rows = lax.broadcasted_iota(jnp.int32, s.shape, 0)
    cols = lax.broadcasted_iota(jnp.int32, s.shape, 1)
    qpos = n * w + (rows & (w - 1))
    kpos = start + cols
    s = jnp.where(jnp.abs(kpos - qpos) <= GQA_WINDOW, s, NEG_INF)
    sc = lax.dot_general(q, kc_ref[...], _NT, preferred_element_type=F32)
    grow = lax.broadcasted_iota(jnp.int32, (_GQA_G * w, 1), 0) // w
    snk = jnp.full((_GQA_G * w, 1), sink_ref[hk * _GQA_G], F32)
    for g in range(1, _GQA_G):
        snk = jnp.where(grow == g, sink_ref[hk * _GQA_G + g], snk)
    m = jnp.maximum(jnp.maximum(jnp.max(s, axis=-1, keepdims=True),
                                jnp.max(sc, axis=-1, keepdims=True)), snk)
    p = jnp.exp(s - m)
    pc = jnp.exp(sc - m)
    l = jnp.sum(p, axis=-1, keepdims=True) + jnp.sum(pc, axis=-1, keepdims=True) + jnp.exp(snk - m)
    o = (jnp.dot(p.astype(BF16), v, preferred_element_type=F32)
         + jnp.dot(pc.astype(BF16), vc_ref[...], preferred_element_type=F32)) / l
    half = lax.broadcasted_iota(jnp.int32, (1, 128), 1) // GQA_HD
    o = jnp.where(half == hk, o, 0.0).astype(o_ref.dtype)
    for g in range(_GQA_G):
        o_ref[:, g * 128:(g + 1) * 128] = o[g * w:(g + 1) * w]


def _gqa_attn(qd, kn, proj, sink):
    nb = SEQ // GQA_WINDOW
    qw = _GQA_G * 128
    cv = _pcol(8, 128)
    gs = pltpu.PrefetchScalarGridSpec(
        num_scalar_prefetch=1,
        grid=(BATCH, GQA_KV_HEADS, nb),
        in_specs=[pl.BlockSpec((GQA_WINDOW, qw), lambda b, h, n, *_: (b * nb + n, h)),
                  pl.BlockSpec((SEQ, 128), lambda b, h, n, *_: (b, 0)),
                  pl.BlockSpec((SEQ, 128), lambda b, h, n, *_: (b, cv)),
                  pl.BlockSpec((CTX_LEN, 128), lambda b, h, n, *_: (CTX_BLK + b, 0)),
                  pl.BlockSpec((CTX_LEN, 128), lambda b, h, n, *_: (CTX_BLK + b, cv))],
        out_specs=pl.BlockSpec((GQA_WINDOW, qw), lambda b, h, n, *_: (b * nb + n, h)),
    )
    return pl.pallas_call(
        _gqa_kernel,
        grid_spec=gs,
        out_shape=jax.ShapeDtypeStruct((N_LAT, _GQA_QW), BF16),
        compiler_params=_cparams(("arbitrary", "arbitrary", "arbitrary")),
        name="gqa_attn",
    )(sink.astype(F32), qd, kn, proj, kn, proj)


def _ctx_kernel(sink_ref, mq_ref, mk_ref, mvt_ref, nq_ref, nk_ref, nv_ref, gnq_ref, gnk_ref,
                gq_ref, gk_ref, gv_ref, oa_ref, ob_ref, oc_ref):
    for h in range(MLA_HEADS):
        st = lax.dot_general(mk_ref[h], mq_ref[h], _NT, preferred_element_type=F32)
        p = jnp.exp2(st - jnp.max(st, axis=0, keepdims=True))
        l = jnp.sum(p, axis=0, keepdims=True)
        ot = jnp.dot(mvt_ref[h], p.astype(BF16), preferred_element_type=F32) / l
        oa_ref[:, h * MLA_V:(h + 1) * MLA_V] = ot.T.astype(oa_ref.dtype)
    for h in range(NA_HEADS):
        sl = slice(h * NA_HD, (h + 1) * NA_HD)
        q = _row_rms(nq_ref[:, sl].astype(F32), gnq_ref[...]).astype(BF16)
        k = _row_rms(nk_ref[:, sl].astype(F32), gnk_ref[...]).astype(BF16)
        s = lax.dot_general(q, k, _NT, preferred_element_type=F32)
        p = jnp.exp(s - jnp.max(s, axis=-1, keepdims=True))
        l = jnp.sum(p, axis=-1, keepdims=True)
        o = jnp.dot(p.astype(BF16), nv_ref[:, sl], preferred_element_type=F32) / l
        ob_ref[:, sl] = o.astype(ob_ref.dtype)
    half = lax.broadcasted_iota(jnp.int32, (1, 128), 1) // GQA_HD
    k = gk_ref[...]
    v = gv_ref[...]
    for h in range(GQA_HEADS):
        sl = slice(h * 128, (h + 1) * 128)
        s = lax.dot_general(gq_ref[:, sl], k, _NT, preferred_element_type=F32)
        snk = sink_ref[h]
        m = jnp.maximum(jnp.max(s, axis=-1, keepdims=True), snk)
        p = jnp.exp(s - m)
        l = jnp.sum(p, axis=-1, keepdims=True) + jnp.exp(snk - m)
        o = jnp.dot(p.astype(BF16), v, preferred_element_type=F32) / l
        oc_ref[:, sl] = jnp.where(half == h // _GQA_G, o, 0.0).astype(oc_ref.dtype)


def _ctx_attn(sink, mq, mk, mvt, proj, gnq, gnk, qd, kn):
    c = CTX_LEN
    nwid = NA_HEADS * NA_HD

    def row(width, col=0):
        return pl.BlockSpec((c, width), lambda b, *_: (CTX_BLK + b, col))

    def out(width):
        return pl.BlockSpec((c, width), lambda b, *_: (b, 0))

    vec = pl.BlockSpec((1, NA_HD), lambda b, *_: (0, 0))
    gs = pltpu.PrefetchScalarGridSpec(
        num_scalar_prefetch=1,
        grid=(BATCH,),
        in_specs=[pl.BlockSpec((MLA_HEADS, c, _MLA_PAD), lambda b, *_: (0, CTX_BLK + b, 0)),
                  pl.BlockSpec((MLA_HEADS, c, _MLA_PAD), lambda b, *_: (0, CTX_BLK + b, 0)),
                  pl.BlockSpec((MLA_HEADS, MLA_V, c), lambda b, *_: (0, 0, CTX_BLK + b)),
                  row(nwid, _pcol(3, nwid)), row(nwid, _pcol(4, nwid)), row(nwid, _pcol(5, nwid)),
                  vec, vec,
                  row(_GQA_QW), row(128), row(128, _pcol(8, 128))],
        out_specs=[out(MLA_HEADS * MLA_V), out(nwid), out(_GQA_QW)],
    )
    return pl.pallas_call(
        _ctx_kernel,
        grid_spec=gs,
        out_shape=[jax.ShapeDtypeStruct((N_CTX, MLA_HEADS * MLA_V), BF16),
                   jax.ShapeDtypeStruct((N_CTX, nwid), BF16),
                   jax.ShapeDtypeStruct((N_CTX, _GQA_QW), BF16)],
        compiler_params=_cparams(("arbitrary",)),
        name="ctx_attn",
    )(sink.astype(F32), mq, mk, mvt, proj, proj, proj, gnq, gnk, qd, kn, proj)


def _merge_kernel(*refs, has_ctx, n_lat_tiles):
    ga_ref, gb_ref, gc_ref, oa_ref, ob_ref, oc_ref = refs[:6]
    refs = refs[6:]
    if has_ctx:
        ca_ref, cb_ref, cc_ref = refs[:3]
        refs = refs[3:]
    wa_ref, wb_ref, wc_ref, y_ref = refs
    is_ctx = pl.program_id(0) >= n_lat_tiles

    def branch(g_ref, o_ref, c_ref, w_ref):
        o = o_ref[...]
        if has_ctx:
            o = jnp.where(is_ctx, c_ref[...], o)
        return jax.nn.sigmoid(g_ref[...].astype(F32)) * jnp.dot(o, w_ref[...], preferred_element_type=F32)

    y = (branch(ga_ref, oa_ref, ca_ref if has_ctx else None, wa_ref)
         + branch(gb_ref, ob_ref, cb_ref if has_ctx else None, wb_ref)
         + branch(gc_ref, oc_ref, cc_ref if has_ctx else None, wc_ref))
    y_ref[...] = y.astype(y_ref.dtype)


def _merge(proj, lat, ctx, wa, wb, wc):
    tm = TOK_TM
    has_ctx = ctx is not None
    m = N_TOK if has_ctx else N_LAT
    nl = N_LAT // tm
    ks = [o.shape[1] for o in lat]
    in_specs = [pl.BlockSpec((tm, D_MODEL), lambda i: (i, 0)),
                pl.BlockSpec((tm, D_MODEL), lambda i: (i, 1)),
                pl.BlockSpec((tm, D_MODEL), lambda i: (i, 2))]
    in_specs += [pl.BlockSpec((tm, k), lambda i: (jnp.minimum(i, nl - 1), 0)) for k in ks]
    args = [proj, proj, proj] + list(lat)
    if has_ctx:
        in_specs += [pl.BlockSpec((tm, k), lambda i: (jnp.maximum(i - nl, 0), 0)) for k in ks]
        args += list(ctx)
    in_specs += [pl.BlockSpec((k, D_MODEL), lambda i: (0, 0)) for k in ks]
    args += [wa, wb, wc]
    return pl.pallas_call(
        functools.partial(_merge_kernel, has_ctx=has_ctx, n_lat_tiles=nl),
        grid=(m // tm,),
        in_specs=in_specs,
        out_specs=pl.BlockSpec((tm, D_MODEL), lambda i: (i, 0)),
        out_shape=jax.ShapeDtypeStruct((m, D_MODEL), BF16),
        compiler_params=_cparams(("arbitrary",)),
        name="merge",
    )(*args)


def _moe_ffn_kernel(te_ref, tok_ref, nu_ref, en_ref, es_ref, h_hbm, wg_hbm, wu_hbm, wd_hbm, y_ref,
                    xbuf, sem, wgs, wus, wds, wsem, wgb, wub, wdb, *, layer):
    i = pl.program_id(0)
    tm = MOE_TM
    slot = i % 2
    n_used = nu_ref[0]

    def row_copy(tok, s, r):
        return pltpu.make_async_copy(h_hbm.at[pl.ds(tok, 1)], xbuf.at[s, pl.ds(r, 1)], sem.at[s])

    def start_gather(tile, s):
        base = tile * tm

        def body(r, c):
            row_copy(tok_ref[base + r], s, r).start()
            return c

        lax.fori_loop(0, tm, body, 0, unroll=8)

    def weight_copies(e, s):
        return (pltpu.make_async_copy(wg_hbm.at[layer, e], wgs.at[s], wsem.at[s]),
                pltpu.make_async_copy(wu_hbm.at[layer, e], wus.at[s], wsem.at[s]),
                pltpu.make_async_copy(wd_hbm.at[layer, e], wds.at[s], wsem.at[s]))

    @pl.when(i == 0)
    def _():
        for cp in weight_copies(te_ref[0], es_ref[0]):
            cp.start(priority=1)
        start_gather(0, 0)

    @pl.when(i + 1 < n_used)
    def _():
        start_gather(i + 1, 1 - slot)

    @pl.when(i < n_used)
    def _():
        @pl.when((i == 0) | (te_ref[i] != te_ref[jnp.maximum(i - 1, 0)]))
        def _():
            s = es_ref[i]

            @pl.when(en_ref[i] >= 0)
            def _():
                for cp in weight_copies(en_ref[i], 1 - s):
                    cp.start(priority=1)

            for cp in weight_copies(te_ref[i], s):
                cp.wait()
            wgb[...] = wgs[s].astype(BF16)
            wub[...] = wus[s].astype(BF16)
            wdb[...] = wds[s].astype(BF16)

        pltpu.make_async_copy(h_hbm.at[pl.ds(0, tm)], xbuf.at[slot], sem.at[slot]).wait()
        x = xbuf[slot].astype(BF16)
        hg = jnp.dot(x, wgb[...], preferred_element_type=F32)
        hu = jnp.dot(x, wub[...], preferred_element_type=F32)
        act = (hg * jax.nn.sigmoid(hg)) * hu
        y_ref[...] = jnp.dot(act.astype(BF16), wdb[...], preferred_element_type=F32)

    @pl.when(i >= n_used)
    def _():
        y_ref[...] = jnp.zeros_like(y_ref)


def _moe_ffn(h, plan, wg, wu, wd, layer):
    tile_expert, slot_token, n_used, next_expert, expert_slot = plan
    p = slot_token.shape[0]
    tm = MOE_TM
    nt = p // tm
    anyspec = pl.BlockSpec(memory_space=pl.ANY)
    gs = pltpu.PrefetchScalarGridSpec(
        num_scalar_prefetch=5,
        grid=(nt,),
        in_specs=[anyspec, anyspec, anyspec, anyspec],
        out_specs=pl.BlockSpec((tm, D_MODEL), lambda i, *_: (i, 0)),
        scratch_shapes=[pltpu.VMEM((2, tm, D_MODEL), F32),
                        pltpu.SemaphoreType.DMA((2,)),
                        pltpu.VMEM((2, D_MODEL, MOE_HIDDEN), F32),
                        pltpu.VMEM((2, D_MODEL, MOE_HIDDEN), F32),
                        pltpu.VMEM((2, MOE_HIDDEN, D_MODEL), F32),
                        pltpu.SemaphoreType.DMA((2,)),
                        pltpu.VMEM((D_MODEL, MOE_HIDDEN), BF16),
                        pltpu.VMEM((D_MODEL, MOE_HIDDEN), BF16),
                        pltpu.VMEM((MOE_HIDDEN, D_MODEL), BF16)],
    )
    return pl.pallas_call(
        functools.partial(_moe_ffn_kernel, layer=layer),
        grid_spec=gs,
        out_shape=jax.ShapeDtypeStruct((p, D_MODEL), F32),
        compiler_params=_cparams(("arbitrary",)),
        name="moe_ffn",
    )(tile_expert, slot_token, n_used, next_expert, expert_slot, h, wg, wu, wd)


def _moe_combine_kernel(pos_ref, y_hbm, x_ref, w_ref, g_ref, o_ref, ybuf, sem):
    i = pl.program_id(0)
    nt = pl.num_programs(0)
    tm = TOK_TM
    slot = i % 2

    def row_copy(src, s, r):
        return pltpu.make_async_copy(y_hbm.at[pl.ds(src, 1)], ybuf.at[s, pl.ds(r, 1)], sem.at[s])

    def start_gather(tile, s):
        base = tile * tm

        def body(r, c):
            row_copy(pos_ref[2 * (base + r)], s, r).start(priority=0)
            row_copy(pos_ref[2 * (base + r) + 1], s, tm + r).start(priority=1)
            return c

        lax.fori_loop(0, tm, body, 0, unroll=4)

    @pl.when(i == 0)
    def _():
        start_gather(0, 0)

    @pl.when(i + 1 < nt)
    def _():
        start_gather(i + 1, 1 - slot)

    pltpu.make_async_copy(y_hbm.at[pl.ds(0, 2 * tm)], ybuf.at[slot], sem.at[slot]).wait()
    w = w_ref[...]
    y = ybuf[slot, pl.ds(0, tm), :] * w[:, 0:1] + ybuf[slot, pl.ds(tm, tm), :] * w[:, 1:2]
    o_ref[...] = x_ref[...] + g_ref[0] * y


def _moe_combine(y, pos, w_sel, x, mod, k_gate):
    m = x.shape[0]
    tm = TOK_TM
    gs = pltpu.PrefetchScalarGridSpec(
        num_scalar_prefetch=1,
        grid=(m // tm,),
        in_specs=[pl.BlockSpec(memory_space=pl.ANY),
                  pl.BlockSpec((tm, D_MODEL), lambda i, pos: (i, 0)),
                  pl.BlockSpec((tm, MOE_TOPK), lambda i, pos: (i, 0)),
                  pl.BlockSpec((1, 1, D_MODEL), lambda i, pos: (_seg(i, tm) * 6 + k_gate, 0, 0))],
        out_specs=pl.BlockSpec((tm, D_MODEL), lambda i, pos: (i, 0)),
        scratch_shapes=[pltpu.VMEM((2, 2 * tm, D_MODEL), F32),
                        pltpu.SemaphoreType.DMA((2,))],
    )
    return pl.pallas_call(
        _moe_combine_kernel,
        grid_spec=gs,
        out_shape=jax.ShapeDtypeStruct((m, D_MODEL), F32),
        compiler_params=_cparams(("arbitrary",)),
        name="moe_combine",
    )(pos, y, x, w_sel, mod)


def _route(logits, m):
    tm = MOE_TM
    gp = jax.nn.softmax(logits[:, :MOE_GROUPS], axis=-1)
    g_idx = jnp.argmax(gp, axis=-1).astype(jnp.int32)[:, None]
    g_w = jnp.max(gp, axis=-1, keepdims=True)
    el = logits[:, MOE_GROUPS:MOE_GROUPS + MOE_EXPERTS].reshape(m, MOE_GROUPS, MOE_PER_GROUP)
    g_onehot = (g_idx == jnp.arange(MOE_GROUPS, dtype=jnp.int32)[None, :]).astype(F32)
    el_g = jnp.sum(el * g_onehot[:, :, None], axis=1)
    i0 = jnp.argmax(el_g, axis=-1).astype(jnp.int32)[:, None]
    l0 = jnp.max(el_g, axis=-1, keepdims=True)
    rest = jnp.where(jnp.arange(MOE_PER_GROUP, dtype=jnp.int32)[None, :] == i0, -jnp.inf, el_g)
    i1 = jnp.argmax(rest, axis=-1).astype(jnp.int32)[:, None]
    l1 = jnp.max(rest, axis=-1, keepdims=True)
    top_l = jnp.concatenate([l0, l1], axis=-1)
    top_i = jnp.concatenate([i0, i1], axis=-1)
    w_sel = jax.nn.softmax(top_l, axis=-1) * g_w
    eid = (g_idx * MOE_PER_GROUP + top_i).astype(jnp.int32)

    a = m * MOE_TOPK
    e_flat = eid.reshape(a)
    onehot = (e_flat[:, None] == jnp.arange(MOE_EXPERTS, dtype=jnp.int32)[None, :]).astype(jnp.int32)
    csum = jnp.cumsum(onehot, axis=0)
    rank = jnp.sum(csum * onehot, axis=1) - 1
    counts = csum[-1]
    padded = ((counts + tm - 1) // tm) * tm
    ends = jnp.cumsum(padded)
    starts = ends - padded
    pos = (jnp.sum(onehot * starts[None, :], axis=1) + rank).astype(jnp.int32)
    p = a + MOE_EXPERTS * tm
    slot_token = jnp.zeros((p,), jnp.int32).at[pos].set(jnp.arange(a, dtype=jnp.int32) // MOE_TOPK)
    n_used = (ends[-1] // tm).astype(jnp.int32).reshape(1)
    tile_start = jnp.arange(p // tm, dtype=jnp.int32) * tm
    last_e = jnp.max(jnp.where(counts > 0, jnp.arange(MOE_EXPERTS, dtype=jnp.int32), 0))
    tile_expert = jnp.minimum(
        jnp.sum((ends[None, :] <= tile_start[:, None]).astype(jnp.int32), axis=1), last_e)
    eidx = jnp.arange(MOE_EXPERTS, dtype=jnp.int32)
    used = counts > 0
    later = used[None, :] & (eidx[None, :] > eidx[:, None])
    next_used = jnp.min(jnp.where(later, eidx[None, :], MOE_EXPERTS), axis=1)
    next_used = jnp.where(next_used == MOE_EXPERTS, -1, next_used).astype(jnp.int32)
    ordinal = (jnp.cumsum(used.astype(jnp.int32)) - 1) % 2
    t_onehot = (tile_expert[:, None] == eidx[None, :]).astype(jnp.int32)
    next_expert = jnp.sum(t_onehot * next_used[None, :], axis=1).astype(jnp.int32)
    expert_slot = jnp.sum(t_onehot * ordinal[None, :], axis=1).astype(jnp.int32)
    return (tile_expert, slot_token, n_used, next_expert, expert_slot), w_sel, pos


def _rope_angles(rot_dim):
    t = np.arange(SEQ)
    row = (t // GRID_W).astype(np.float64)
    col = (t % GRID_W).astype(np.float64)
    n_freq = rot_dim // 4
    inv = ROPE_THETA ** (-np.arange(n_freq, dtype=np.float64) / n_freq)
    ang = np.concatenate([row[:, None] * inv, col[:, None] * inv], axis=-1)
    return np.cos(ang), np.sin(ang)


def _rope_tables():
    def rows(lat, ident):
        ctx = np.zeros((TOK_TM, 128)) + ident
        return jnp.asarray(np.concatenate([lat, ctx], axis=0).astype(np.float32))

    z32 = np.zeros((SEQ, 32))
    z64 = np.zeros((SEQ, 64))
    lane = np.arange(128)
    cm, sm = _rope_angles(MLA_ROPE)
    mla_c = rows(np.concatenate([cm, cm, z64], axis=1), (lane < 64).astype(np.float64))
    mla_s = rows(np.concatenate([-sm, sm, z64], axis=1), 0.0)
    cg, sg = _rope_angles(GQA_HD)
    gqa_c = rows(np.concatenate([cg, cg, cg, cg], axis=1), 1.0)
    gqa_s1 = rows(np.concatenate([-sg, z32, -sg, z32], axis=1), 0.0)
    gqa_s2 = rows(np.concatenate([z32, sg, z32, sg], axis=1), 0.0)
    return {'mla_c': mla_c, 'mla_s': mla_s, 'gqa_c': gqa_c, 'gqa_s1': gqa_s1, 'gqa_s2': gqa_s2}


def _rope_block(i):
    per_seq = SEQ // TOK_TM
    return jnp.where(i < BATCH * per_seq, i % per_seq, per_seq)


def _pack_w_in_kernel(w_ref, o_ref):
    w = w_ref[0]
    bounds = (0,) + IN_SPLITS + (sum(IN_SIZES),)
    parts = [w[:, bounds[i]:bounds[i + 1]] for i in _PACK_ORDER]
    parts.append(jnp.zeros((w.shape[0], _PACK_COLS - sum(IN_SIZES)), w.dtype))
    o_ref[0] = jnp.concatenate(parts, axis=1).astype(BF16)


def _pack_w_in(w_in):
    tk = 256
    n_in = sum(IN_SIZES)
    return pl.pallas_call(
        _pack_w_in_kernel,
        grid=(DEPTH, D_MODEL // tk),
        in_specs=[pl.BlockSpec((1, tk, n_in), lambda l, i: (l, i, 0))],
        out_specs=pl.BlockSpec((1, tk, _PACK_COLS), lambda l, i: (l, i, 0)),
        out_shape=jax.ShapeDtypeStruct((DEPTH, D_MODEL, _PACK_COLS), BF16),
        compiler_params=_cparams(("arbitrary", "arbitrary")),
        name="pack_w_in",
    )(w_in)


def _pad_w_o_gqa(w):
    w4 = w.reshape(GQA_KV_HEADS, _GQA_G, GQA_HD, D_MODEL)
    z = jnp.zeros_like(w4[0:1])
    halves = [jnp.concatenate([w4[hk:hk + 1] if hk == half else z for hk in range(GQA_KV_HEADS)], axis=0)
              for half in range(GQA_KV_HEADS)]
    return jnp.stack(halves, axis=2).reshape(_GQA_QW, D_MODEL).astype(BF16)


def _token_mixer(h, p, tabs, ctx_out):
    proj = _mm(h, p['w_in_packed'], p['layer'], BF16, _PACK_COLS // 4)
    mq, mk, mvt = _mla_prep(proj, p, tabs)
    oa = _mla_attn(mq, mk, mvt, tq=512, tk=1024)
    gnq = (p['na_qn_g'] * (NA_HD ** -0.5)).reshape(1, NA_HD)
    gnk = p['na_kn_g'].reshape(1, NA_HD)
    ob = _na_attn(proj, _na_bias_table(p['na_rpb']), gnq, gnk)
    qd, kn = _gqa_prep(proj, p, tabs)
    oc = _gqa_attn(qd, kn, proj, p['gqa_sink'])
    ctx = _ctx_attn(p['gqa_sink'], mq, mk, mvt, proj, gnq, gnk, qd, kn) if ctx_out else None
    return _merge(proj, (oa, ob, oc), ctx, p['w_o_mla'].astype(BF16), p['w_o_na'].astype(BF16),
                  _pad_w_o_gqa(p['w_o_gqa']))


def _post_mixer(y, xt, mod, norm_g, w_out_l, p):
    m = y.shape[0]
    wr = jnp.concatenate([p['moe_w_group'], p['moe_w_expert'],
                          jnp.zeros((D_MODEL, ROUTE_COLS - MOE_GROUPS - MOE_EXPERTS), F32)], axis=1)
    br = jnp.concatenate([p['moe_b_group'], p['moe_b_expert'],
                          jnp.zeros((ROUTE_COLS - MOE_GROUPS - MOE_EXPERTS,), F32)]).reshape(1, ROUTE_COLS)
    x_mid, h, logits = _out_route(y, w_out_l.astype(BF16), xt, mod, norm_g, wr, br)
    plan, w_sel, pos = _route(logits, m)
    yy = _moe_ffn(h, plan, p['moe_w_gate'], p['moe_w_up'], p['moe_w_down'], p['layer'])
    return _moe_combine(yy, pos, w_sel, x_mid, mod, 5)


def kernel(x, c, ctx, c_ctx, ada_w, ada_b, norm_mix_g, norm_ffn_g, w_in,
           mla_q_norm_g, mla_w_uq, mla_kv_norm_g, mla_w_ukv, mla_qn_g, mla_kn_g,
           na_qn_g, na_kn_g, na_rpb, gqa_qn_g, gqa_kn_g, gqa_sink,
           w_o_mla, w_o_na, w_o_gqa, w_out,
           moe_w_group, moe_b_group, moe_w_expert, moe_b_expert,
           moe_w_gate, moe_w_up, moe_w_down):
    xt = jnp.concatenate([x.reshape(N_LAT, D_MODEL), ctx.reshape(N_CTX, D_MODEL)], axis=0)
    c_rows = jnp.concatenate([c, c_ctx[None, :], jnp.zeros((8 - BATCH - 1, D_MODEL), F32)], axis=0)
    mod_all = _ada(c_rows, ada_w, ada_b)
    tabs = _rope_tables()
    w_in_packed = _pack_w_in(w_in)
    for l in range(DEPTH):
        ctx_out = l < DEPTH - 1
        p = {
            'w_in_packed': w_in_packed, 'mla_q_norm_g': mla_q_norm_g[l], 'mla_w_uq': mla_w_uq[l],
            'mla_kv_norm_g': mla_kv_norm_g[l], 'mla_w_ukv': mla_w_ukv[l],
            'mla_qn_g': mla_qn_g[l], 'mla_kn_g': mla_kn_g[l],
            'na_qn_g': na_qn_g[l], 'na_kn_g': na_kn_g[l], 'na_rpb': na_rpb[l],
            'gqa_qn_g': gqa_qn_g[l], 'gqa_kn_g': gqa_kn_g[l], 'gqa_sink': gqa_sink[l],
            'w_o_mla': w_o_mla[l], 'w_o_na': w_o_na[l], 'w_o_gqa': w_o_gqa[l],
            'moe_w_group': moe_w_group[l], 'moe_b_group': moe_b_group[l],
            'moe_w_expert': moe_w_expert[l], 'moe_b_expert': moe_b_expert[l],
            'moe_w_gate': moe_w_gate, 'moe_w_up': moe_w_up, 'moe_w_down': moe_w_down, 'layer': l,
        }
        mod = mod_all[l].reshape(8 * 6, 1, D_MODEL)
        h = _norm_mod(xt, norm_mix_g[l], mod, 0, 1)
        y = _token_mixer(h, p, tabs, ctx_out)
        xt = _post_mixer(y, xt, mod, norm_ffn_g[l], w_out[l], p)
    return xt[:N_LAT].reshape(BATCH, SEQ, D_MODEL)
```

```python
import functools

import numpy as np
import jax
import jax.numpy as jnp
from jax import lax
from jax.experimental import pallas as pl
from jax.experimental.pallas import tpu as pltpu

D_MODEL = 2048
BATCH = 2
SEQ = 4096
DEPTH = 2
GRID_W = 64
CTX_LEN = 256
EPS = 1e-6
ROPE_THETA = 10000.0
NEG_INF = -1e30

MLA_HEADS = 8
MLA_Q_RANK = 512
MLA_KV_RANK = 512
MLA_NOPE = 128
MLA_ROPE = 64
MLA_QK = MLA_NOPE + MLA_ROPE
MLA_V = 128
NA_HEADS = 4
NA_HD = 128
NA_ROWS = 8
NA_COLS = 16
GQA_HEADS = 8
GQA_KV_HEADS = 2
GQA_HD = 64
GQA_WINDOW = 128
MOE_GROUPS = 4
MOE_PER_GROUP = 8
MOE_EXPERTS = MOE_GROUPS * MOE_PER_GROUP
MOE_TOPK = 2
MOE_HIDDEN = 512

IN_SIZES = (MLA_Q_RANK, MLA_KV_RANK, MLA_ROPE,
            NA_HEADS * NA_HD, NA_HEADS * NA_HD, NA_HEADS * NA_HD,
            GQA_HEADS * GQA_HD, GQA_KV_HEADS * GQA_HD, GQA_KV_HEADS * GQA_HD,
            D_MODEL, D_MODEL, D_MODEL)
IN_SPLITS = tuple(int(s) for s in np.cumsum(IN_SIZES)[:-1])

N_LAT = BATCH * SEQ
N_CTX = BATCH * CTX_LEN
N_TOK = N_LAT + N_CTX
ROWS = SEQ // GRID_W

V7X_LANES = 128
V7X_VMEM_LIMIT = 56 * 1024 * 1024

_PACK_ORDER = (9, 10, 11, 0, 1, 3, 4, 5, 6, 7, 8, 2)
_PACK_COLS = 9728
_PACK_OFF = {}
_off = 0
for _i in _PACK_ORDER:
    _PACK_OFF[_i] = _off
    _off += IN_SIZES[_i]

ROUTE_COLS = V7X_LANES
MOE_TM = 256
TOK_TM = 256
MM_TM = 512
CTX_BLK = N_LAT // CTX_LEN

F32 = jnp.float32
BF16 = jnp.bfloat16
_LOG2E = 1.4426950408889634
_NT = (((1,), (1,)), ((), ()))


def _cparams(sem):
    return pltpu.CompilerParams(dimension_semantics=sem, vmem_limit_bytes=V7X_VMEM_LIMIT)


def _seg(i, tm):
    return jnp.minimum(i // (SEQ // tm), 2)


def _pcol(idx, width):
    assert _PACK_OFF[idx] % width == 0
    return _PACK_OFF[idx] // width


def _ada_kernel(c_ref, w_ref, b_ref, o_ref):
    c = c_ref[...]
    a = c * jax.nn.sigmoid(c)
    o_ref[0] = jnp.dot(a, w_ref[0], preferred_element_type=F32,
                       precision=lax.Precision.HIGHEST) + b_ref[0]


def _ada(c_rows, ada_w, ada_b):
    tn = 1024
    n = 6 * D_MODEL
    return pl.pallas_call(
        _ada_kernel,
        grid=(DEPTH, n // tn),
        in_specs=[
            pl.BlockSpec((8, D_MODEL), lambda l, j: (0, 0)),
            pl.BlockSpec((1, D_MODEL, tn), lambda l, j: (l, 0, j)),
            pl.BlockSpec((1, 1, tn), lambda l, j: (l, 0, j)),
        ],
        out_specs=pl.BlockSpec((1, 8, tn), lambda l, j: (l, 0, j)),
        out_shape=jax.ShapeDtypeStruct((DEPTH, 8, n), F32),
        compiler_params=_cparams(("arbitrary", "arbitrary")),
        name="ada",
    )(c_rows, ada_w, ada_b.reshape(DEPTH, 1, n))


def _norm_mod_kernel(x_ref, g_ref, sh_ref, sc_ref, h_ref):
    x = x_ref[...]
    xn = x * lax.rsqrt(jnp.mean(x * x, axis=-1, keepdims=True) + EPS) * g_ref[...]
    h_ref[...] = (xn * (1.0 + sc_ref[0]) + sh_ref[0]).astype(h_ref.dtype)


def _norm_mod(x, g, mod, k_shift, k_scale):
    m = x.shape[0]
    tm = TOK_TM
    return pl.pallas_call(
        _norm_mod_kernel,
        grid=(m // tm,),
        in_specs=[pl.BlockSpec((tm, D_MODEL), lambda i: (i, 0)),
                  pl.BlockSpec((1, D_MODEL), lambda i: (0, 0)),
                  pl.BlockSpec((1, 1, D_MODEL), lambda i: (_seg(i, tm) * 6 + k_shift, 0, 0)),
                  pl.BlockSpec((1, 1, D_MODEL), lambda i: (_seg(i, tm) * 6 + k_scale, 0, 0))],
        out_specs=pl.BlockSpec((tm, D_MODEL), lambda i: (i, 0)),
        out_shape=jax.ShapeDtypeStruct((m, D_MODEL), BF16),
        compiler_params=_cparams(("arbitrary",)),
        name="norm_mod",
    )(x, g.reshape(1, D_MODEL), mod, mod)


def _out_route_kernel(y_ref, w_ref, r_ref, g1_ref, gn_ref, sh_ref, sc_ref, wr_ref, br_ref,
                      xo_ref, h_ref, lg_ref):
    acc = jnp.dot(y_ref[...], w_ref[...], preferred_element_type=F32)
    x = r_ref[...] + g1_ref[0] * acc
    xo_ref[...] = x
    xn = x * lax.rsqrt(jnp.mean(x * x, axis=-1, keepdims=True) + EPS) * gn_ref[...]
    h = xn * (1.0 + sc_ref[0]) + sh_ref[0]
    h_ref[...] = h
    hi = h.astype(BF16)
    lo = (h - hi.astype(F32)).astype(BF16)
    wr = wr_ref[...]
    w_hi = wr.astype(BF16)
    w_lo = (wr - w_hi.astype(F32)).astype(BF16)
    lg_ref[...] = (jnp.dot(hi, w_hi, preferred_element_type=F32)
                   + jnp.dot(hi, w_lo, preferred_element_type=F32)
                   + jnp.dot(lo, w_hi, preferred_element_type=F32) + br_ref[...])


def _out_route(y, w_out_bf16, res, mod, norm_g, wr, br):
    m = y.shape[0]
    tm = TOK_TM

    def modrow(k):
        return pl.BlockSpec((1, 1, D_MODEL), lambda i: (_seg(i, tm) * 6 + k, 0, 0))

    row = pl.BlockSpec((tm, D_MODEL), lambda i: (i, 0))
    return pl.pallas_call(
        _out_route_kernel,
        grid=(m // tm,),
        in_specs=[row,
                  pl.BlockSpec((D_MODEL, D_MODEL), lambda i: (0, 0)),
                  row, modrow(2),
                  pl.BlockSpec((1, D_MODEL), lambda i: (0, 0)),
                  modrow(3), modrow(4),
                  pl.BlockSpec((D_MODEL, ROUTE_COLS), lambda i: (0, 0)),
                  pl.BlockSpec((1, ROUTE_COLS), lambda i: (0, 0))],
        out_specs=[row, row, pl.BlockSpec((tm, ROUTE_COLS), lambda i: (i, 0))],
        out_shape=[jax.ShapeDtypeStruct((m, D_MODEL), F32),
                   jax.ShapeDtypeStruct((m, D_MODEL), F32),
                   jax.ShapeDtypeStruct((m, ROUTE_COLS), F32)],
        compiler_params=_cparams(("arbitrary",)),
        name="out_route",
    )(y, w_out_bf16, res, mod, norm_g.reshape(1, D_MODEL), mod, mod, wr, br)


def _mm_bf16_kernel(x_ref, w_ref, o_ref):
    o_ref[...] = lax.dot_general(x_ref[...], w_ref[0], _NT,
                                 preferred_element_type=F32).astype(o_ref.dtype)


def _mm(x, wt, layer, out_dtype, tn):
    m, k = x.shape
    n = wt.shape[1]
    tm = MM_TM
    assert x.dtype == BF16 and wt.dtype == BF16
    w = wt
    return pl.pallas_call(
        _mm_bf16_kernel,
        grid=(n // tn, m // tm),
        in_specs=[pl.BlockSpec((tm, k), lambda j, i: (i, 0)),
                  pl.BlockSpec((1, tn, k), lambda j, i: (layer, j, 0))],
        out_specs=pl.BlockSpec((tm, tn), lambda j, i: (i, j)),
        out_shape=jax.ShapeDtypeStruct((m, n), out_dtype),
        compiler_params=_cparams(("arbitrary", "arbitrary")),
        name="mm_bf16",
    )(x, w)


def _row_rms(x, g):
    return x * lax.rsqrt(jnp.mean(x * x, axis=-1, keepdims=True) + EPS) * g


_MLA_PAD = 2 * V7X_LANES


def _mla_prep_kernel(cq_ref, ckv_ref, kr_ref, wq_ref, wkv_ref, gqi_ref, gkvi_ref,
                     gq_ref, gkn_ref, gkr_ref, c_ref, s_ref, q_ref, k_ref, vt_ref):
    c = c_ref[...]
    s = s_ref[...]
    hw = MLA_HEADS * MLA_NOPE

    def rot(t):
        return t * c + (pltpu.roll(t, 32, 1) + pltpu.roll(t, 96, 1)) * s

    cqn = _row_rms(cq_ref[...].astype(F32), gqi_ref[...]).astype(BF16)
    qf = jnp.dot(cqn, wq_ref[...], preferred_element_type=F32)
    gq = gq_ref[...]
    inv = 1.0 / MLA_QK
    for h in range(MLA_HEADS):
        nope = qf[:, h * 128:(h + 1) * 128]
        t = qf[:, hw + h * 128:hw + (h + 1) * 128]
        ss = jnp.sum(nope * nope, axis=-1, keepdims=True) + jnp.sum(t * t, axis=-1, keepdims=True)
        r = lax.rsqrt(ss * inv + EPS)
        q_ref[h, :, 0:128] = (nope * r * gq[:, 0:128]).astype(BF16)
        q_ref[h, :, 128:256] = rot(t * r * gq[:, 128:256]).astype(BF16)

    ckvn = _row_rms(ckv_ref[...].astype(F32), gkvi_ref[...]).astype(BF16)
    kvf = jnp.dot(ckvn, wkv_ref[...], preferred_element_type=F32)
    kr = kr_ref[...].astype(F32)
    ssr = jnp.sum(kr * kr, axis=-1, keepdims=True)
    yrot = rot(kr * gkr_ref[...])
    gkn = gkn_ref[...]
    for h in range(MLA_HEADS):
        nope = kvf[:, h * 128:(h + 1) * 128]
        r = lax.rsqrt((jnp.sum(nope * nope, axis=-1, keepdims=True) + ssr) * inv + EPS)
        k_ref[h, :, 0:128] = (nope * r * gkn).astype(BF16)
        k_ref[h, :, 128:256] = (yrot * r).astype(BF16)
        vt_ref[h] = kvf[:, hw + h * 128:hw + (h + 1) * 128].T.astype(BF16)


def _mla_prep(proj, p, tabs):
    tm = TOK_TM
    hw = MLA_HEADS * MLA_NOPE
    wq = p['mla_w_uq'].reshape(MLA_Q_RANK, MLA_HEADS, MLA_QK)
    wq_rope = jnp.pad(wq[:, :, MLA_NOPE:], ((0, 0), (0, 0), (0, 128 - MLA_ROPE)))
    wq = jnp.concatenate([wq[:, :, :MLA_NOPE].reshape(MLA_Q_RANK, hw),
                          wq_rope.reshape(MLA_Q_RANK, hw)], axis=1).astype(BF16)
    wkv = p['mla_w_ukv'].reshape(MLA_KV_RANK, MLA_HEADS, MLA_NOPE + MLA_V)
    wkv = jnp.concatenate([wkv[:, :, :MLA_NOPE].reshape(MLA_KV_RANK, hw),
                           wkv[:, :, MLA_NOPE:].reshape(MLA_KV_RANK, hw)], axis=1).astype(BF16)
    zpad = jnp.zeros((128 - MLA_ROPE,), F32)
    gq = (jnp.concatenate([p['mla_qn_g'], zpad]) * (MLA_QK ** -0.5 * _LOG2E)).reshape(1, _MLA_PAD)
    gkn = p['mla_kn_g'][:MLA_NOPE].reshape(1, 128)
    gkr = jnp.concatenate([p['mla_kn_g'][MLA_NOPE:], zpad]).reshape(1, 128)

    def const(shape):
        return pl.BlockSpec(shape, lambda i: (0,) * len(shape))

    return pl.pallas_call(
        _mla_prep_kernel,
        grid=(N_TOK // tm,),
        in_specs=[pl.BlockSpec((tm, MLA_Q_RANK), lambda i: (i, _pcol(0, MLA_Q_RANK))),
                  pl.BlockSpec((tm, MLA_KV_RANK), lambda i: (i, _pcol(1, MLA_KV_RANK))),
                  pl.BlockSpec((tm, 128), lambda i: (i, _pcol(2, 128))),
                  const((MLA_Q_RANK, 2 * hw)), const((MLA_KV_RANK, 2 * hw)),
                  const((1, MLA_Q_RANK)), const((1, MLA_KV_RANK)),
                  const((1, _MLA_PAD)), const((1, 128)), const((1, 128)),
                  pl.BlockSpec((tm, 128), lambda i: (_rope_block(i), 0)),
                  pl.BlockSpec((tm, 128), lambda i: (_rope_block(i), 0))],
        out_specs=[pl.BlockSpec((MLA_HEADS, tm, _MLA_PAD), lambda i: (0, i, 0)),
                   pl.BlockSpec((MLA_HEADS, tm, _MLA_PAD), lambda i: (0, i, 0)),
                   pl.BlockSpec((MLA_HEADS, MLA_V, tm), lambda i: (0, 0, i))],
        out_shape=[jax.ShapeDtypeStruct((MLA_HEADS, N_TOK, _MLA_PAD), BF16),
                   jax.ShapeDtypeStruct((MLA_HEADS, N_TOK, _MLA_PAD), BF16),
                   jax.ShapeDtypeStruct((MLA_HEADS, MLA_V, N_TOK), BF16)],
        compiler_params=_cparams(("arbitrary",)),
        name="mla_prep",
    )(proj, proj, proj, wq, wkv, p['mla_q_norm_g'].reshape(1, -1), p['mla_kv_norm_g'].reshape(1, -1),
      gq, gkn, gkr, tabs['mla_c'], tabs['mla_s'])


def _mla_kernel(q_ref, k1_ref, vt1_ref, k2_ref, vt2_ref, o_ref, sa_ref, sb_ref, sc_ref, acc_ref,
                *, tk, n_chunks):
    q = q_ref[0]
    tq = q.shape[0]
    hq = tq // 2
    qs = (q[:hq], q[hq:])

    def scores(dst_ref, kc):
        out = []
        for j in range(2):
            st = lax.dot_general(kc, qs[j], _NT, preferred_element_type=F32)
            dst_ref[j] = st
            out.append(jnp.max(st, axis=0, keepdims=True))
        return out

    def k_chunk(c):
        return k1_ref[0, pl.ds(pl.multiple_of(c * tk, tk), tk), :]

    def vt_chunk(c):
        return vt1_ref[0, :, pl.ds(pl.multiple_of(c * tk, tk), tk)]

    def accumulate(s_ref, smax, vtc, m, l):
        m_out, l_out = [], []
        for j in range(2):
            m_new = jnp.maximum(m[j], smax[j])
            a = jnp.exp2(m[j] - m_new)
            p = jnp.exp2(s_ref[j] - m_new)
            l_out.append(a * l[j] + jnp.sum(p, axis=0, keepdims=True))
            acc_ref[j] = a * acc_ref[j] + jnp.dot(vtc, p.astype(BF16), preferred_element_type=F32)
            m_out.append(m_new)
        return m_out, l_out

    m = [jnp.full((1, hq), NEG_INF, F32)] * 2
    l = [jnp.zeros((1, hq), F32)] * 2
    acc_ref[...] = jnp.zeros_like(acc_ref)
    mx_c = scores(sc_ref, k2_ref[0])
    mx_a = scores(sa_ref, k_chunk(0))
    m, l = accumulate(sc_ref, mx_c, vt2_ref[0], m, l)

    for i in range(n_chunks // 2 - 1):
        mx_b = scores(sb_ref, k_chunk(2 * i + 1))
        m, l = accumulate(sa_ref, mx_a, vt_chunk(2 * i), m, l)
        mx_a = scores(sa_ref, k_chunk(2 * i + 2))
        m, l = accumulate(sb_ref, mx_b, vt_chunk(2 * i + 1), m, l)
    mx_b = scores(sb_ref, k_chunk(n_chunks - 1))
    m, l = accumulate(sa_ref, mx_a, vt_chunk(n_chunks - 2), m, l)
    m, l = accumulate(sb_ref, mx_b, vt_chunk(n_chunks - 1), m, l)
    for j in range(2):
        o_ref[j * hq:(j + 1) * hq, :] = (acc_ref[j] / l[j]).T.astype(o_ref.dtype)


def _mla_attn(q, k, vt, *, tq, tk):
    nq = SEQ // tq
    return pl.pallas_call(
        functools.partial(_mla_kernel, tk=tk, n_chunks=SEQ // tk),
        grid=(BATCH, MLA_HEADS, nq),
        in_specs=[pl.BlockSpec((1, tq, _MLA_PAD), lambda b, h, i: (h, b * nq + i, 0)),
                  pl.BlockSpec((1, SEQ, _MLA_PAD), lambda b, h, i: (h, b, 0)),
                  pl.BlockSpec((1, MLA_V, SEQ), lambda b, h, i: (h, 0, b)),
                  pl.BlockSpec((1, CTX_LEN, _MLA_PAD), lambda b, h, i: (h, CTX_BLK + b, 0)),
                  pl.BlockSpec((1, MLA_V, CTX_LEN), lambda b, h, i: (h, 0, CTX_BLK + b))],
        out_specs=pl.BlockSpec((tq, MLA_V), lambda b, h, i: (b * nq + i, h)),
        out_shape=jax.ShapeDtypeStruct((N_LAT, MLA_HEADS * MLA_V), BF16),
        scratch_shapes=[pltpu.VMEM((2, tk, tq // 2), F32), pltpu.VMEM((2, tk, tq // 2), F32),
                        pltpu.VMEM((2, CTX_LEN, tq // 2), F32), pltpu.VMEM((2, MLA_V, tq // 2), F32)],
        compiler_params=_cparams(("arbitrary", "arbitrary", "arbitrary")),
        name="mla_attn",
    )(q, k, vt, k, vt)


_NA_KEYS = NA_ROWS * GRID_W
NA_RB = 8


def _na_pattern(r):
    half = NA_ROWS // 2
    return jnp.where(r < half, r, jnp.where(r <= ROWS - half, half, r - (ROWS - NA_ROWS)))


def _na_kernel(q_ref, k_ref, v_ref, kc_ref, vc_ref, bias_ref, gq_ref, gk_ref, o_ref, kn_ref, kcn_ref):
    rb = pl.program_id(2)

    @pl.when(rb == 0)
    def _():
        kn_ref[...] = _row_rms(k_ref[...].astype(F32), gk_ref[...]).astype(BF16)
        kcn_ref[...] = _row_rms(kc_ref[...].astype(F32), gk_ref[...]).astype(BF16)

    q_all = _row_rms(q_ref[...].astype(F32), gq_ref[...]).astype(BF16)
    vc = vc_ref[...]
    sc_all = lax.dot_general(q_all, kcn_ref[...], _NT, preferred_element_type=F32)
    ms, ls, os_ = [], [], []
    for j in range(NA_RB):
        r = rb * NA_RB + j
        start = pl.multiple_of(jnp.clip(r - NA_ROWS // 2, 0, ROWS - NA_ROWS) * GRID_W, GRID_W)
        q = q_all[j * GRID_W:(j + 1) * GRID_W]
        k = kn_ref[pl.ds(start, _NA_KEYS), :]
        v = v_ref[pl.ds(start, _NA_KEYS), :]
        s = lax.dot_general(q, k, _NT, preferred_element_type=F32) + bias_ref[0, _na_pattern(r)]
        sc = sc_all[j * GRID_W:(j + 1) * GRID_W]
        m = jnp.maximum(jnp.max(s, axis=-1, keepdims=True), jnp.max(sc, axis=-1, keepdims=True))
        p = jnp.exp(s - m)
        ms.append(m)
        ls.append(jnp.sum(p, axis=-1, keepdims=True))
        os_.append(jnp.dot(p.astype(BF16), v, preferred_element_type=F32))
    m_all = jnp.concatenate(ms, axis=0)
    pc = jnp.exp(sc_all - m_all)
    l_all = jnp.concatenate(ls, axis=0) + jnp.sum(pc, axis=-1, keepdims=True)
    o = jnp.concatenate(os_, axis=0) + jnp.dot(pc.astype(BF16), vc, preferred_element_type=F32)
    o_ref[...] = (o / l_all).astype(o_ref.dtype)


def _na_attn(proj, bias, gq, gk):
    qrows = NA_RB * GRID_W
    nrb = ROWS // NA_RB
    cq, ck, cv = _pcol(3, NA_HD), _pcol(4, NA_HD), _pcol(5, NA_HD)
    return pl.pallas_call(
        _na_kernel,
        grid=(BATCH, NA_HEADS, nrb),
        in_specs=[pl.BlockSpec((qrows, NA_HD), lambda b, h, r: (b * nrb + r, cq + h)),
                  pl.BlockSpec((SEQ, NA_HD), lambda b, h, r: (b, ck + h)),
                  pl.BlockSpec((SEQ, NA_HD), lambda b, h, r: (b, cv + h)),
                  pl.BlockSpec((CTX_LEN, NA_HD), lambda b, h, r: (CTX_BLK + b, ck + h)),
                  pl.BlockSpec((CTX_LEN, NA_HD), lambda b, h, r: (CTX_BLK + b, cv + h)),
                  pl.BlockSpec((1, NA_ROWS, GRID_W, _NA_KEYS), lambda b, h, r: (h, 0, 0, 0)),
                  pl.BlockSpec((1, NA_HD), lambda b, h, r: (0, 0)),
                  pl.BlockSpec((1, NA_HD), lambda b, h, r: (0, 0))],
        out_specs=pl.BlockSpec((qrows, NA_HD), lambda b, h, r: (b * nrb + r, h)),
        out_shape=jax.ShapeDtypeStruct((N_LAT, NA_HEADS * NA_HD), BF16),
        scratch_shapes=[pltpu.VMEM((SEQ, NA_HD), BF16), pltpu.VMEM((CTX_LEN, NA_HD), BF16)],
        compiler_params=_cparams(("arbitrary", "arbitrary", "arbitrary")),
        name="na_attn",
    )(proj, proj, proj, proj, proj, bias, gq, gk)


def _na_bias_table(rpb):
    half = NA_ROWS // 2
    r_rep = np.array(list(range(half)) + [half] + list(range(ROWS - half + 1, ROWS)))
    start = np.clip(r_rep - half, 0, ROWS - NA_ROWS)
    dr = start[:, None] + np.arange(NA_ROWS)[None, :] - r_rep[:, None] + NA_ROWS - 1
    qc = np.arange(GRID_W)
    kcol = np.arange(GRID_W)
    col_start = np.clip(qc - NA_COLS // 2, 0, GRID_W - NA_COLS)
    in_win = (kcol[None, :] >= col_start[:, None]) & (kcol[None, :] < col_start[:, None] + NA_COLS)
    dc = np.clip(kcol[None, :] - qc[:, None], 1 - NA_COLS, NA_COLS - 1) + NA_COLS - 1
    rsel = (dr[:, :, None] == np.arange(2 * NA_ROWS - 1)).astype(np.float32)
    csel = (dc[:, :, None] == np.arange(2 * NA_COLS - 1)).astype(np.float32)
    b = jnp.einsum('pja,hab,qkb->hpqjk', rsel, rpb.astype(F32), csel,
                   precision=lax.Precision.HIGHEST)
    b = jnp.where(in_win[None, None, :, None, :], b.astype(F32), NEG_INF)
    return b.reshape(NA_HEADS, NA_ROWS, GRID_W, _NA_KEYS)


_GQA_G = GQA_HEADS // GQA_KV_HEADS
_GQA_BAND = 3 * GQA_WINDOW
_GQA_QW = GQA_HEADS * V7X_LANES


def _gqa_prep_kernel(q_ref, k_ref, gq_ref, gk_ref, c_ref, s1_ref, s2_ref, qd_ref, kn_ref):
    c = c_ref[...]
    s1 = s1_ref[...]
    s2 = s2_ref[...]
    lo = lax.broadcasted_iota(jnp.int32, (1, 128), 1) < GQA_HD

    def head_rms(x, g):
        x2 = x * x
        s_lo = jnp.sum(jnp.where(lo, x2, 0.0), axis=-1, keepdims=True)
        s_hi = jnp.sum(jnp.where(lo, 0.0, x2), axis=-1, keepdims=True)
        inv = 1.0 / GQA_HD
        r = jnp.where(lo, lax.rsqrt(s_lo * inv + EPS), lax.rsqrt(s_hi * inv + EPS))
        return x * r * g

    def rot(x):
        return x * c + pltpu.roll(x, 96, 1) * s1 + pltpu.roll(x, 32, 1) * s2

    gq = gq_ref[...]
    for j in range(GQA_HEADS // 2):
        y = rot(head_rms(q_ref[:, j * 128:(j + 1) * 128].astype(F32), gq))
        sw = pltpu.roll(y, 64, 1)
        hk = (2 * j) // _GQA_G
        if hk == 0:
            even, odd = jnp.where(lo, y, 0.0), jnp.where(lo, sw, 0.0)
        else:
            even, odd = jnp.where(lo, 0.0, sw), jnp.where(lo, 0.0, y)
        qd_ref[:, (2 * j) * 128:(2 * j + 1) * 128] = even.astype(BF16)
        qd_ref[:, (2 * j + 1) * 128:(2 * j + 2) * 128] = odd.astype(BF16)
    kn_ref[...] = rot(head_rms(k_ref[...].astype(F32), gk_ref[...])).astype(BF16)


def _gqa_prep(proj, p, tabs):
    tm = TOK_TM
    gq = (jnp.tile(p['gqa_qn_g'], 2) * (GQA_HD ** -0.5)).reshape(1, 128)
    gk = jnp.tile(p['gqa_kn_g'], 2).reshape(1, 128)
    row = pl.BlockSpec((tm, 128), lambda i: (i, 0))
    tab = pl.BlockSpec((tm, 128), lambda i: (_rope_block(i), 0))
    vec = pl.BlockSpec((1, 128), lambda i: (0, 0))
    return pl.pallas_call(
        _gqa_prep_kernel,
        grid=(N_TOK // tm,),
        in_specs=[pl.BlockSpec((tm, GQA_HEADS * GQA_HD), lambda i: (i, _pcol(6, GQA_HEADS * GQA_HD))),
                  pl.BlockSpec((tm, 128), lambda i: (i, _pcol(7, 128))),
                  vec, vec, tab, tab, tab],
        out_specs=[pl.BlockSpec((tm, _GQA_QW), lambda i: (i, 0)), row],
        out_shape=[jax.ShapeDtypeStruct((N_TOK, _GQA_QW), BF16),
                   jax.ShapeDtypeStruct((N_TOK, 128), BF16)],
        compiler_params=_cparams(("arbitrary",)),
        name="gqa_prep",
    )(proj, proj, gq, gk, tabs['gqa_c'], tabs['gqa_s1'], tabs['gqa_s2'])


def _gqa_kernel(sink_ref, q_ref, k_ref, v_ref, kc_ref, vc_ref, o_ref):
    hk = pl.program_id(1)
    n = pl.program_id(2)
    w = GQA_WINDOW
    start = pl.multiple_of(jnp.clip((n - 1) * w, 0, SEQ - _GQA_BAND), w)
    q = jnp.concatenate([q_ref[:, g * 128:(g + 1) * 128] for g in range(_GQA_G)], axis=0)
    k = k_ref[pl.ds(start, _GQA_BAND), :]
    v = v_ref[pl.ds(start, _GQA_BAND), :]
    s = lax.dot_general(q, k, _NT, preferred_element_type=F32)
    rows = lax.broadcasted_iota(jnp.int32, s.shape, 0)
    cols = lax.broadcasted_iota(jnp.int32, s.shape, 1)
    qpos = n * w + (rows & (w - 1))
    kpos = start + cols
    s = jnp.where(jnp.abs(kpos - qpos) <= GQA_WINDOW, s, NEG_INF)
    sc = lax.dot_general(q, kc_ref[...], _NT, preferred_element_type=F32)
    grow = lax.broadcasted_iota(jnp.int32, (_GQA_G * w, 1), 0) // w
    snk = jnp.full((_GQA_G * w, 1), sink_ref[hk * _GQA_G], F32)
    for g in range(1, _GQA_G):
        snk = jnp.where(grow == g, sink_ref[hk * _GQA_G + g], snk)
    m = jnp.maximum(jnp.maximum(jnp.max(s, axis=-1, keepdims=True),
                                jnp.max(sc, axis=-1, keepdims=True)), snk)
    p = jnp.exp(s - m)
    pc = jnp.exp(sc - m)
    l = jnp.sum(p, axis=-1, keepdims=True) + jnp.sum(pc, axis=-1, keepdims=True) + jnp.exp(snk - m)
    o = (jnp.dot(p.astype(BF16), v, preferred_element_type=F32)
         + jnp.dot(pc.astype(BF16), vc_ref[...], preferred_element_type=F32)) / l
    half = lax.broadcasted_iota(jnp.int32, (1, 128), 1) // GQA_HD
    o = jnp.where(half == hk, o, 0.0).astype(o_ref.dtype)
    for g in range(_GQA_G):
        o_ref[:, g * 128:(g + 1) * 128] = o[g * w:(g + 1) * w]


def _gqa_attn(qd, kn, proj, sink):
    nb = SEQ // GQA_WINDOW
    qw = _GQA_G * 128
    cv = _pcol(8, 128)
    gs = pltpu.PrefetchScalarGridSpec(
        num_scalar_prefetch=1,
        grid=(BATCH, GQA_KV_HEADS, nb),
        in_specs=[pl.BlockSpec((GQA_WINDOW, qw), lambda b, h, n, *_: (b * nb + n, h)),
                  pl.BlockSpec((SEQ, 128), lambda b, h, n, *_: (b, 0)),
                  pl.BlockSpec((SEQ, 128), lambda b, h, n, *_: (b, cv)),
                  pl.BlockSpec((CTX_LEN, 128), lambda b, h, n, *_: (CTX_BLK + b, 0)),
                  pl.BlockSpec((CTX_LEN, 128), lambda b, h, n, *_: (CTX_BLK + b, cv))],
        out_specs=pl.BlockSpec((GQA_WINDOW, qw), lambda b, h, n, *_: (b * nb + n, h)),
    )
    return pl.pallas_call(
        _gqa_kernel,
        grid_spec=gs,
        out_shape=jax.ShapeDtypeStruct((N_LAT, _GQA_QW), BF16),
        compiler_params=_cparams(("arbitrary", "arbitrary", "arbitrary")),
        name="gqa_attn",
    )(sink.astype(F32), qd, kn, proj, kn, proj)


def _ctx_kernel(sink_ref, mq_ref, mk_ref, mvt_ref, nq_ref, nk_ref, nv_ref, gnq_ref, gnk_ref,
                gq_ref, gk_ref, gv_ref, oa_ref, ob_ref, oc_ref):
    for h in range(MLA_HEADS):
        st = lax.dot_general(mk_ref[h], mq_ref[h], _NT, preferred_element_type=F32)
        p = jnp.exp2(st - jnp.max(st, axis=0, keepdims=True))
        l = jnp.sum(p, axis=0, keepdims=True)
        ot = jnp.dot(mvt_ref[h], p.astype(BF16), preferred_element_type=F32) / l
        oa_ref[:, h * MLA_V:(h + 1) * MLA_V] = ot.T.astype(oa_ref.dtype)
    for h in range(NA_HEADS):
        sl = slice(h * NA_HD, (h + 1) * NA_HD)
        q = _row_rms(nq_ref[:, sl].astype(F32), gnq_ref[...]).astype(BF16)
        k = _row_rms(nk_ref[:, sl].astype(F32), gnk_ref[...]).astype(BF16)
        s = lax.dot_general(q, k, _NT, preferred_element_type=F32)
        p = jnp.exp(s - jnp.max(s, axis=-1, keepdims=True))
        l = jnp.sum(p, axis=-1, keepdims=True)
        o = jnp.dot(p.astype(BF16), nv_ref[:, sl], preferred_element_type=F32) / l
        ob_ref[:, sl] = o.astype(ob_ref.dtype)
    half = lax.broadcasted_iota(jnp.int32, (1, 128), 1) // GQA_HD
    k = gk_ref[...]
    v = gv_ref[...]
    for h in range(GQA_HEADS):
        sl = slice(h * 128, (h + 1) * 128)
        s = lax.dot_general(gq_ref[:, sl], k, _NT, preferred_element_type=F32)
        snk = sink_ref[h]
        m = jnp.maximum(jnp.max(s, axis=-1, keepdims=True), snk)
        p = jnp.exp(s - m)
        l = jnp.sum(p, axis=-1, keepdims=True) + jnp.exp(snk - m)
        o = jnp.dot(p.astype(BF16), v, preferred_element_type=F32) / l
        oc_ref[:, sl] = jnp.where(half == h // _GQA_G, o, 0.0).astype(oc_ref.dtype)


def _ctx_attn(sink, mq, mk, mvt, proj, gnq, gnk, qd, kn):
    c = CTX_LEN
    nwid = NA_HEADS * NA_HD

    def row(width, col=0):
        return pl.BlockSpec((c, width), lambda b, *_: (CTX_BLK + b, col))

    def out(width):
        return pl.BlockSpec((c, width), lambda b, *_: (b, 0))

    vec = pl.BlockSpec((1, NA_HD), lambda b, *_: (0, 0))
    gs = pltpu.PrefetchScalarGridSpec(
        num_scalar_prefetch=1,
        grid=(BATCH,),
        in_specs=[pl.BlockSpec((MLA_HEADS, c, _MLA_PAD), lambda b, *_: (0, CTX_BLK + b, 0)),
                  pl.BlockSpec((MLA_HEADS, c, _MLA_PAD), lambda b, *_: (0, CTX_BLK + b, 0)),
                  pl.BlockSpec((MLA_HEADS, MLA_V, c), lambda b, *_: (0, 0, CTX_BLK + b)),
                  row(nwid, _pcol(3, nwid)), row(nwid, _pcol(4, nwid)), row(nwid, _pcol(5, nwid)),
                  vec, vec,
                  row(_GQA_QW), row(128), row(128, _pcol(8, 128))],
        out_specs=[out(MLA_HEADS * MLA_V), out(nwid), out(_GQA_QW)],
    )
    return pl.pallas_call(
        _ctx_kernel,
        grid_spec=gs,
        out_shape=[jax.ShapeDtypeStruct((N_CTX, MLA_HEADS * MLA_V), BF16),
                   jax.ShapeDtypeStruct((N_CTX, nwid), BF16),
                   jax.ShapeDtypeStruct((N_CTX, _GQA_QW), BF16)],
        compiler_params=_cparams(("arbitrary",)),
        name="ctx_attn",
    )(sink.astype(F32), mq, mk, mvt, proj, proj, proj, gnq, gnk, qd, kn, proj)


def _merge_kernel(*refs, has_ctx, n_lat_tiles):
    ga_ref, gb_ref, gc_ref, oa_ref, ob_ref, oc_ref = refs[:6]
    refs = refs[6:]
    if has_ctx:
        ca_ref, cb_ref, cc_ref = refs[:3]
        refs = refs[3:]
    wa_ref, wb_ref, wc_ref, y_ref = refs
    is_ctx = pl.program_id(0) >= n_lat_tiles

    def branch(g_ref, o_ref, c_ref, w_ref):
        o = o_ref[...]
        if has_ctx:
            o = jnp.where(is_ctx, c_ref[...], o)
        return jax.nn.sigmoid(g_ref[...].astype(F32)) * jnp.dot(o, w_ref[...], preferred_element_type=F32)

    y = (branch(ga_ref, oa_ref, ca_ref if has_ctx else None, wa_ref)
         + branch(gb_ref, ob_ref, cb_ref if has_ctx else None, wb_ref)
         + branch(gc_ref, oc_ref, cc_ref if has_ctx else None, wc_ref))
    y_ref[...] = y.astype(y_ref.dtype)


def _merge(proj, lat, ctx, wa, wb, wc):
    tm = TOK_TM
    has_ctx = ctx is not None
    m = N_TOK if has_ctx else N_LAT
    nl = N_LAT // tm
    ks = [o.shape[1] for o in lat]
    in_specs = [pl.BlockSpec((tm, D_MODEL), lambda i: (i, 0)),
                pl.BlockSpec((tm, D_MODEL), lambda i: (i, 1)),
                pl.BlockSpec((tm, D_MODEL), lambda i: (i, 2))]
    in_specs += [pl.BlockSpec((tm, k), lambda i: (jnp.minimum(i, nl - 1), 0)) for k in ks]
    args = [proj, proj, proj] + list(lat)
    if has_ctx:
        in_specs += [pl.BlockSpec((tm, k), lambda i: (jnp.maximum(i - nl, 0), 0)) for k in ks]
        args += list(ctx)
    in_specs += [pl.BlockSpec((k, D_MODEL), lambda i: (0, 0)) for k in ks]
    args += [wa, wb, wc]
    return pl.pallas_call(
        functools.partial(_merge_kernel, has_ctx=has_ctx, n_lat_tiles=nl),
        grid=(m // tm,),
        in_specs=in_specs,
        out_specs=pl.BlockSpec((tm, D_MODEL), lambda i: (i, 0)),
        out_shape=jax.ShapeDtypeStruct((m, D_MODEL), BF16),
        compiler_params=_cparams(("arbitrary",)),
        name="merge",
    )(*args)


def _moe_ffn_kernel(te_ref, tok_ref, nu_ref, en_ref, es_ref, h_hbm, wg_hbm, wu_hbm, wd_hbm, y_ref,
                    xbuf, sem, wgs, wus, wds, wsem, wgb, wub, wdb, *, layer):
    i = pl.program_id(0)
    tm = MOE_TM
    slot = i % 2
    n_used = nu_ref[0]

    def row_copy(tok, s, r):
        return pltpu.make_async_copy(h_hbm.at[pl.ds(tok, 1)], xbuf.at[s, pl.ds(r, 1)], sem.at[s])

    def start_gather(tile, s):
        base = tile * tm

        def body(r, c):
            row_copy(tok_ref[base + r], s, r).start()
            return c

        lax.fori_loop(0, tm, body, 0, unroll=8)

    def weight_copies(e, s):
        return (pltpu.make_async_copy(wg_hbm.at[layer, e], wgs.at[s], wsem.at[s]),
                pltpu.make_async_copy(wu_hbm.at[layer, e], wus.at[s], wsem.at[s]),
                pltpu.make_async_copy(wd_hbm.at[layer, e], wds.at[s], wsem.at[s]))

    @pl.when(i == 0)
    def _():
        for cp in weight_copies(te_ref[0], es_ref[0]):
            cp.start(priority=1)
        start_gather(0, 0)

    @pl.when(i + 1 < n_used)
    def _():
        start_gather(i + 1, 1 - slot)

    @pl.when(i < n_used)
    def _():
        @pl.when((i == 0) | (te_ref[i] != te_ref[jnp.maximum(i - 1, 0)]))
        def _():
            s = es_ref[i]

            @pl.when(en_ref[i] >= 0)
            def _():
                for cp in weight_copies(en_ref[i], 1 - s):
                    cp.start(priority=1)

            for cp in weight_copies(te_ref[i], s):
                cp.wait()
            wgb[...] = wgs[s].astype(BF16)
            wub[...] = wus[s].astype(BF16)
            wdb[...] = wds[s].astype(BF16)

        pltpu.make_async_copy(h_hbm.at[pl.ds(0, tm)], xbuf.at[slot], sem.at[slot]).wait()
        x = xbuf[slot].astype(BF16)
        hg = jnp.dot(x, wgb[...], preferred_element_type=F32)
        hu = jnp.dot(x, wub[...], preferred_element_type=F32)
        act = (hg * jax.nn.sigmoid(hg)) * hu
        y_ref[...] = jnp.dot(act.astype(BF16), wdb[...], preferred_element_type=F32)

    @pl.when(i >= n_used)
    def _():
        y_ref[...] = jnp.zeros_like(y_ref)


def _moe_ffn(h, plan, wg, wu, wd, layer):
    tile_expert, slot_token, n_used, next_expert, expert_slot = plan
    p = slot_token.shape[0]
    tm = MOE_TM
    nt = p // tm
    anyspec = pl.BlockSpec(memory_space=pl.ANY)
    gs = pltpu.PrefetchScalarGridSpec(
        num_scalar_prefetch=5,
        grid=(nt,),
        in_specs=[anyspec, anyspec, anyspec, anyspec],
        out_specs=pl.BlockSpec((tm, D_MODEL), lambda i, *_: (i, 0)),
        scratch_shapes=[pltpu.VMEM((2, tm, D_MODEL), F32),
                        pltpu.SemaphoreType.DMA((2,)),
                        pltpu.VMEM((2, D_MODEL, MOE_HIDDEN), F32),
                        pltpu.VMEM((2, D_MODEL, MOE_HIDDEN), F32),
                        pltpu.VMEM((2, MOE_HIDDEN, D_MODEL), F32),
                        pltpu.SemaphoreType.DMA((2,)),
                        pltpu.VMEM((D_MODEL, MOE_HIDDEN), BF16),
                        pltpu.VMEM((D_MODEL, MOE_HIDDEN), BF16),
                        pltpu.VMEM((MOE_HIDDEN, D_MODEL), BF16)],
    )
    return pl.pallas_call(
        functools.partial(_moe_ffn_kernel, layer=layer),
        grid_spec=gs,
        out_shape=jax.ShapeDtypeStruct((p, D_MODEL), F32),
        compiler_params=_cparams(("arbitrary",)),
        name="moe_ffn",
    )(tile_expert, slot_token, n_used, next_expert, expert_slot, h, wg, wu, wd)


def _moe_combine_kernel(pos_ref, y_hbm, x_ref, w_ref, g_ref, o_ref, ybuf, sem):
    i = pl.program_id(0)
    nt = pl.num_programs(0)
    tm = TOK_TM
    slot = i % 2

    def row_copy(src, s, r):
        return pltpu.make_async_copy(y_hbm.at[pl.ds(src, 1)], ybuf.at[s, pl.ds(r, 1)], sem.at[s])

    def start_gather(tile, s):
        base = tile * tm

        def body(r, c):
            row_copy(pos_ref[2 * (base + r)], s, r).start(priority=0)
            row_copy(pos_ref[2 * (base + r) + 1], s, tm + r).start(priority=1)
            return c

        lax.fori_loop(0, tm, body, 0, unroll=4)

    @pl.when(i == 0)
    def _():
        start_gather(0, 0)

    @pl.when(i + 1 < nt)
    def _():
        start_gather(i + 1, 1 - slot)

    pltpu.make_async_copy(y_hbm.at[pl.ds(0, 2 * tm)], ybuf.at[slot], sem.at[slot]).wait()
    w = w_ref[...]
    y = ybuf[slot, pl.ds(0, tm), :] * w[:, 0:1] + ybuf[slot, pl.ds(tm, tm), :] * w[:, 1:2]
    o_ref[...] = x_ref[...] + g_ref[0] * y


def _moe_combine(y, pos, w_sel, x, mod, k_gate):
    m = x.shape[0]
    tm = TOK_TM
    gs = pltpu.PrefetchScalarGridSpec(
        num_scalar_prefetch=1,
        grid=(m // tm,),
        in_specs=[pl.BlockSpec(memory_space=pl.ANY),
                  pl.BlockSpec((tm, D_MODEL), lambda i, pos: (i, 0)),
                  pl.BlockSpec((tm, MOE_TOPK), lambda i, pos: (i, 0)),
                  pl.BlockSpec((1, 1, D_MODEL), lambda i, pos: (_seg(i, tm) * 6 + k_gate, 0, 0))],
        out_specs=pl.BlockSpec((tm, D_MODEL), lambda i, pos: (i, 0)),
        scratch_shapes=[pltpu.VMEM((2, 2 * tm, D_MODEL), F32),
                        pltpu.SemaphoreType.DMA((2,))],
    )
    return pl.pallas_call(
        _moe_combine_kernel,
        grid_spec=gs,
        out_shape=jax.ShapeDtypeStruct((m, D_MODEL), F32),
        compiler_params=_cparams(("arbitrary",)),
        name="moe_combine",
    )(pos, y, x, w_sel, mod)


def _route(logits, m):
    tm = MOE_TM
    gp = jax.nn.softmax(logits[:, :MOE_GROUPS], axis=-1)
    g_idx = jnp.argmax(gp, axis=-1).astype(jnp.int32)[:, None]
    g_w = jnp.max(gp, axis=-1, keepdims=True)
    el = logits[:, MOE_GROUPS:MOE_GROUPS + MOE_EXPERTS].reshape(m, MOE_GROUPS, MOE_PER_GROUP)
    g_onehot = (g_idx == jnp.arange(MOE_GROUPS, dtype=jnp.int32)[None, :]).astype(F32)
    el_g = jnp.sum(el * g_onehot[:, :, None], axis=1)
    i0 = jnp.argmax(el_g, axis=-1).astype(jnp.int32)[:, None]
    l0 = jnp.max(el_g, axis=-1, keepdims=True)
    rest = jnp.where(jnp.arange(MOE_PER_GROUP, dtype=jnp.int32)[None, :] == i0, -jnp.inf, el_g)
    i1 = jnp.argmax(rest, axis=-1).astype(jnp.int32)[:, None]
    l1 = jnp.max(rest, axis=-1, keepdims=True)
    top_l = jnp.concatenate([l0, l1], axis=-1)
    top_i = jnp.concatenate([i0, i1], axis=-1)
    w_sel = jax.nn.softmax(top_l, axis=-1) * g_w
    eid = (g_idx * MOE_PER_GROUP + top_i).astype(jnp.int32)

    a = m * MOE_TOPK
    e_flat = eid.reshape(a)
    onehot = (e_flat[:, None] == jnp.arange(MOE_EXPERTS, dtype=jnp.int32)[None, :]).astype(jnp.int32)
    csum = jnp.cumsum(onehot, axis=0)
    rank = jnp.sum(csum * onehot, axis=1) - 1
    counts = csum[-1]
    padded = ((counts + tm - 1) // tm) * tm
    ends = jnp.cumsum(padded)
    starts = ends - padded
    pos = (jnp.sum(onehot * starts[None, :], axis=1) + rank).astype(jnp.int32)
    p = a + MOE_EXPERTS * tm
    slot_token = jnp.zeros((p,), jnp.int32).at[pos].set(jnp.arange(a, dtype=jnp.int32) // MOE_TOPK)
    n_used = (ends[-1] // tm).astype(jnp.int32).reshape(1)
    tile_start = jnp.arange(p // tm, dtype=jnp.int32) * tm
    last_e = jnp.max(jnp.where(counts > 0, jnp.arange(MOE_EXPERTS, dtype=jnp.int32), 0))
    tile_expert = jnp.minimum(
        jnp.sum((ends[None, :] <= tile_start[:, None]).astype(jnp.int32), axis=1), last_e)
    eidx = jnp.arange(MOE_EXPERTS, dtype=jnp.int32)
    used = counts > 0
    later = used[None, :] & (eidx[None, :] > eidx[:, None])
    next_used = jnp.min(jnp.where(later, eidx[None, :], MOE_EXPERTS), axis=1)
    next_used = jnp.where(next_used == MOE_EXPERTS, -1, next_used).astype(jnp.int32)
    ordinal = (jnp.cumsum(used.astype(jnp.int32)) - 1) % 2
    t_onehot = (tile_expert[:, None] == eidx[None, :]).astype(jnp.int32)
    next_expert = jnp.sum(t_onehot * next_used[None, :], axis=1).astype(jnp.int32)
    expert_slot = jnp.sum(t_onehot * ordinal[None, :], axis=1).astype(jnp.int32)
    return (tile_expert, slot_token, n_used, next_expert, expert_slot), w_sel, pos


def _rope_angles(rot_dim):
    t = np.arange(SEQ)
    row = (t // GRID_W).astype(np.float64)
    col = (t % GRID_W).astype(np.float64)
    n_freq = rot_dim // 4
    inv = ROPE_THETA ** (-np.arange(n_freq, dtype=np.float64) / n_freq)
    ang = np.concatenate([row[:, None] * inv, col[:, None] * inv], axis=-1)
    return np.cos(ang), np.sin(ang)


def _rope_tables():
    def rows(lat, ident):
        ctx = np.zeros((TOK_TM, 128)) + ident
        return jnp.asarray(np.concatenate([lat, ctx], axis=0).astype(np.float32))

    z32 = np.zeros((SEQ, 32))
    z64 = np.zeros((SEQ, 64))
    lane = np.arange(128)
    cm, sm = _rope_angles(MLA_ROPE)
    mla_c = rows(np.concatenate([cm, cm, z64], axis=1), (lane < 64).astype(np.float64))
    mla_s = rows(np.concatenate([-sm, sm, z64], axis=1), 0.0)
    cg, sg = _rope_angles(GQA_HD)
    gqa_c = rows(np.concatenate([cg, cg, cg, cg], axis=1), 1.0)
    gqa_s1 = rows(np.concatenate([-sg, z32, -sg, z32], axis=1), 0.0)
    gqa_s2 = rows(np.concatenate([z32, sg, z32, sg], axis=1), 0.0)
    return {'mla_c': mla_c, 'mla_s': mla_s, 'gqa_c': gqa_c, 'gqa_s1': gqa_s1, 'gqa_s2': gqa_s2}


def _rope_block(i):
    per_seq = SEQ // TOK_TM
    return jnp.where(i < BATCH * per_seq, i % per_seq, per_seq)


def _pack_w_in_kernel(w_ref, o_ref):
    bounds = (0,) + IN_SPLITS + (sum(IN_SIZES),)
    off = 0
    for i in _PACK_ORDER:
        size = bounds[i + 1] - bounds[i]
        o_ref[0, off:off + size, :] = w_ref[0, bounds[i]:bounds[i + 1], :].astype(BF16)
        off += size
    o_ref[0, off:, :] = jnp.zeros((_PACK_COLS - off, o_ref.shape[2]), BF16)


def _pack_w_in(w_in):
    tk = 256
    n_in = sum(IN_SIZES)
    wt = jnp.swapaxes(w_in, 1, 2)
    return pl.pallas_call(
        _pack_w_in_kernel,
        grid=(DEPTH, D_MODEL // tk),
        in_specs=[pl.BlockSpec((1, n_in, tk), lambda l, i: (l, 0, i))],
        out_specs=pl.BlockSpec((1, _PACK_COLS, tk), lambda l, i: (l, 0, i)),
        out_shape=jax.ShapeDtypeStruct((DEPTH, _PACK_COLS, D_MODEL), BF16),
        compiler_params=_cparams(("arbitrary", "arbitrary")),
        name="pack_w_in",
    )(wt)


def _pad_w_o_gqa(w):
    w4 = w.reshape(GQA_KV_HEADS, _GQA_G, GQA_HD, D_MODEL)
    z = jnp.zeros_like(w4[0:1])
    halves = [jnp.concatenate([w4[hk:hk + 1] if hk == half else z for hk in range(GQA_KV_HEADS)], axis=0)
              for half in range(GQA_KV_HEADS)]
    return jnp.stack(halves, axis=2).reshape(_GQA_QW, D_MODEL).astype(BF16)


def _token_mixer(h, p, tabs, ctx_out):
    proj = _mm(h, p['w_in_packed'], p['layer'], BF16, _PACK_COLS // 4)
    mq, mk, mvt = _mla_prep(proj, p, tabs)
    oa = _mla_attn(mq, mk, mvt, tq=512, tk=1024)
    gnq = (p['na_qn_g'] * (NA_HD ** -0.5)).reshape(1, NA_HD)
    gnk = p['na_kn_g'].reshape(1, NA_HD)
    ob = _na_attn(proj, _na_bias_table(p['na_rpb']), gnq, gnk)
    qd, kn = _gqa_prep(proj, p, tabs)
    oc = _gqa_attn(qd, kn, proj, p['gqa_sink'])
    ctx = _ctx_attn(p['gqa_sink'], mq, mk, mvt, proj, gnq, gnk, qd, kn) if ctx_out else None
    return _merge(proj, (oa, ob, oc), ctx, p['w_o_mla'].astype(BF16), p['w_o_na'].astype(BF16),
                  _pad_w_o_gqa(p['w_o_gqa']))


def _post_mixer(y, xt, mod, norm_g, w_out_l, p):
    m = y.shape[0]
    wr = jnp.concatenate([p['moe_w_group'], p['moe_w_expert'],
                          jnp.zeros((D_MODEL, ROUTE_COLS - MOE_GROUPS - MOE_EXPERTS), F32)], axis=1)
    br = jnp.concatenate([p['moe_b_group'], p['moe_b_expert'],
                          jnp.zeros((ROUTE_COLS - MOE_GROUPS - MOE_EXPERTS,), F32)]).reshape(1, ROUTE_COLS)
    x_mid, h, logits = _out_route(y, w_out_l.astype(BF16), xt, mod, norm_g, wr, br)
    plan, w_sel, pos = _route(logits, m)
    yy = _moe_ffn(h, plan, p['moe_w_gate'], p['moe_w_up'], p['moe_w_down'], p['layer'])
    return _moe_combine(yy, pos, w_sel, x_mid, mod, 5)


def kernel(x, c, ctx, c_ctx, ada_w, ada_b, norm_mix_g, norm_ffn_g, w_in,
           mla_q_norm_g, mla_w_uq, mla_kv_norm_g, mla_w_ukv, mla_qn_g, mla_kn_g,
           na_qn_g, na_kn_g, na_rpb, gqa_qn_g, gqa_kn_g, gqa_sink,
           w_o_mla, w_o_na, w_o_gqa, w_out,
           moe_w_group, moe_b_group, moe_w_expert, moe_b_expert,
           moe_w_gate, moe_w_up, moe_w_down):
    xt = jnp.concatenate([x.reshape(N_LAT, D_MODEL), ctx.reshape(N_CTX, D_MODEL)], axis=0)
    c_rows = jnp.concatenate([c, c_ctx[None, :], jnp.zeros((8 - BATCH - 1, D_MODEL), F32)], axis=0)
    mod_all = _ada(c_rows, ada_w, ada_b)
    tabs = _rope_tables()
    w_in_packed = _pack_w_in(w_in)
    for l in range(DEPTH):
        ctx_out = l < DEPTH - 1
        p = {
            'w_in_packed': w_in_packed, 'mla_q_norm_g': mla_q_norm_g[l], 'mla_w_uq': mla_w_uq[l],
            'mla_kv_norm_g': mla_kv_norm_g[l], 'mla_w_ukv': mla_w_ukv[l],
            'mla_qn_g': mla_qn_g[l], 'mla_kn_g': mla_kn_g[l],
            'na_qn_g': na_qn_g[l], 'na_kn_g': na_kn_g[l], 'na_rpb': na_rpb[l],
            'gqa_qn_g': gqa_qn_g[l], 'gqa_kn_g': gqa_kn_g[l], 'gqa_sink': gqa_sink[l],
            'w_o_mla': w_o_mla[l], 'w_o_na': w_o_na[l], 'w_o_gqa': w_o_gqa[l],
            'moe_w_group': moe_w_group[l], 'moe_b_group': moe_b_group[l],
            'moe_w_expert': moe_w_expert[l], 'moe_b_expert': moe_b_expert[l],
            'moe_w_gate': moe_w_gate, 'moe_w_up': moe_w_up, 'moe_w_down': moe_w_down, 'layer': l,
        }
        mod = mod_all[l].reshape(8 * 6, 1, D_MODEL)
        h = _norm_mod(xt, norm_mix_g[l], mod, 0, 1)
        y = _token_mixer(h, p, tabs, ctx_out)
        xt = _post_mixer(y, xt, mod, norm_ffn_g[l], w_out[l], p)
    return xt[:N_LAT].reshape(BATCH, SEQ, D_MODEL)
```

```python
import functools

import numpy as np
import jax
import jax.numpy as jnp
from jax import lax
from jax.experimental import pallas as pl
from jax.experimental.pallas import tpu as pltpu

D_MODEL = 2048
BATCH = 2
SEQ = 4096
DEPTH = 2
GRID_W = 64
CTX_LEN = 256
EPS = 1e-6
ROPE_THETA = 10000.0
NEG_INF = -1e30

MLA_HEADS = 8
MLA_Q_RANK = 512
MLA_KV_RANK = 512
MLA_NOPE = 128
MLA_ROPE = 64
MLA_QK = MLA_NOPE + MLA_ROPE
MLA_V = 128
NA_HEADS = 4
NA_HD = 128
NA_ROWS = 8
NA_COLS = 16
GQA_HEADS = 8
GQA_KV_HEADS = 2
GQA_HD = 64
GQA_WINDOW = 128
MOE_GROUPS = 4
MOE_PER_GROUP = 8
MOE_EXPERTS = MOE_GROUPS * MOE_PER_GROUP
MOE_TOPK = 2
MOE_HIDDEN = 512

IN_SIZES = (MLA_Q_RANK, MLA_KV_RANK, MLA_ROPE,
            NA_HEADS * NA_HD, NA_HEADS * NA_HD, NA_HEADS * NA_HD,
            GQA_HEADS * GQA_HD, GQA_KV_HEADS * GQA_HD, GQA_KV_HEADS * GQA_HD,
            D_MODEL, D_MODEL, D_MODEL)
IN_SPLITS = tuple(int(s) for s in np.cumsum(IN_SIZES)[:-1])

N_LAT = BATCH * SEQ
N_CTX = BATCH * CTX_LEN
N_TOK = N_LAT + N_CTX
ROWS = SEQ // GRID_W

V7X_LANES = 128
V7X_VMEM_LIMIT = 56 * 1024 * 1024

_PACK_ORDER = (9, 10, 11, 0, 1, 3, 4, 5, 6, 7, 8, 2)
_PACK_COLS = 9728
_PACK_OFF = {}
_off = 0
for _i in _PACK_ORDER:
    _PACK_OFF[_i] = _off
    _off += IN_SIZES[_i]

ROUTE_COLS = V7X_LANES
MOE_TM = 256
TOK_TM = 256
MM_TM = 512
CTX_BLK = N_LAT // CTX_LEN

F32 = jnp.float32
BF16 = jnp.bfloat16
_LOG2E = 1.4426950408889634
_NT = (((1,), (1,)), ((), ()))


def _cparams(sem):
    return pltpu.CompilerParams(dimension_semantics=sem, vmem_limit_bytes=V7X_VMEM_LIMIT)


def _seg(i, tm):
    return jnp.minimum(i // (SEQ // tm), 2)


def _pcol(idx, width):
    assert _PACK_OFF[idx] % width == 0
    return _PACK_OFF[idx] // width


def _ada_kernel(ct_ref, w_ref, b_ref, o_ref):
    ct = ct_ref[...]
    act = ct * jax.nn.sigmoid(ct)
    w = w_ref[0]
    rows = [jnp.sum(act[:, r:r + 1] * w, axis=0, keepdims=True) for r in range(BATCH + 1)]
    rows.append(jnp.zeros((8 - len(rows), w.shape[1]), F32))
    o_ref[0] = jnp.concatenate(rows, axis=0) + b_ref[0]


def _ada(c_rows, ada_w, ada_b):
    tn = 1024
    n = 6 * D_MODEL
    return pl.pallas_call(
        _ada_kernel,
        grid=(DEPTH, n // tn),
        in_specs=[
            pl.BlockSpec((D_MODEL, 8), lambda l, j: (0, 0)),
            pl.BlockSpec((1, D_MODEL, tn), lambda l, j: (l, 0, j)),
            pl.BlockSpec((1, 1, tn), lambda l, j: (l, 0, j)),
        ],
        out_specs=pl.BlockSpec((1, 8, tn), lambda l, j: (l, 0, j)),
        out_shape=jax.ShapeDtypeStruct((DEPTH, 8, n), F32),
        compiler_params=_cparams(("arbitrary", "arbitrary")),
        name="ada",
    )(c_rows.T, ada_w, ada_b.reshape(DEPTH, 1, n))


def _stream_specs(xs):
    tm = TOK_TM
    if not isinstance(xs, tuple):
        return [pl.BlockSpec((tm, D_MODEL), lambda i: (i, 0))], [xs], xs.shape[0]
    nl = N_LAT // tm
    return ([pl.BlockSpec((tm, D_MODEL), lambda i: (jnp.minimum(i, nl - 1), 0)),
             pl.BlockSpec((tm, D_MODEL), lambda i: (jnp.maximum(i - nl, 0), 0))],
            list(xs), xs[0].shape[0] + xs[1].shape[0])


def _stream_tile(refs):
    if len(refs) == 1:
        return refs[0][...]
    return jnp.where(pl.program_id(0) >= N_LAT // TOK_TM, refs[1][...], refs[0][...])


def _norm_mod_kernel(*refs, n_src):
    g_ref, sh_ref, sc_ref, h_ref = refs[n_src:]
    x = _stream_tile(refs[:n_src])
    xn = x * lax.rsqrt(jnp.mean(x * x, axis=-1, keepdims=True) + EPS) * g_ref[...]
    h_ref[...] = (xn * (1.0 + sc_ref[0]) + sh_ref[0]).astype(h_ref.dtype)


def _norm_mod(xs, g, mod, k_shift, k_scale):
    tm = TOK_TM
    specs, args, m = _stream_specs(xs)
    return pl.pallas_call(
        functools.partial(_norm_mod_kernel, n_src=len(args)),
        grid=(m // tm,),
        in_specs=specs + [
            pl.BlockSpec((1, D_MODEL), lambda i: (0, 0)),
            pl.BlockSpec((1, 1, D_MODEL), lambda i: (_seg(i, tm) * 6 + k_shift, 0, 0)),
            pl.BlockSpec((1, 1, D_MODEL), lambda i: (_seg(i, tm) * 6 + k_scale, 0, 0))],
        out_specs=pl.BlockSpec((tm, D_MODEL), lambda i: (i, 0)),
        out_shape=jax.ShapeDtypeStruct((m, D_MODEL), BF16),
        compiler_params=_cparams(("arbitrary",)),
        name="norm_mod",
    )(*args, g.reshape(1, D_MODEL), mod, mod)


def _out_route_kernel(*refs, n_src):
    y_ref, w_ref = refs[:2]
    g1_ref, gn_ref, sh_ref, sc_ref, wr_ref, br_ref, xo_ref, h_ref, lg_ref = refs[2 + n_src:]
    acc = jnp.dot(y_ref[...], w_ref[...], preferred_element_type=F32)
    x = _stream_tile(refs[2:2 + n_src]) + g1_ref[0] * acc
    xo_ref[...] = x
    xn = x * lax.rsqrt(jnp.mean(x * x, axis=-1, keepdims=True) + EPS) * gn_ref[...]
    h = xn * (1.0 + sc_ref[0]) + sh_ref[0]
    h_ref[...] = h
    hi = h.astype(BF16)
    lo = (h - hi.astype(F32)).astype(BF16)
    wr = wr_ref[...]
    w_hi = wr.astype(BF16)
    w_lo = (wr - w_hi.astype(F32)).astype(BF16)
    lg_ref[...] = (jnp.dot(hi, w_hi, preferred_element_type=F32)
                   + jnp.dot(hi, w_lo, preferred_element_type=F32)
                   + jnp.dot(lo, w_hi, preferred_element_type=F32) + br_ref[...])


def _out_route(y, w_out_bf16, res, mod, norm_g, wr, br):
    m = y.shape[0]
    tm = TOK_TM

    def modrow(k):
        return pl.BlockSpec((1, 1, D_MODEL), lambda i: (_seg(i, tm) * 6 + k, 0, 0))

    row = pl.BlockSpec((tm, D_MODEL), lambda i: (i, 0))
    res_specs, res_args, _ = _stream_specs(res)
    return pl.pallas_call(
        functools.partial(_out_route_kernel, n_src=len(res_args)),
        grid=(m // tm,),
        in_specs=[row,
                  pl.BlockSpec((D_MODEL, D_MODEL), lambda i: (0, 0))] + res_specs + [
                  modrow(2),
                  pl.BlockSpec((1, D_MODEL), lambda i: (0, 0)),
                  modrow(3), modrow(4),
                  pl.BlockSpec((D_MODEL, ROUTE_COLS), lambda i: (0, 0)),
                  pl.BlockSpec((1, ROUTE_COLS), lambda i: (0, 0))],
        out_specs=[row, row, pl.BlockSpec((tm, ROUTE_COLS), lambda i: (i, 0))],
        out_shape=[jax.ShapeDtypeStruct((m, D_MODEL), F32),
                   jax.ShapeDtypeStruct((m, D_MODEL), F32),
                   jax.ShapeDtypeStruct((m, ROUTE_COLS), F32)],
        compiler_params=_cparams(("arbitrary",)),
        name="out_route",
    )(y, w_out_bf16, *res_args, mod, norm_g.reshape(1, D_MODEL), mod, mod, wr, br)


def _mm_bf16_kernel(x_ref, w_ref, o_ref):
    o_ref[...] = lax.dot_general(x_ref[...], w_ref[0], _NT,
                                 preferred_element_type=F32).astype(o_ref.dtype)


def _mm(x, wt, layer, out_dtype, tn):
    m, k = x.shape
    n = wt.shape[1]
    tm = MM_TM
    assert x.dtype == BF16 and wt.dtype == BF16
    w = wt
    return pl.pallas_call(
        _mm_bf16_kernel,
        grid=(n // tn, m // tm),
        in_specs=[pl.BlockSpec((tm, k), lambda j, i: (i, 0)),
                  pl.BlockSpec((1, tn, k), lambda j, i: (layer, j, 0))],
        out_specs=pl.BlockSpec((tm, tn), lambda j, i: (i, j)),
        out_shape=jax.ShapeDtypeStruct((m, n), out_dtype),
        compiler_params=_cparams(("arbitrary", "arbitrary")),
        name="mm_bf16",
    )(x, w)


def _row_rms(x, g):
    return x * lax.rsqrt(jnp.mean(x * x, axis=-1, keepdims=True) + EPS) * g


_MLA_PAD = 2 * V7X_LANES


def _mla_prep_kernel(cq_ref, ckv_ref, kr_ref, wq_ref, wk_ref, wvt_ref, gqi_ref, gkvi_ref,
                     gq_ref, gkn_ref, gkr_ref, c_ref, s_ref, q_ref, k_ref, vt_ref):
    c = c_ref[...]
    s = s_ref[...]
    hw = MLA_HEADS * MLA_NOPE

    def rot(t):
        return t * c + (pltpu.roll(t, 32, 1) + pltpu.roll(t, 96, 1)) * s

    cqn = _row_rms(cq_ref[...].astype(F32), gqi_ref[...]).astype(BF16)
    qf = jnp.dot(cqn, wq_ref[...], preferred_element_type=F32)
    gq = gq_ref[...]
    inv = 1.0 / MLA_QK
    for h in range(MLA_HEADS):
        nope = qf[:, h * 128:(h + 1) * 128]
        t = qf[:, hw + h * 128:hw + (h + 1) * 128]
        ss = jnp.sum(nope * nope, axis=-1, keepdims=True) + jnp.sum(t * t, axis=-1, keepdims=True)
        r = lax.rsqrt(ss * inv + EPS)
        q_ref[h, :, 0:128] = (nope * r * gq[:, 0:128]).astype(BF16)
        q_ref[h, :, 128:256] = rot(t * r * gq[:, 128:256]).astype(BF16)

    ckvn = _row_rms(ckv_ref[...].astype(F32), gkvi_ref[...]).astype(BF16)
    kvf = jnp.dot(ckvn, wk_ref[...], preferred_element_type=F32)
    kr = kr_ref[...].astype(F32)
    ssr = jnp.sum(kr * kr, axis=-1, keepdims=True)
    yrot = rot(kr * gkr_ref[...])
    gkn = gkn_ref[...]
    for h in range(MLA_HEADS):
        nope = kvf[:, h * 128:(h + 1) * 128]
        r = lax.rsqrt((jnp.sum(nope * nope, axis=-1, keepdims=True) + ssr) * inv + EPS)
        k_ref[h, :, 0:128] = (nope * r * gkn).astype(BF16)
        k_ref[h, :, 128:256] = (yrot * r).astype(BF16)
        vt_ref[h] = lax.dot_general(wvt_ref[h * MLA_V:(h + 1) * MLA_V, :], ckvn, _NT,
                                    preferred_element_type=F32).astype(BF16)


def _mla_prep(proj, p, tabs):
    tm = TOK_TM
    hw = MLA_HEADS * MLA_NOPE
    wq = p['mla_w_uq'].reshape(MLA_Q_RANK, MLA_HEADS, MLA_QK)
    wq_rope = jnp.pad(wq[:, :, MLA_NOPE:], ((0, 0), (0, 0), (0, 128 - MLA_ROPE)))
    wq = jnp.concatenate([wq[:, :, :MLA_NOPE].reshape(MLA_Q_RANK, hw),
                          wq_rope.reshape(MLA_Q_RANK, hw)], axis=1).astype(BF16)
    wkv = p['mla_w_ukv'].reshape(MLA_KV_RANK, MLA_HEADS, MLA_NOPE + MLA_V)
    wk = wkv[:, :, :MLA_NOPE].reshape(MLA_KV_RANK, hw).astype(BF16)
    wvt = jnp.transpose(wkv[:, :, MLA_NOPE:], (1, 2, 0)).reshape(MLA_HEADS * MLA_V, MLA_KV_RANK).astype(BF16)
    zpad = jnp.zeros((128 - MLA_ROPE,), F32)
    gq = (jnp.concatenate([p['mla_qn_g'], zpad]) * (MLA_QK ** -0.5 * _LOG2E)).reshape(1, _MLA_PAD)
    gkn = p['mla_kn_g'][:MLA_NOPE].reshape(1, 128)
    gkr = jnp.concatenate([p['mla_kn_g'][MLA_NOPE:], zpad]).reshape(1, 128)

    def const(shape):
        return pl.BlockSpec(shape, lambda i: (0,) * len(shape))

    return pl.pallas_call(
        _mla_prep_kernel,
        grid=(N_TOK // tm,),
        in_specs=[pl.BlockSpec((tm, MLA_Q_RANK), lambda i: (i, _pcol(0, MLA_Q_RANK))),
                  pl.BlockSpec((tm, MLA_KV_RANK), lambda i: (i, _pcol(1, MLA_KV_RANK))),
                  pl.BlockSpec((tm, 128), lambda i: (i, _pcol(2, 128))),
                  const((MLA_Q_RANK, 2 * hw)), const((MLA_KV_RANK, hw)),
                  const((MLA_HEADS * MLA_V, MLA_KV_RANK)),
                  const((1, MLA_Q_RANK)), const((1, MLA_KV_RANK)),
                  const((1, _MLA_PAD)), const((1, 128)), const((1, 128)),
                  pl.BlockSpec((tm, 128), lambda i: (_rope_block(i), 0)),
                  pl.BlockSpec((tm, 128), lambda i: (_rope_block(i), 0))],
        out_specs=[pl.BlockSpec((MLA_HEADS, tm, _MLA_PAD), lambda i: (0, i, 0)),
                   pl.BlockSpec((MLA_HEADS, tm, _MLA_PAD), lambda i: (0, i, 0)),
                   pl.BlockSpec((MLA_HEADS, MLA_V, tm), lambda i: (0, 0, i))],
        out_shape=[jax.ShapeDtypeStruct((MLA_HEADS, N_TOK, _MLA_PAD), BF16),
                   jax.ShapeDtypeStruct((MLA_HEADS, N_TOK, _MLA_PAD), BF16),
                   jax.ShapeDtypeStruct((MLA_HEADS, MLA_V, N_TOK), BF16)],
        compiler_params=_cparams(("arbitrary",)),
        name="mla_prep",
    )(proj, proj, proj, wq, wk, wvt, p['mla_q_norm_g'].reshape(1, -1), p['mla_kv_norm_g'].reshape(1, -1),
      gq, gkn, gkr, tabs['mla_c'], tabs['mla_s'])


def _mla_kernel(q_ref, k1_ref, vt1_ref, k2_ref, vt2_ref, o_ref, sa_ref, sb_ref, sc_ref, acc_ref,
                *, tk, n_chunks):
    q = q_ref[0]
    tq = q.shape[0]
    hq = tq // 2
    qs = (q[:hq], q[hq:])

    def scores(dst_ref, kc):
        out = []
        for j in range(2):
            st = lax.dot_general(kc, qs[j], _NT, preferred_element_type=F32)
            dst_ref[j] = st
            out.append(jnp.max(st, axis=0, keepdims=True))
        return out

    def k_chunk(c):
        return k1_ref[0, pl.ds(pl.multiple_of(c * tk, tk), tk), :]

    def vt_chunk(c):
        return vt1_ref[0, :, pl.ds(pl.multiple_of(c * tk, tk), tk)]

    def accumulate(s_ref, smax, vtc, m, l):
        m_out, l_out = [], []
        for j in range(2):
            m_new = jnp.maximum(m[j], smax[j])
            a = jnp.exp2(m[j] - m_new)
            p = jnp.exp2(s_ref[j] - m_new)
            l_out.append(a * l[j] + jnp.sum(p, axis=0, keepdims=True))
            acc_ref[j] = a * acc_ref[j] + jnp.dot(vtc, p.astype(BF16), preferred_element_type=F32)
            m_out.append(m_new)
        return m_out, l_out

    m = [jnp.full((1, hq), NEG_INF, F32)] * 2
    l = [jnp.zeros((1, hq), F32)] * 2
    acc_ref[...] = jnp.zeros_like(acc_ref)
    mx_c = scores(sc_ref, k2_ref[0])
    mx_a = scores(sa_ref, k_chunk(0))
    m, l = accumulate(sc_ref, mx_c, vt2_ref[0], m, l)

    for i in range(n_chunks // 2 - 1):
        mx_b = scores(sb_ref, k_chunk(2 * i + 1))
        m, l = accumulate(sa_ref, mx_a, vt_chunk(2 * i), m, l)
        mx_a = scores(sa_ref, k_chunk(2 * i + 2))
        m, l = accumulate(sb_ref, mx_b, vt_chunk(2 * i + 1), m, l)
    mx_b = scores(sb_ref, k_chunk(n_chunks - 1))
    m, l = accumulate(sa_ref, mx_a, vt_chunk(n_chunks - 2), m, l)
    m, l = accumulate(sb_ref, mx_b, vt_chunk(n_chunks - 1), m, l)
    for j in range(2):
        o_ref[j * hq:(j + 1) * hq, :] = (acc_ref[j] / l[j]).T.astype(o_ref.dtype)


def _mla_attn(q, k, vt, *, tq, tk):
    nq = SEQ // tq
    return pl.pallas_call(
        functools.partial(_mla_kernel, tk=tk, n_chunks=SEQ // tk),
        grid=(BATCH, MLA_HEADS, nq),
        in_specs=[pl.BlockSpec((1, tq, _MLA_PAD), lambda b, h, i: (h, b * nq + i, 0)),
                  pl.BlockSpec((1, SEQ, _MLA_PAD), lambda b, h, i: (h, b, 0)),
                  pl.BlockSpec((1, MLA_V, SEQ), lambda b, h, i: (h, 0, b)),
                  pl.BlockSpec((1, CTX_LEN, _MLA_PAD), lambda b, h, i: (h, CTX_BLK + b, 0)),
                  pl.BlockSpec((1, MLA_V, CTX_LEN), lambda b, h, i: (h, 0, CTX_BLK + b))],
        out_specs=pl.BlockSpec((tq, MLA_V), lambda b, h, i: (b * nq + i, h)),
        out_shape=jax.ShapeDtypeStruct((N_LAT, MLA_HEADS * MLA_V), BF16),
        scratch_shapes=[pltpu.VMEM((2, tk, tq // 2), F32), pltpu.VMEM((2, tk, tq // 2), F32),
                        pltpu.VMEM((2, CTX_LEN, tq // 2), F32), pltpu.VMEM((2, MLA_V, tq // 2), F32)],
        compiler_params=_cparams(("arbitrary", "arbitrary", "arbitrary")),
        name="mla_attn",
    )(q, k, vt, k, vt)


_NA_KEYS = NA_ROWS * GRID_W
NA_RB = 8


def _na_pattern(r):
    half = NA_ROWS // 2
    return jnp.where(r < half, r, jnp.where(r <= ROWS - half, half, r - (ROWS - NA_ROWS)))


def _na_kernel(q_ref, k_ref, v_ref, kc_ref, vc_ref, bias_ref, gq_ref, gk_ref, o_ref, kn_ref, kcn_ref):
    rb = pl.program_id(2)

    @pl.when(rb == 0)
    def _():
        kn_ref[...] = _row_rms(k_ref[...].astype(F32), gk_ref[...]).astype(BF16)
        kcn_ref[...] = _row_rms(kc_ref[...].astype(F32), gk_ref[...]).astype(BF16)

    q_all = _row_rms(q_ref[...].astype(F32), gq_ref[...]).astype(BF16)
    vc = vc_ref[...]
    sc_all = lax.dot_general(q_all, kcn_ref[...], _NT, preferred_element_type=F32)
    ms, ls, os_ = [], [], []
    for j in range(NA_RB):
        r = rb * NA_RB + j
        start = pl.multiple_of(jnp.clip(r - NA_ROWS // 2, 0, ROWS - NA_ROWS) * GRID_W, GRID_W)
        q = q_all[j * GRID_W:(j + 1) * GRID_W]
        k = kn_ref[pl.ds(start, _NA_KEYS), :]
        v = v_ref[pl.ds(start, _NA_KEYS), :]
        s = lax.dot_general(q, k, _NT, preferred_element_type=F32) + bias_ref[0, _na_pattern(r)]
        sc = sc_all[j * GRID_W:(j + 1) * GRID_W]
        m = jnp.maximum(jnp.max(s, axis=-1, keepdims=True), jnp.max(sc, axis=-1, keepdims=True))
        p = jnp.exp(s - m)
        ms.append(m)
        ls.append(jnp.sum(p, axis=-1, keepdims=True))
        os_.append(jnp.dot(p.astype(BF16), v, preferred_element_type=F32))
    m_all = jnp.concatenate(ms, axis=0)
    pc = jnp.exp(sc_all - m_all)
    l_all = jnp.concatenate(ls, axis=0) + jnp.sum(pc, axis=-1, keepdims=True)
    o = jnp.concatenate(os_, axis=0) + jnp.dot(pc.astype(BF16), vc, preferred_element_type=F32)
    o_ref[...] = (o / l_all).astype(o_ref.dtype)


def _na_attn(proj, bias, gq, gk):
    qrows = NA_RB * GRID_W
    nrb = ROWS // NA_RB
    cq, ck, cv = _pcol(3, NA_HD), _pcol(4, NA_HD), _pcol(5, NA_HD)
    return pl.pallas_call(
        _na_kernel,
        grid=(BATCH, NA_HEADS, nrb),
        in_specs=[pl.BlockSpec((qrows, NA_HD), lambda b, h, r: (b * nrb + r, cq + h)),
                  pl.BlockSpec((SEQ, NA_HD), lambda b, h, r: (b, ck + h)),
                  pl.BlockSpec((SEQ, NA_HD), lambda b, h, r: (b, cv + h)),
                  pl.BlockSpec((CTX_LEN, NA_HD), lambda b, h, r: (CTX_BLK + b, ck + h)),
                  pl.BlockSpec((CTX_LEN, NA_HD), lambda b, h, r: (CTX_BLK + b, cv + h)),
                  pl.BlockSpec((1, NA_ROWS, GRID_W, _NA_KEYS), lambda b, h, r: (h, 0, 0, 0)),
                  pl.BlockSpec((1, NA_HD), lambda b, h, r: (0, 0)),
                  pl.BlockSpec((1, NA_HD), lambda b, h, r: (0, 0))],
        out_specs=pl.BlockSpec((qrows, NA_HD), lambda b, h, r: (b * nrb + r, h)),
        out_shape=jax.ShapeDtypeStruct((N_LAT, NA_HEADS * NA_HD), BF16),
        scratch_shapes=[pltpu.VMEM((SEQ, NA_HD), BF16), pltpu.VMEM((CTX_LEN, NA_HD), BF16)],
        compiler_params=_cparams(("arbitrary", "arbitrary", "arbitrary")),
        name="na_attn",
    )(proj, proj, proj, proj, proj, bias, gq, gk)


def _na_bias_table(rpb):
    half = NA_ROWS // 2
    r_rep = np.array(list(range(half)) + [half] + list(range(ROWS - half + 1, ROWS)))
    start = np.clip(r_rep - half, 0, ROWS - NA_ROWS)
    dr = start[:, None] + np.arange(NA_ROWS)[None, :] - r_rep[:, None] + NA_ROWS - 1
    qc = np.arange(GRID_W)
    kcol = np.arange(GRID_W)
    col_start = np.clip(qc - NA_COLS // 2, 0, GRID_W - NA_COLS)
    in_win = (kcol[None, :] >= col_start[:, None]) & (kcol[None, :] < col_start[:, None] + NA_COLS)
    dc = np.clip(kcol[None, :] - qc[:, None], 1 - NA_COLS, NA_COLS - 1) + NA_COLS - 1
    rsel = (dr[:, :, None] == np.arange(2 * NA_ROWS - 1)).astype(np.float32)
    csel = (dc[:, :, None] == np.arange(2 * NA_COLS - 1)).astype(np.float32)
    b = jnp.einsum('pja,hab,qkb->hpqjk', rsel, rpb.astype(F32), csel,
                   precision=lax.Precision.HIGHEST)
    b = jnp.where(in_win[None, None, :, None, :], b.astype(F32), NEG_INF)
    return b.reshape(NA_HEADS, NA_ROWS, GRID_W, _NA_KEYS)


_GQA_G = GQA_HEADS // GQA_KV_HEADS
_GQA_BAND = 3 * GQA_WINDOW
_GQA_QW = GQA_HEADS * V7X_LANES


def _gqa_prep_kernel(q_ref, k_ref, gq_ref, gk_ref, c_ref, s1_ref, s2_ref, qd_ref, kn_ref):
    c = c_ref[...]
    s1 = s1_ref[...]
    s2 = s2_ref[...]
    lo = lax.broadcasted_iota(jnp.int32, (1, 128), 1) < GQA_HD

    def head_rms(x, g):
        x2 = x * x
        s_lo = jnp.sum(jnp.where(lo, x2, 0.0), axis=-1, keepdims=True)
        s_hi = jnp.sum(jnp.where(lo, 0.0, x2), axis=-1, keepdims=True)
        inv = 1.0 / GQA_HD
        r = jnp.where(lo, lax.rsqrt(s_lo * inv + EPS), lax.rsqrt(s_hi * inv + EPS))
        return x * r * g

    def rot(x):
        return x * c + pltpu.roll(x, 96, 1) * s1 + pltpu.roll(x, 32, 1) * s2

    gq = gq_ref[...]
    for j in range(GQA_HEADS // 2):
        y = rot(head_rms(q_ref[:, j * 128:(j + 1) * 128].astype(F32), gq))
        sw = pltpu.roll(y, 64, 1)
        hk = (2 * j) // _GQA_G
        if hk == 0:
            even, odd = jnp.where(lo, y, 0.0), jnp.where(lo, sw, 0.0)
        else:
            even, odd = jnp.where(lo, 0.0, sw), jnp.where(lo, 0.0, y)
        qd_ref[:, (2 * j) * 128:(2 * j + 1) * 128] = even.astype(BF16)
        qd_ref[:, (2 * j + 1) * 128:(2 * j + 2) * 128] = odd.astype(BF16)
    kn_ref[...] = rot(head_rms(k_ref[...].astype(F32), gk_ref[...])).astype(BF16)


def _gqa_prep(proj, p, tabs):
    tm = TOK_TM
    gq = (jnp.tile(p['gqa_qn_g'], 2) * (GQA_HD ** -0.5)).reshape(1, 128)
    gk = jnp.tile(p['gqa_kn_g'], 2).reshape(1, 128)
    row = pl.BlockSpec((tm, 128), lambda i: (i, 0))
    tab = pl.BlockSpec((tm, 128), lambda i: (_rope_block(i), 0))
    vec = pl.BlockSpec((1, 128), lambda i: (0, 0))
    return pl.pallas_call(
        _gqa_prep_kernel,
        grid=(N_TOK // tm,),
        in_specs=[pl.BlockSpec((tm, GQA_HEADS * GQA_HD), lambda i: (i, _pcol(6, GQA_HEADS * GQA_HD))),
                  pl.BlockSpec((tm, 128), lambda i: (i, _pcol(7, 128))),
                  vec, vec, tab, tab, tab],
        out_specs=[pl.BlockSpec((tm, _GQA_QW), lambda i: (i, 0)), row],
        out_shape=[jax.ShapeDtypeStruct((N_TOK, _GQA_QW), BF16),
                   jax.ShapeDtypeStruct((N_TOK, 128), BF16)],
        compiler_params=_cparams(("arbitrary",)),
        name="gqa_prep",
    )(proj, proj, gq, gk, tabs['gqa_c'], tabs['gqa_s1'], tabs['gqa_s2'])


def _gqa_kernel(sink_ref, q_ref, k_ref, v_ref, kc_ref, vc_ref, o_ref):
    hk = pl.program_id(1)
    n = pl.program_id(2)
    w = GQA_WINDOW
    start = pl.multiple_of(jnp.clip((n - 1) * w, 0, SEQ - _GQA_BAND), w)
    q = jnp.concatenate([q_ref[:, g * 128:(g + 1) * 128] for g in range(_GQA_G)], axis=0)
    k = k_ref[pl.ds(start, _GQA_BAND), :]
    v = v_ref[pl.ds(start, _GQA_BAND), :]
    s = lax.dot_general(q, k, _NT, preferred_element_type=F32)
    rows = lax.broadcasted_iota(jnp.int32, s.shape, 0)
    cols = lax.broadcasted_iota(jnp.int32, s.shape, 1)
    qpos = n * w + (rows & (w - 1))
    kpos = start + cols
    s = jnp.where(jnp.abs(kpos - qpos) <= GQA_WINDOW, s, NEG_INF)
    sc = lax.dot_general(q, kc_ref[...], _NT, preferred_element_type=F32)
    grow = lax.broadcasted_iota(jnp.int32, (_GQA_G * w, 1), 0) // w
    snk = jnp.full((_GQA_G * w, 1), sink_ref[hk * _GQA_G], F32)
    for g in range(1, _GQA_G):
        snk = jnp.where(grow == g, sink_ref[hk * _GQA_G + g], snk)
    m = jnp.maximum(jnp.maximum(jnp.max(s, axis=-1, keepdims=True),
                                jnp.max(sc, axis=-1, keepdims=True)), snk)
    p = jnp.exp(s - m)
    pc = jnp.exp(sc - m)
    l = jnp.sum(p, axis=-1, keepdims=True) + jnp.sum(pc, axis=-1, keepdims=True) + jnp.exp(snk - m)
    o = (jnp.dot(p.astype(BF16), v, preferred_element_type=F32)
         + jnp.dot(pc.astype(BF16), vc_ref[...], preferred_element_type=F32)) / l
    half = lax.broadcasted_iota(jnp.int32, (1, 128), 1) // GQA_HD
    o = jnp.where(half == hk, o, 0.0).astype(o_ref.dtype)
    for g in range(_GQA_G):
        o_ref[:, g * 128:(g + 1) * 128] = o[g * w:(g + 1) * w]


def _gqa_attn(qd, kn, proj, sink):
    nb = SEQ // GQA_WINDOW
    qw = _GQA_G * 128
    cv = _pcol(8, 128)
    gs = pltpu.PrefetchScalarGridSpec(
        num_scalar_prefetch=1,
        grid=(BATCH, GQA_KV_HEADS, nb),
        in_specs=[pl.BlockSpec((GQA_WINDOW, qw), lambda b, h, n, *_: (b * nb + n, h)),
                  pl.BlockSpec((SEQ, 128), lambda b, h, n, *_: (b, 0)),
                  pl.BlockSpec((SEQ, 128), lambda b, h, n, *_: (b, cv)),
                  pl.BlockSpec((CTX_LEN, 128), lambda b, h, n, *_: (CTX_BLK + b, 0)),
                  pl.BlockSpec((CTX_LEN, 128), lambda b, h, n, *_: (CTX_BLK + b, cv))],
        out_specs=pl.BlockSpec((GQA_WINDOW, qw), lambda b, h, n, *_: (b * nb + n, h)),
    )
    return pl.pallas_call(
        _gqa_kernel,
        grid_spec=gs,
        out_shape=jax.ShapeDtypeStruct((N_LAT, _GQA_QW), BF16),
        compiler_params=_cparams(("arbitrary", "arbitrary", "arbitrary")),
        name="gqa_attn",
    )(sink.astype(F32), qd, kn, proj, kn, proj)


def _ctx_kernel(sink_ref, mq_ref, mk_ref, mvt_ref, nq_ref, nk_ref, nv_ref, gnq_ref, gnk_ref,
                gq_ref, gk_ref, gv_ref, oa_ref, ob_ref, oc_ref):
    for h in range(MLA_HEADS):
        st = lax.dot_general(mk_ref[h], mq_ref[h], _NT, preferred_element_type=F32)
        p = jnp.exp2(st - jnp.max(st, axis=0, keepdims=True))
        l = jnp.sum(p, axis=0, keepdims=True)
        ot = jnp.dot(mvt_ref[h], p.astype(BF16), preferred_element_type=F32) / l
        oa_ref[:, h * MLA_V:(h + 1) * MLA_V] = ot.T.astype(oa_ref.dtype)
    for h in range(NA_HEADS):
        sl = slice(h * NA_HD, (h + 1) * NA_HD)
        q = _row_rms(nq_ref[:, sl].astype(F32), gnq_ref[...]).astype(BF16)
        k = _row_rms(nk_ref[:, sl].astype(F32), gnk_ref[...]).astype(BF16)
        s = lax.dot_general(q, k, _NT, preferred_element_type=F32)
        p = jnp.exp(s - jnp.max(s, axis=-1, keepdims=True))
        l = jnp.sum(p, axis=-1, keepdims=True)
        o = jnp.dot(p.astype(BF16), nv_ref[:, sl], preferred_element_type=F32) / l
        ob_ref[:, sl] = o.astype(ob_ref.dtype)
    half = lax.broadcasted_iota(jnp.int32, (1, 128), 1) // GQA_HD
    k = gk_ref[...]
    v = gv_ref[...]
    for h in range(GQA_HEADS):
        sl = slice(h * 128, (h + 1) * 128)
        s = lax.dot_general(gq_ref[:, sl], k, _NT, preferred_element_type=F32)
        snk = sink_ref[h]
        m = jnp.maximum(jnp.max(s, axis=-1, keepdims=True), snk)
        p = jnp.exp(s - m)
        l = jnp.sum(p, axis=-1, keepdims=True) + jnp.exp(snk - m)
        o = jnp.dot(p.astype(BF16), v, preferred_element_type=F32) / l
        oc_ref[:, sl] = jnp.where(half == h // _GQA_G, o, 0.0).astype(oc_ref.dtype)


def _ctx_attn(sink, mq, mk, mvt, proj, gnq, gnk, qd, kn):
    c = CTX_LEN
    nwid = NA_HEADS * NA_HD

    def row(width, col=0):
        return pl.BlockSpec((c, width), lambda b, *_: (CTX_BLK + b, col))

    def out(width):
        return pl.BlockSpec((c, width), lambda b, *_: (b, 0))

    vec = pl.BlockSpec((1, NA_HD), lambda b, *_: (0, 0))
    gs = pltpu.PrefetchScalarGridSpec(
        num_scalar_prefetch=1,
        grid=(BATCH,),
        in_specs=[pl.BlockSpec((MLA_HEADS, c, _MLA_PAD), lambda b, *_: (0, CTX_BLK + b, 0)),
                  pl.BlockSpec((MLA_HEADS, c, _MLA_PAD), lambda b, *_: (0, CTX_BLK + b, 0)),
                  pl.BlockSpec((MLA_HEADS, MLA_V, c), lambda b, *_: (0, 0, CTX_BLK + b)),
                  row(nwid, _pcol(3, nwid)), row(nwid, _pcol(4, nwid)), row(nwid, _pcol(5, nwid)),
                  vec, vec,
                  row(_GQA_QW), row(128), row(128, _pcol(8, 128))],
        out_specs=[out(MLA_HEADS * MLA_V), out(nwid), out(_GQA_QW)],
    )
    return pl.pallas_call(
        _ctx_kernel,
        grid_spec=gs,
        out_shape=[jax.ShapeDtypeStruct((N_CTX, MLA_HEADS * MLA_V), BF16),
                   jax.ShapeDtypeStruct((N_CTX, nwid), BF16),
                   jax.ShapeDtypeStruct((N_CTX, _GQA_QW), BF16)],
        compiler_params=_cparams(("arbitrary",)),
        name="ctx_attn",
    )(sink.astype(F32), mq, mk, mvt, proj, proj, proj, gnq, gnk, qd, kn, proj)


def _merge_kernel(*refs, has_ctx, n_lat_tiles):
    ga_ref, gb_ref, gc_ref, oa_ref, ob_ref, oc_ref = refs[:6]
    refs = refs[6:]
    if has_ctx:
        ca_ref, cb_ref, cc_ref = refs[:3]
        refs = refs[3:]
    wa_ref, wb_ref, wc_ref, y_ref = refs
    is_ctx = pl.program_id(0) >= n_lat_tiles

    def branch(g_ref, o_ref, c_ref, w_ref):
        o = o_ref[...]
        if has_ctx:
            o = jnp.where(is_ctx, c_ref[...], o)
        return jax.nn.sigmoid(g_ref[...].astype(F32)) * jnp.dot(o, w_ref[...], preferred_element_type=F32)

    y = (branch(ga_ref, oa_ref, ca_ref if has_ctx else None, wa_ref)
         + branch(gb_ref, ob_ref, cb_ref if has_ctx else None, wb_ref)
         + branch(gc_ref, oc_ref, cc_ref if has_ctx else None, wc_ref))
    y_ref[...] = y.astype(y_ref.dtype)


def _merge(proj, lat, ctx, wa, wb, wc):
    tm = TOK_TM
    has_ctx = ctx is not None
    m = N_TOK if has_ctx else N_LAT
    nl = N_LAT // tm
    ks = [o.shape[1] for o in lat]
    in_specs = [pl.BlockSpec((tm, D_MODEL), lambda i: (i, 0)),
                pl.BlockSpec((tm, D_MODEL), lambda i: (i, 1)),
                pl.BlockSpec((tm, D_MODEL), lambda i: (i, 2))]
    in_specs += [pl.BlockSpec((tm, k), lambda i: (jnp.minimum(i, nl - 1), 0)) for k in ks]
    args = [proj, proj, proj] + list(lat)
    if has_ctx:
        in_specs += [pl.BlockSpec((tm, k), lambda i: (jnp.maximum(i - nl, 0), 0)) for k in ks]
        args += list(ctx)
    in_specs += [pl.BlockSpec((k, D_MODEL), lambda i: (0, 0)) for k in ks]
    args += [wa, wb, wc]
    return pl.pallas_call(
        functools.partial(_merge_kernel, has_ctx=has_ctx, n_lat_tiles=nl),
        grid=(m // tm,),
        in_specs=in_specs,
        out_specs=pl.BlockSpec((tm, D_MODEL), lambda i: (i, 0)),
        out_shape=jax.ShapeDtypeStruct((m, D_MODEL), BF16),
        compiler_params=_cparams(("arbitrary",)),
        name="merge",
    )(*args)


def _moe_ffn_kernel(te_ref, tok_ref, nu_ref, en_ref, es_ref, h_hbm, wg_hbm, wu_hbm, wd_hbm, y_ref,
                    xbuf, sem, wgs, wus, wds, wsem, wgb, wub, wdb, *, layer):
    i = pl.program_id(0)
    tm = MOE_TM
    slot = i % 2
    n_used = nu_ref[0]

    def row_copy(tok, s, r):
        return pltpu.make_async_copy(h_hbm.at[pl.ds(tok, 1)], xbuf.at[s, pl.ds(r, 1)], sem.at[s])

    def start_gather(tile, s):
        base = tile * tm

        def body(r, c):
            row_copy(tok_ref[base + r], s, r).start()
            return c

        lax.fori_loop(0, tm, body, 0, unroll=8)

    def weight_copies(e, s):
        return (pltpu.make_async_copy(wg_hbm.at[layer, e], wgs.at[s], wsem.at[s]),
                pltpu.make_async_copy(wu_hbm.at[layer, e], wus.at[s], wsem.at[s]),
                pltpu.make_async_copy(wd_hbm.at[layer, e], wds.at[s], wsem.at[s]))

    @pl.when(i == 0)
    def _():
        for cp in weight_copies(te_ref[0], es_ref[0]):
            cp.start(priority=1)
        start_gather(0, 0)

    @pl.when(i + 1 < n_used)
    def _():
        start_gather(i + 1, 1 - slot)

    @pl.when(i < n_used)
    def _():
        @pl.when((i == 0) | (te_ref[i] != te_ref[jnp.maximum(i - 1, 0)]))
        def _():
            s = es_ref[i]

            @pl.when(en_ref[i] >= 0)
            def _():
                for cp in weight_copies(en_ref[i], 1 - s):
                    cp.start(priority=1)

            for cp in weight_copies(te_ref[i], s):
                cp.wait()
            wgb[...] = wgs[s].astype(BF16)
            wub[...] = wus[s].astype(BF16)
            wdb[...] = wds[s].astype(BF16)

        pltpu.make_async_copy(h_hbm.at[pl.ds(0, tm)], xbuf.at[slot], sem.at[slot]).wait()
        x = xbuf[slot].astype(BF16)
        hg = jnp.dot(x, wgb[...], preferred_element_type=F32)
        hu = jnp.dot(x, wub[...], preferred_element_type=F32)
        act = (hg * jax.nn.sigmoid(hg)) * hu
        y_ref[...] = jnp.dot(act.astype(BF16), wdb[...], preferred_element_type=F32)

    @pl.when(i >= n_used)
    def _():
        y_ref[...] = jnp.zeros_like(y_ref)


def _moe_ffn(h, plan, wg, wu, wd, layer):
    tile_expert, slot_token, n_used, next_expert, expert_slot = plan
    p = slot_token.shape[0]
    tm = MOE_TM
    nt = p // tm
    anyspec = pl.BlockSpec(memory_space=pl.ANY)
    gs = pltpu.PrefetchScalarGridSpec(
        num_scalar_prefetch=5,
        grid=(nt,),
        in_specs=[anyspec, anyspec, anyspec, anyspec],
        out_specs=pl.BlockSpec((tm, D_MODEL), lambda i, *_: (i, 0)),
        scratch_shapes=[pltpu.VMEM((2, tm, D_MODEL), F32),
                        pltpu.SemaphoreType.DMA((2,)),
                        pltpu.VMEM((2, D_MODEL, MOE_HIDDEN), F32),
                        pltpu.VMEM((2, D_MODEL, MOE_HIDDEN), F32),
                        pltpu.VMEM((2, MOE_HIDDEN, D_MODEL), F32),
                        pltpu.SemaphoreType.DMA((2,)),
                        pltpu.VMEM((D_MODEL, MOE_HIDDEN), BF16),
                        pltpu.VMEM((D_MODEL, MOE_HIDDEN), BF16),
                        pltpu.VMEM((MOE_HIDDEN, D_MODEL), BF16)],
    )
    return pl.pallas_call(
        functools.partial(_moe_ffn_kernel, layer=layer),
        grid_spec=gs,
        out_shape=jax.ShapeDtypeStruct((p, D_MODEL), F32),
        compiler_params=_cparams(("arbitrary",)),
        name="moe_ffn",
    )(tile_expert, slot_token, n_used, next_expert, expert_slot, h, wg, wu, wd)


def _moe_combine_kernel(pos_ref, y_hbm, x_ref, w_ref, g_ref, o_ref, ybuf, sem):
    i = pl.program_id(0)
    nt = pl.num_programs(0)
    tm = TOK_TM
    slot = i % 2

    def row_copy(src, s, r):
        return pltpu.make_async_copy(y_hbm.at[pl.ds(src, 1)], ybuf.at[s, pl.ds(r, 1)], sem.at[s])

    def start_gather(tile, s):
        base = tile * tm

        def body(r, c):
            row_copy(pos_ref[2 * (base + r)], s, r).start(priority=0)
            row_copy(pos_ref[2 * (base + r) + 1], s, tm + r).start(priority=1)
            return c

        lax.fori_loop(0, tm, body, 0, unroll=4)

    @pl.when(i == 0)
    def _():
        start_gather(0, 0)

    @pl.when(i + 1 < nt)
    def _():
        start_gather(i + 1, 1 - slot)

    pltpu.make_async_copy(y_hbm.at[pl.ds(0, 2 * tm)], ybuf.at[slot], sem.at[slot]).wait()
    w = w_ref[...]
    y = ybuf[slot, pl.ds(0, tm), :] * w[:, 0:1] + ybuf[slot, pl.ds(tm, tm), :] * w[:, 1:2]
    o_ref[...] = x_ref[...] + g_ref[0] * y


def _moe_combine(y, pos, w_sel, x, mod, k_gate):
    m = x.shape[0]
    tm = TOK_TM
    gs = pltpu.PrefetchScalarGridSpec(
        num_scalar_prefetch=1,
        grid=(m // tm,),
        in_specs=[pl.BlockSpec(memory_space=pl.ANY),
                  pl.BlockSpec((tm, D_MODEL), lambda i, pos: (i, 0)),
                  pl.BlockSpec((tm, MOE_TOPK), lambda i, pos: (i, 0)),
                  pl.BlockSpec((1, 1, D_MODEL), lambda i, pos: (_seg(i, tm) * 6 + k_gate, 0, 0))],
        out_specs=pl.BlockSpec((tm, D_MODEL), lambda i, pos: (i, 0)),
        scratch_shapes=[pltpu.VMEM((2, 2 * tm, D_MODEL), F32),
                        pltpu.SemaphoreType.DMA((2,))],
    )
    return pl.pallas_call(
        _moe_combine_kernel,
        grid_spec=gs,
        out_shape=jax.ShapeDtypeStruct((m, D_MODEL), F32),
        compiler_params=_cparams(("arbitrary",)),
        name="moe_combine",
    )(pos, y, x, w_sel, mod)


def _route(logits, m):
    tm = MOE_TM
    gp = jax.nn.softmax(logits[:, :MOE_GROUPS], axis=-1)
    g_idx = jnp.argmax(gp, axis=-1).astype(jnp.int32)[:, None]
    g_w = jnp.max(gp, axis=-1, keepdims=True)
    el = logits[:, MOE_GROUPS:MOE_GROUPS + MOE_EXPERTS].reshape(m, MOE_GROUPS, MOE_PER_GROUP)
    g_onehot = (g_idx == jnp.arange(MOE_GROUPS, dtype=jnp.int32)[None, :]).astype(F32)
    el_g = jnp.sum(el * g_onehot[:, :, None], axis=1)
    i0 = jnp.argmax(el_g, axis=-1).astype(jnp.int32)[:, None]
    l0 = jnp.max(el_g, axis=-1, keepdims=True)
    rest = jnp.where(jnp.arange(MOE_PER_GROUP, dtype=jnp.int32)[None, :] == i0, -jnp.inf, el_g)
    i1 = jnp.argmax(rest, axis=-1).astype(jnp.int32)[:, None]
    l1 = jnp.max(rest, axis=-1, keepdims=True)
    top_l = jnp.concatenate([l0, l1], axis=-1)
    top_i = jnp.concatenate([i0, i1], axis=-1)
    w_sel = jax.nn.softmax(top_l, axis=-1) * g_w
    eid = (g_idx * MOE_PER_GROUP + top_i).astype(jnp.int32)

    a = m * MOE_TOPK
    e_flat = eid.reshape(a)
    onehot = (e_flat[:, None] == jnp.arange(MOE_EXPERTS, dtype=jnp.int32)[None, :]).astype(jnp.int32)
    csum = jnp.cumsum(onehot, axis=0)
    rank = jnp.sum(csum * onehot, axis=1) - 1
    counts = csum[-1]
    padded = ((counts + tm - 1) // tm) * tm
    ends = jnp.cumsum(padded)
    starts = ends - padded
    pos = (jnp.sum(onehot * starts[None, :], axis=1) + rank).astype(jnp.int32)
    p = a + MOE_EXPERTS * tm
    slot_token = jnp.zeros((p,), jnp.int32).at[pos].set(jnp.arange(a, dtype=jnp.int32) // MOE_TOPK)
    n_used = (ends[-1] // tm).astype(jnp.int32).reshape(1)
    tile_start = jnp.arange(p // tm, dtype=jnp.int32) * tm
    last_e = jnp.max(jnp.where(counts > 0, jnp.arange(MOE_EXPERTS, dtype=jnp.int32), 0))
    tile_expert = jnp.minimum(
        jnp.sum((ends[None, :] <= tile_start[:, None]).astype(jnp.int32), axis=1), last_e)
    eidx = jnp.arange(MOE_EXPERTS, dtype=jnp.int32)
    used = counts > 0
    later = used[None, :] & (eidx[None, :] > eidx[:, None])
    next_used = jnp.min(jnp.where(later, eidx[None, :], MOE_EXPERTS), axis=1)
    next_used = jnp.where(next_used == MOE_EXPERTS, -1, next_used).astype(jnp.int32)
    ordinal = (jnp.cumsum(used.astype(jnp.int32)) - 1) % 2
    t_onehot = (tile_expert[:, None] == eidx[None, :]).astype(jnp.int32)
    next_expert = jnp.sum(t_onehot * next_used[None, :], axis=1).astype(jnp.int32)
    expert_slot = jnp.sum(t_onehot * ordinal[None, :], axis=1).astype(jnp.int32)
    return (tile_expert, slot_token, n_used, next_expert, expert_slot), w_sel, pos


def _rope_angles(rot_dim):
    t = np.arange(SEQ)
    row = (t // GRID_W).astype(np.float64)
    col = (t % GRID_W).astype(np.float64)
    n_freq = rot_dim // 4
    inv = ROPE_THETA ** (-np.arange(n_freq, dtype=np.float64) / n_freq)
    ang = np.concatenate([row[:, None] * inv, col[:, None] * inv], axis=-1)
    return np.cos(ang), np.sin(ang)


def _rope_tables():
    def rows(lat, ident):
        ctx = np.zeros((TOK_TM, 128)) + ident
        return jnp.asarray(np.concatenate([lat, ctx], axis=0).astype(np.float32))

    z32 = np.zeros((SEQ, 32))
    z64 = np.zeros((SEQ, 64))
    lane = np.arange(128)
    cm, sm = _rope_angles(MLA_ROPE)
    mla_c = rows(np.concatenate([cm, cm, z64], axis=1), (lane < 64).astype(np.float64))
    mla_s = rows(np.concatenate([-sm, sm, z64], axis=1), 0.0)
    cg, sg = _rope_angles(GQA_HD)
    gqa_c = rows(np.concatenate([cg, cg, cg, cg], axis=1), 1.0)
    gqa_s1 = rows(np.concatenate([-sg, z32, -sg, z32], axis=1), 0.0)
    gqa_s2 = rows(np.concatenate([z32, sg, z32, sg], axis=1), 0.0)
    return {'mla_c': mla_c, 'mla_s': mla_s, 'gqa_c': gqa_c, 'gqa_s1': gqa_s1, 'gqa_s2': gqa_s2}


def _rope_block(i):
    per_seq = SEQ // TOK_TM
    return jnp.where(i < BATCH * per_seq, i % per_seq, per_seq)


def _pack_w_in_kernel(w_ref, o_ref):
    bounds = (0,) + IN_SPLITS + (sum(IN_SIZES),)
    off = 0
    for i in _PACK_ORDER:
        size = bounds[i + 1] - bounds[i]
        o_ref[0, off:off + size, :] = w_ref[0, bounds[i]:bounds[i + 1], :].astype(BF16)
        off += size
    o_ref[0, off:, :] = jnp.zeros((_PACK_COLS - off, o_ref.shape[2]), BF16)


def _pack_w_in(w_in):
    tk = 256
    n_in = sum(IN_SIZES)
    wt = jnp.swapaxes(w_in, 1, 2)
    return pl.pallas_call(
        _pack_w_in_kernel,
        grid=(DEPTH, D_MODEL // tk),
        in_specs=[pl.BlockSpec((1, n_in, tk), lambda l, i: (l, 0, i))],
        out_specs=pl.BlockSpec((1, _PACK_COLS, tk), lambda l, i: (l, 0, i)),
        out_shape=jax.ShapeDtypeStruct((DEPTH, _PACK_COLS, D_MODEL), BF16),
        compiler_params=_cparams(("arbitrary", "arbitrary")),
        name="pack_w_in",
    )(wt)


def _pad_w_o_gqa(w):
    w4 = w.reshape(GQA_KV_HEADS, _GQA_G, GQA_HD, D_MODEL)
    z = jnp.zeros_like(w4[0:1])
    halves = [jnp.concatenate([w4[hk:hk + 1] if hk == half else z for hk in range(GQA_KV_HEADS)], axis=0)
              for half in range(GQA_KV_HEADS)]
    return jnp.stack(halves, axis=2).reshape(_GQA_QW, D_MODEL).astype(BF16)


def _token_mixer(h, p, tabs, ctx_out):
    proj = _mm(h, p['w_in_packed'], p['layer'], BF16, _PACK_COLS // 4)
    mq, mk, mvt = _mla_prep(proj, p, tabs)
    oa = _mla_attn(mq, mk, mvt, tq=512, tk=1024)
    gnq = (p['na_qn_g'] * (NA_HD ** -0.5)).reshape(1, NA_HD)
    gnk = p['na_kn_g'].reshape(1, NA_HD)
    ob = _na_attn(proj, _na_bias_table(p['na_rpb']), gnq, gnk)
    qd, kn = _gqa_prep(proj, p, tabs)
    oc = _gqa_attn(qd, kn, proj, p['gqa_sink'])
    ctx = _ctx_attn(p['gqa_sink'], mq, mk, mvt, proj, gnq, gnk, qd, kn) if ctx_out else None
    return _merge(proj, (oa, ob, oc), ctx, p['w_o_mla'].astype(BF16), p['w_o_na'].astype(BF16),
                  _pad_w_o_gqa(p['w_o_gqa']))


def _post_mixer(y, xt, mod, norm_g, w_out_l, p):
    m = y.shape[0]
    wr = jnp.concatenate([p['moe_w_group'], p['moe_w_expert'],
                          jnp.zeros((D_MODEL, ROUTE_COLS - MOE_GROUPS - MOE_EXPERTS), F32)], axis=1)
    br = jnp.concatenate([p['moe_b_group'], p['moe_b_expert'],
                          jnp.zeros((ROUTE_COLS - MOE_GROUPS - MOE_EXPERTS,), F32)]).reshape(1, ROUTE_COLS)
    x_mid, h, logits = _out_route(y, w_out_l.astype(BF16), xt, mod, norm_g, wr, br)
    plan, w_sel, pos = _route(logits, m)
    yy = _moe_ffn(h, plan, p['moe_w_gate'], p['moe_w_up'], p['moe_w_down'], p['layer'])
    return _moe_combine(yy, pos, w_sel, x_mid, mod, 5)


def kernel(x, c, ctx, c_ctx, ada_w, ada_b, norm_mix_g, norm_ffn_g, w_in,
           mla_q_norm_g, mla_w_uq, mla_kv_norm_g, mla_w_ukv, mla_qn_g, mla_kn_g,
           na_qn_g, na_kn_g, na_rpb, gqa_qn_g, gqa_kn_g, gqa_sink,
           w_o_mla, w_o_na, w_o_gqa, w_out,
           moe_w_group, moe_b_group, moe_w_expert, moe_b_expert,
           moe_w_gate, moe_w_up, moe_w_down):
    xt = (x.reshape(N_LAT, D_MODEL), ctx.reshape(N_CTX, D_MODEL))
    c_rows = jnp.concatenate([c, c_ctx[None, :], jnp.zeros((8 - BATCH - 1, D_MODEL), F32)], axis=0)
    mod_all = _ada(c_rows, ada_w, ada_b)
    tabs = _rope_tables()
    w_in_packed = _pack_w_in(w_in)
    for l in range(DEPTH):
        ctx_out = l < DEPTH - 1
        p = {
            'w_in_packed': w_in_packed, 'mla_q_norm_g': mla_q_norm_g[l], 'mla_w_uq': mla_w_uq[l],
            'mla_kv_norm_g': mla_kv_norm_g[l], 'mla_w_ukv': mla_w_ukv[l],
            'mla_qn_g': mla_qn_g[l], 'mla_kn_g': mla_kn_g[l],
            'na_qn_g': na_qn_g[l], 'na_kn_g': na_kn_g[l], 'na_rpb': na_rpb[l],
            'gqa_qn_g': gqa_qn_g[l], 'gqa_kn_g': gqa_kn_g[l], 'gqa_sink': gqa_sink[l],
            'w_o_mla': w_o_mla[l], 'w_o_na': w_o_na[l], 'w_o_gqa': w_o_gqa[l],
            'moe_w_group': moe_w_group[l], 'moe_b_group': moe_b_group[l],
            'moe_w_expert': moe_w_expert[l], 'moe_b_expert': moe_b_expert[l],
            'moe_w_gate': moe_w_gate, 'moe_w_up': moe_w_up, 'moe_w_down': moe_w_down, 'layer': l,
        }
        mod = mod_all[l].reshape(8 * 6, 1, D_MODEL)
        h = _norm_mod(xt, norm_mix_g[l], mod, 0, 1)
        y = _token_mixer(h, p, tabs, ctx_out)
        xt = _post_mixer(y, xt, mod, norm_ffn_g[l], w_out[l], p)
    return xt[:N_LAT].reshape(BATCH, SEQ, D_MODEL)
```

```python
import functools

import numpy as np
import jax
import jax.numpy as jnp
from jax import lax
from jax.experimental import pallas as pl
from jax.experimental.pallas import tpu as pltpu

D_MODEL = 2048
BATCH = 2
SEQ = 4096
DEPTH = 2
GRID_W = 64
CTX_LEN = 256
EPS = 1e-6
ROPE_THETA = 10000.0
NEG_INF = -1e30

MLA_HEADS = 8
MLA_Q_RANK = 512
MLA_KV_RANK = 512
MLA_NOPE = 128
MLA_ROPE = 64
MLA_QK = MLA_NOPE + MLA_ROPE
MLA_V = 128
NA_HEADS = 4
NA_HD = 128
NA_ROWS = 8
NA_COLS = 16
GQA_HEADS = 8
GQA_KV_HEADS = 2
GQA_HD = 64
GQA_WINDOW = 128
MOE_GROUPS = 4
MOE_PER_GROUP = 8
MOE_EXPERTS = MOE_GROUPS * MOE_PER_GROUP
MOE_TOPK = 2
MOE_HIDDEN = 512

IN_SIZES = (MLA_Q_RANK, MLA_KV_RANK, MLA_ROPE,
            NA_HEADS * NA_HD, NA_HEADS * NA_HD, NA_HEADS * NA_HD,
            GQA_HEADS * GQA_HD, GQA_KV_HEADS * GQA_HD, GQA_KV_HEADS * GQA_HD,
            D_MODEL, D_MODEL, D_MODEL)
IN_SPLITS = tuple(int(s) for s in np.cumsum(IN_SIZES)[:-1])

N_LAT = BATCH * SEQ
N_CTX = BATCH * CTX_LEN
N_TOK = N_LAT + N_CTX
ROWS = SEQ // GRID_W

V7X_LANES = 128
V7X_VMEM_LIMIT = 56 * 1024 * 1024

_PACK_ORDER = (9, 10, 11, 0, 1, 3, 4, 5, 6, 7, 8, 2)
_PACK_COLS = 9728
_PACK_OFF = {}
_off = 0
for _i in _PACK_ORDER:
    _PACK_OFF[_i] = _off
    _off += IN_SIZES[_i]

ROUTE_COLS = V7X_LANES
MOE_TM = 256
TOK_TM = 256
MM_TM = 512
CTX_BLK = N_LAT // CTX_LEN

F32 = jnp.float32
BF16 = jnp.bfloat16
_LOG2E = 1.4426950408889634
_NT = (((1,), (1,)), ((), ()))


def _cparams(sem):
    return pltpu.CompilerParams(dimension_semantics=sem, vmem_limit_bytes=V7X_VMEM_LIMIT)


def _seg(i, tm):
    return jnp.minimum(i // (SEQ // tm), 2)


def _pcol(idx, width):
    assert _PACK_OFF[idx] % width == 0
    return _PACK_OFF[idx] // width


def _ada_kernel(ct_ref, w_ref, b_ref, o_ref):
    ct = ct_ref[...]
    act = ct * jax.nn.sigmoid(ct)
    w = w_ref[0]
    rows = [jnp.sum(act[:, r:r + 1] * w, axis=0, keepdims=True) for r in range(BATCH + 1)]
    rows.append(jnp.zeros((8 - len(rows), w.shape[1]), F32))
    o_ref[0] = jnp.concatenate(rows, axis=0) + b_ref[0]


def _ada(c_rows, ada_w, ada_b):
    tn = 1024
    n = 6 * D_MODEL
    return pl.pallas_call(
        _ada_kernel,
        grid=(DEPTH, n // tn),
        in_specs=[
            pl.BlockSpec((D_MODEL, 8), lambda l, j: (0, 0)),
            pl.BlockSpec((1, D_MODEL, tn), lambda l, j: (l, 0, j)),
            pl.BlockSpec((1, 1, tn), lambda l, j: (l, 0, j)),
        ],
        out_specs=pl.BlockSpec((1, 8, tn), lambda l, j: (l, 0, j)),
        out_shape=jax.ShapeDtypeStruct((DEPTH, 8, n), F32),
        compiler_params=_cparams(("arbitrary", "arbitrary")),
        name="ada",
    )(c_rows.T, ada_w, ada_b.reshape(DEPTH, 1, n))


def _stream_specs(xs):
    tm = TOK_TM
    if not isinstance(xs, tuple):
        return [pl.BlockSpec((tm, D_MODEL), lambda i: (i, 0))], [xs], xs.shape[0]
    nl = N_LAT // tm
    return ([pl.BlockSpec((tm, D_MODEL), lambda i: (jnp.minimum(i, nl - 1), 0)),
             pl.BlockSpec((tm, D_MODEL), lambda i: (jnp.maximum(i - nl, 0), 0))],
            list(xs), xs[0].shape[0] + xs[1].shape[0])


def _stream_tile(refs):
    if len(refs) == 1:
        return refs[0][...]
    return jnp.where(pl.program_id(0) >= N_LAT // TOK_TM, refs[1][...], refs[0][...])


def _norm_mod_kernel(*refs, n_src):
    g_ref, sh_ref, sc_ref, h_ref = refs[n_src:]
    x = _stream_tile(refs[:n_src])
    xn = x * lax.rsqrt(jnp.mean(x * x, axis=-1, keepdims=True) + EPS) * g_ref[...]
    h_ref[...] = (xn * (1.0 + sc_ref[0]) + sh_ref[0]).astype(h_ref.dtype)


def _norm_mod(xs, g, mod, k_shift, k_scale):
    tm = TOK_TM
    specs, args, m = _stream_specs(xs)
    return pl.pallas_call(
        functools.partial(_norm_mod_kernel, n_src=len(args)),
        grid=(m // tm,),
        in_specs=specs + [
            pl.BlockSpec((1, D_MODEL), lambda i: (0, 0)),
            pl.BlockSpec((1, 1, D_MODEL), lambda i: (_seg(i, tm) * 6 + k_shift, 0, 0)),
            pl.BlockSpec((1, 1, D_MODEL), lambda i: (_seg(i, tm) * 6 + k_scale, 0, 0))],
        out_specs=pl.BlockSpec((tm, D_MODEL), lambda i: (i, 0)),
        out_shape=jax.ShapeDtypeStruct((m, D_MODEL), BF16),
        compiler_params=_cparams(("arbitrary",)),
        name="norm_mod",
    )(*args, g.reshape(1, D_MODEL), mod, mod)


def _out_route_kernel(*refs, n_src):
    y_ref, w_ref = refs[:2]
    g1_ref, gn_ref, sh_ref, sc_ref, wr_ref, br_ref, xo_ref, h_ref, lg_ref = refs[2 + n_src:]
    acc = jnp.dot(y_ref[...], w_ref[...], preferred_element_type=F32)
    x = _stream_tile(refs[2:2 + n_src]) + g1_ref[0] * acc
    xo_ref[...] = x
    xn = x * lax.rsqrt(jnp.mean(x * x, axis=-1, keepdims=True) + EPS) * gn_ref[...]
    h = xn * (1.0 + sc_ref[0]) + sh_ref[0]
    h_ref[...] = h
    hi = h.astype(BF16)
    lo = (h - hi.astype(F32)).astype(BF16)
    wr = wr_ref[...]
    w_hi = wr.astype(BF16)
    w_lo = (wr - w_hi.astype(F32)).astype(BF16)
    lg_ref[...] = (jnp.dot(hi, w_hi, preferred_element_type=F32)
                   + jnp.dot(hi, w_lo, preferred_element_type=F32)
                   + jnp.dot(lo, w_hi, preferred_element_type=F32) + br_ref[...])


def _out_route(y, w_out_bf16, res, mod, norm_g, wr, br):
    m = y.shape[0]
    tm = TOK_TM

    def modrow(k):
        return pl.BlockSpec((1, 1, D_MODEL), lambda i: (_seg(i, tm) * 6 + k, 0, 0))

    row = pl.BlockSpec((tm, D_MODEL), lambda i: (i, 0))
    res_specs, res_args, _ = _stream_specs(res)
    return pl.pallas_call(
        functools.partial(_out_route_kernel, n_src=len(res_args)),
        grid=(m // tm,),
        in_specs=[row,
                  pl.BlockSpec((D_MODEL, D_MODEL), lambda i: (0, 0))] + res_specs + [
                  modrow(2),
                  pl.BlockSpec((1, D_MODEL), lambda i: (0, 0)),
                  modrow(3), modrow(4),
                  pl.BlockSpec((D_MODEL, ROUTE_COLS), lambda i: (0, 0)),
                  pl.BlockSpec((1, ROUTE_COLS), lambda i: (0, 0))],
        out_specs=[row, row, pl.BlockSpec((tm, ROUTE_COLS), lambda i: (i, 0))],
        out_shape=[jax.ShapeDtypeStruct((m, D_MODEL), F32),
                   jax.ShapeDtypeStruct((m, D_MODEL), F32),
                   jax.ShapeDtypeStruct((m, ROUTE_COLS), F32)],
        compiler_params=_cparams(("arbitrary",)),
        name="out_route",
    )(y, w_out_bf16, *res_args, mod, norm_g.reshape(1, D_MODEL), mod, mod, wr, br)


def _mm_bf16_kernel(x_ref, w_ref, o_ref):
    o_ref[...] = lax.dot_general(x_ref[...], w_ref[0], _NT,
                                 preferred_element_type=F32).astype(o_ref.dtype)


def _mm(x, wt, layer, out_dtype, tn):
    m, k = x.shape
    n = wt.shape[1]
    tm = MM_TM
    assert x.dtype == BF16 and wt.dtype == BF16
    w = wt
    return pl.pallas_call(
        _mm_bf16_kernel,
        grid=(n // tn, m // tm),
        in_specs=[pl.BlockSpec((tm, k), lambda j, i: (i, 0)),
                  pl.BlockSpec((1, tn, k), lambda j, i: (layer, j, 0))],
        out_specs=pl.BlockSpec((tm, tn), lambda j, i: (i, j)),
        out_shape=jax.ShapeDtypeStruct((m, n), out_dtype),
        compiler_params=_cparams(("arbitrary", "arbitrary")),
        name="mm_bf16",
    )(x, w)


def _row_rms(x, g):
    return x * lax.rsqrt(jnp.mean(x * x, axis=-1, keepdims=True) + EPS) * g


_MLA_PAD = 2 * V7X_LANES


def _mla_prep_kernel(cq_ref, ckv_ref, kr_ref, wq_ref, wk_ref, wvt_ref, gqi_ref, gkvi_ref,
                     gq_ref, gkn_ref, gkr_ref, c_ref, s_ref, q_ref, k_ref, vt_ref):
    c = c_ref[...]
    s = s_ref[...]
    hw = MLA_HEADS * MLA_NOPE

    def rot(t):
        return t * c + (pltpu.roll(t, 32, 1) + pltpu.roll(t, 96, 1)) * s

    cqn = _row_rms(cq_ref[...].astype(F32), gqi_ref[...]).astype(BF16)
    qf = jnp.dot(cqn, wq_ref[...], preferred_element_type=F32)
    gq = gq_ref[...]
    inv = 1.0 / MLA_QK
    for h in range(MLA_HEADS):
        nope = qf[:, h * 128:(h + 1) * 128]
        t = qf[:, hw + h * 128:hw + (h + 1) * 128]
        ss = jnp.sum(nope * nope, axis=-1, keepdims=True) + jnp.sum(t * t, axis=-1, keepdims=True)
        r = lax.rsqrt(ss * inv + EPS)
        q_ref[h, :, 0:128] = (nope * r * gq[:, 0:128]).astype(BF16)
        q_ref[h, :, 128:256] = rot(t * r * gq[:, 128:256]).astype(BF16)

    ckvn = _row_rms(ckv_ref[...].astype(F32), gkvi_ref[...]).astype(BF16)
    kvf = jnp.dot(ckvn, wk_ref[...], preferred_element_type=F32)
    kr = kr_ref[...].astype(F32)
    ssr = jnp.sum(kr * kr, axis=-1, keepdims=True)
    yrot = rot(kr * gkr_ref[...])
    gkn = gkn_ref[...]
    for h in range(MLA_HEADS):
        nope = kvf[:, h * 128:(h + 1) * 128]
        r = lax.rsqrt((jnp.sum(nope * nope, axis=-1, keepdims=True) + ssr) * inv + EPS)
        k_ref[h, :, 0:128] = (nope * r * gkn).astype(BF16)
        k_ref[h, :, 128:256] = (yrot * r).astype(BF16)
        vt_ref[h] = lax.dot_general(wvt_ref[h * MLA_V:(h + 1) * MLA_V, :], ckvn, _NT,
                                    preferred_element_type=F32).astype(BF16)


def _mla_prep(proj, p, tabs):
    tm = TOK_TM
    hw = MLA_HEADS * MLA_NOPE
    wq = p['mla_w_uq'].reshape(MLA_Q_RANK, MLA_HEADS, MLA_QK)
    wq_rope = jnp.pad(wq[:, :, MLA_NOPE:], ((0, 0), (0, 0), (0, 128 - MLA_ROPE)))
    wq = jnp.concatenate([wq[:, :, :MLA_NOPE].reshape(MLA_Q_RANK, hw),
                          wq_rope.reshape(MLA_Q_RANK, hw)], axis=1).astype(BF16)
    wkv = p['mla_w_ukv'].reshape(MLA_KV_RANK, MLA_HEADS, MLA_NOPE + MLA_V)
    wk = wkv[:, :, :MLA_NOPE].reshape(MLA_KV_RANK, hw).astype(BF16)
    wvt = jnp.transpose(wkv[:, :, MLA_NOPE:], (1, 2, 0)).reshape(MLA_HEADS * MLA_V, MLA_KV_RANK).astype(BF16)
    zpad = jnp.zeros((128 - MLA_ROPE,), F32)
    gq = (jnp.concatenate([p['mla_qn_g'], zpad]) * (MLA_QK ** -0.5 * _LOG2E)).reshape(1, _MLA_PAD)
    gkn = p['mla_kn_g'][:MLA_NOPE].reshape(1, 128)
    gkr = jnp.concatenate([p['mla_kn_g'][MLA_NOPE:], zpad]).reshape(1, 128)

    def const(shape):
        return pl.BlockSpec(shape, lambda i: (0,) * len(shape))

    return pl.pallas_call(
        _mla_prep_kernel,
        grid=(N_TOK // tm,),
        in_specs=[pl.BlockSpec((tm, MLA_Q_RANK), lambda i: (i, _pcol(0, MLA_Q_RANK))),
                  pl.BlockSpec((tm, MLA_KV_RANK), lambda i: (i, _pcol(1, MLA_KV_RANK))),
                  pl.BlockSpec((tm, 128), lambda i: (i, _pcol(2, 128))),
                  const((MLA_Q_RANK, 2 * hw)), const((MLA_KV_RANK, hw)),
                  const((MLA_HEADS * MLA_V, MLA_KV_RANK)),
                  const((1, MLA_Q_RANK)), const((1, MLA_KV_RANK)),
                  const((1, _MLA_PAD)), const((1, 128)), const((1, 128)),
                  pl.BlockSpec((tm, 128), lambda i: (_rope_block(i), 0)),
                  pl.BlockSpec((tm, 128), lambda i: (_rope_block(i), 0))],
        out_specs=[pl.BlockSpec((MLA_HEADS, tm, _MLA_PAD), lambda i: (0, i, 0)),
                   pl.BlockSpec((MLA_HEADS, tm, _MLA_PAD), lambda i: (0, i, 0)),
                   pl.BlockSpec((MLA_HEADS, MLA_V, tm), lambda i: (0, 0, i))],
        out_shape=[jax.ShapeDtypeStruct((MLA_HEADS, N_TOK, _MLA_PAD), BF16),
                   jax.ShapeDtypeStruct((MLA_HEADS, N_TOK, _MLA_PAD), BF16),
                   jax.ShapeDtypeStruct((MLA_HEADS, MLA_V, N_TOK), BF16)],
        compiler_params=_cparams(("arbitrary",)),
        name="mla_prep",
    )(proj, proj, proj, wq, wk, wvt, p['mla_q_norm_g'].reshape(1, -1), p['mla_kv_norm_g'].reshape(1, -1),
      gq, gkn, gkr, tabs['mla_c'], tabs['mla_s'])


def _mla_kernel(q_ref, k1_ref, vt1_ref, k2_ref, vt2_ref, o_ref, sa_ref, sb_ref, sc_ref, acc_ref,
                *, tk, n_chunks):
    q = q_ref[0]
    tq = q.shape[0]
    hq = tq // 2
    qs = (q[:hq], q[hq:])

    def scores(dst_ref, kc):
        out = []
        for j in range(2):
            st = lax.dot_general(kc, qs[j], _NT, preferred_element_type=F32)
            dst_ref[j] = st
            out.append(jnp.max(st, axis=0, keepdims=True))
        return out

    def k_chunk(c):
        return k1_ref[0, pl.ds(pl.multiple_of(c * tk, tk), tk), :]

    def vt_chunk(c):
        return vt1_ref[0, :, pl.ds(pl.multiple_of(c * tk, tk), tk)]

    def accumulate(s_ref, smax, vtc, m, l):
        m_out, l_out = [], []
        for j in range(2):
            m_new = jnp.maximum(m[j], smax[j])
            a = jnp.exp2(m[j] - m_new)
            p = jnp.exp2(s_ref[j] - m_new)
            l_out.append(a * l[j] + jnp.sum(p, axis=0, keepdims=True))
            acc_ref[j] = a * acc_ref[j] + jnp.dot(vtc, p.astype(BF16), preferred_element_type=F32)
            m_out.append(m_new)
        return m_out, l_out

    m = [jnp.full((1, hq), NEG_INF, F32)] * 2
    l = [jnp.zeros((1, hq), F32)] * 2
    acc_ref[...] = jnp.zeros_like(acc_ref)
    mx_c = scores(sc_ref, k2_ref[0])
    mx_a = scores(sa_ref, k_chunk(0))
    m, l = accumulate(sc_ref, mx_c, vt2_ref[0], m, l)

    for i in range(n_chunks // 2 - 1):
        mx_b = scores(sb_ref, k_chunk(2 * i + 1))
        m, l = accumulate(sa_ref, mx_a, vt_chunk(2 * i), m, l)
        mx_a = scores(sa_ref, k_chunk(2 * i + 2))
        m, l = accumulate(sb_ref, mx_b, vt_chunk(2 * i + 1), m, l)
    mx_b = scores(sb_ref, k_chunk(n_chunks - 1))
    m, l = accumulate(sa_ref, mx_a, vt_chunk(n_chunks - 2), m, l)
    m, l = accumulate(sb_ref, mx_b, vt_chunk(n_chunks - 1), m, l)
    for j in range(2):
        o_ref[j * hq:(j + 1) * hq, :] = (acc_ref[j] / l[j]).T.astype(o_ref.dtype)


def _mla_attn(q, k, vt, *, tq, tk):
    nq = SEQ // tq
    return pl.pallas_call(
        functools.partial(_mla_kernel, tk=tk, n_chunks=SEQ // tk),
        grid=(BATCH, MLA_HEADS, nq),
        in_specs=[pl.BlockSpec((1, tq, _MLA_PAD), lambda b, h, i: (h, b * nq + i, 0)),
                  pl.BlockSpec((1, SEQ, _MLA_PAD), lambda b, h, i: (h, b, 0)),
                  pl.BlockSpec((1, MLA_V, SEQ), lambda b, h, i: (h, 0, b)),
                  pl.BlockSpec((1, CTX_LEN, _MLA_PAD), lambda b, h, i: (h, CTX_BLK + b, 0)),
                  pl.BlockSpec((1, MLA_V, CTX_LEN), lambda b, h, i: (h, 0, CTX_BLK + b))],
        out_specs=pl.BlockSpec((tq, MLA_V), lambda b, h, i: (b * nq + i, h)),
        out_shape=jax.ShapeDtypeStruct((N_LAT, MLA_HEADS * MLA_V), BF16),
        scratch_shapes=[pltpu.VMEM((2, tk, tq // 2), F32), pltpu.VMEM((2, tk, tq // 2), F32),
                        pltpu.VMEM((2, CTX_LEN, tq // 2), F32), pltpu.VMEM((2, MLA_V, tq // 2), F32)],
        compiler_params=_cparams(("arbitrary", "arbitrary", "arbitrary")),
        name="mla_attn",
    )(q, k, vt, k, vt)


NA_RB = 8
NA_WIN_ROWS = 16
NA_QCHUNKS = 2
_NA_Q = NA_RB * GRID_W
_NA_WIN = NA_WIN_ROWS * GRID_W
assert NA_WIN_ROWS >= NA_RB + NA_ROWS - 1 and ROWS % NA_RB == 0


def _na_win_start(r0):
    return np.clip(r0 - NA_ROWS // 2, 0, ROWS - NA_WIN_ROWS)


def _na_kernel(q_ref, k_ref, v_ref, kc_ref, vc_ref, bias_ref, gq_ref, gk_ref, o_ref, kn_ref, kcn_ref):
    rb = pl.program_id(2)

    @pl.when(rb == 0)
    def _():
        kn_ref[...] = _row_rms(k_ref[...].astype(F32), gk_ref[...]).astype(BF16)
        kcn_ref[...] = _row_rms(kc_ref[...].astype(F32), gk_ref[...]).astype(BF16)

    q = _row_rms(q_ref[...].astype(F32), gq_ref[...]).astype(BF16)
    u = jnp.clip(rb * NA_RB - NA_ROWS // 2, 0, ROWS - NA_WIN_ROWS) * GRID_W
    u = pl.multiple_of(u, GRID_W)
    k = kn_ref[pl.ds(u, _NA_WIN), :]
    v = v_ref[pl.ds(u, _NA_WIN), :]
    kc = kcn_ref[...]
    vc = vc_ref[...]
    nq = _NA_Q // NA_QCHUNKS
    for j in range(NA_QCHUNKS):
        rows = slice(j * nq, (j + 1) * nq)
        qj = q[rows]
        s = lax.dot_general(qj, k, _NT, preferred_element_type=F32) + bias_ref[0, 0, rows, :]
        sc = lax.dot_general(qj, kc, _NT, preferred_element_type=F32)
        m = jnp.maximum(jnp.max(s, axis=-1, keepdims=True), jnp.max(sc, axis=-1, keepdims=True))
        p = jnp.exp(s - m)
        pc = jnp.exp(sc - m)
        l = jnp.sum(p, axis=-1, keepdims=True) + jnp.sum(pc, axis=-1, keepdims=True)
        o = (jnp.dot(p.astype(BF16), v, preferred_element_type=F32)
             + jnp.dot(pc.astype(BF16), vc, preferred_element_type=F32))
        o_ref[rows, :] = (o / l).astype(o_ref.dtype)


def _na_attn(proj, bias, gq, gk):
    nrb = ROWS // NA_RB
    cq, ck, cv = _pcol(3, NA_HD), _pcol(4, NA_HD), _pcol(5, NA_HD)

    def pattern(r):
        return jnp.where(r == 0, 0, jnp.where(r == nrb - 1, 2, 1))

    return pl.pallas_call(
        _na_kernel,
        grid=(BATCH, NA_HEADS, nrb),
        in_specs=[pl.BlockSpec((_NA_Q, NA_HD), lambda b, h, r: (b * nrb + r, cq + h)),
                  pl.BlockSpec((SEQ, NA_HD), lambda b, h, r: (b, ck + h)),
                  pl.BlockSpec((SEQ, NA_HD), lambda b, h, r: (b, cv + h)),
                  pl.BlockSpec((CTX_LEN, NA_HD), lambda b, h, r: (CTX_BLK + b, ck + h)),
                  pl.BlockSpec((CTX_LEN, NA_HD), lambda b, h, r: (CTX_BLK + b, cv + h)),
                  pl.BlockSpec((1, 1, _NA_Q, _NA_WIN), lambda b, h, r: (h, pattern(r), 0, 0)),
                  pl.BlockSpec((1, NA_HD), lambda b, h, r: (0, 0)),
                  pl.BlockSpec((1, NA_HD), lambda b, h, r: (0, 0))],
        out_specs=pl.BlockSpec((_NA_Q, NA_HD), lambda b, h, r: (b * nrb + r, h)),
        out_shape=jax.ShapeDtypeStruct((N_LAT, NA_HEADS * NA_HD), BF16),
        scratch_shapes=[pltpu.VMEM((SEQ, NA_HD), BF16), pltpu.VMEM((CTX_LEN, NA_HD), BF16)],
        compiler_params=_cparams(("arbitrary", "arbitrary", "arbitrary")),
        name="na_attn",
    )(proj, proj, proj, proj, proj, bias, gq, gk)


def _na_bias_table(rpb):
    half = NA_ROWS // 2
    r0 = np.array([0, NA_RB, ROWS - NA_RB])
    r = r0[:, None] + np.arange(NA_RB)[None, :]
    krow = _na_win_start(r0)[:, None] + np.arange(NA_WIN_ROWS)[None, :]
    start = np.clip(r - half, 0, ROWS - NA_ROWS)
    row_ok = (krow[:, None, :] >= start[:, :, None]) & (krow[:, None, :] < start[:, :, None] + NA_ROWS)
    dr = krow[:, None, :] - r[:, :, None] + NA_ROWS - 1
    qc = np.arange(GRID_W)
    kcol = np.arange(GRID_W)
    col_start = np.clip(qc - NA_COLS // 2, 0, GRID_W - NA_COLS)
    in_win = (kcol[None, :] >= col_start[:, None]) & (kcol[None, :] < col_start[:, None] + NA_COLS)
    dc = np.clip(kcol[None, :] - qc[:, None], 1 - NA_COLS, NA_COLS - 1) + NA_COLS - 1
    rsel = ((dr[..., None] == np.arange(2 * NA_ROWS - 1)) & row_ok[..., None]).astype(np.float32)
    csel = (dc[:, :, None] == np.arange(2 * NA_COLS - 1)).astype(np.float32)
    b = jnp.einsum('pjka,hab,qcb->hpjqkc', rsel, rpb.astype(F32), csel,
                   precision=lax.Precision.HIGHEST)
    ok = row_ok[None, :, :, None, :, None] & in_win[None, None, None, :, None, :]
    b = jnp.where(ok, b, NEG_INF)
    return b.reshape(NA_HEADS, 3, _NA_Q, _NA_WIN)


_GQA_G = GQA_HEADS // GQA_KV_HEADS
_GQA_BAND = 3 * GQA_WINDOW
_GQA_QW = GQA_HEADS * V7X_LANES


def _gqa_prep_kernel(q_ref, k_ref, gq_ref, gk_ref, c_ref, s1_ref, s2_ref, qd_ref, kn_ref):
    c = c_ref[...]
    s1 = s1_ref[...]
    s2 = s2_ref[...]
    lo = lax.broadcasted_iota(jnp.int32, (1, 128), 1) < GQA_HD

    def head_rms(x, g):
        x2 = x * x
        s_lo = jnp.sum(jnp.where(lo, x2, 0.0), axis=-1, keepdims=True)
        s_hi = jnp.sum(jnp.where(lo, 0.0, x2), axis=-1, keepdims=True)
        inv = 1.0 / GQA_HD
        r = jnp.where(lo, lax.rsqrt(s_lo * inv + EPS), lax.rsqrt(s_hi * inv + EPS))
        return x * r * g

    def rot(x):
        return x * c + pltpu.roll(x, 96, 1) * s1 + pltpu.roll(x, 32, 1) * s2

    gq = gq_ref[...]
    for j in range(GQA_HEADS // 2):
        y = rot(head_rms(q_ref[:, j * 128:(j + 1) * 128].astype(F32), gq))
        sw = pltpu.roll(y, 64, 1)
        hk = (2 * j) // _GQA_G
        if hk == 0:
            even, odd = jnp.where(lo, y, 0.0), jnp.where(lo, sw, 0.0)
        else:
            even, odd = jnp.where(lo, 0.0, sw), jnp.where(lo, 0.0, y)
        qd_ref[:, (2 * j) * 128:(2 * j + 1) * 128] = even.astype(BF16)
        qd_ref[:, (2 * j + 1) * 128:(2 * j + 2) * 128] = odd.astype(BF16)
    kn_ref[...] = rot(head_rms(k_ref[...].astype(F32), gk_ref[...])).astype(BF16)


def _gqa_prep(proj, p, tabs):
    tm = TOK_TM
    gq = (jnp.tile(p['gqa_qn_g'], 2) * (GQA_HD ** -0.5)).reshape(1, 128)
    gk = jnp.tile(p['gqa_kn_g'], 2).reshape(1, 128)
    row = pl.BlockSpec((tm, 128), lambda i: (i, 0))
    tab = pl.BlockSpec((tm, 128), lambda i: (_rope_block(i), 0))
    vec = pl.BlockSpec((1, 128), lambda i: (0, 0))
    return pl.pallas_call(
        _gqa_prep_kernel,
        grid=(N_TOK // tm,),
        in_specs=[pl.BlockSpec((tm, GQA_HEADS * GQA_HD), lambda i: (i, _pcol(6, GQA_HEADS * GQA_HD))),
                  pl.BlockSpec((tm, 128), lambda i: (i, _pcol(7, 128))),
                  vec, vec, tab, tab, tab],
        out_specs=[pl.BlockSpec((tm, _GQA_QW), lambda i: (i, 0)), row],
        out_shape=[jax.ShapeDtypeStruct((N_TOK, _GQA_QW), BF16),
                   jax.ShapeDtypeStruct((N_TOK, 128), BF16)],
        compiler_params=_cparams(("arbitrary",)),
        name="gqa_prep",
    )(proj, proj, gq, gk, tabs['gqa_c'], tabs['gqa_s1'], tabs['gqa_s2'])


def _gqa_kernel(sink_ref, q_ref, k_ref, v_ref, kc_ref, vc_ref, o_ref):
    hk = pl.program_id(1)
    n = pl.program_id(2)
    w = GQA_WINDOW
    start = pl.multiple_of(jnp.clip((n - 1) * w, 0, SEQ - _GQA_BAND), w)
    q = jnp.concatenate([q_ref[:, g * 128:(g + 1) * 128] for g in range(_GQA_G)], axis=0)
    k = k_ref[pl.ds(start, _GQA_BAND), :]
    v = v_ref[pl.ds(start, _GQA_BAND), :]
    s = lax.dot_general(q, k, _NT, preferred_element_type=F32)
    rows = lax.broadcasted_iota(jnp.int32, s.shape, 0)
    cols = lax.broadcasted_iota(jnp.int32, s.shape, 1)
    qpos = n * w + (rows & (w - 1))
    kpos = start + cols
    s = jnp.where(jnp.abs(kpos - qpos) <= GQA_WINDOW, s, NEG_INF)
    sc = lax.dot_general(q, kc_ref[...], _NT, preferred_element_type=F32)
    grow = lax.broadcasted_iota(jnp.int32, (_GQA_G * w, 1), 0) // w
    snk = jnp.full((_GQA_G * w, 1), sink_ref[hk * _GQA_G], F32)
    for g in range(1, _GQA_G):
        snk = jnp.where(grow == g, sink_ref[hk * _GQA_G + g], snk)
    m = jnp.maximum(jnp.maximum(jnp.max(s, axis=-1, keepdims=True),
                                jnp.max(sc, axis=-1, keepdims=True)), snk)
    p = jnp.exp(s - m)
    pc = jnp.exp(sc - m)
    l = jnp.sum(p, axis=-1, keepdims=True) + jnp.sum(pc, axis=-1, keepdims=True) + jnp.exp(snk - m)
    o = (jnp.dot(p.astype(BF16), v, preferred_element_type=F32)
         + jnp.dot(pc.astype(BF16), vc_ref[...], preferred_element_type=F32)) / l
    half = lax.broadcasted_iota(jnp.int32, (1, 128), 1) // GQA_HD
    o = jnp.where(half == hk, o, 0.0).astype(o_ref.dtype)
    for g in range(_GQA_G):
        o_ref[:, g * 128:(g + 1) * 128] = o[g * w:(g + 1) * w]


def _gqa_attn(qd, kn, proj, sink):
    nb = SEQ // GQA_WINDOW
    qw = _GQA_G * 128
    cv = _pcol(8, 128)
    gs = pltpu.PrefetchScalarGridSpec(
        num_scalar_prefetch=1,
        grid=(BATCH, GQA_KV_HEADS, nb),
        in_specs=[pl.BlockSpec((GQA_WINDOW, qw), lambda b, h, n, *_: (b * nb + n, h)),
                  pl.BlockSpec((SEQ, 128), lambda b, h, n, *_: (b, 0)),
                  pl.BlockSpec((SEQ, 128), lambda b, h, n, *_: (b, cv)),
                  pl.BlockSpec((CTX_LEN, 128), lambda b, h, n, *_: (CTX_BLK + b, 0)),
                  pl.BlockSpec((CTX_LEN, 128), lambda b, h, n, *_: (CTX_BLK + b, cv))],
        out_specs=pl.BlockSpec((GQA_WINDOW, qw), lambda b, h, n, *_: (b * nb + n, h)),
    )
    return pl.pallas_call(
        _gqa_kernel,
        grid_spec=gs,
        out_shape=jax.ShapeDtypeStruct((N_LAT, _GQA_QW), BF16),
        compiler_params=_cparams(("arbitrary", "arbitrary", "arbitrary")),
        name="gqa_attn",
    )(sink.astype(F32), qd, kn, proj, kn, proj)


def _ctx_kernel(sink_ref, mq_ref, mk_ref, mvt_ref, nq_ref, nk_ref, nv_ref, gnq_ref, gnk_ref,
                gq_ref, gk_ref, gv_ref, oa_ref, ob_ref, oc_ref):
    for h in range(MLA_HEADS):
        st = lax.dot_general(mk_ref[h], mq_ref[h], _NT, preferred_element_type=F32)
        p = jnp.exp2(st - jnp.max(st, axis=0, keepdims=True))
        l = jnp.sum(p, axis=0, keepdims=True)
        ot = jnp.dot(mvt_ref[h], p.astype(BF16), preferred_element_type=F32) / l
        oa_ref[:, h * MLA_V:(h + 1) * MLA_V] = ot.T.astype(oa_ref.dtype)
    for h in range(NA_HEADS):
        sl = slice(h * NA_HD, (h + 1) * NA_HD)
        q = _row_rms(nq_ref[:, sl].astype(F32), gnq_ref[...]).astype(BF16)
        k = _row_rms(nk_ref[:, sl].astype(F32), gnk_ref[...]).astype(BF16)
        s = lax.dot_general(q, k, _NT, preferred_element_type=F32)
        p = jnp.exp(s - jnp.max(s, axis=-1, keepdims=True))
        l = jnp.sum(p, axis=-1, keepdims=True)
        o = jnp.dot(p.astype(BF16), nv_ref[:, sl], preferred_element_type=F32) / l
        ob_ref[:, sl] = o.astype(ob_ref.dtype)
    half = lax.broadcasted_iota(jnp.int32, (1, 128), 1) // GQA_HD
    k = gk_ref[...]
    v = gv_ref[...]
    for h in range(GQA_HEADS):
        sl = slice(h * 128, (h + 1) * 128)
        s = lax.dot_general(gq_ref[:, sl], k, _NT, preferred_element_type=F32)
        snk = sink_ref[h]
        m = jnp.maximum(jnp.max(s, axis=-1, keepdims=True), snk)
        p = jnp.exp(s - m)
        l = jnp.sum(p, axis=-1, keepdims=True) + jnp.exp(snk - m)
        o = jnp.dot(p.astype(BF16), v, preferred_element_type=F32) / l
        oc_ref[:, sl] = jnp.where(half == h // _GQA_G, o, 0.0).astype(oc_ref.dtype)


def _ctx_attn(sink, mq, mk, mvt, proj, gnq, gnk, qd, kn):
    c = CTX_LEN
    nwid = NA_HEADS * NA_HD

    def row(width, col=0):
        return pl.BlockSpec((c, width), lambda b, *_: (CTX_BLK + b, col))

    def out(width):
        return pl.BlockSpec((c, width), lambda b, *_: (b, 0))

    vec = pl.BlockSpec((1, NA_HD), lambda b, *_: (0, 0))
    gs = pltpu.PrefetchScalarGridSpec(
        num_scalar_prefetch=1,
        grid=(BATCH,),
        in_specs=[pl.BlockSpec((MLA_HEADS, c, _MLA_PAD), lambda b, *_: (0, CTX_BLK + b, 0)),
                  pl.BlockSpec((MLA_HEADS, c, _MLA_PAD), lambda b, *_: (0, CTX_BLK + b, 0)),
                  pl.BlockSpec((MLA_HEADS, MLA_V, c), lambda b, *_: (0, 0, CTX_BLK + b)),
                  row(nwid, _pcol(3, nwid)), row(nwid, _pcol(4, nwid)), row(nwid, _pcol(5, nwid)),
                  vec, vec,
                  row(_GQA_QW), row(128), row(128, _pcol(8, 128))],
        out_specs=[out(MLA_HEADS * MLA_V), out(nwid), out(_GQA_QW)],
    )
    return pl.pallas_call(
        _ctx_kernel,
        grid_spec=gs,
        out_shape=[jax.ShapeDtypeStruct((N_CTX, MLA_HEADS * MLA_V), BF16),
                   jax.ShapeDtypeStruct((N_CTX, nwid), BF16),
                   jax.ShapeDtypeStruct((N_CTX, _GQA_QW), BF16)],
        compiler_params=_cparams(("arbitrary",)),
        name="ctx_attn",
    )(sink.astype(F32), mq, mk, mvt, proj, proj, proj, gnq, gnk, qd, kn, proj)


def _merge_kernel(*refs, has_ctx, n_lat_tiles):
    ga_ref, gb_ref, gc_ref, oa_ref, ob_ref, oc_ref = refs[:6]
    refs = refs[6:]
    if has_ctx:
        ca_ref, cb_ref, cc_ref = refs[:3]
        refs = refs[3:]
    wa_ref, wb_ref, wc_ref, y_ref = refs
    is_ctx = pl.program_id(0) >= n_lat_tiles

    def branch(g_ref, o_ref, c_ref, w_ref):
        o = o_ref[...]
        if has_ctx:
            o = jnp.where(is_ctx, c_ref[...], o)
        return jax.nn.sigmoid(g_ref[...].astype(F32)) * jnp.dot(o, w_ref[...], preferred_element_type=F32)

    y = (branch(ga_ref, oa_ref, ca_ref if has_ctx else None, wa_ref)
         + branch(gb_ref, ob_ref, cb_ref if has_ctx else None, wb_ref)
         + branch(gc_ref, oc_ref, cc_ref if has_ctx else None, wc_ref))
    y_ref[...] = y.astype(y_ref.dtype)


def _merge(proj, lat, ctx, wa, wb, wc):
    tm = TOK_TM
    has_ctx = ctx is not None
    m = N_TOK if has_ctx else N_LAT
    nl = N_LAT // tm
    ks = [o.shape[1] for o in lat]
    in_specs = [pl.BlockSpec((tm, D_MODEL), lambda i: (i, 0)),
                pl.BlockSpec((tm, D_MODEL), lambda i: (i, 1)),
                pl.BlockSpec((tm, D_MODEL), lambda i: (i, 2))]
    in_specs += [pl.BlockSpec((tm, k), lambda i: (jnp.minimum(i, nl - 1), 0)) for k in ks]
    args = [proj, proj, proj] + list(lat)
    if has_ctx:
        in_specs += [pl.BlockSpec((tm, k), lambda i: (jnp.maximum(i - nl, 0), 0)) for k in ks]
        args += list(ctx)
    in_specs += [pl.BlockSpec((k, D_MODEL), lambda i: (0, 0)) for k in ks]
    args += [wa, wb, wc]
    return pl.pallas_call(
        functools.partial(_merge_kernel, has_ctx=has_ctx, n_lat_tiles=nl),
        grid=(m // tm,),
        in_specs=in_specs,
        out_specs=pl.BlockSpec((tm, D_MODEL), lambda i: (i, 0)),
        out_shape=jax.ShapeDtypeStruct((m, D_MODEL), BF16),
        compiler_params=_cparams(("arbitrary",)),
        name="merge",
    )(*args)


def _moe_ffn_kernel(te_ref, tok_ref, nu_ref, en_ref, es_ref, h_hbm, wg_hbm, wu_hbm, wd_hbm, y_ref,
                    xbuf, sem, wgs, wus, wds, wsem, wgb, wub, wdb, *, layer):
    i = pl.program_id(0)
    tm = MOE_TM
    slot = i % 2
    n_used = nu_ref[0]

    def row_copy(tok, s, r):
        return pltpu.make_async_copy(h_hbm.at[pl.ds(tok, 1)], xbuf.at[s, pl.ds(r, 1)], sem.at[s])

    def start_gather(tile, s):
        base = tile * tm

        def body(r, c):
            row_copy(tok_ref[base + r], s, r).start()
            return c

        lax.fori_loop(0, tm, body, 0, unroll=8)

    def weight_copies(e, s):
        return (pltpu.make_async_copy(wg_hbm.at[layer, e], wgs.at[s], wsem.at[s]),
                pltpu.make_async_copy(wu_hbm.at[layer, e], wus.at[s], wsem.at[s]),
                pltpu.make_async_copy(wd_hbm.at[layer, e], wds.at[s], wsem.at[s]))

    @pl.when(i == 0)
    def _():
        for cp in weight_copies(te_ref[0], es_ref[0]):
            cp.start(priority=1)
        start_gather(0, 0)

    @pl.when(i + 1 < n_used)
    def _():
        start_gather(i + 1, 1 - slot)

    @pl.when(i < n_used)
    def _():
        @pl.when((i == 0) | (te_ref[i] != te_ref[jnp.maximum(i - 1, 0)]))
        def _():
            s = es_ref[i]

            @pl.when(en_ref[i] >= 0)
            def _():
                for cp in weight_copies(en_ref[i], 1 - s):
                    cp.start(priority=1)

            for cp in weight_copies(te_ref[i], s):
                cp.wait()
            wgb[...] = wgs[s].astype(BF16)
            wub[...] = wus[s].astype(BF16)
            wdb[...] = wds[s].astype(BF16)

        pltpu.make_async_copy(h_hbm.at[pl.ds(0, tm)], xbuf.at[slot], sem.at[slot]).wait()
        x = xbuf[slot].astype(BF16)
        hg = jnp.dot(x, wgb[...], preferred_element_type=F32)
        hu = jnp.dot(x, wub[...], preferred_element_type=F32)
        act = (hg * jax.nn.sigmoid(hg)) * hu
        y_ref[...] = jnp.dot(act.astype(BF16), wdb[...], preferred_element_type=F32)

    @pl.when(i >= n_used)
    def _():
        y_ref[...] = jnp.zeros_like(y_ref)


def _moe_ffn(h, plan, wg, wu, wd, layer):
    tile_expert, slot_token, n_used, next_expert, expert_slot = plan
    p = slot_token.shape[0]
    tm = MOE_TM
    nt = p // tm
    anyspec = pl.BlockSpec(memory_space=pl.ANY)
    gs = pltpu.PrefetchScalarGridSpec(
        num_scalar_prefetch=5,
        grid=(nt,),
        in_specs=[anyspec, anyspec, anyspec, anyspec],
        out_specs=pl.BlockSpec((tm, D_MODEL), lambda i, *_: (i, 0)),
        scratch_shapes=[pltpu.VMEM((2, tm, D_MODEL), F32),
                        pltpu.SemaphoreType.DMA((2,)),
                        pltpu.VMEM((2, D_MODEL, MOE_HIDDEN), F32),
                        pltpu.VMEM((2, D_MODEL, MOE_HIDDEN), F32),
                        pltpu.VMEM((2, MOE_HIDDEN, D_MODEL), F32),
                        pltpu.SemaphoreType.DMA((2,)),
                        pltpu.VMEM((D_MODEL, MOE_HIDDEN), BF16),
                        pltpu.VMEM((D_MODEL, MOE_HIDDEN), BF16),
                        pltpu.VMEM((MOE_HIDDEN, D_MODEL), BF16)],
    )
    return pl.pallas_call(
        functools.partial(_moe_ffn_kernel, layer=layer),
        grid_spec=gs,
        out_shape=jax.ShapeDtypeStruct((p, D_MODEL), F32),
        compiler_params=_cparams(("arbitrary",)),
        name="moe_ffn",
    )(tile_expert, slot_token, n_used, next_expert, expert_slot, h, wg, wu, wd)


def _moe_combine_kernel(pos_ref, y_hbm, x_ref, w_ref, g_ref, o_ref, ybuf, sem):
    i = pl.program_id(0)
    nt = pl.num_programs(0)
    tm = TOK_TM
    slot = i % 2

    def row_copy(src, s, r):
        return pltpu.make_async_copy(y_hbm.at[pl.ds(src, 1)], ybuf.at[s, pl.ds(r, 1)], sem.at[s])

    def start_gather(tile, s):
        base = tile * tm

        def body(r, c):
            row_copy(pos_ref[2 * (base + r)], s, r).start(priority=0)
            row_copy(pos_ref[2 * (base + r) + 1], s, tm + r).start(priority=1)
            return c

        lax.fori_loop(0, tm, body, 0, unroll=4)

    @pl.when(i == 0)
    def _():
        start_gather(0, 0)

    @pl.when(i + 1 < nt)
    def _():
        start_gather(i + 1, 1 - slot)

    pltpu.make_async_copy(y_hbm.at[pl.ds(0, 2 * tm)], ybuf.at[slot], sem.at[slot]).wait()
    w = w_ref[...]
    y = ybuf[slot, pl.ds(0, tm), :] * w[:, 0:1] + ybuf[slot, pl.ds(tm, tm), :] * w[:, 1:2]
    o_ref[...] = x_ref[...] + g_ref[0] * y


def _moe_combine(y, pos, w_sel, x, mod, k_gate):
    m = x.shape[0]
    tm = TOK_TM
    gs = pltpu.PrefetchScalarGridSpec(
        num_scalar_prefetch=1,
        grid=(m // tm,),
        in_specs=[pl.BlockSpec(memory_space=pl.ANY),
                  pl.BlockSpec((tm, D_MODEL), lambda i, pos: (i, 0)),
                  pl.BlockSpec((tm, MOE_TOPK), lambda i, pos: (i, 0)),
                  pl.BlockSpec((1, 1, D_MODEL), lambda i, pos: (_seg(i, tm) * 6 + k_gate, 0, 0))],
        out_specs=pl.BlockSpec((tm, D_MODEL), lambda i, pos: (i, 0)),
        scratch_shapes=[pltpu.VMEM((2, 2 * tm, D_MODEL), F32),
                        pltpu.SemaphoreType.DMA((2,))],
    )
    return pl.pallas_call(
        _moe_combine_kernel,
        grid_spec=gs,
        out_shape=jax.ShapeDtypeStruct((m, D_MODEL), F32),
        compiler_params=_cparams(("arbitrary",)),
        name="moe_combine",
    )(pos, y, x, w_sel, mod)


def _route(logits, m):
    tm = MOE_TM
    gp = jax.nn.softmax(logits[:, :MOE_GROUPS], axis=-1)
    g_idx = jnp.argmax(gp, axis=-1).astype(jnp.int32)[:, None]
    g_w = jnp.max(gp, axis=-1, keepdims=True)
    el = logits[:, MOE_GROUPS:MOE_GROUPS + MOE_EXPERTS].reshape(m, MOE_GROUPS, MOE_PER_GROUP)
    g_onehot = (g_idx == jnp.arange(MOE_GROUPS, dtype=jnp.int32)[None, :]).astype(F32)
    el_g = jnp.sum(el * g_onehot[:, :, None], axis=1)
    i0 = jnp.argmax(el_g, axis=-1).astype(jnp.int32)[:, None]
    l0 = jnp.max(el_g, axis=-1, keepdims=True)
    rest = jnp.where(jnp.arange(MOE_PER_GROUP, dtype=jnp.int32)[None, :] == i0, -jnp.inf, el_g)
    i1 = jnp.argmax(rest, axis=-1).astype(jnp.int32)[:, None]
    l1 = jnp.max(rest, axis=-1, keepdims=True)
    top_l = jnp.concatenate([l0, l1], axis=-1)
    top_i = jnp.concatenate([i0, i1], axis=-1)
    w_sel = jax.nn.softmax(top_l, axis=-1) * g_w
    eid = (g_idx * MOE_PER_GROUP + top_i).astype(jnp.int32)

    a = m * MOE_TOPK
    e_flat = eid.reshape(a)
    onehot = (e_flat[:, None] == jnp.arange(MOE_EXPERTS, dtype=jnp.int32)[None, :]).astype(jnp.int32)
    csum = jnp.cumsum(onehot, axis=0)
    rank = jnp.sum(csum * onehot, axis=1) - 1
    counts = csum[-1]
    padded = ((counts + tm - 1) // tm) * tm
    ends = jnp.cumsum(padded)
    starts = ends - padded
    pos = (jnp.sum(onehot * starts[None, :], axis=1) + rank).astype(jnp.int32)
    p = a + MOE_EXPERTS * tm
    slot_token = jnp.zeros((p,), jnp.int32).at[pos].set(jnp.arange(a, dtype=jnp.int32) // MOE_TOPK)
    n_used = (ends[-1] // tm).astype(jnp.int32).reshape(1)
    tile_start = jnp.arange(p // tm, dtype=jnp.int32) * tm
    last_e = jnp.max(jnp.where(counts > 0, jnp.arange(MOE_EXPERTS, dtype=jnp.int32), 0))
    tile_expert = jnp.minimum(
        jnp.sum((ends[None, :] <= tile_start[:, None]).astype(jnp.int32), axis=1), last_e)
    eidx = jnp.arange(MOE_EXPERTS, dtype=jnp.int32)
    used = counts > 0
    later = used[None, :] & (eidx[None, :] > eidx[:, None])
    next_used = jnp.min(jnp.where(later, eidx[None, :], MOE_EXPERTS), axis=1)
    next_used = jnp.where(next_used == MOE_EXPERTS, -1, next_used).astype(jnp.int32)
    ordinal = (jnp.cumsum(used.astype(jnp.int32)) - 1) % 2
    t_onehot = (tile_expert[:, None] == eidx[None, :]).astype(jnp.int32)
    next_expert = jnp.sum(t_onehot * next_used[None, :], axis=1).astype(jnp.int32)
    expert_slot = jnp.sum(t_onehot * ordinal[None, :], axis=1).astype(jnp.int32)
    return (tile_expert, slot_token, n_used, next_expert, expert_slot), w_sel, pos


def _rope_angles(rot_dim):
    t = np.arange(SEQ)
    row = (t // GRID_W).astype(np.float64)
    col = (t % GRID_W).astype(np.float64)
    n_freq = rot_dim // 4
    inv = ROPE_THETA ** (-np.arange(n_freq, dtype=np.float64) / n_freq)
    ang = np.concatenate([row[:, None] * inv, col[:, None] * inv], axis=-1)
    return np.cos(ang), np.sin(ang)


def _rope_tables():
    def rows(lat, ident):
        ctx = np.zeros((TOK_TM, 128)) + ident
        return jnp.asarray(np.concatenate([lat, ctx], axis=0).astype(np.float32))

    z32 = np.zeros((SEQ, 32))
    z64 = np.zeros((SEQ, 64))
    lane = np.arange(128)
    cm, sm = _rope_angles(MLA_ROPE)
    mla_c = rows(np.concatenate([cm, cm, z64], axis=1), (lane < 64).astype(np.float64))
    mla_s = rows(np.concatenate([-sm, sm, z64], axis=1), 0.0)
    cg, sg = _rope_angles(GQA_HD)
    gqa_c = rows(np.concatenate([cg, cg, cg, cg], axis=1), 1.0)
    gqa_s1 = rows(np.concatenate([-sg, z32, -sg, z32], axis=1), 0.0)
    gqa_s2 = rows(np.concatenate([z32, sg, z32, sg], axis=1), 0.0)
    return {'mla_c': mla_c, 'mla_s': mla_s, 'gqa_c': gqa_c, 'gqa_s1': gqa_s1, 'gqa_s2': gqa_s2}


def _rope_block(i):
    per_seq = SEQ // TOK_TM
    return jnp.where(i < BATCH * per_seq, i % per_seq, per_seq)


def _pack_w_in_kernel(w_ref, o_ref):
    bounds = (0,) + IN_SPLITS + (sum(IN_SIZES),)
    off = 0
    for i in _PACK_ORDER:
        size = bounds[i + 1] - bounds[i]
        o_ref[0, off:off + size, :] = w_ref[0, bounds[i]:bounds[i + 1], :].astype(BF16)
        off += size
    o_ref[0, off:, :] = jnp.zeros((_PACK_COLS - off, o_ref.shape[2]), BF16)


def _pack_w_in(w_in):
    tk = 256
    n_in = sum(IN_SIZES)
    wt = jnp.swapaxes(w_in, 1, 2)
    return pl.pallas_call(
        _pack_w_in_kernel,
        grid=(DEPTH, D_MODEL // tk),
        in_specs=[pl.BlockSpec((1, n_in, tk), lambda l, i: (l, 0, i))],
        out_specs=pl.BlockSpec((1, _PACK_COLS, tk), lambda l, i: (l, 0, i)),
        out_shape=jax.ShapeDtypeStruct((DEPTH, _PACK_COLS, D_MODEL), BF16),
        compiler_params=_cparams(("arbitrary", "arbitrary")),
        name="pack_w_in",
    )(wt)


def _pad_w_o_gqa(w):
    w4 = w.reshape(GQA_KV_HEADS, _GQA_G, GQA_HD, D_MODEL)
    z = jnp.zeros_like(w4[0:1])
    halves = [jnp.concatenate([w4[hk:hk + 1] if hk == half else z for hk in range(GQA_KV_HEADS)], axis=0)
              for half in range(GQA_KV_HEADS)]
    return jnp.stack(halves, axis=2).reshape(_GQA_QW, D_MODEL).astype(BF16)


def _token_mixer(h, p, tabs, ctx_out):
    proj = _mm(h, p['w_in_packed'], p['layer'], BF16, _PACK_COLS // 4)
    mq, mk, mvt = _mla_prep(proj, p, tabs)
    oa = _mla_attn(mq, mk, mvt, tq=512, tk=1024)
    gnq = (p['na_qn_g'] * (NA_HD ** -0.5)).reshape(1, NA_HD)
    gnk = p['na_kn_g'].reshape(1, NA_HD)
    ob = _na_attn(proj, _na_bias_table(p['na_rpb']), gnq, gnk)
    qd, kn = _gqa_prep(proj, p, tabs)
    oc = _gqa_attn(qd, kn, proj, p['gqa_sink'])
    ctx = _ctx_attn(p['gqa_sink'], mq, mk, mvt, proj, gnq, gnk, qd, kn) if ctx_out else None
    return _merge(proj, (oa, ob, oc), ctx, p['w_o_mla'].astype(BF16), p['w_o_na'].astype(BF16),
                  _pad_w_o_gqa(p['w_o_gqa']))


def _post_mixer(y, xt, mod, norm_g, w_out_l, p):
    m = y.shape[0]
    wr = jnp.concatenate([p['moe_w_group'], p['moe_w_expert'],
                          jnp.zeros((D_MODEL, ROUTE_COLS - MOE_GROUPS - MOE_EXPERTS), F32)], axis=1)
    br = jnp.concatenate([p['moe_b_group'], p['moe_b_expert'],
                          jnp.zeros((ROUTE_COLS - MOE_GROUPS - MOE_EXPERTS,), F32)]).reshape(1, ROUTE_COLS)
    x_mid, h, logits = _out_route(y, w_out_l.astype(BF16), xt, mod, norm_g, wr, br)
    plan, w_sel, pos = _route(logits, m)
    yy = _moe_ffn(h, plan, p['moe_w_gate'], p['moe_w_up'], p['moe_w_down'], p['layer'])
    return _moe_combine(yy, pos, w_sel, x_mid, mod, 5)


def kernel(x, c, ctx, c_ctx, ada_w, ada_b, norm_mix_g, norm_ffn_g, w_in,
           mla_q_norm_g, mla_w_uq, mla_kv_norm_g, mla_w_ukv, mla_qn_g, mla_kn_g,
           na_qn_g, na_kn_g, na_rpb, gqa_qn_g, gqa_kn_g, gqa_sink,
           w_o_mla, w_o_na, w_o_gqa, w_out,
           moe_w_group, moe_b_group, moe_w_expert, moe_b_expert,
           moe_w_gate, moe_w_up, moe_w_down):
    xt = (x.reshape(N_LAT, D_MODEL), ctx.reshape(N_CTX, D_MODEL))
    c_rows = jnp.concatenate([c, c_ctx[None, :], jnp.zeros((8 - BATCH - 1, D_MODEL), F32)], axis=0)
    mod_all = _ada(c_rows, ada_w, ada_b)
    tabs = _rope_tables()
    w_in_packed = _pack_w_in(w_in)
    for l in range(DEPTH):
        ctx_out = l < DEPTH - 1
        p = {
            'w_in_packed': w_in_packed, 'mla_q_norm_g': mla_q_norm_g[l], 'mla_w_uq': mla_w_uq[l],
            'mla_kv_norm_g': mla_kv_norm_g[l], 'mla_w_ukv': mla_w_ukv[l],
            'mla_qn_g': mla_qn_g[l], 'mla_kn_g': mla_kn_g[l],
            'na_qn_g': na_qn_g[l], 'na_kn_g': na_kn_g[l], 'na_rpb': na_rpb[l],
            'gqa_qn_g': gqa_qn_g[l], 'gqa_kn_g': gqa_kn_g[l], 'gqa_sink': gqa_sink[l],
            'w_o_mla': w_o_mla[l], 'w_o_na': w_o_na[l], 'w_o_gqa': w_o_gqa[l],
            'moe_w_group': moe_w_group[l], 'moe_b_group': moe_b_group[l],
            'moe_w_expert': moe_w_expert[l], 'moe_b_expert': moe_b_expert[l],
            'moe_w_gate': moe_w_gate, 'moe_w_up': moe_w_up, 'moe_w_down': moe_w_down, 'layer': l,
        }
        mod = mod_all[l].reshape(8 * 6, 1, D_MODEL)
        h = _norm_mod(xt, norm_mix_g[l], mod, 0, 1)
        y = _token_mixer(h, p, tabs, ctx_out)
        xt = _post_mixer(y, xt, mod, norm_ffn_g[l], w_out[l], p)
    return xt[:N_LAT].reshape(BATCH, SEQ, D_MODEL)
```

```python
import functools

import numpy as np
import jax
import jax.numpy as jnp
from jax import lax
from jax.experimental import pallas as pl
from jax.experimental.pallas import tpu as pltpu

D_MODEL = 2048
BATCH = 2
SEQ = 4096
DEPTH = 2
GRID_W = 64
CTX_LEN = 256
EPS = 1e-6
ROPE_THETA = 10000.0
NEG_INF = -1e30

MLA_HEADS = 8
MLA_Q_RANK = 512
MLA_KV_RANK = 512
MLA_NOPE = 128
MLA_ROPE = 64
MLA_QK = MLA_NOPE + MLA_ROPE
MLA_V = 128
NA_HEADS = 4
NA_HD = 128
NA_ROWS = 8
NA_COLS = 16
GQA_HEADS = 8
GQA_KV_HEADS = 2
GQA_HD = 64
GQA_WINDOW = 128
MOE_GROUPS = 4
MOE_PER_GROUP = 8
MOE_EXPERTS = MOE_GROUPS * MOE_PER_GROUP
MOE_TOPK = 2
MOE_HIDDEN = 512

IN_SIZES = (MLA_Q_RANK, MLA_KV_RANK, MLA_ROPE,
            NA_HEADS * NA_HD, NA_HEADS * NA_HD, NA_HEADS * NA_HD,
            GQA_HEADS * GQA_HD, GQA_KV_HEADS * GQA_HD, GQA_KV_HEADS * GQA_HD,
            D_MODEL, D_MODEL, D_MODEL)
IN_SPLITS = tuple(int(s) for s in np.cumsum(IN_SIZES)[:-1])

N_LAT = BATCH * SEQ
N_CTX = BATCH * CTX_LEN
N_TOK = N_LAT + N_CTX
ROWS = SEQ // GRID_W

V7X_LANES = 128
V7X_VMEM_LIMIT = 56 * 1024 * 1024

_PACK_ORDER = (9, 10, 11, 0, 1, 3, 4, 5, 6, 7, 8, 2)
_PACK_COLS = 9728
_PACK_OFF = {}
_off = 0
for _i in _PACK_ORDER:
    _PACK_OFF[_i] = _off
    _off += IN_SIZES[_i]

ROUTE_COLS = V7X_LANES
MOE_TM = 256
TOK_TM = 256
MM_TM = 512
CTX_BLK = N_LAT // CTX_LEN

F32 = jnp.float32
BF16 = jnp.bfloat16
_LOG2E = 1.4426950408889634
_NT = (((1,), (1,)), ((), ()))


def _cparams(sem):
    return pltpu.CompilerParams(dimension_semantics=sem, vmem_limit_bytes=V7X_VMEM_LIMIT)


def _seg(i, tm):
    return jnp.minimum(i // (SEQ // tm), 2)


def _pcol(idx, width):
    assert _PACK_OFF[idx] % width == 0
    return _PACK_OFF[idx] // width


def _ada_kernel(ct_ref, w_ref, b_ref, o_ref):
    ct = ct_ref[...]
    act = ct * jax.nn.sigmoid(ct)
    w = w_ref[0]
    rows = [jnp.sum(act[:, r:r + 1] * w, axis=0, keepdims=True) for r in range(BATCH + 1)]
    rows.append(jnp.zeros((8 - len(rows), w.shape[1]), F32))
    o_ref[0] = jnp.concatenate(rows, axis=0) + b_ref[0]


def _ada(c_rows, ada_w, ada_b):
    tn = 1024
    n = 6 * D_MODEL
    return pl.pallas_call(
        _ada_kernel,
        grid=(DEPTH, n // tn),
        in_specs=[
            pl.BlockSpec((D_MODEL, 8), lambda l, j: (0, 0)),
            pl.BlockSpec((1, D_MODEL, tn), lambda l, j: (l, 0, j)),
            pl.BlockSpec((1, 1, tn), lambda l, j: (l, 0, j)),
        ],
        out_specs=pl.BlockSpec((1, 8, tn), lambda l, j: (l, 0, j)),
        out_shape=jax.ShapeDtypeStruct((DEPTH, 8, n), F32),
        compiler_params=_cparams(("arbitrary", "arbitrary")),
        name="ada",
    )(c_rows.T, ada_w, ada_b.reshape(DEPTH, 1, n))


def _stream_specs(xs):
    tm = TOK_TM
    if not isinstance(xs, tuple):
        return [pl.BlockSpec((tm, D_MODEL), lambda i: (i, 0))], [xs], xs.shape[0]
    nl = N_LAT // tm
    return ([pl.BlockSpec((tm, D_MODEL), lambda i: (jnp.minimum(i, nl - 1), 0)),
             pl.BlockSpec((tm, D_MODEL), lambda i: (jnp.maximum(i - nl, 0), 0))],
            list(xs), xs[0].shape[0] + xs[1].shape[0])


def _stream_tile(refs):
    if len(refs) == 1:
        return refs[0][...]
    return jnp.where(pl.program_id(0) >= N_LAT // TOK_TM, refs[1][...], refs[0][...])


def _norm_mod_kernel(*refs, n_src):
    g_ref, sh_ref, sc_ref, h_ref = refs[n_src:]
    x = _stream_tile(refs[:n_src])
    xn = x * lax.rsqrt(jnp.mean(x * x, axis=-1, keepdims=True) + EPS) * g_ref[...]
    h_ref[...] = (xn * (1.0 + sc_ref[0]) + sh_ref[0]).astype(h_ref.dtype)


def _norm_mod(xs, g, mod, k_shift, k_scale):
    tm = TOK_TM
    specs, args, m = _stream_specs(xs)
    return pl.pallas_call(
        functools.partial(_norm_mod_kernel, n_src=len(args)),
        grid=(m // tm,),
        in_specs=specs + [
            pl.BlockSpec((1, D_MODEL), lambda i: (0, 0)),
            pl.BlockSpec((1, 1, D_MODEL), lambda i: (_seg(i, tm) * 6 + k_shift, 0, 0)),
            pl.BlockSpec((1, 1, D_MODEL), lambda i: (_seg(i, tm) * 6 + k_scale, 0, 0))],
        out_specs=pl.BlockSpec((tm, D_MODEL), lambda i: (i, 0)),
        out_shape=jax.ShapeDtypeStruct((m, D_MODEL), BF16),
        compiler_params=_cparams(("arbitrary",)),
        name="norm_mod",
    )(*args, g.reshape(1, D_MODEL), mod, mod)


def _out_route_kernel(*refs, n_src):
    y_ref, w_ref = refs[:2]
    g1_ref, gn_ref, sh_ref, sc_ref, wr_ref, br_ref, xo_ref, h_ref, lg_ref = refs[2 + n_src:]
    acc = jnp.dot(y_ref[...], w_ref[...], preferred_element_type=F32)
    x = _stream_tile(refs[2:2 + n_src]) + g1_ref[0] * acc
    xo_ref[...] = x
    xn = x * lax.rsqrt(jnp.mean(x * x, axis=-1, keepdims=True) + EPS) * gn_ref[...]
    h = xn * (1.0 + sc_ref[0]) + sh_ref[0]
    h_ref[...] = h
    hi = h.astype(BF16)
    lo = (h - hi.astype(F32)).astype(BF16)
    wr = wr_ref[...]
    w_hi = wr.astype(BF16)
    w_lo = (wr - w_hi.astype(F32)).astype(BF16)
    lg_ref[...] = (jnp.dot(hi, w_hi, preferred_element_type=F32)
                   + jnp.dot(hi, w_lo, preferred_element_type=F32)
                   + jnp.dot(lo, w_hi, preferred_element_type=F32) + br_ref[...])


def _out_route(y, w_out_bf16, res, mod, norm_g, wr, br):
    m = y.shape[0]
    tm = TOK_TM

    def modrow(k):
        return pl.BlockSpec((1, 1, D_MODEL), lambda i: (_seg(i, tm) * 6 + k, 0, 0))

    row = pl.BlockSpec((tm, D_MODEL), lambda i: (i, 0))
    res_specs, res_args, _ = _stream_specs(res)
    return pl.pallas_call(
        functools.partial(_out_route_kernel, n_src=len(res_args)),
        grid=(m // tm,),
        in_specs=[row,
                  pl.BlockSpec((D_MODEL, D_MODEL), lambda i: (0, 0))] + res_specs + [
                  modrow(2),
                  pl.BlockSpec((1, D_MODEL), lambda i: (0, 0)),
                  modrow(3), modrow(4),
                  pl.BlockSpec((D_MODEL, ROUTE_COLS), lambda i: (0, 0)),
                  pl.BlockSpec((1, ROUTE_COLS), lambda i: (0, 0))],
        out_specs=[row, row, pl.BlockSpec((tm, ROUTE_COLS), lambda i: (i, 0))],
        out_shape=[jax.ShapeDtypeStruct((m, D_MODEL), F32),
                   jax.ShapeDtypeStruct((m, D_MODEL), F32),
                   jax.ShapeDtypeStruct((m, ROUTE_COLS), F32)],
        compiler_params=_cparams(("arbitrary",)),
        name="out_route",
    )(y, w_out_bf16, *res_args, mod, norm_g.reshape(1, D_MODEL), mod, mod, wr, br)


def _mm_bf16_kernel(x_ref, w_ref, o_ref):
    o_ref[...] = lax.dot_general(x_ref[...], w_ref[0], _NT,
                                 preferred_element_type=F32).astype(o_ref.dtype)


def _mm(x, wt, layer, out_dtype, tn):
    m, k = x.shape
    n = wt.shape[1]
    tm = MM_TM
    assert x.dtype == BF16 and wt.dtype == BF16
    w = wt
    return pl.pallas_call(
        _mm_bf16_kernel,
        grid=(n // tn, m // tm),
        in_specs=[pl.BlockSpec((tm, k), lambda j, i: (i, 0)),
                  pl.BlockSpec((1, tn, k), lambda j, i: (layer, j, 0))],
        out_specs=pl.BlockSpec((tm, tn), lambda j, i: (i, j)),
        out_shape=jax.ShapeDtypeStruct((m, n), out_dtype),
        compiler_params=_cparams(("arbitrary", "arbitrary")),
        name="mm_bf16",
    )(x, w)


def _row_rms(x, g):
    return x * lax.rsqrt(jnp.mean(x * x, axis=-1, keepdims=True) + EPS) * g


_MLA_PAD = 2 * V7X_LANES


def _mla_prep_kernel(cq_ref, ckv_ref, kr_ref, wq_ref, wk_ref, wvt_ref, gqi_ref, gkvi_ref,
                     gq_ref, gkn_ref, gkr_ref, c_ref, s_ref, q_ref, k_ref, vt_ref):
    c = c_ref[...]
    s = s_ref[...]
    hw = MLA_HEADS * MLA_NOPE

    def rot(t):
        return t * c + (pltpu.roll(t, 32, 1) + pltpu.roll(t, 96, 1)) * s

    cqn = _row_rms(cq_ref[...].astype(F32), gqi_ref[...]).astype(BF16)
    qf = jnp.dot(cqn, wq_ref[...], preferred_element_type=F32)
    gq = gq_ref[...]
    inv = 1.0 / MLA_QK
    for h in range(MLA_HEADS):
        nope = qf[:, h * 128:(h + 1) * 128]
        t = qf[:, hw + h * 128:hw + (h + 1) * 128]
        ss = jnp.sum(nope * nope, axis=-1, keepdims=True) + jnp.sum(t * t, axis=-1, keepdims=True)
        r = lax.rsqrt(ss * inv + EPS)
        q_ref[h, :, 0:128] = (nope * r * gq[:, 0:128]).astype(BF16)
        q_ref[h, :, 128:256] = rot(t * r * gq[:, 128:256]).astype(BF16)

    ckvn = _row_rms(ckv_ref[...].astype(F32), gkvi_ref[...]).astype(BF16)
    kvf = jnp.dot(ckvn, wk_ref[...], preferred_element_type=F32)
    kr = kr_ref[...].astype(F32)
    ssr = jnp.sum(kr * kr, axis=-1, keepdims=True)
    yrot = rot(kr * gkr_ref[...])
    gkn = gkn_ref[...]
    for h in range(MLA_HEADS):
        nope = kvf[:, h * 128:(h + 1) * 128]
        r = lax.rsqrt((jnp.sum(nope * nope, axis=-1, keepdims=True) + ssr) * inv + EPS)
        k_ref[h, :, 0:128] = (nope * r * gkn).astype(BF16)
        k_ref[h, :, 128:256] = (yrot * r).astype(BF16)
        vt_ref[h] = lax.dot_general(wvt_ref[h * MLA_V:(h + 1) * MLA_V, :], ckvn, _NT,
                                    preferred_element_type=F32).astype(BF16)


def _mla_prep(proj, p, tabs):
    tm = TOK_TM
    hw = MLA_HEADS * MLA_NOPE
    wq = p['mla_w_uq'].reshape(MLA_Q_RANK, MLA_HEADS, MLA_QK)
    wq_rope = jnp.pad(wq[:, :, MLA_NOPE:], ((0, 0), (0, 0), (0, 128 - MLA_ROPE)))
    wq = jnp.concatenate([wq[:, :, :MLA_NOPE].reshape(MLA_Q_RANK, hw),
                          wq_rope.reshape(MLA_Q_RANK, hw)], axis=1).astype(BF16)
    wkv = p['mla_w_ukv'].reshape(MLA_KV_RANK, MLA_HEADS, MLA_NOPE + MLA_V)
    wk = wkv[:, :, :MLA_NOPE].reshape(MLA_KV_RANK, hw).astype(BF16)
    wvt = jnp.transpose(wkv[:, :, MLA_NOPE:], (1, 2, 0)).reshape(MLA_HEADS * MLA_V, MLA_KV_RANK).astype(BF16)
    zpad = jnp.zeros((128 - MLA_ROPE,), F32)
    gq = (jnp.concatenate([p['mla_qn_g'], zpad]) * (MLA_QK ** -0.5 * _LOG2E)).reshape(1, _MLA_PAD)
    gkn = p['mla_kn_g'][:MLA_NOPE].reshape(1, 128)
    gkr = jnp.concatenate([p['mla_kn_g'][MLA_NOPE:], zpad]).reshape(1, 128)

    def const(shape):
        return pl.BlockSpec(shape, lambda i: (0,) * len(shape))

    return pl.pallas_call(
        _mla_prep_kernel,
        grid=(N_TOK // tm,),
        in_specs=[pl.BlockSpec((tm, MLA_Q_RANK), lambda i: (i, _pcol(0, MLA_Q_RANK))),
                  pl.BlockSpec((tm, MLA_KV_RANK), lambda i: (i, _pcol(1, MLA_KV_RANK))),
                  pl.BlockSpec((tm, 128), lambda i: (i, _pcol(2, 128))),
                  const((MLA_Q_RANK, 2 * hw)), const((MLA_KV_RANK, hw)),
                  const((MLA_HEADS * MLA_V, MLA_KV_RANK)),
                  const((1, MLA_Q_RANK)), const((1, MLA_KV_RANK)),
                  const((1, _MLA_PAD)), const((1, 128)), const((1, 128)),
                  pl.BlockSpec((tm, 128), lambda i: (_rope_block(i), 0)),
                  pl.BlockSpec((tm, 128), lambda i: (_rope_block(i), 0))],
        out_specs=[pl.BlockSpec((MLA_HEADS, tm, _MLA_PAD), lambda i: (0, i, 0)),
                   pl.BlockSpec((MLA_HEADS, tm, _MLA_PAD), lambda i: (0, i, 0)),
                   pl.BlockSpec((MLA_HEADS, MLA_V, tm), lambda i: (0, 0, i))],
        out_shape=[jax.ShapeDtypeStruct((MLA_HEADS, N_TOK, _MLA_PAD), BF16),
                   jax.ShapeDtypeStruct((MLA_HEADS, N_TOK, _MLA_PAD), BF16),
                   jax.ShapeDtypeStruct((MLA_HEADS, MLA_V, N_TOK), BF16)],
        compiler_params=_cparams(("arbitrary",)),
        name="mla_prep",
    )(proj, proj, proj, wq, wk, wvt, p['mla_q_norm_g'].reshape(1, -1), p['mla_kv_norm_g'].reshape(1, -1),
      gq, gkn, gkr, tabs['mla_c'], tabs['mla_s'])


def _mla_kernel(q_ref, k1_ref, vt1_ref, k2_ref, vt2_ref, o_ref, sa_ref, sb_ref, sc_ref, acc_ref,
                *, tk, n_chunks):
    q = q_ref[0]
    tq = q.shape[0]
    hq = tq // 2
    qs = (q[:hq], q[hq:])

    def scores(dst_ref, kc):
        out = []
        for j in range(2):
            st = lax.dot_general(kc, qs[j], _NT, preferred_element_type=F32)
            dst_ref[j] = st
            out.append(jnp.max(st, axis=0, keepdims=True))
        return out

    def k_chunk(c):
        return k1_ref[0, pl.ds(pl.multiple_of(c * tk, tk), tk), :]

    def vt_chunk(c):
        return vt1_ref[0, :, pl.ds(pl.multiple_of(c * tk, tk), tk)]

    def accumulate(s_ref, smax, vtc, m, l):
        m_out, l_out = [], []
        for j in range(2):
            m_new = jnp.maximum(m[j], smax[j])
            a = jnp.exp2(m[j] - m_new)
            p = jnp.exp2(s_ref[j] - m_new)
            l_out.append(a * l[j] + jnp.sum(p, axis=0, keepdims=True))
            acc_ref[j] = a * acc_ref[j] + jnp.dot(vtc, p.astype(BF16), preferred_element_type=F32)
            m_out.append(m_new)
        return m_out, l_out

    m = [jnp.full((1, hq), NEG_INF, F32)] * 2
    l = [jnp.zeros((1, hq), F32)] * 2
    acc_ref[...] = jnp.zeros_like(acc_ref)
    mx_c = scores(sc_ref, k2_ref[0])
    mx_a = scores(sa_ref, k_chunk(0))
    m, l = accumulate(sc_ref, mx_c, vt2_ref[0], m, l)

    for i in range(n_chunks // 2 - 1):
        mx_b = scores(sb_ref, k_chunk(2 * i + 1))
        m, l = accumulate(sa_ref, mx_a, vt_chunk(2 * i), m, l)
        mx_a = scores(sa_ref, k_chunk(2 * i + 2))
        m, l = accumulate(sb_ref, mx_b, vt_chunk(2 * i + 1), m, l)
    mx_b = scores(sb_ref, k_chunk(n_chunks - 1))
    m, l = accumulate(sa_ref, mx_a, vt_chunk(n_chunks - 2), m, l)
    m, l = accumulate(sb_ref, mx_b, vt_chunk(n_chunks - 1), m, l)
    for j in range(2):
        o_ref[j * hq:(j + 1) * hq, :] = (acc_ref[j] / l[j]).T.astype(o_ref.dtype)


def _mla_attn(q, k, vt, *, tq, tk):
    nq = SEQ // tq
    return pl.pallas_call(
        functools.partial(_mla_kernel, tk=tk, n_chunks=SEQ // tk),
        grid=(BATCH, MLA_HEADS, nq),
        in_specs=[pl.BlockSpec((1, tq, _MLA_PAD), lambda b, h, i: (h, b * nq + i, 0)),
                  pl.BlockSpec((1, SEQ, _MLA_PAD), lambda b, h, i: (h, b, 0)),
                  pl.BlockSpec((1, MLA_V, SEQ), lambda b, h, i: (h, 0, b)),
                  pl.BlockSpec((1, CTX_LEN, _MLA_PAD), lambda b, h, i: (h, CTX_BLK + b, 0)),
                  pl.BlockSpec((1, MLA_V, CTX_LEN), lambda b, h, i: (h, 0, CTX_BLK + b))],
        out_specs=pl.BlockSpec((tq, MLA_V), lambda b, h, i: (b * nq + i, h)),
        out_shape=jax.ShapeDtypeStruct((N_LAT, MLA_HEADS * MLA_V), BF16),
        scratch_shapes=[pltpu.VMEM((2, tk, tq // 2), F32), pltpu.VMEM((2, tk, tq // 2), F32),
                        pltpu.VMEM((2, CTX_LEN, tq // 2), F32), pltpu.VMEM((2, MLA_V, tq // 2), F32)],
        compiler_params=_cparams(("arbitrary", "arbitrary", "arbitrary")),
        name="mla_attn",
    )(q, k, vt, k, vt)


NA_RB = 8
NA_WIN_ROWS = 16
NA_QCHUNKS = 2
_NA_Q = NA_RB * GRID_W
_NA_WIN = NA_WIN_ROWS * GRID_W
assert NA_WIN_ROWS >= NA_RB + NA_ROWS - 1 and ROWS % NA_RB == 0


def _na_win_start(r0):
    return np.clip(r0 - NA_ROWS // 2, 0, ROWS - NA_WIN_ROWS)


def _na_small_pattern(r):
    half = NA_ROWS // 2
    return r if r < half else (half if r <= ROWS - half else r - (ROWS - NA_ROWS))


def _na_step_layout():
    out = []
    for r0 in (0, NA_RB, ROWS - NA_RB):
        u = int(_na_win_start(r0))
        rows = []
        for j in range(NA_RB):
            start = int(np.clip(r0 + j - NA_ROWS // 2, 0, ROWS - NA_ROWS))
            rows.append((start - u, _na_small_pattern(r0 + j)))
        out.append(rows)
    return out


def _na_kernel(q_ref, k_ref, v_ref, kc_ref, vc_ref, bias_ref, gq_ref, gk_ref, o_ref,
               kn_ref, kcn_ref, big_ref):
    rb = pl.program_id(2)
    nrb = pl.num_programs(2)

    @pl.when(rb == 0)
    def _():
        kn_ref[...] = _row_rms(k_ref[...].astype(F32), gk_ref[...]).astype(BF16)
        kcn_ref[...] = _row_rms(kc_ref[...].astype(F32), gk_ref[...]).astype(BF16)
        big_ref[...] = jnp.full(big_ref.shape, NEG_INF, F32)
        for p, rows in enumerate(_na_step_layout()):
            for j, (off, pat) in enumerate(rows):
                big_ref[p, j * GRID_W:(j + 1) * GRID_W, off * GRID_W:(off + NA_ROWS) * GRID_W] = bias_ref[0, pat]

    pattern = jnp.where(rb == 0, 0, jnp.where(rb == nrb - 1, 2, 1))
    q = _row_rms(q_ref[...].astype(F32), gq_ref[...]).astype(BF16)
    u = jnp.clip(rb * NA_RB - NA_ROWS // 2, 0, ROWS - NA_WIN_ROWS) * GRID_W
    u = pl.multiple_of(u, GRID_W)
    k = kn_ref[pl.ds(u, _NA_WIN), :]
    v = v_ref[pl.ds(u, _NA_WIN), :]
    kc = kcn_ref[...]
    vc = vc_ref[...]
    nq = _NA_Q // NA_QCHUNKS
    for j in range(NA_QCHUNKS):
        rows = slice(j * nq, (j + 1) * nq)
        qj = q[rows]
        s = lax.dot_general(qj, k, _NT, preferred_element_type=F32) + big_ref[pattern, rows, :]
        sc = lax.dot_general(qj, kc, _NT, preferred_element_type=F32)
        m = jnp.maximum(jnp.max(s, axis=-1, keepdims=True), jnp.max(sc, axis=-1, keepdims=True))
        p = jnp.exp(s - m)
        pc = jnp.exp(sc - m)
        l = jnp.sum(p, axis=-1, keepdims=True) + jnp.sum(pc, axis=-1, keepdims=True)
        o = (jnp.dot(p.astype(BF16), v, preferred_element_type=F32)
             + jnp.dot(pc.astype(BF16), vc, preferred_element_type=F32))
        o_ref[rows, :] = (o / l).astype(o_ref.dtype)


def _na_attn(proj, bias, gq, gk):
    nrb = ROWS // NA_RB
    cq, ck, cv = _pcol(3, NA_HD), _pcol(4, NA_HD), _pcol(5, NA_HD)
    return pl.pallas_call(
        _na_kernel,
        grid=(BATCH, NA_HEADS, nrb),
        in_specs=[pl.BlockSpec((_NA_Q, NA_HD), lambda b, h, r: (b * nrb + r, cq + h)),
                  pl.BlockSpec((SEQ, NA_HD), lambda b, h, r: (b, ck + h)),
                  pl.BlockSpec((SEQ, NA_HD), lambda b, h, r: (b, cv + h)),
                  pl.BlockSpec((CTX_LEN, NA_HD), lambda b, h, r: (CTX_BLK + b, ck + h)),
                  pl.BlockSpec((CTX_LEN, NA_HD), lambda b, h, r: (CTX_BLK + b, cv + h)),
                  pl.BlockSpec((1, NA_ROWS, GRID_W, NA_ROWS * GRID_W), lambda b, h, r: (h, 0, 0, 0)),
                  pl.BlockSpec((1, NA_HD), lambda b, h, r: (0, 0)),
                  pl.BlockSpec((1, NA_HD), lambda b, h, r: (0, 0))],
        out_specs=pl.BlockSpec((_NA_Q, NA_HD), lambda b, h, r: (b * nrb + r, h)),
        out_shape=jax.ShapeDtypeStruct((N_LAT, NA_HEADS * NA_HD), BF16),
        scratch_shapes=[pltpu.VMEM((SEQ, NA_HD), BF16), pltpu.VMEM((CTX_LEN, NA_HD), BF16),
                        pltpu.VMEM((3, _NA_Q, _NA_WIN), F32)],
        compiler_params=_cparams(("arbitrary", "arbitrary", "arbitrary")),
        name="na_attn",
    )(proj, proj, proj, proj, proj, bias, gq, gk)


def _na_bias_table(rpb):
    half = NA_ROWS // 2
    r_rep = np.array(list(range(half)) + [half] + list(range(ROWS - half + 1, ROWS)))
    assert all(_na_small_pattern(int(r)) == i for i, r in enumerate(r_rep))
    start = np.clip(r_rep - half, 0, ROWS - NA_ROWS)
    dr = start[:, None] + np.arange(NA_ROWS)[None, :] - r_rep[:, None] + NA_ROWS - 1
    qc = np.arange(GRID_W)
    kcol = np.arange(GRID_W)
    col_start = np.clip(qc - NA_COLS // 2, 0, GRID_W - NA_COLS)
    in_win = (kcol[None, :] >= col_start[:, None]) & (kcol[None, :] < col_start[:, None] + NA_COLS)
    dc = np.clip(kcol[None, :] - qc[:, None], 1 - NA_COLS, NA_COLS - 1) + NA_COLS - 1
    rsel = (dr[:, :, None] == np.arange(2 * NA_ROWS - 1)).astype(np.float32)
    csel = (dc[:, :, None] == np.arange(2 * NA_COLS - 1)).astype(np.float32)
    b = jnp.einsum('pja,hab,qkb->hpqjk', rsel, rpb.astype(F32), csel,
                   precision=lax.Precision.HIGHEST)
    b = jnp.where(in_win[None, None, :, None, :], b.astype(F32), NEG_INF)
    return b.reshape(NA_HEADS, NA_ROWS, GRID_W, NA_ROWS * GRID_W)


_GQA_G = GQA_HEADS // GQA_KV_HEADS
_GQA_BAND = 3 * GQA_WINDOW
_GQA_QW = GQA_HEADS * V7X_LANES


def _gqa_prep_kernel(q_ref, k_ref, gq_ref, gk_ref, c_ref, s1_ref, s2_ref, qd_ref, kn_ref):
    c = c_ref[...]
    s1 = s1_ref[...]
    s2 = s2_ref[...]
    lo = lax.broadcasted_iota(jnp.int32, (1, 128), 1) < GQA_HD

    def head_rms(x, g):
        x2 = x * x
        s_lo = jnp.sum(jnp.where(lo, x2, 0.0), axis=-1, keepdims=True)
        s_hi = jnp.sum(jnp.where(lo, 0.0, x2), axis=-1, keepdims=True)
        inv = 1.0 / GQA_HD
        r = jnp.where(lo, lax.rsqrt(s_lo * inv + EPS), lax.rsqrt(s_hi * inv + EPS))
        return x * r * g

    def rot(x):
        return x * c + pltpu.roll(x, 96, 1) * s1 + pltpu.roll(x, 32, 1) * s2

    gq = gq_ref[...]
    for j in range(GQA_HEADS // 2):
        y = rot(head_rms(q_ref[:, j * 128:(j + 1) * 128].astype(F32), gq))
        sw = pltpu.roll(y, 64, 1)
        hk = (2 * j) // _GQA_G
        if hk == 0:
            even, odd = jnp.where(lo, y, 0.0), jnp.where(lo, sw, 0.0)
        else:
            even, odd = jnp.where(lo, 0.0, sw), jnp.where(lo, 0.0, y)
        qd_ref[:, (2 * j) * 128:(2 * j + 1) * 128] = even.astype(BF16)
        qd_ref[:, (2 * j + 1) * 128:(2 * j + 2) * 128] = odd.astype(BF16)
    kn_ref[...] = rot(head_rms(k_ref[...].astype(F32), gk_ref[...])).astype(BF16)


def _gqa_prep(proj, p, tabs):
    tm = TOK_TM
    gq = (jnp.tile(p['gqa_qn_g'], 2) * (GQA_HD ** -0.5)).reshape(1, 128)
    gk = jnp.tile(p['gqa_kn_g'], 2).reshape(1, 128)
    row = pl.BlockSpec((tm, 128), lambda i: (i, 0))
    tab = pl.BlockSpec((tm, 128), lambda i: (_rope_block(i), 0))
    vec = pl.BlockSpec((1, 128), lambda i: (0, 0))
    return pl.pallas_call(
        _gqa_prep_kernel,
        grid=(N_TOK // tm,),
        in_specs=[pl.BlockSpec((tm, GQA_HEADS * GQA_HD), lambda i: (i, _pcol(6, GQA_HEADS * GQA_HD))),
                  pl.BlockSpec((tm, 128), lambda i: (i, _pcol(7, 128))),
                  vec, vec, tab, tab, tab],
        out_specs=[pl.BlockSpec((tm, _GQA_QW), lambda i: (i, 0)), row],
        out_shape=[jax.ShapeDtypeStruct((N_TOK, _GQA_QW), BF16),
                   jax.ShapeDtypeStruct((N_TOK, 128), BF16)],
        compiler_params=_cparams(("arbitrary",)),
        name="gqa_prep",
    )(proj, proj, gq, gk, tabs['gqa_c'], tabs['gqa_s1'], tabs['gqa_s2'])


def _gqa_kernel(sink_ref, q_ref, k_ref, v_ref, kc_ref, vc_ref, o_ref):
    hk = pl.program_id(1)
    n = pl.program_id(2)
    w = GQA_WINDOW
    start = pl.multiple_of(jnp.clip((n - 1) * w, 0, SEQ - _GQA_BAND), w)
    q = jnp.concatenate([q_ref[:, g * 128:(g + 1) * 128] for g in range(_GQA_G)], axis=0)
    k = k_ref[pl.ds(start, _GQA_BAND), :]
    v = v_ref[pl.ds(start, _GQA_BAND), :]
    s = lax.dot_general(q, k, _NT, preferred_element_type=F32)
    rows = lax.broadcasted_iota(jnp.int32, s.shape, 0)
    cols = lax.broadcasted_iota(jnp.int32, s.shape, 1)
    qpos = n * w + (rows & (w - 1))
    kpos = start + cols
    s = jnp.where(jnp.abs(kpos - qpos) <= GQA_WINDOW, s, NEG_INF)
    sc = lax.dot_general(q, kc_ref[...], _NT, preferred_element_type=F32)
    grow = lax.broadcasted_iota(jnp.int32, (_GQA_G * w, 1), 0) // w
    snk = jnp.full((_GQA_G * w, 1), sink_ref[hk * _GQA_G], F32)
    for g in range(1, _GQA_G):
        snk = jnp.where(grow == g, sink_ref[hk * _GQA_G + g], snk)
    m = jnp.maximum(jnp.maximum(jnp.max(s, axis=-1, keepdims=True),
                                jnp.max(sc, axis=-1, keepdims=True)), snk)
    p = jnp.exp(s - m)
    pc = jnp.exp(sc - m)
    l = jnp.sum(p, axis=-1, keepdims=True) + jnp.sum(pc, axis=-1, keepdims=True) + jnp.exp(snk - m)
    o = (jnp.dot(p.astype(BF16), v, preferred_element_type=F32)
         + jnp.dot(pc.astype(BF16), vc_ref[...], preferred_element_type=F32)) / l
    half = lax.broadcasted_iota(jnp.int32, (1, 128), 1) // GQA_HD
    o = jnp.where(half == hk, o, 0.0).astype(o_ref.dtype)
    for g in range(_GQA_G):
        o_ref[:, g * 128:(g + 1) * 128] = o[g * w:(g + 1) * w]


def _gqa_attn(qd, kn, proj, sink):
    nb = SEQ // GQA_WINDOW
    qw = _GQA_G * 128
    cv = _pcol(8, 128)
    gs = pltpu.PrefetchScalarGridSpec(
        num_scalar_prefetch=1,
        grid=(BATCH, GQA_KV_HEADS, nb),
        in_specs=[pl.BlockSpec((GQA_WINDOW, qw), lambda b, h, n, *_: (b * nb + n, h)),
                  pl.BlockSpec((SEQ, 128), lambda b, h, n, *_: (b, 0)),
                  pl.BlockSpec((SEQ, 128), lambda b, h, n, *_: (b, cv)),
                  pl.BlockSpec((CTX_LEN, 128), lambda b, h, n, *_: (CTX_BLK + b, 0)),
                  pl.BlockSpec((CTX_LEN, 128), lambda b, h, n, *_: (CTX_BLK + b, cv))],
        out_specs=pl.BlockSpec((GQA_WINDOW, qw), lambda b, h, n, *_: (b * nb + n, h)),
    )
    return pl.pallas_call(
        _gqa_kernel,
        grid_spec=gs,
        out_shape=jax.ShapeDtypeStruct((N_LAT, _GQA_QW), BF16),
        compiler_params=_cparams(("arbitrary", "arbitrary", "arbitrary")),
        name="gqa_attn",
    )(sink.astype(F32), qd, kn, proj, kn, proj)


def _ctx_kernel(sink_ref, mq_ref, mk_ref, mvt_ref, nq_ref, nk_ref, nv_ref, gnq_ref, gnk_ref,
                gq_ref, gk_ref, gv_ref, oa_ref, ob_ref, oc_ref):
    for h in range(MLA_HEADS):
        st = lax.dot_general(mk_ref[h], mq_ref[h], _NT, preferred_element_type=F32)
        p = jnp.exp2(st - jnp.max(st, axis=0, keepdims=True))
        l = jnp.sum(p, axis=0, keepdims=True)
        ot = jnp.dot(mvt_ref[h], p.astype(BF16), preferred_element_type=F32) / l
        oa_ref[:, h * MLA_V:(h + 1) * MLA_V] = ot.T.astype(oa_ref.dtype)
    for h in range(NA_HEADS):
        sl = slice(h * NA_HD, (h + 1) * NA_HD)
        q = _row_rms(nq_ref[:, sl].astype(F32), gnq_ref[...]).astype(BF16)
        k = _row_rms(nk_ref[:, sl].astype(F32), gnk_ref[...]).astype(BF16)
        s = lax.dot_general(q, k, _NT, preferred_element_type=F32)
        p = jnp.exp(s - jnp.max(s, axis=-1, keepdims=True))
        l = jnp.sum(p, axis=-1, keepdims=True)
        o = jnp.dot(p.astype(BF16), nv_ref[:, sl], preferred_element_type=F32) / l
        ob_ref[:, sl] = o.astype(ob_ref.dtype)
    half = lax.broadcasted_iota(jnp.int32, (1, 128), 1) // GQA_HD
    k = gk_ref[...]
    v = gv_ref[...]
    for h in range(GQA_HEADS):
        sl = slice(h * 128, (h + 1) * 128)
        s = lax.dot_general(gq_ref[:, sl], k, _NT, preferred_element_type=F32)
        snk = sink_ref[h]
        m = jnp.maximum(jnp.max(s, axis=-1, keepdims=True), snk)
        p = jnp.exp(s - m)
        l = jnp.sum(p, axis=-1, keepdims=True) + jnp.exp(snk - m)
        o = jnp.dot(p.astype(BF16), v, preferred_element_type=F32) / l
        oc_ref[:, sl] = jnp.where(half == h // _GQA_G, o, 0.0).astype(oc_ref.dtype)


def _ctx_attn(sink, mq, mk, mvt, proj, gnq, gnk, qd, kn):
    c = CTX_LEN
    nwid = NA_HEADS * NA_HD

    def row(width, col=0):
        return pl.BlockSpec((c, width), lambda b, *_: (CTX_BLK + b, col))

    def out(width):
        return pl.BlockSpec((c, width), lambda b, *_: (b, 0))

    vec = pl.BlockSpec((1, NA_HD), lambda b, *_: (0, 0))
    gs = pltpu.PrefetchScalarGridSpec(
        num_scalar_prefetch=1,
        grid=(BATCH,),
        in_specs=[pl.BlockSpec((MLA_HEADS, c, _MLA_PAD), lambda b, *_: (0, CTX_BLK + b, 0)),
                  pl.BlockSpec((MLA_HEADS, c, _MLA_PAD), lambda b, *_: (0, CTX_BLK + b, 0)),
                  pl.BlockSpec((MLA_HEADS, MLA_V, c), lambda b, *_: (0, 0, CTX_BLK + b)),
                  row(nwid, _pcol(3, nwid)), row(nwid, _pcol(4, nwid)), row(nwid, _pcol(5, nwid)),
                  vec, vec,
                  row(_GQA_QW), row(128), row(128, _pcol(8, 128))],
        out_specs=[out(MLA_HEADS * MLA_V), out(nwid), out(_GQA_QW)],
    )
    return pl.pallas_call(
        _ctx_kernel,
        grid_spec=gs,
        out_shape=[jax.ShapeDtypeStruct((N_CTX, MLA_HEADS * MLA_V), BF16),
                   jax.ShapeDtypeStruct((N_CTX, nwid), BF16),
                   jax.ShapeDtypeStruct((N_CTX, _GQA_QW), BF16)],
        compiler_params=_cparams(("arbitrary",)),
        name="ctx_attn",
    )(sink.astype(F32), mq, mk, mvt, proj, proj, proj, gnq, gnk, qd, kn, proj)


def _merge_kernel(*refs, has_ctx, n_lat_tiles):
    ga_ref, gb_ref, gc_ref, oa_ref, ob_ref, oc_ref = refs[:6]
    refs = refs[6:]
    if has_ctx:
        ca_ref, cb_ref, cc_ref = refs[:3]
        refs = refs[3:]
    wa_ref, wb_ref, wc_ref, y_ref = refs
    is_ctx = pl.program_id(0) >= n_lat_tiles

    def branch(g_ref, o_ref, c_ref, w_ref):
        o = o_ref[...]
        if has_ctx:
            o = jnp.where(is_ctx, c_ref[...], o)
        return jax.nn.sigmoid(g_ref[...].astype(F32)) * jnp.dot(o, w_ref[...], preferred_element_type=F32)

    y = (branch(ga_ref, oa_ref, ca_ref if has_ctx else None, wa_ref)
         + branch(gb_ref, ob_ref, cb_ref if has_ctx else None, wb_ref)
         + branch(gc_ref, oc_ref, cc_ref if has_ctx else None, wc_ref))
    y_ref[...] = y.astype(y_ref.dtype)


def _merge(proj, lat, ctx, wa, wb, wc):
    tm = TOK_TM
    has_ctx = ctx is not None
    m = N_TOK if has_ctx else N_LAT
    nl = N_LAT // tm
    ks = [o.shape[1] for o in lat]
    in_specs = [pl.BlockSpec((tm, D_MODEL), lambda i: (i, 0)),
                pl.BlockSpec((tm, D_MODEL), lambda i: (i, 1)),
                pl.BlockSpec((tm, D_MODEL), lambda i: (i, 2))]
    in_specs += [pl.BlockSpec((tm, k), lambda i: (jnp.minimum(i, nl - 1), 0)) for k in ks]
    args = [proj, proj, proj] + list(lat)
    if has_ctx:
        in_specs += [pl.BlockSpec((tm, k), lambda i: (jnp.maximum(i - nl, 0), 0)) for k in ks]
        args += list(ctx)
    in_specs += [pl.BlockSpec((k, D_MODEL), lambda i: (0, 0)) for k in ks]
    args += [wa, wb, wc]
    return pl.pallas_call(
        functools.partial(_merge_kernel, has_ctx=has_ctx, n_lat_tiles=nl),
        grid=(m // tm,),
        in_specs=in_specs,
        out_specs=pl.BlockSpec((tm, D_MODEL), lambda i: (i, 0)),
        out_shape=jax.ShapeDtypeStruct((m, D_MODEL), BF16),
        compiler_params=_cparams(("arbitrary",)),
        name="merge",
    )(*args)


def _moe_ffn_kernel(te_ref, tok_ref, nu_ref, en_ref, es_ref, h_hbm, wg_hbm, wu_hbm, wd_hbm, y_ref,
                    xbuf, sem, wgs, wus, wds, wsem, wgb, wub, wdb, *, layer):
    i = pl.program_id(0)
    tm = MOE_TM
    slot = i % 2
    n_used = nu_ref[0]

    def row_copy(tok, s, r):
        return pltpu.make_async_copy(h_hbm.at[pl.ds(tok, 1)], xbuf.at[s, pl.ds(r, 1)], sem.at[s])

    def start_gather(tile, s):
        base = tile * tm

        def body(r, c):
            row_copy(tok_ref[base + r], s, r).start()
            return c

        lax.fori_loop(0, tm, body, 0, unroll=8)

    def weight_copies(e, s):
        return (pltpu.make_async_copy(wg_hbm.at[layer, e], wgs.at[s], wsem.at[s]),
                pltpu.make_async_copy(wu_hbm.at[layer, e], wus.at[s], wsem.at[s]),
                pltpu.make_async_copy(wd_hbm.at[layer, e], wds.at[s], wsem.at[s]))

    @pl.when(i == 0)
    def _():
        for cp in weight_copies(te_ref[0], es_ref[0]):
            cp.start(priority=1)
        start_gather(0, 0)

    @pl.when(i + 1 < n_used)
    def _():
        start_gather(i + 1, 1 - slot)

    @pl.when(i < n_used)
    def _():
        @pl.when((i == 0) | (te_ref[i] != te_ref[jnp.maximum(i - 1, 0)]))
        def _():
            s = es_ref[i]

            @pl.when(en_ref[i] >= 0)
            def _():
                for cp in weight_copies(en_ref[i], 1 - s):
                    cp.start(priority=1)

            for cp in weight_copies(te_ref[i], s):
                cp.wait()
            wgb[...] = wgs[s].astype(BF16)
            wub[...] = wus[s].astype(BF16)
            wdb[...] = wds[s].astype(BF16)

        pltpu.make_async_copy(h_hbm.at[pl.ds(0, tm)], xbuf.at[slot], sem.at[slot]).wait()
        x = xbuf[slot].astype(BF16)
        hg = jnp.dot(x, wgb[...], preferred_element_type=F32)
        hu = jnp.dot(x, wub[...], preferred_element_type=F32)
        act = (hg * jax.nn.sigmoid(hg)) * hu
        y_ref[...] = jnp.dot(act.astype(BF16), wdb[...], preferred_element_type=F32)

    @pl.when(i >= n_used)
    def _():
        y_ref[...] = jnp.zeros_like(y_ref)


def _moe_ffn(h, plan, wg, wu, wd, layer):
    tile_expert, slot_token, n_used, next_expert, expert_slot = plan
    p = slot_token.shape[0]
    tm = MOE_TM
    nt = p // tm
    anyspec = pl.BlockSpec(memory_space=pl.ANY)
    gs = pltpu.PrefetchScalarGridSpec(
        num_scalar_prefetch=5,
        grid=(nt,),
        in_specs=[anyspec, anyspec, anyspec, anyspec],
        out_specs=pl.BlockSpec((tm, D_MODEL), lambda i, *_: (i, 0)),
        scratch_shapes=[pltpu.VMEM((2, tm, D_MODEL), F32),
                        pltpu.SemaphoreType.DMA((2,)),
                        pltpu.VMEM((2, D_MODEL, MOE_HIDDEN), F32),
                        pltpu.VMEM((2, D_MODEL, MOE_HIDDEN), F32),
                        pltpu.VMEM((2, MOE_HIDDEN, D_MODEL), F32),
                        pltpu.SemaphoreType.DMA((2,)),
                        pltpu.VMEM((D_MODEL, MOE_HIDDEN), BF16),
                        pltpu.VMEM((D_MODEL, MOE_HIDDEN), BF16),
                        pltpu.VMEM((MOE_HIDDEN, D_MODEL), BF16)],
    )
    return pl.pallas_call(
        functools.partial(_moe_ffn_kernel, layer=layer),
        grid_spec=gs,
        out_shape=jax.ShapeDtypeStruct((p, D_MODEL), F32),
        compiler_params=_cparams(("arbitrary",)),
        name="moe_ffn",
    )(tile_expert, slot_token, n_used, next_expert, expert_slot, h, wg, wu, wd)


def _moe_combine_kernel(pos_ref, y_hbm, x_ref, w_ref, g_ref, o_ref, ybuf, sem):
    i = pl.program_id(0)
    nt = pl.num_programs(0)
    tm = TOK_TM
    slot = i % 2

    def row_copy(src, s, r):
        return pltpu.make_async_copy(y_hbm.at[pl.ds(src, 1)], ybuf.at[s, pl.ds(r, 1)], sem.at[s])

    def start_gather(tile, s):
        base = tile * tm

        def body(r, c):
            row_copy(pos_ref[2 * (base + r)], s, r).start(priority=0)
            row_copy(pos_ref[2 * (base + r) + 1], s, tm + r).start(priority=1)
            return c

        lax.fori_loop(0, tm, body, 0, unroll=4)

    @pl.when(i == 0)
    def _():
        start_gather(0, 0)

    @pl.when(i + 1 < nt)
    def _():
        start_gather(i + 1, 1 - slot)

    pltpu.make_async_copy(y_hbm.at[pl.ds(0, 2 * tm)], ybuf.at[slot], sem.at[slot]).wait()
    w = w_ref[...]
    y = ybuf[slot, pl.ds(0, tm), :] * w[:, 0:1] + ybuf[slot, pl.ds(tm, tm), :] * w[:, 1:2]
    o_ref[...] = x_ref[...] + g_ref[0] * y


def _moe_combine(y, pos, w_sel, x, mod, k_gate):
    m = x.shape[0]
    tm = TOK_TM
    gs = pltpu.PrefetchScalarGridSpec(
        num_scalar_prefetch=1,
        grid=(m // tm,),
        in_specs=[pl.BlockSpec(memory_space=pl.ANY),
                  pl.BlockSpec((tm, D_MODEL), lambda i, pos: (i, 0)),
                  pl.BlockSpec((tm, MOE_TOPK), lambda i, pos: (i, 0)),
                  pl.BlockSpec((1, 1, D_MODEL), lambda i, pos: (_seg(i, tm) * 6 + k_gate, 0, 0))],
        out_specs=pl.BlockSpec((tm, D_MODEL), lambda i, pos: (i, 0)),
        scratch_shapes=[pltpu.VMEM((2, 2 * tm, D_MODEL), F32),
                        pltpu.SemaphoreType.DMA((2,))],
    )
    return pl.pallas_call(
        _moe_combine_kernel,
        grid_spec=gs,
        out_shape=jax.ShapeDtypeStruct((m, D_MODEL), F32),
        compiler_params=_cparams(("arbitrary",)),
        name="moe_combine",
    )(pos, y, x, w_sel, mod)


def _route(logits, m):
    tm = MOE_TM
    gp = jax.nn.softmax(logits[:, :MOE_GROUPS], axis=-1)
    g_idx = jnp.argmax(gp, axis=-1).astype(jnp.int32)[:, None]
    g_w = jnp.max(gp, axis=-1, keepdims=True)
    el = logits[:, MOE_GROUPS:MOE_GROUPS + MOE_EXPERTS].reshape(m, MOE_GROUPS, MOE_PER_GROUP)
    g_onehot = (g_idx == jnp.arange(MOE_GROUPS, dtype=jnp.int32)[None, :]).astype(F32)
    el_g = jnp.sum(el * g_onehot[:, :, None], axis=1)
    i0 = jnp.argmax(el_g, axis=-1).astype(jnp.int32)[:, None]
    l0 = jnp.max(el_g, axis=-1, keepdims=True)
    rest = jnp.where(jnp.arange(MOE_PER_GROUP, dtype=jnp.int32)[None, :] == i0, -jnp.inf, el_g)
    i1 = jnp.argmax(rest, axis=-1).astype(jnp.int32)[:, None]
    l1 = jnp.max(rest, axis=-1, keepdims=True)
    top_l = jnp.concatenate([l0, l1], axis=-1)
    top_i = jnp.concatenate([i0, i1], axis=-1)
    w_sel = jax.nn.softmax(top_l, axis=-1) * g_w
    eid = (g_idx * MOE_PER_GROUP + top_i).astype(jnp.int32)

    a = m * MOE_TOPK
    e_flat = eid.reshape(a)
    onehot = (e_flat[:, None] == jnp.arange(MOE_EXPERTS, dtype=jnp.int32)[None, :]).astype(jnp.int32)
    csum = jnp.cumsum(onehot, axis=0)
    rank = jnp.sum(csum * onehot, axis=1) - 1
    counts = csum[-1]
    padded = ((counts + tm - 1) // tm) * tm
    ends = jnp.cumsum(padded)
    starts = ends - padded
    pos = (jnp.sum(onehot * starts[None, :], axis=1) + rank).astype(jnp.int32)
    p = a + MOE_EXPERTS * tm
    slot_token = jnp.zeros((p,), jnp.int32).at[pos].set(jnp.arange(a, dtype=jnp.int32) // MOE_TOPK)
    n_used = (ends[-1] // tm).astype(jnp.int32).reshape(1)
    tile_start = jnp.arange(p // tm, dtype=jnp.int32) * tm
    last_e = jnp.max(jnp.where(counts > 0, jnp.arange(MOE_EXPERTS, dtype=jnp.int32), 0))
    tile_expert = jnp.minimum(
        jnp.sum((ends[None, :] <= tile_start[:, None]).astype(jnp.int32), axis=1), last_e)
    eidx = jnp.arange(MOE_EXPERTS, dtype=jnp.int32)
    used = counts > 0
    later = used[None, :] & (eidx[None, :] > eidx[:, None])
    next_used = jnp.min(jnp.where(later, eidx[None, :], MOE_EXPERTS), axis=1)
    next_used = jnp.where(next_used == MOE_EXPERTS, -1, next_used).astype(jnp.int32)
    ordinal = (jnp.cumsum(used.astype(jnp.int32)) - 1) % 2
    t_onehot = (tile_expert[:, None] == eidx[None, :]).astype(jnp.int32)
    next_expert = jnp.sum(t_onehot * next_used[None, :], axis=1).astype(jnp.int32)
    expert_slot = jnp.sum(t_onehot * ordinal[None, :], axis=1).astype(jnp.int32)
    return (tile_expert, slot_token, n_used, next_expert, expert_slot), w_sel, pos


def _rope_angles(rot_dim):
    t = np.arange(SEQ)
    row = (t // GRID_W).astype(np.float64)
    col = (t % GRID_W).astype(np.float64)
    n_freq = rot_dim // 4
    inv = ROPE_THETA ** (-np.arange(n_freq, dtype=np.float64) / n_freq)
    ang = np.concatenate([row[:, None] * inv, col[:, None] * inv], axis=-1)
    return np.cos(ang), np.sin(ang)


def _rope_tables():
    def rows(lat, ident):
        ctx = np.zeros((TOK_TM, 128)) + ident
        return jnp.asarray(np.concatenate([lat, ctx], axis=0).astype(np.float32))

    z32 = np.zeros((SEQ, 32))
    z64 = np.zeros((SEQ, 64))
    lane = np.arange(128)
    cm, sm = _rope_angles(MLA_ROPE)
    mla_c = rows(np.concatenate([cm, cm, z64], axis=1), (lane < 64).astype(np.float64))
    mla_s = rows(np.concatenate([-sm, sm, z64], axis=1), 0.0)
    cg, sg = _rope_angles(GQA_HD)
    gqa_c = rows(np.concatenate([cg, cg, cg, cg], axis=1), 1.0)
    gqa_s1 = rows(np.concatenate([-sg, z32, -sg, z32], axis=1), 0.0)
    gqa_s2 = rows(np.concatenate([z32, sg, z32, sg], axis=1), 0.0)
    return {'mla_c': mla_c, 'mla_s': mla_s, 'gqa_c': gqa_c, 'gqa_s1': gqa_s1, 'gqa_s2': gqa_s2}


def _rope_block(i):
    per_seq = SEQ // TOK_TM
    return jnp.where(i < BATCH * per_seq, i % per_seq, per_seq)


def _pack_w_in_kernel(w_ref, o_ref):
    bounds = (0,) + IN_SPLITS + (sum(IN_SIZES),)
    off = 0
    for i in _PACK_ORDER:
        size = bounds[i + 1] - bounds[i]
        o_ref[0, off:off + size, :] = w_ref[0, bounds[i]:bounds[i + 1], :].astype(BF16)
        off += size
    o_ref[0, off:, :] = jnp.zeros((_PACK_COLS - off, o_ref.shape[2]), BF16)


def _pack_w_in(w_in):
    tk = 256
    n_in = sum(IN_SIZES)
    wt = jnp.swapaxes(w_in, 1, 2)
    return pl.pallas_call(
        _pack_w_in_kernel,
        grid=(DEPTH, D_MODEL // tk),
        in_specs=[pl.BlockSpec((1, n_in, tk), lambda l, i: (l, 0, i))],
        out_specs=pl.BlockSpec((1, _PACK_COLS, tk), lambda l, i: (l, 0, i)),
        out_shape=jax.ShapeDtypeStruct((DEPTH, _PACK_COLS, D_MODEL), BF16),
        compiler_params=_cparams(("arbitrary", "arbitrary")),
        name="pack_w_in",
    )(wt)


def _pad_w_o_gqa(w):
    w4 = w.reshape(GQA_KV_HEADS, _GQA_G, GQA_HD, D_MODEL)
    z = jnp.zeros_like(w4[0:1])
    halves = [jnp.concatenate([w4[hk:hk + 1] if hk == half else z for hk in range(GQA_KV_HEADS)], axis=0)
              for half in range(GQA_KV_HEADS)]
    return jnp.stack(halves, axis=2).reshape(_GQA_QW, D_MODEL).astype(BF16)


def _token_mixer(h, p, tabs, ctx_out):
    proj = _mm(h, p['w_in_packed'], p['layer'], BF16, _PACK_COLS // 4)
    mq, mk, mvt = _mla_prep(proj, p, tabs)
    oa = _mla_attn(mq, mk, mvt, tq=512, tk=1024)
    gnq = (p['na_qn_g'] * (NA_HD ** -0.5)).reshape(1, NA_HD)
    gnk = p['na_kn_g'].reshape(1, NA_HD)
    ob = _na_attn(proj, _na_bias_table(p['na_rpb']), gnq, gnk)
    qd, kn = _gqa_prep(proj, p, tabs)
    oc = _gqa_attn(qd, kn, proj, p['gqa_sink'])
    ctx = _ctx_attn(p['gqa_sink'], mq, mk, mvt, proj, gnq, gnk, qd, kn) if ctx_out else None
    return _merge(proj, (oa, ob, oc), ctx, p['w_o_mla'].astype(BF16), p['w_o_na'].astype(BF16),
                  _pad_w_o_gqa(p['w_o_gqa']))


def _post_mixer(y, xt, mod, norm_g, w_out_l, p):
    m = y.shape[0]
    wr = jnp.concatenate([p['moe_w_group'], p['moe_w_expert'],
                          jnp.zeros((D_MODEL, ROUTE_COLS - MOE_GROUPS - MOE_EXPERTS), F32)], axis=1)
    br = jnp.concatenate([p['moe_b_group'], p['moe_b_expert'],
                          jnp.zeros((ROUTE_COLS - MOE_GROUPS - MOE_EXPERTS,), F32)]).reshape(1, ROUTE_COLS)
    x_mid, h, logits = _out_route(y, w_out_l.astype(BF16), xt, mod, norm_g, wr, br)
    plan, w_sel, pos = _route(logits, m)
    yy = _moe_ffn(h, plan, p['moe_w_gate'], p['moe_w_up'], p['moe_w_down'], p['layer'])
    return _moe_combine(yy, pos, w_sel, x_mid, mod, 5)


def kernel(x, c, ctx, c_ctx, ada_w, ada_b, norm_mix_g, norm_ffn_g, w_in,
           mla_q_norm_g, mla_w_uq, mla_kv_norm_g, mla_w_ukv, mla_qn_g, mla_kn_g,
           na_qn_g, na_kn_g, na_rpb, gqa_qn_g, gqa_kn_g, gqa_sink,
           w_o_mla, w_o_na, w_o_gqa, w_out,
           moe_w_group, moe_b_group, moe_w_expert, moe_b_expert,
           moe_w_gate, moe_w_up, moe_w_down):
    xt = (x.reshape(N_LAT, D_MODEL), ctx.reshape(N_CTX, D_MODEL))
    c_rows = jnp.concatenate([c, c_ctx[None, :], jnp.zeros((8 - BATCH - 1, D_MODEL), F32)], axis=0)
    mod_all = _ada(c_rows, ada_w, ada_b)
    tabs = _rope_tables()
    w_in_packed = _pack_w_in(w_in)
    for l in range(DEPTH):
        ctx_out = l < DEPTH - 1
        p = {
            'w_in_packed': w_in_packed, 'mla_q_norm_g': mla_q_norm_g[l], 'mla_w_uq': mla_w_uq[l],
            'mla_kv_norm_g': mla_kv_norm_g[l], 'mla_w_ukv': mla_w_ukv[l],
            'mla_qn_g': mla_qn_g[l], 'mla_kn_g': mla_kn_g[l],
            'na_qn_g': na_qn_g[l], 'na_kn_g': na_kn_g[l], 'na_rpb': na_rpb[l],
            'gqa_qn_g': gqa_qn_g[l], 'gqa_kn_g': gqa_kn_g[l], 'gqa_sink': gqa_sink[l],
            'w_o_mla': w_o_mla[l], 'w_o_na': w_o_na[l], 'w_o_gqa': w_o_gqa[l],
            'moe_w_group': moe_w_group[l], 'moe_b_group': moe_b_group[l],
            'moe_w_expert': moe_w_expert[l], 'moe_b_expert': moe_b_expert[l],
            'moe_w_gate': moe_w_gate, 'moe_w_up': moe_w_up, 'moe_w_down': moe_w_down, 'layer': l,
        }
        mod = mod_all[l].reshape(8 * 6, 1, D_MODEL)
        h = _norm_mod(xt, norm_mix_g[l], mod, 0, 1)
        y = _token_mixer(h, p, tabs, ctx_out)
        xt = _post_mixer(y, xt, mod, norm_ffn_g[l], w_out[l], p)
    return xt[:N_LAT].reshape(BATCH, SEQ, D_MODEL)
```

```python
import functools

import numpy as np
import jax
import jax.numpy as jnp
from jax import lax
from jax.experimental import pallas as pl
from jax.experimental.pallas import tpu as pltpu

D_MODEL = 2048
BATCH = 2
SEQ = 4096
DEPTH = 2
GRID_W = 64
CTX_LEN = 256
EPS = 1e-6
ROPE_THETA = 10000.0
NEG_INF = -1e30

MLA_HEADS = 8
MLA_Q_RANK = 512
MLA_KV_RANK = 512
MLA_NOPE = 128
MLA_ROPE = 64
MLA_QK = MLA_NOPE + MLA_ROPE
MLA_V = 128
NA_HEADS = 4
NA_HD = 128
NA_ROWS = 8
NA_COLS = 16
GQA_HEADS = 8
GQA_KV_HEADS = 2
GQA_HD = 64
GQA_WINDOW = 128
MOE_GROUPS = 4
MOE_PER_GROUP = 8
MOE_EXPERTS = MOE_GROUPS * MOE_PER_GROUP
MOE_TOPK = 2
MOE_HIDDEN = 512

IN_SIZES = (MLA_Q_RANK, MLA_KV_RANK, MLA_ROPE,
            NA_HEADS * NA_HD, NA_HEADS * NA_HD, NA_HEADS * NA_HD,
            GQA_HEADS * GQA_HD, GQA_KV_HEADS * GQA_HD, GQA_KV_HEADS * GQA_HD,
            D_MODEL, D_MODEL, D_MODEL)
IN_SPLITS = tuple(int(s) for s in np.cumsum(IN_SIZES)[:-1])

N_LAT = BATCH * SEQ
N_CTX = BATCH * CTX_LEN
N_TOK = N_LAT + N_CTX
ROWS = SEQ // GRID_W

V7X_LANES = 128
V7X_VMEM_LIMIT = 56 * 1024 * 1024

_PACK_ORDER = (9, 10, 11, 0, 1, 3, 4, 5, 6, 7, 8, 2)
_PACK_COLS = 9728
_PACK_OFF = {}
_off = 0
for _i in _PACK_ORDER:
    _PACK_OFF[_i] = _off
    _off += IN_SIZES[_i]

ROUTE_COLS = V7X_LANES
MOE_TM = 256
TOK_TM = 256
MM_TM = 512
CTX_BLK = N_LAT // CTX_LEN

F32 = jnp.float32
BF16 = jnp.bfloat16
_LOG2E = 1.4426950408889634
_NT = (((1,), (1,)), ((), ()))


def _cparams(sem):
    return pltpu.CompilerParams(dimension_semantics=sem, vmem_limit_bytes=V7X_VMEM_LIMIT)


def _seg(i, tm):
    return jnp.minimum(i // (SEQ // tm), 2)


def _pcol(idx, width):
    assert _PACK_OFF[idx] % width == 0
    return _PACK_OFF[idx] // width


def _ada_kernel(ct_ref, w_ref, b_ref, o_ref):
    ct = ct_ref[...]
    act = ct * jax.nn.sigmoid(ct)
    w = w_ref[0]
    rows = [jnp.sum(act[:, r:r + 1] * w, axis=0, keepdims=True) for r in range(BATCH + 1)]
    rows.append(jnp.zeros((8 - len(rows), w.shape[1]), F32))
    o_ref[0] = jnp.concatenate(rows, axis=0) + b_ref[0]


def _ada(c_rows, ada_w, ada_b):
    tn = 1024
    n = 6 * D_MODEL
    return pl.pallas_call(
        _ada_kernel,
        grid=(DEPTH, n // tn),
        in_specs=[
            pl.BlockSpec((D_MODEL, 8), lambda l, j: (0, 0)),
            pl.BlockSpec((1, D_MODEL, tn), lambda l, j: (l, 0, j)),
            pl.BlockSpec((1, 1, tn), lambda l, j: (l, 0, j)),
        ],
        out_specs=pl.BlockSpec((1, 8, tn), lambda l, j: (l, 0, j)),
        out_shape=jax.ShapeDtypeStruct((DEPTH, 8, n), F32),
        compiler_params=_cparams(("arbitrary", "arbitrary")),
        name="ada",
    )(c_rows.T, ada_w, ada_b.reshape(DEPTH, 1, n))


def _stream_specs(xs):
    tm = TOK_TM
    if not isinstance(xs, tuple):
        return [pl.BlockSpec((tm, D_MODEL), lambda i: (i, 0))], [xs], xs.shape[0]
    nl = N_LAT // tm
    return ([pl.BlockSpec((tm, D_MODEL), lambda i: (jnp.minimum(i, nl - 1), 0)),
             pl.BlockSpec((tm, D_MODEL), lambda i: (jnp.maximum(i - nl, 0), 0))],
            list(xs), xs[0].shape[0] + xs[1].shape[0])


def _stream_tile(refs):
    if len(refs) == 1:
        return refs[0][...]
    return jnp.where(pl.program_id(0) >= N_LAT // TOK_TM, refs[1][...], refs[0][...])


def _norm_mod_kernel(*refs, n_src):
    g_ref, sh_ref, sc_ref, h_ref = refs[n_src:]
    x = _stream_tile(refs[:n_src])
    xn = x * lax.rsqrt(jnp.mean(x * x, axis=-1, keepdims=True) + EPS) * g_ref[...]
    h_ref[...] = (xn * (1.0 + sc_ref[0]) + sh_ref[0]).astype(h_ref.dtype)


def _norm_mod(xs, g, mod, k_shift, k_scale):
    tm = TOK_TM
    specs, args, m = _stream_specs(xs)
    return pl.pallas_call(
        functools.partial(_norm_mod_kernel, n_src=len(args)),
        grid=(m // tm,),
        in_specs=specs + [
            pl.BlockSpec((1, D_MODEL), lambda i: (0, 0)),
            pl.BlockSpec((1, 1, D_MODEL), lambda i: (_seg(i, tm) * 6 + k_shift, 0, 0)),
            pl.BlockSpec((1, 1, D_MODEL), lambda i: (_seg(i, tm) * 6 + k_scale, 0, 0))],
        out_specs=pl.BlockSpec((tm, D_MODEL), lambda i: (i, 0)),
        out_shape=jax.ShapeDtypeStruct((m, D_MODEL), BF16),
        compiler_params=_cparams(("arbitrary",)),
        name="norm_mod",
    )(*args, g.reshape(1, D_MODEL), mod, mod)


def _out_route_kernel(*refs, n_src):
    y_ref, w_ref = refs[:2]
    g1_ref, gn_ref, sh_ref, sc_ref, wr_ref, br_ref, xo_ref, h_ref, lg_ref = refs[2 + n_src:]
    acc = jnp.dot(y_ref[...], w_ref[...], preferred_element_type=F32)
    x = _stream_tile(refs[2:2 + n_src]) + g1_ref[0] * acc
    xo_ref[...] = x
    xn = x * lax.rsqrt(jnp.mean(x * x, axis=-1, keepdims=True) + EPS) * gn_ref[...]
    h = xn * (1.0 + sc_ref[0]) + sh_ref[0]
    h_ref[...] = h
    hi = h.astype(BF16)
    lo = (h - hi.astype(F32)).astype(BF16)
    wr = wr_ref[...]
    w_hi = wr.astype(BF16)
    w_lo = (wr - w_hi.astype(F32)).astype(BF16)
    lg_ref[...] = (jnp.dot(hi, w_hi, preferred_element_type=F32)
                   + jnp.dot(hi, w_lo, preferred_element_type=F32)
                   + jnp.dot(lo, w_hi, preferred_element_type=F32) + br_ref[...])


def _out_route(y, w_out_bf16, res, mod, norm_g, wr, br):
    m = y.shape[0]
    tm = TOK_TM

    def modrow(k):
        return pl.BlockSpec((1, 1, D_MODEL), lambda i: (_seg(i, tm) * 6 + k, 0, 0))

    row = pl.BlockSpec((tm, D_MODEL), lambda i: (i, 0))
    res_specs, res_args, _ = _stream_specs(res)
    return pl.pallas_call(
        functools.partial(_out_route_kernel, n_src=len(res_args)),
        grid=(m // tm,),
        in_specs=[row,
                  pl.BlockSpec((D_MODEL, D_MODEL), lambda i: (0, 0))] + res_specs + [
                  modrow(2),
                  pl.BlockSpec((1, D_MODEL), lambda i: (0, 0)),
                  modrow(3), modrow(4),
                  pl.BlockSpec((D_MODEL, ROUTE_COLS), lambda i: (0, 0)),
                  pl.BlockSpec((1, ROUTE_COLS), lambda i: (0, 0))],
        out_specs=[row, row, pl.BlockSpec((tm, ROUTE_COLS), lambda i: (i, 0))],
        out_shape=[jax.ShapeDtypeStruct((m, D_MODEL), F32),
                   jax.ShapeDtypeStruct((m, D_MODEL), F32),
                   jax.ShapeDtypeStruct((m, ROUTE_COLS), F32)],
        compiler_params=_cparams(("arbitrary",)),
        name="out_route",
    )(y, w_out_bf16, *res_args, mod, norm_g.reshape(1, D_MODEL), mod, mod, wr, br)


def _mm_bf16_kernel(x_ref, w_ref, o_ref):
    o_ref[...] = lax.dot_general(x_ref[...], w_ref[0], _NT,
                                 preferred_element_type=F32).astype(o_ref.dtype)


def _mm(x, wt, layer, out_dtype, tn):
    m, k = x.shape
    n = wt.shape[1]
    tm = MM_TM
    assert x.dtype == BF16 and wt.dtype == BF16
    w = wt
    return pl.pallas_call(
        _mm_bf16_kernel,
        grid=(n // tn, m // tm),
        in_specs=[pl.BlockSpec((tm, k), lambda j, i: (i, 0)),
                  pl.BlockSpec((1, tn, k), lambda j, i: (layer, j, 0))],
        out_specs=pl.BlockSpec((tm, tn), lambda j, i: (i, j)),
        out_shape=jax.ShapeDtypeStruct((m, n), out_dtype),
        compiler_params=_cparams(("arbitrary", "arbitrary")),
        name="mm_bf16",
    )(x, w)


def _row_rms(x, g):
    return x * lax.rsqrt(jnp.mean(x * x, axis=-1, keepdims=True) + EPS) * g


_MLA_PAD = 2 * V7X_LANES


def _mla_prep_kernel(cq_ref, ckv_ref, kr_ref, wq_ref, wk_ref, wvt_ref, gqi_ref, gkvi_ref,
                     gq_ref, gkn_ref, gkr_ref, c_ref, s_ref, q_ref, k_ref, vt_ref):
    c = c_ref[...]
    s = s_ref[...]
    hw = MLA_HEADS * MLA_NOPE

    def rot(t):
        return t * c + (pltpu.roll(t, 32, 1) + pltpu.roll(t, 96, 1)) * s

    cqn = _row_rms(cq_ref[...].astype(F32), gqi_ref[...]).astype(BF16)
    qf = jnp.dot(cqn, wq_ref[...], preferred_element_type=F32)
    gq = gq_ref[...]
    inv = 1.0 / MLA_QK
    for h in range(MLA_HEADS):
        nope = qf[:, h * 128:(h + 1) * 128]
        t = qf[:, hw + h * 128:hw + (h + 1) * 128]
        ss = jnp.sum(nope * nope, axis=-1, keepdims=True) + jnp.sum(t * t, axis=-1, keepdims=True)
        r = lax.rsqrt(ss * inv + EPS)
        q_ref[h, :, 0:128] = (nope * r * gq[:, 0:128]).astype(BF16)
        q_ref[h, :, 128:256] = rot(t * r * gq[:, 128:256]).astype(BF16)

    ckvn = _row_rms(ckv_ref[...].astype(F32), gkvi_ref[...]).astype(BF16)
    kvf = jnp.dot(ckvn, wk_ref[...], preferred_element_type=F32)
    kr = kr_ref[...].astype(F32)
    ssr = jnp.sum(kr * kr, axis=-1, keepdims=True)
    yrot = rot(kr * gkr_ref[...])
    gkn = gkn_ref[...]
    for h in range(MLA_HEADS):
        nope = kvf[:, h * 128:(h + 1) * 128]
        r = lax.rsqrt((jnp.sum(nope * nope, axis=-1, keepdims=True) + ssr) * inv + EPS)
        k_ref[h, :, 0:128] = (nope * r * gkn).astype(BF16)
        k_ref[h, :, 128:256] = (yrot * r).astype(BF16)
        vt_ref[h] = lax.dot_general(wvt_ref[h * MLA_V:(h + 1) * MLA_V, :], ckvn, _NT,
                                    preferred_element_type=F32).astype(BF16)


def _mla_prep(proj, p, tabs):
    tm = TOK_TM
    hw = MLA_HEADS * MLA_NOPE
    wq = p['mla_w_uq'].reshape(MLA_Q_RANK, MLA_HEADS, MLA_QK)
    wq_rope = jnp.pad(wq[:, :, MLA_NOPE:], ((0, 0), (0, 0), (0, 128 - MLA_ROPE)))
    wq = jnp.concatenate([wq[:, :, :MLA_NOPE].reshape(MLA_Q_RANK, hw),
                          wq_rope.reshape(MLA_Q_RANK, hw)], axis=1).astype(BF16)
    wkv = p['mla_w_ukv'].reshape(MLA_KV_RANK, MLA_HEADS, MLA_NOPE + MLA_V)
    wk = wkv[:, :, :MLA_NOPE].reshape(MLA_KV_RANK, hw).astype(BF16)
    wvt = jnp.transpose(wkv[:, :, MLA_NOPE:], (1, 2, 0)).reshape(MLA_HEADS * MLA_V, MLA_KV_RANK).astype(BF16)
    zpad = jnp.zeros((128 - MLA_ROPE,), F32)
    gq = (jnp.concatenate([p['mla_qn_g'], zpad]) * (MLA_QK ** -0.5 * _LOG2E)).reshape(1, _MLA_PAD)
    gkn = p['mla_kn_g'][:MLA_NOPE].reshape(1, 128)
    gkr = jnp.concatenate([p['mla_kn_g'][MLA_NOPE:], zpad]).reshape(1, 128)

    def const(shape):
        return pl.BlockSpec(shape, lambda i: (0,) * len(shape))

    return pl.pallas_call(
        _mla_prep_kernel,
        grid=(N_TOK // tm,),
        in_specs=[pl.BlockSpec((tm, MLA_Q_RANK), lambda i: (i, _pcol(0, MLA_Q_RANK))),
                  pl.BlockSpec((tm, MLA_KV_RANK), lambda i: (i, _pcol(1, MLA_KV_RANK))),
                  pl.BlockSpec((tm, 128), lambda i: (i, _pcol(2, 128))),
                  const((MLA_Q_RANK, 2 * hw)), const((MLA_KV_RANK, hw)),
                  const((MLA_HEADS * MLA_V, MLA_KV_RANK)),
                  const((1, MLA_Q_RANK)), const((1, MLA_KV_RANK)),
                  const((1, _MLA_PAD)), const((1, 128)), const((1, 128)),
                  pl.BlockSpec((tm, 128), lambda i: (_rope_block(i), 0)),
                  pl.BlockSpec((tm, 128), lambda i: (_rope_block(i), 0))],
        out_specs=[pl.BlockSpec((MLA_HEADS, tm, _MLA_PAD), lambda i: (0, i, 0)),
                   pl.BlockSpec((MLA_HEADS, tm, _MLA_PAD), lambda i: (0, i, 0)),
                   pl.BlockSpec((MLA_HEADS, MLA_V, tm), lambda i: (0, 0, i))],
        out_shape=[jax.ShapeDtypeStruct((MLA_HEADS, N_TOK, _MLA_PAD), BF16),
                   jax.ShapeDtypeStruct((MLA_HEADS, N_TOK, _MLA_PAD), BF16),
                   jax.ShapeDtypeStruct((MLA_HEADS, MLA_V, N_TOK), BF16)],
        compiler_params=_cparams(("arbitrary",)),
        name="mla_prep",
    )(proj, proj, proj, wq, wk, wvt, p['mla_q_norm_g'].reshape(1, -1), p['mla_kv_norm_g'].reshape(1, -1),
      gq, gkn, gkr, tabs['mla_c'], tabs['mla_s'])


def _mla_kernel(q_ref, k1_ref, vt1_ref, k2_ref, vt2_ref, o_ref, sa_ref, sb_ref, sc_ref, acc_ref,
                *, tk, n_chunks):
    q = q_ref[0]
    tq = q.shape[0]
    hq = tq // 2
    qs = (q[:hq], q[hq:])

    def scores(dst_ref, kc):
        out = []
        for j in range(2):
            st = lax.dot_general(kc, qs[j], _NT, preferred_element_type=F32)
            dst_ref[j] = st
            out.append(jnp.max(st, axis=0, keepdims=True))
        return out

    def k_chunk(c):
        return k1_ref[0, pl.ds(pl.multiple_of(c * tk, tk), tk), :]

    def vt_chunk(c):
        return vt1_ref[0, :, pl.ds(pl.multiple_of(c * tk, tk), tk)]

    def accumulate(s_ref, smax, vtc, m, l):
        m_out, l_out = [], []
        for j in range(2):
            m_new = jnp.maximum(m[j], smax[j])
            a = jnp.exp2(m[j] - m_new)
            p = jnp.exp2(s_ref[j] - m_new)
            l_out.append(a * l[j] + jnp.sum(p, axis=0, keepdims=True))
            acc_ref[j] = a * acc_ref[j] + jnp.dot(vtc, p.astype(BF16), preferred_element_type=F32)
            m_out.append(m_new)
        return m_out, l_out

    m = [jnp.full((1, hq), NEG_INF, F32)] * 2
    l = [jnp.zeros((1, hq), F32)] * 2
    acc_ref[...] = jnp.zeros_like(acc_ref)
    mx_c = scores(sc_ref, k2_ref[0])
    mx_a = scores(sa_ref, k_chunk(0))
    m, l = accumulate(sc_ref, mx_c, vt2_ref[0], m, l)

    for i in range(n_chunks // 2 - 1):
        mx_b = scores(sb_ref, k_chunk(2 * i + 1))
        m, l = accumulate(sa_ref, mx_a, vt_chunk(2 * i), m, l)
        mx_a = scores(sa_ref, k_chunk(2 * i + 2))
        m, l = accumulate(sb_ref, mx_b, vt_chunk(2 * i + 1), m, l)
    mx_b = scores(sb_ref, k_chunk(n_chunks - 1))
    m, l = accumulate(sa_ref, mx_a, vt_chunk(n_chunks - 2), m, l)
    m, l = accumulate(sb_ref, mx_b, vt_chunk(n_chunks - 1), m, l)
    for j in range(2):
        o_ref[j * hq:(j + 1) * hq, :] = (acc_ref[j] / l[j]).T.astype(o_ref.dtype)


def _mla_attn(q, k, vt, *, tq, tk):
    nq = SEQ // tq
    return pl.pallas_call(
        functools.partial(_mla_kernel, tk=tk, n_chunks=SEQ // tk),
        grid=(BATCH, MLA_HEADS, nq),
        in_specs=[pl.BlockSpec((1, tq, _MLA_PAD), lambda b, h, i: (h, b * nq + i, 0)),
                  pl.BlockSpec((1, SEQ, _MLA_PAD), lambda b, h, i: (h, b, 0)),
                  pl.BlockSpec((1, MLA_V, SEQ), lambda b, h, i: (h, 0, b)),
                  pl.BlockSpec((1, CTX_LEN, _MLA_PAD), lambda b, h, i: (h, CTX_BLK + b, 0)),
                  pl.BlockSpec((1, MLA_V, CTX_LEN), lambda b, h, i: (h, 0, CTX_BLK + b))],
        out_specs=pl.BlockSpec((tq, MLA_V), lambda b, h, i: (b * nq + i, h)),
        out_shape=jax.ShapeDtypeStruct((N_LAT, MLA_HEADS * MLA_V), BF16),
        scratch_shapes=[pltpu.VMEM((2, tk, tq // 2), F32), pltpu.VMEM((2, tk, tq // 2), F32),
                        pltpu.VMEM((2, CTX_LEN, tq // 2), F32), pltpu.VMEM((2, MLA_V, tq // 2), F32)],
        compiler_params=_cparams(("arbitrary", "arbitrary", "arbitrary")),
        name="mla_attn",
    )(q, k, vt, k, vt)


NA_RB = 8
NA_WIN_ROWS = 16
NA_QCHUNKS = 2
_NA_Q = NA_RB * GRID_W
_NA_WIN = NA_WIN_ROWS * GRID_W
assert NA_WIN_ROWS >= NA_RB + NA_ROWS - 1 and ROWS % NA_RB == 0


def _na_win_start(r0):
    return np.clip(r0 - NA_ROWS // 2, 0, ROWS - NA_WIN_ROWS)


def _na_small_pattern(r):
    half = NA_ROWS // 2
    return r if r < half else (half if r <= ROWS - half else r - (ROWS - NA_ROWS))


def _na_step_layout():
    out = []
    for r0 in (0, NA_RB, ROWS - NA_RB):
        u = int(_na_win_start(r0))
        rows = []
        for j in range(NA_RB):
            start = int(np.clip(r0 + j - NA_ROWS // 2, 0, ROWS - NA_ROWS))
            rows.append((start - u, _na_small_pattern(r0 + j)))
        out.append(rows)
    return out


def _na_kernel(q_ref, k_ref, v_ref, kc_ref, vc_ref, bias_ref, gq_ref, gk_ref, o_ref,
               kn_ref, kcn_ref, big_ref):
    rb = pl.program_id(2)
    nrb = pl.num_programs(2)

    @pl.when(rb == 0)
    def _():
        kn_ref[...] = _row_rms(k_ref[...].astype(F32), gk_ref[...]).astype(BF16)
        kcn_ref[...] = _row_rms(kc_ref[...].astype(F32), gk_ref[...]).astype(BF16)
        big_ref[...] = jnp.full(big_ref.shape, NEG_INF, F32)
        for p, rows in enumerate(_na_step_layout()):
            for j, (off, pat) in enumerate(rows):
                big_ref[p, j * GRID_W:(j + 1) * GRID_W, off * GRID_W:(off + NA_ROWS) * GRID_W] = bias_ref[0, pat]

    pattern = jnp.where(rb == 0, 0, jnp.where(rb == nrb - 1, 2, 1))
    q = _row_rms(q_ref[...].astype(F32), gq_ref[...]).astype(BF16)
    u = jnp.clip(rb * NA_RB - NA_ROWS // 2, 0, ROWS - NA_WIN_ROWS) * GRID_W
    u = pl.multiple_of(u, GRID_W)
    k = kn_ref[pl.ds(u, _NA_WIN), :]
    v = v_ref[pl.ds(u, _NA_WIN), :]
    kc = kcn_ref[...]
    vc = vc_ref[...]
    nq = _NA_Q // NA_QCHUNKS
    for j in range(NA_QCHUNKS):
        rows = slice(j * nq, (j + 1) * nq)
        qj = q[rows]
        s = lax.dot_general(qj, k, _NT, preferred_element_type=F32) + big_ref[pattern, rows, :]
        sc = lax.dot_general(qj, kc, _NT, preferred_element_type=F32)
        m = jnp.maximum(jnp.max(s, axis=-1, keepdims=True), jnp.max(sc, axis=-1, keepdims=True))
        p = jnp.exp(s - m)
        pc = jnp.exp(sc - m)
        l = jnp.sum(p, axis=-1, keepdims=True) + jnp.sum(pc, axis=-1, keepdims=True)
        o = (jnp.dot(p.astype(BF16), v, preferred_element_type=F32)
             + jnp.dot(pc.astype(BF16), vc, preferred_element_type=F32))
        o_ref[rows, :] = (o / l).astype(o_ref.dtype)


def _na_attn(proj, bias, gq, gk):
    nrb = ROWS // NA_RB
    cq, ck, cv = _pcol(3, NA_HD), _pcol(4, NA_HD), _pcol(5, NA_HD)
    return pl.pallas_call(
        _na_kernel,
        grid=(BATCH, NA_HEADS, nrb),
        in_specs=[pl.BlockSpec((_NA_Q, NA_HD), lambda b, h, r: (b * nrb + r, cq + h)),
                  pl.BlockSpec((SEQ, NA_HD), lambda b, h, r: (b, ck + h)),
                  pl.BlockSpec((SEQ, NA_HD), lambda b, h, r: (b, cv + h)),
                  pl.BlockSpec((CTX_LEN, NA_HD), lambda b, h, r: (CTX_BLK + b, ck + h)),
                  pl.BlockSpec((CTX_LEN, NA_HD), lambda b, h, r: (CTX_BLK + b, cv + h)),
                  pl.BlockSpec((1, NA_ROWS, GRID_W, NA_ROWS * GRID_W), lambda b, h, r: (h, 0, 0, 0)),
                  pl.BlockSpec((1, NA_HD), lambda b, h, r: (0, 0)),
                  pl.BlockSpec((1, NA_HD), lambda b, h, r: (0, 0))],
        out_specs=pl.BlockSpec((_NA_Q, NA_HD), lambda b, h, r: (b * nrb + r, h)),
        out_shape=jax.ShapeDtypeStruct((N_LAT, NA_HEADS * NA_HD), BF16),
        scratch_shapes=[pltpu.VMEM((SEQ, NA_HD), BF16), pltpu.VMEM((CTX_LEN, NA_HD), BF16),
                        pltpu.VMEM((3, _NA_Q, _NA_WIN), F32)],
        compiler_params=_cparams(("arbitrary", "arbitrary", "arbitrary")),
        name="na_attn",
    )(proj, proj, proj, proj, proj, bias, gq, gk)


def _na_bias_table(rpb):
    half = NA_ROWS // 2
    r_rep = np.array(list(range(half)) + [half] + list(range(ROWS - half + 1, ROWS)))
    assert all(_na_small_pattern(int(r)) == i for i, r in enumerate(r_rep))
    start = np.clip(r_rep - half, 0, ROWS - NA_ROWS)
    dr = start[:, None] + np.arange(NA_ROWS)[None, :] - r_rep[:, None] + NA_ROWS - 1
    qc = np.arange(GRID_W)
    kcol = np.arange(GRID_W)
    col_start = np.clip(qc - NA_COLS // 2, 0, GRID_W - NA_COLS)
    in_win = (kcol[None, :] >= col_start[:, None]) & (kcol[None, :] < col_start[:, None] + NA_COLS)
    dc = np.clip(kcol[None, :] - qc[:, None], 1 - NA_COLS, NA_COLS - 1) + NA_COLS - 1
    rsel = (dr[:, :, None] == np.arange(2 * NA_ROWS - 1)).astype(np.float32)
    csel = (dc[:, :, None] == np.arange(2 * NA_COLS - 1)).astype(np.float32)
    b = jnp.einsum('pja,hab,qkb->hpqjk', rsel, rpb.astype(F32), csel,
                   precision=lax.Precision.HIGHEST)
    b = jnp.where(in_win[None, None, :, None, :], b.astype(F32), NEG_INF)
    return b.reshape(NA_HEADS, NA_ROWS, GRID_W, NA_ROWS * GRID_W)


_GQA_G = GQA_HEADS // GQA_KV_HEADS
_GQA_BAND = 3 * GQA_WINDOW
_GQA_QW = GQA_HEADS * V7X_LANES


def _gqa_prep_kernel(q_ref, k_ref, gq_ref, gk_ref, c_ref, s1_ref, s2_ref, qd_ref, kn_ref):
    c = c_ref[...]
    s1 = s1_ref[...]
    s2 = s2_ref[...]
    lo = lax.broadcasted_iota(jnp.int32, (1, 128), 1) < GQA_HD

    def head_rms(x, g):
        x2 = x * x
        s_lo = jnp.sum(jnp.where(lo, x2, 0.0), axis=-1, keepdims=True)
        s_hi = jnp.sum(jnp.where(lo, 0.0, x2), axis=-1, keepdims=True)
        inv = 1.0 / GQA_HD
        r = jnp.where(lo, lax.rsqrt(s_lo * inv + EPS), lax.rsqrt(s_hi * inv + EPS))
        return x * r * g

    def rot(x):
        return x * c + pltpu.roll(x, 96, 1) * s1 + pltpu.roll(x, 32, 1) * s2

    gq = gq_ref[...]
    for j in range(GQA_HEADS // 2):
        y = rot(head_rms(q_ref[:, j * 128:(j + 1) * 128].astype(F32), gq))
        sw = pltpu.roll(y, 64, 1)
        hk = (2 * j) // _GQA_G
        if hk == 0:
            even, odd = jnp.where(lo, y, 0.0), jnp.where(lo, sw, 0.0)
        else:
            even, odd = jnp.where(lo, 0.0, sw), jnp.where(lo, 0.0, y)
        qd_ref[:, (2 * j) * 128:(2 * j + 1) * 128] = even.astype(BF16)
        qd_ref[:, (2 * j + 1) * 128:(2 * j + 2) * 128] = odd.astype(BF16)
    kn_ref[...] = rot(head_rms(k_ref[...].astype(F32), gk_ref[...])).astype(BF16)


def _gqa_prep(proj, p, tabs):
    tm = TOK_TM
    gq = (jnp.tile(p['gqa_qn_g'], 2) * (GQA_HD ** -0.5)).reshape(1, 128)
    gk = jnp.tile(p['gqa_kn_g'], 2).reshape(1, 128)
    row = pl.BlockSpec((tm, 128), lambda i: (i, 0))
    tab = pl.BlockSpec((tm, 128), lambda i: (_rope_block(i), 0))
    vec = pl.BlockSpec((1, 128), lambda i: (0, 0))
    return pl.pallas_call(
        _gqa_prep_kernel,
        grid=(N_TOK // tm,),
        in_specs=[pl.BlockSpec((tm, GQA_HEADS * GQA_HD), lambda i: (i, _pcol(6, GQA_HEADS * GQA_HD))),
                  pl.BlockSpec((tm, 128), lambda i: (i, _pcol(7, 128))),
                  vec, vec, tab, tab, tab],
        out_specs=[pl.BlockSpec((tm, _GQA_QW), lambda i: (i, 0)), row],
        out_shape=[jax.ShapeDtypeStruct((N_TOK, _GQA_QW), BF16),
                   jax.ShapeDtypeStruct((N_TOK, 128), BF16)],
        compiler_params=_cparams(("arbitrary",)),
        name="gqa_prep",
    )(proj, proj, gq, gk, tabs['gqa_c'], tabs['gqa_s1'], tabs['gqa_s2'])


def _gqa_kernel(sink_ref, q_ref, k_ref, v_ref, kc_ref, vc_ref, o_ref):
    hk = pl.program_id(1)
    n = pl.program_id(2)
    w = GQA_WINDOW
    start = pl.multiple_of(jnp.clip((n - 1) * w, 0, SEQ - _GQA_BAND), w)
    q = jnp.concatenate([q_ref[:, g * 128:(g + 1) * 128] for g in range(_GQA_G)], axis=0)
    k = k_ref[pl.ds(start, _GQA_BAND), :]
    v = v_ref[pl.ds(start, _GQA_BAND), :]
    s = lax.dot_general(q, k, _NT, preferred_element_type=F32)
    rows = lax.broadcasted_iota(jnp.int32, s.shape, 0)
    cols = lax.broadcasted_iota(jnp.int32, s.shape, 1)
    qpos = n * w + (rows & (w - 1))
    kpos = start + cols
    s = jnp.where(jnp.abs(kpos - qpos) <= GQA_WINDOW, s, NEG_INF)
    sc = lax.dot_general(q, kc_ref[...], _NT, preferred_element_type=F32)
    grow = lax.broadcasted_iota(jnp.int32, (_GQA_G * w, 1), 0) // w
    snk = jnp.full((_GQA_G * w, 1), sink_ref[hk * _GQA_G], F32)
    for g in range(1, _GQA_G):
        snk = jnp.where(grow == g, sink_ref[hk * _GQA_G + g], snk)
    m = jnp.maximum(jnp.maximum(jnp.max(s, axis=-1, keepdims=True),
                                jnp.max(sc, axis=-1, keepdims=True)), snk)
    p = jnp.exp(s - m)
    pc = jnp.exp(sc - m)
    l = jnp.sum(p, axis=-1, keepdims=True) + jnp.sum(pc, axis=-1, keepdims=True) + jnp.exp(snk - m)
    o = (jnp.dot(p.astype(BF16), v, preferred_element_type=F32)
         + jnp.dot(pc.astype(BF16), vc_ref[...], preferred_element_type=F32)) / l
    half = lax.broadcasted_iota(jnp.int32, (1, 128), 1) // GQA_HD
    o = jnp.where(half == hk, o, 0.0).astype(o_ref.dtype)
    for g in range(_GQA_G):
        o_ref[:, g * 128:(g + 1) * 128] = o[g * w:(g + 1) * w]


def _gqa_attn(qd, kn, proj, sink):
    nb = SEQ // GQA_WINDOW
    qw = _GQA_G * 128
    cv = _pcol(8, 128)
    gs = pltpu.PrefetchScalarGridSpec(
        num_scalar_prefetch=1,
        grid=(BATCH, GQA_KV_HEADS, nb),
        in_specs=[pl.BlockSpec((GQA_WINDOW, qw), lambda b, h, n, *_: (b * nb + n, h)),
                  pl.BlockSpec((SEQ, 128), lambda b, h, n, *_: (b, 0)),
                  pl.BlockSpec((SEQ, 128), lambda b, h, n, *_: (b, cv)),
                  pl.BlockSpec((CTX_LEN, 128), lambda b, h, n, *_: (CTX_BLK + b, 0)),
                  pl.BlockSpec((CTX_LEN, 128), lambda b, h, n, *_: (CTX_BLK + b, cv))],
        out_specs=pl.BlockSpec((GQA_WINDOW, qw), lambda b, h, n, *_: (b * nb + n, h)),
    )
    return pl.pallas_call(
        _gqa_kernel,
        grid_spec=gs,
        out_shape=jax.ShapeDtypeStruct((N_LAT, _GQA_QW), BF16),
        compiler_params=_cparams(("arbitrary", "arbitrary", "arbitrary")),
        name="gqa_attn",
    )(sink.astype(F32), qd, kn, proj, kn, proj)


def _ctx_kernel(sink_ref, mq_ref, mk_ref, mvt_ref, nq_ref, nk_ref, nv_ref, gnq_ref, gnk_ref,
                gq_ref, gk_ref, gv_ref, oa_ref, ob_ref, oc_ref):
    for h in range(MLA_HEADS):
        st = lax.dot_general(mk_ref[h], mq_ref[h], _NT, preferred_element_type=F32)
        p = jnp.exp2(st - jnp.max(st, axis=0, keepdims=True))
        l = jnp.sum(p, axis=0, keepdims=True)
        ot = jnp.dot(mvt_ref[h], p.astype(BF16), preferred_element_type=F32) / l
        oa_ref[:, h * MLA_V:(h + 1) * MLA_V] = ot.T.astype(oa_ref.dtype)
    for h in range(NA_HEADS):
        sl = slice(h * NA_HD, (h + 1) * NA_HD)
        q = _row_rms(nq_ref[:, sl].astype(F32), gnq_ref[...]).astype(BF16)
        k = _row_rms(nk_ref[:, sl].astype(F32), gnk_ref[...]).astype(BF16)
        s = lax.dot_general(q, k, _NT, preferred_element_type=F32)
        p = jnp.exp(s - jnp.max(s, axis=-1, keepdims=True))
        l = jnp.sum(p, axis=-1, keepdims=True)
        o = jnp.dot(p.astype(BF16), nv_ref[:, sl], preferred_element_type=F32) / l
        ob_ref[:, sl] = o.astype(ob_ref.dtype)
    half = lax.broadcasted_iota(jnp.int32, (1, 128), 1) // GQA_HD
    k = gk_ref[...]
    v = gv_ref[...]
    for h in range(GQA_HEADS):
        sl = slice(h * 128, (h + 1) * 128)
        s = lax.dot_general(gq_ref[:, sl], k, _NT, preferred_element_type=F32)
        snk = sink_ref[h]
        m = jnp.maximum(jnp.max(s, axis=-1, keepdims=True), snk)
        p = jnp.exp(s - m)
        l = jnp.sum(p, axis=-1, keepdims=True) + jnp.exp(snk - m)
        o = jnp.dot(p.astype(BF16), v, preferred_element_type=F32) / l
        oc_ref[:, sl] = jnp.where(half == h // _GQA_G, o, 0.0).astype(oc_ref.dtype)


def _ctx_attn(sink, mq, mk, mvt, proj, gnq, gnk, qd, kn):
    c = CTX_LEN
    nwid = NA_HEADS * NA_HD

    def row(width, col=0):
        return pl.BlockSpec((c, width), lambda b, *_: (CTX_BLK + b, col))

    def out(width):
        return pl.BlockSpec((c, width), lambda b, *_: (b, 0))

    vec = pl.BlockSpec((1, NA_HD), lambda b, *_: (0, 0))
    gs = pltpu.PrefetchScalarGridSpec(
        num_scalar_prefetch=1,
        grid=(BATCH,),
        in_specs=[pl.BlockSpec((MLA_HEADS, c, _MLA_PAD), lambda b, *_: (0, CTX_BLK + b, 0)),
                  pl.BlockSpec((MLA_HEADS, c, _MLA_PAD), lambda b, *_: (0, CTX_BLK + b, 0)),
                  pl.BlockSpec((MLA_HEADS, MLA_V, c), lambda b, *_: (0, 0, CTX_BLK + b)),
                  row(nwid, _pcol(3, nwid)), row(nwid, _pcol(4, nwid)), row(nwid, _pcol(5, nwid)),
                  vec, vec,
                  row(_GQA_QW), row(128), row(128, _pcol(8, 128))],
        out_specs=[out(MLA_HEADS * MLA_V), out(nwid), out(_GQA_QW)],
    )
    return pl.pallas_call(
        _ctx_kernel,
        grid_spec=gs,
        out_shape=[jax.ShapeDtypeStruct((N_CTX, MLA_HEADS * MLA_V), BF16),
                   jax.ShapeDtypeStruct((N_CTX, nwid), BF16),
                   jax.ShapeDtypeStruct((N_CTX, _GQA_QW), BF16)],
        compiler_params=_cparams(("arbitrary",)),
        name="ctx_attn",
    )(sink.astype(F32), mq, mk, mvt, proj, proj, proj, gnq, gnk, qd, kn, proj)


def _merge_kernel(*refs, has_ctx, n_lat_tiles):
    ga_ref, gb_ref, gc_ref, oa_ref, ob_ref, oc_ref = refs[:6]
    refs = refs[6:]
    if has_ctx:
        ca_ref, cb_ref, cc_ref = refs[:3]
        refs = refs[3:]
    wa_ref, wb_ref, wc_ref, y_ref = refs
    is_ctx = pl.program_id(0) >= n_lat_tiles

    def branch(g_ref, o_ref, c_ref, w_ref):
        o = o_ref[...]
        if has_ctx:
            o = jnp.where(is_ctx, c_ref[...], o)
        return jax.nn.sigmoid(g_ref[...].astype(F32)) * jnp.dot(o, w_ref[...], preferred_element_type=F32)

    y = (branch(ga_ref, oa_ref, ca_ref if has_ctx else None, wa_ref)
         + branch(gb_ref, ob_ref, cb_ref if has_ctx else None, wb_ref)
         + branch(gc_ref, oc_ref, cc_ref if has_ctx else None, wc_ref))
    y_ref[...] = y.astype(y_ref.dtype)


def _merge(proj, lat, ctx, wa, wb, wc):
    tm = TOK_TM
    has_ctx = ctx is not None
    m = N_TOK if has_ctx else N_LAT
    nl = N_LAT // tm
    ks = [o.shape[1] for o in lat]
    in_specs = [pl.BlockSpec((tm, D_MODEL), lambda i: (i, 0)),
                pl.BlockSpec((tm, D_MODEL), lambda i: (i, 1)),
                pl.BlockSpec((tm, D_MODEL), lambda i: (i, 2))]
    in_specs += [pl.BlockSpec((tm, k), lambda i: (jnp.minimum(i, nl - 1), 0)) for k in ks]
    args = [proj, proj, proj] + list(lat)
    if has_ctx:
        in_specs += [pl.BlockSpec((tm, k), lambda i: (jnp.maximum(i - nl, 0), 0)) for k in ks]
        args += list(ctx)
    in_specs += [pl.BlockSpec((k, D_MODEL), lambda i: (0, 0)) for k in ks]
    args += [wa, wb, wc]
    return pl.pallas_call(
        functools.partial(_merge_kernel, has_ctx=has_ctx, n_lat_tiles=nl),
        grid=(m // tm,),
        in_specs=in_specs,
        out_specs=pl.BlockSpec((tm, D_MODEL), lambda i: (i, 0)),
        out_shape=jax.ShapeDtypeStruct((m, D_MODEL), BF16),
        compiler_params=_cparams(("arbitrary",)),
        name="merge",
    )(*args)


def _moe_ffn_kernel(te_ref, tok_ref, nu_ref, en_ref, es_ref, h_hbm, wg_hbm, wu_hbm, wd_hbm, y_ref,
                    xbuf, sem, wgs, wus, wds, wsem, wgb, wub, wdb, *, layer):
    i = pl.program_id(0)
    tm = MOE_TM
    slot = i % 2
    n_used = nu_ref[0]

    def row_copy(tok, s, r):
        return pltpu.make_async_copy(h_hbm.at[pl.ds(tok, 1)], xbuf.at[s, pl.ds(r, 1)], sem.at[s])

    def start_gather(tile, s):
        base = tile * tm

        def body(r, c):
            row_copy(tok_ref[base + r], s, r).start()
            return c

        lax.fori_loop(0, tm, body, 0, unroll=8)

    def weight_copies(e, s):
        return (pltpu.make_async_copy(wg_hbm.at[layer, e], wgs.at[s], wsem.at[s]),
                pltpu.make_async_copy(wu_hbm.at[layer, e], wus.at[s], wsem.at[s]),
                pltpu.make_async_copy(wd_hbm.at[layer, e], wds.at[s], wsem.at[s]))

    @pl.when(i == 0)
    def _():
        for cp in weight_copies(te_ref[0], es_ref[0]):
            cp.start(priority=1)
        start_gather(0, 0)

    @pl.when(i + 1 < n_used)
    def _():
        start_gather(i + 1, 1 - slot)

    @pl.when(i < n_used)
    def _():
        @pl.when((i == 0) | (te_ref[i] != te_ref[jnp.maximum(i - 1, 0)]))
        def _():
            s = es_ref[i]

            @pl.when(en_ref[i] >= 0)
            def _():
                for cp in weight_copies(en_ref[i], 1 - s):
                    cp.start(priority=1)

            for cp in weight_copies(te_ref[i], s):
                cp.wait()
            wgb[...] = wgs[s].astype(BF16)
            wub[...] = wus[s].astype(BF16)
            wdb[...] = wds[s].astype(BF16)

        pltpu.make_async_copy(h_hbm.at[pl.ds(0, tm)], xbuf.at[slot], sem.at[slot]).wait()
        x = xbuf[slot].astype(BF16)
        hg = jnp.dot(x, wgb[...], preferred_element_type=F32)
        hu = jnp.dot(x, wub[...], preferred_element_type=F32)
        act = (hg * jax.nn.sigmoid(hg)) * hu
        y_ref[...] = jnp.dot(act.astype(BF16), wdb[...], preferred_element_type=F32)

    @pl.when(i >= n_used)
    def _():
        y_ref[...] = jnp.zeros_like(y_ref)


def _moe_ffn(h, plan, wg, wu, wd, layer):
    tile_expert, slot_token, n_used, next_expert, expert_slot = plan
    p = slot_token.shape[0]
    tm = MOE_TM
    nt = p // tm
    anyspec = pl.BlockSpec(memory_space=pl.ANY)
    gs = pltpu.PrefetchScalarGridSpec(
        num_scalar_prefetch=5,
        grid=(nt,),
        in_specs=[anyspec, anyspec, anyspec, anyspec],
        out_specs=pl.BlockSpec((tm, D_MODEL), lambda i, *_: (i, 0)),
        scratch_shapes=[pltpu.VMEM((2, tm, D_MODEL), F32),
                        pltpu.SemaphoreType.DMA((2,)),
                        pltpu.VMEM((2, D_MODEL, MOE_HIDDEN), F32),
                        pltpu.VMEM((2, D_MODEL, MOE_HIDDEN), F32),
                        pltpu.VMEM((2, MOE_HIDDEN, D_MODEL), F32),
                        pltpu.SemaphoreType.DMA((2,)),
                        pltpu.VMEM((D_MODEL, MOE_HIDDEN), BF16),
                        pltpu.VMEM((D_MODEL, MOE_HIDDEN), BF16),
                        pltpu.VMEM((MOE_HIDDEN, D_MODEL), BF16)],
    )
    return pl.pallas_call(
        functools.partial(_moe_ffn_kernel, layer=layer),
        grid_spec=gs,
        out_shape=jax.ShapeDtypeStruct((p, D_MODEL), F32),
        compiler_params=_cparams(("arbitrary",)),
        name="moe_ffn",
    )(tile_expert, slot_token, n_used, next_expert, expert_slot, h, wg, wu, wd)


def _moe_combine_kernel(pos_ref, y_hbm, x_ref, w_ref, g_ref, o_ref, ybuf, sem):
    i = pl.program_id(0)
    nt = pl.num_programs(0)
    tm = TOK_TM
    slot = i % 2

    def row_copy(src, s, r):
        return pltpu.make_async_copy(y_hbm.at[pl.ds(src, 1)], ybuf.at[s, pl.ds(r, 1)], sem.at[s])

    def start_gather(tile, s):
        base = tile * tm

        def body(r, c):
            row_copy(pos_ref[2 * (base + r)], s, r).start(priority=0)
            row_copy(pos_ref[2 * (base + r) + 1], s, tm + r).start(priority=1)
            return c

        lax.fori_loop(0, tm, body, 0, unroll=4)

    @pl.when(i == 0)
    def _():
        start_gather(0, 0)

    @pl.when(i + 1 < nt)
    def _():
        start_gather(i + 1, 1 - slot)

    pltpu.make_async_copy(y_hbm.at[pl.ds(0, 2 * tm)], ybuf.at[slot], sem.at[slot]).wait()
    w = w_ref[...]
    y = ybuf[slot, pl.ds(0, tm), :] * w[:, 0:1] + ybuf[slot, pl.ds(tm, tm), :] * w[:, 1:2]
    o_ref[...] = x_ref[...] + g_ref[0] * y


def _moe_combine(y, pos, w_sel, x, mod, k_gate):
    m = x.shape[0]
    tm = TOK_TM
    gs = pltpu.PrefetchScalarGridSpec(
        num_scalar_prefetch=1,
        grid=(m // tm,),
        in_specs=[pl.BlockSpec(memory_space=pl.ANY),
                  pl.BlockSpec((tm, D_MODEL), lambda i, pos: (i, 0)),
                  pl.BlockSpec((tm, MOE_TOPK), lambda i, pos: (i, 0)),
                  pl.BlockSpec((1, 1, D_MODEL), lambda i, pos: (_seg(i, tm) * 6 + k_gate, 0, 0))],
        out_specs=pl.BlockSpec((tm, D_MODEL), lambda i, pos: (i, 0)),
        scratch_shapes=[pltpu.VMEM((2, 2 * tm, D_MODEL), F32),
                        pltpu.SemaphoreType.DMA((2,))],
    )
    return pl.pallas_call(
        _moe_combine_kernel,
        grid_spec=gs,
        out_shape=jax.ShapeDtypeStruct((m, D_MODEL), F32),
        compiler_params=_cparams(("arbitrary",)),
        name="moe_combine",
    )(pos, y, x, w_sel, mod)


def _route(logits, m):
    tm = MOE_TM
    gp = jax.nn.softmax(logits[:, :MOE_GROUPS], axis=-1)
    g_idx = jnp.argmax(gp, axis=-1).astype(jnp.int32)[:, None]
    g_w = jnp.max(gp, axis=-1, keepdims=True)
    el = logits[:, MOE_GROUPS:MOE_GROUPS + MOE_EXPERTS].reshape(m, MOE_GROUPS, MOE_PER_GROUP)
    g_onehot = (g_idx == jnp.arange(MOE_GROUPS, dtype=jnp.int32)[None, :]).astype(F32)
    el_g = jnp.sum(el * g_onehot[:, :, None], axis=1)
    i0 = jnp.argmax(el_g, axis=-1).astype(jnp.int32)[:, None]
    l0 = jnp.max(el_g, axis=-1, keepdims=True)
    rest = jnp.where(jnp.arange(MOE_PER_GROUP, dtype=jnp.int32)[None, :] == i0, -jnp.inf, el_g)
    i1 = jnp.argmax(rest, axis=-1).astype(jnp.int32)[:, None]
    l1 = jnp.max(rest, axis=-1, keepdims=True)
    top_l = jnp.concatenate([l0, l1], axis=-1)
    top_i = jnp.concatenate([i0, i1], axis=-1)
    w_sel = jax.nn.softmax(top_l, axis=-1) * g_w
    eid = (g_idx * MOE_PER_GROUP + top_i).astype(jnp.int32)

    a = m * MOE_TOPK
    e_flat = eid.reshape(a)
    onehot = (e_flat[:, None] == jnp.arange(MOE_EXPERTS, dtype=jnp.int32)[None, :]).astype(jnp.int32)
    csum = jnp.cumsum(onehot, axis=0)
    rank = jnp.sum(csum * onehot, axis=1) - 1
    counts = csum[-1]
    padded = ((counts + tm - 1) // tm) * tm
    ends = jnp.cumsum(padded)
    starts = ends - padded
    pos = (jnp.sum(onehot * starts[None, :], axis=1) + rank).astype(jnp.int32)
    p = a + MOE_EXPERTS * tm
    slot_token = jnp.zeros((p,), jnp.int32).at[pos].set(jnp.arange(a, dtype=jnp.int32) // MOE_TOPK)
    n_used = (ends[-1] // tm).astype(jnp.int32).reshape(1)
    tile_start = jnp.arange(p // tm, dtype=jnp.int32) * tm
    last_e = jnp.max(jnp.where(counts > 0, jnp.arange(MOE_EXPERTS, dtype=jnp.int32), 0))
    tile_expert = jnp.minimum(
        jnp.sum((ends[None, :] <= tile_start[:, None]).astype(jnp.int32), axis=1), last_e)
    eidx = jnp.arange(MOE_EXPERTS, dtype=jnp.int32)
    used = counts > 0
    later = used[None, :] & (eidx[None, :] > eidx[:, None])
    next_used = jnp.min(jnp.where(later, eidx[None, :], MOE_EXPERTS), axis=1)
    next_used = jnp.where(next_used == MOE_EXPERTS, -1, next_used).astype(jnp.int32)
    ordinal = (jnp.cumsum(used.astype(jnp.int32)) - 1) % 2
    t_onehot = (tile_expert[:, None] == eidx[None, :]).astype(jnp.int32)
    next_expert = jnp.sum(t_onehot * next_used[None, :], axis=1).astype(jnp.int32)
    expert_slot = jnp.sum(t_onehot * ordinal[None, :], axis=1).astype(jnp.int32)
    return (tile_expert, slot_token, n_used, next_expert, expert_slot), w_sel, pos


def _rope_angles(rot_dim):
    t = np.arange(SEQ)
    row = (t // GRID_W).astype(np.float64)
    col = (t % GRID_W).astype(np.float64)
    n_freq = rot_dim // 4
    inv = ROPE_THETA ** (-np.arange(n_freq, dtype=np.float64) / n_freq)
    ang = np.concatenate([row[:, None] * inv, col[:, None] * inv], axis=-1)
    return np.cos(ang), np.sin(ang)


def _rope_tables():
    def rows(lat, ident):
        ctx = np.zeros((TOK_TM, 128)) + ident
        return jnp.asarray(np.concatenate([lat, ctx], axis=0).astype(np.float32))

    z32 = np.zeros((SEQ, 32))
    z64 = np.zeros((SEQ, 64))
    lane = np.arange(128)
    cm, sm = _rope_angles(MLA_ROPE)
    mla_c = rows(np.concatenate([cm, cm, z64], axis=1), (lane < 64).astype(np.float64))
    mla_s = rows(np.concatenate([-sm, sm, z64], axis=1), 0.0)
    cg, sg = _rope_angles(GQA_HD)
    gqa_c = rows(np.concatenate([cg, cg, cg, cg], axis=1), 1.0)
    gqa_s1 = rows(np.concatenate([-sg, z32, -sg, z32], axis=1), 0.0)
    gqa_s2 = rows(np.concatenate([z32, sg, z32, sg], axis=1), 0.0)
    return {'mla_c': mla_c, 'mla_s': mla_s, 'gqa_c': gqa_c, 'gqa_s1': gqa_s1, 'gqa_s2': gqa_s2}


def _rope_block(i):
    per_seq = SEQ // TOK_TM
    return jnp.where(i < BATCH * per_seq, i % per_seq, per_seq)


def _pack_w_in_kernel(w_ref, o_ref):
    bounds = (0,) + IN_SPLITS + (sum(IN_SIZES),)
    off = 0
    for i in _PACK_ORDER:
        size = bounds[i + 1] - bounds[i]
        o_ref[0, off:off + size, :] = w_ref[0, bounds[i]:bounds[i + 1], :].astype(BF16)
        off += size
    o_ref[0, off:, :] = jnp.zeros((_PACK_COLS - off, o_ref.shape[2]), BF16)


def _pack_w_in(w_in):
    tk = 256
    n_in = sum(IN_SIZES)
    wt = jnp.swapaxes(w_in, 1, 2)
    return pl.pallas_call(
        _pack_w_in_kernel,
        grid=(DEPTH, D_MODEL // tk),
        in_specs=[pl.BlockSpec((1, n_in, tk), lambda l, i: (l, 0, i))],
        out_specs=pl.BlockSpec((1, _PACK_COLS, tk), lambda l, i: (l, 0, i)),
        out_shape=jax.ShapeDtypeStruct((DEPTH, _PACK_COLS, D_MODEL), BF16),
        compiler_params=_cparams(("arbitrary", "arbitrary")),
        name="pack_w_in",
    )(wt)


def _pad_w_o_gqa(w):
    w4 = w.reshape(GQA_KV_HEADS, _GQA_G, GQA_HD, D_MODEL)
    z = jnp.zeros_like(w4[0:1])
    halves = [jnp.concatenate([w4[hk:hk + 1] if hk == half else z for hk in range(GQA_KV_HEADS)], axis=0)
              for half in range(GQA_KV_HEADS)]
    return jnp.stack(halves, axis=2).reshape(_GQA_QW, D_MODEL).astype(BF16)


def _token_mixer(h, p, tabs, ctx_out):
    proj = _mm(h, p['w_in_packed'], p['layer'], BF16, _PACK_COLS // 4)
    mq, mk, mvt = _mla_prep(proj, p, tabs)
    oa = _mla_attn(mq, mk, mvt, tq=1024, tk=1024)
    gnq = (p['na_qn_g'] * (NA_HD ** -0.5)).reshape(1, NA_HD)
    gnk = p['na_kn_g'].reshape(1, NA_HD)
    ob = _na_attn(proj, _na_bias_table(p['na_rpb']), gnq, gnk)
    qd, kn = _gqa_prep(proj, p, tabs)
    oc = _gqa_attn(qd, kn, proj, p['gqa_sink'])
    ctx = _ctx_attn(p['gqa_sink'], mq, mk, mvt, proj, gnq, gnk, qd, kn) if ctx_out else None
    return _merge(proj, (oa, ob, oc), ctx, p['w_o_mla'].astype(BF16), p['w_o_na'].astype(BF16),
                  _pad_w_o_gqa(p['w_o_gqa']))


def _post_mixer(y, xt, mod, norm_g, w_out_l, p):
    m = y.shape[0]
    wr = jnp.concatenate([p['moe_w_group'], p['moe_w_expert'],
                          jnp.zeros((D_MODEL, ROUTE_COLS - MOE_GROUPS - MOE_EXPERTS), F32)], axis=1)
    br = jnp.concatenate([p['moe_b_group'], p['moe_b_expert'],
                          jnp.zeros((ROUTE_COLS - MOE_GROUPS - MOE_EXPERTS,), F32)]).reshape(1, ROUTE_COLS)
    x_mid, h, logits = _out_route(y, w_out_l.astype(BF16), xt, mod, norm_g, wr, br)
    plan, w_sel, pos = _route(logits, m)
    yy = _moe_ffn(h, plan, p['moe_w_gate'], p['moe_w_up'], p['moe_w_down'], p['layer'])
    return _moe_combine(yy, pos, w_sel, x_mid, mod, 5)


def kernel(x, c, ctx, c_ctx, ada_w, ada_b, norm_mix_g, norm_ffn_g, w_in,
           mla_q_norm_g, mla_w_uq, mla_kv_norm_g, mla_w_ukv, mla_qn_g, mla_kn_g,
           na_qn_g, na_kn_g, na_rpb, gqa_qn_g, gqa_kn_g, gqa_sink,
           w_o_mla, w_o_na, w_o_gqa, w_out,
           moe_w_group, moe_b_group, moe_w_expert, moe_b_expert,
           moe_w_gate, moe_w_up, moe_w_down):
    xt = (x.reshape(N_LAT, D_MODEL), ctx.reshape(N_CTX, D_MODEL))
    c_rows = jnp.concatenate([c, c_ctx[None, :], jnp.zeros((8 - BATCH - 1, D_MODEL), F32)], axis=0)
    mod_all = _ada(c_rows, ada_w, ada_b)
    tabs = _rope_tables()
    w_in_packed = _pack_w_in(w_in)
    for l in range(DEPTH):
        ctx_out = l < DEPTH - 1
        p = {
            'w_in_packed': w_in_packed, 'mla_q_norm_g': mla_q_norm_g[l], 'mla_w_uq': mla_w_uq[l],
            'mla_kv_norm_g': mla_kv_norm_g[l], 'mla_w_ukv': mla_w_ukv[l],
            'mla_qn_g': mla_qn_g[l], 'mla_kn_g': mla_kn_g[l],
            'na_qn_g': na_qn_g[l], 'na_kn_g': na_kn_g[l], 'na_rpb': na_rpb[l],
            'gqa_qn_g': gqa_qn_g[l], 'gqa_kn_g': gqa_kn_g[l], 'gqa_sink': gqa_sink[l],
            'w_o_mla': w_o_mla[l], 'w_o_na': w_o_na[l], 'w_o_gqa': w_o_gqa[l],
            'moe_w_group': moe_w_group[l], 'moe_b_group': moe_b_group[l],
            'moe_w_expert': moe_w_expert[l], 'moe_b_expert': moe_b_expert[l],
            'moe_w_gate': moe_w_gate, 'moe_w_up': moe_w_up, 'moe_w_down': moe_w_down, 'layer': l,
        }
        mod = mod_all[l].reshape(8 * 6, 1, D_MODEL)
        h = _norm_mod(xt, norm_mix_g[l], mod, 0, 1)
        y = _token_mixer(h, p, tabs, ctx_out)
        xt = _post_mixer(y, xt, mod, norm_ffn_g[l], w_out[l], p)
    return xt[:N_LAT].reshape(BATCH, SEQ, D_MODEL)
```

```python
import functools

import numpy as np
import jax
import jax.numpy as jnp
from jax import lax
from jax.experimental import pallas as pl
from jax.experimental.pallas import tpu as pltpu

D_MODEL = 2048
BATCH = 2
SEQ = 4096
DEPTH = 2
GRID_W = 64
CTX_LEN = 256
EPS = 1e-6
ROPE_THETA = 10000.0
NEG_INF = -1e30

MLA_HEADS = 8
MLA_Q_RANK = 512
MLA_KV_RANK = 512
MLA_NOPE = 128
MLA_ROPE = 64
MLA_QK = MLA_NOPE + MLA_ROPE
MLA_V = 128
NA_HEADS = 4
NA_HD = 128
NA_ROWS = 8
NA_COLS = 16
GQA_HEADS = 8
GQA_KV_HEADS = 2
GQA_HD = 64
GQA_WINDOW = 128
MOE_GROUPS = 4
MOE_PER_GROUP = 8
MOE_EXPERTS = MOE_GROUPS * MOE_PER_GROUP
MOE_TOPK = 2
MOE_HIDDEN = 512

IN_SIZES = (MLA_Q_RANK, MLA_KV_RANK, MLA_ROPE,
            NA_HEADS * NA_HD, NA_HEADS * NA_HD, NA_HEADS * NA_HD,
            GQA_HEADS * GQA_HD, GQA_KV_HEADS * GQA_HD, GQA_KV_HEADS * GQA_HD,
            D_MODEL, D_MODEL, D_MODEL)
IN_SPLITS = tuple(int(s) for s in np.cumsum(IN_SIZES)[:-1])

N_LAT = BATCH * SEQ
N_CTX = BATCH * CTX_LEN
N_TOK = N_LAT + N_CTX
ROWS = SEQ // GRID_W

V7X_LANES = 128
V7X_VMEM_LIMIT = 56 * 1024 * 1024

_PACK_ORDER = (9, 10, 11, 0, 1, 3, 4, 5, 6, 7, 8, 2)
_PACK_COLS = 9728
_PACK_OFF = {}
_off = 0
for _i in _PACK_ORDER:
    _PACK_OFF[_i] = _off
    _off += IN_SIZES[_i]

ROUTE_COLS = V7X_LANES
MOE_TM = 256
TOK_TM = 256
MM_TM = 512
CTX_BLK = N_LAT // CTX_LEN

F32 = jnp.float32
BF16 = jnp.bfloat16
_LOG2E = 1.4426950408889634
_NT = (((1,), (1,)), ((), ()))


def _cparams(sem):
    return pltpu.CompilerParams(dimension_semantics=sem, vmem_limit_bytes=V7X_VMEM_LIMIT)


def _seg(i, tm):
    return jnp.minimum(i // (SEQ // tm), 2)


def _pcol(idx, width):
    assert _PACK_OFF[idx] % width == 0
    return _PACK_OFF[idx] // width


def _ada_kernel(ct_ref, w_ref, b_ref, o_ref):
    ct = ct_ref[...]
    act = ct * jax.nn.sigmoid(ct)
    w = w_ref[0]
    rows = [jnp.sum(act[:, r:r + 1] * w, axis=0, keepdims=True) for r in range(BATCH + 1)]
    rows.append(jnp.zeros((8 - len(rows), w.shape[1]), F32))
    o_ref[0] = jnp.concatenate(rows, axis=0) + b_ref[0]


def _ada(c_rows, ada_w, ada_b):
    tn = 1024
    n = 6 * D_MODEL
    return pl.pallas_call(
        _ada_kernel,
        grid=(DEPTH, n // tn),
        in_specs=[
            pl.BlockSpec((D_MODEL, 8), lambda l, j: (0, 0)),
            pl.BlockSpec((1, D_MODEL, tn), lambda l, j: (l, 0, j)),
            pl.BlockSpec((1, 1, tn), lambda l, j: (l, 0, j)),
        ],
        out_specs=pl.BlockSpec((1, 8, tn), lambda l, j: (l, 0, j)),
        out_shape=jax.ShapeDtypeStruct((DEPTH, 8, n), F32),
        compiler_params=_cparams(("arbitrary", "arbitrary")),
        name="ada",
    )(c_rows.T, ada_w, ada_b.reshape(DEPTH, 1, n))


def _stream_specs(xs):
    tm = TOK_TM
    if not isinstance(xs, tuple):
        return [pl.BlockSpec((tm, D_MODEL), lambda i: (i, 0))], [xs], xs.shape[0]
    nl = N_LAT // tm
    return ([pl.BlockSpec((tm, D_MODEL), lambda i: (jnp.minimum(i, nl - 1), 0)),
             pl.BlockSpec((tm, D_MODEL), lambda i: (jnp.maximum(i - nl, 0), 0))],
            list(xs), xs[0].shape[0] + xs[1].shape[0])


def _stream_tile(refs):
    if len(refs) == 1:
        return refs[0][...]
    return jnp.where(pl.program_id(0) >= N_LAT // TOK_TM, refs[1][...], refs[0][...])


def _norm_mod_kernel(*refs, n_src):
    g_ref, sh_ref, sc_ref, h_ref = refs[n_src:]
    x = _stream_tile(refs[:n_src])
    xn = x * lax.rsqrt(jnp.mean(x * x, axis=-1, keepdims=True) + EPS) * g_ref[...]
    h_ref[...] = (xn * (1.0 + sc_ref[0]) + sh_ref[0]).astype(h_ref.dtype)


def _norm_mod(xs, g, mod, k_shift, k_scale):
    tm = TOK_TM
    specs, args, m = _stream_specs(xs)
    return pl.pallas_call(
        functools.partial(_norm_mod_kernel, n_src=len(args)),
        grid=(m // tm,),
        in_specs=specs + [
            pl.BlockSpec((1, D_MODEL), lambda i: (0, 0)),
            pl.BlockSpec((1, 1, D_MODEL), lambda i: (_seg(i, tm) * 6 + k_shift, 0, 0)),
            pl.BlockSpec((1, 1, D_MODEL), lambda i: (_seg(i, tm) * 6 + k_scale, 0, 0))],
        out_specs=pl.BlockSpec((tm, D_MODEL), lambda i: (i, 0)),
        out_shape=jax.ShapeDtypeStruct((m, D_MODEL), BF16),
        compiler_params=_cparams(("arbitrary",)),
        name="norm_mod",
    )(*args, g.reshape(1, D_MODEL), mod, mod)


def _out_route_kernel(*refs, n_src):
    y_ref, w_ref = refs[:2]
    g1_ref, gn_ref, sh_ref, sc_ref, wr_ref, br_ref, xo_ref, h_ref, lg_ref = refs[2 + n_src:]
    acc = jnp.dot(y_ref[...], w_ref[...], preferred_element_type=F32)
    x = _stream_tile(refs[2:2 + n_src]) + g1_ref[0] * acc
    xo_ref[...] = x
    xn = x * lax.rsqrt(jnp.mean(x * x, axis=-1, keepdims=True) + EPS) * gn_ref[...]
    h = xn * (1.0 + sc_ref[0]) + sh_ref[0]
    h_ref[...] = h
    hi = h.astype(BF16)
    lo = (h - hi.astype(F32)).astype(BF16)
    wr = wr_ref[...]
    w_hi = wr.astype(BF16)
    w_lo = (wr - w_hi.astype(F32)).astype(BF16)
    lg_ref[...] = (jnp.dot(hi, w_hi, preferred_element_type=F32)
                   + jnp.dot(hi, w_lo, preferred_element_type=F32)
                   + jnp.dot(lo, w_hi, preferred_element_type=F32) + br_ref[...])


def _out_route(y, w_out_bf16, res, mod, norm_g, wr, br):
    m = y.shape[0]
    tm = TOK_TM

    def modrow(k):
        return pl.BlockSpec((1, 1, D_MODEL), lambda i: (_seg(i, tm) * 6 + k, 0, 0))

    row = pl.BlockSpec((tm, D_MODEL), lambda i: (i, 0))
    res_specs, res_args, _ = _stream_specs(res)
    return pl.pallas_call(
        functools.partial(_out_route_kernel, n_src=len(res_args)),
        grid=(m // tm,),
        in_specs=[row,
                  pl.BlockSpec((D_MODEL, D_MODEL), lambda i: (0, 0))] + res_specs + [
                  modrow(2),
                  pl.BlockSpec((1, D_MODEL), lambda i: (0, 0)),
                  modrow(3), modrow(4),
                  pl.BlockSpec((D_MODEL, ROUTE_COLS), lambda i: (0, 0)),
                  pl.BlockSpec((1, ROUTE_COLS), lambda i: (0, 0))],
        out_specs=[row, row, pl.BlockSpec((tm, ROUTE_COLS), lambda i: (i, 0))],
        out_shape=[jax.ShapeDtypeStruct((m, D_MODEL), F32),
                   jax.ShapeDtypeStruct((m, D_MODEL), F32),
                   jax.ShapeDtypeStruct((m, ROUTE_COLS), F32)],
        compiler_params=_cparams(("arbitrary",)),
        name="out_route",
    )(y, w_out_bf16, *res_args, mod, norm_g.reshape(1, D_MODEL), mod, mod, wr, br)


def _mm_bf16_kernel(x_ref, w_ref, o_ref):
    o_ref[...] = lax.dot_general(x_ref[...], w_ref[0], _NT,
                                 preferred_element_type=F32).astype(o_ref.dtype)


def _mm(x, wt, layer, out_dtype, tn):
    m, k = x.shape
    n = wt.shape[1]
    tm = MM_TM
    assert x.dtype == BF16 and wt.dtype == BF16
    w = wt
    return pl.pallas_call(
        _mm_bf16_kernel,
        grid=(n // tn, m // tm),
        in_specs=[pl.BlockSpec((tm, k), lambda j, i: (i, 0)),
                  pl.BlockSpec((1, tn, k), lambda j, i: (layer, j, 0))],
        out_specs=pl.BlockSpec((tm, tn), lambda j, i: (i, j)),
        out_shape=jax.ShapeDtypeStruct((m, n), out_dtype),
        compiler_params=_cparams(("arbitrary", "arbitrary")),
        name="mm_bf16",
    )(x, w)


def _row_rms(x, g):
    return x * lax.rsqrt(jnp.mean(x * x, axis=-1, keepdims=True) + EPS) * g


_MLA_PAD = 2 * V7X_LANES


def _mla_prep_kernel(cq_ref, ckv_ref, kr_ref, wq_ref, wk_ref, wvt_ref, gqi_ref, gkvi_ref,
                     gq_ref, gkn_ref, gkr_ref, c_ref, s_ref, q_ref, k_ref, vt_ref):
    c = c_ref[...]
    s = s_ref[...]
    hw = MLA_HEADS * MLA_NOPE

    def rot(t):
        return t * c + (pltpu.roll(t, 32, 1) + pltpu.roll(t, 96, 1)) * s

    cqn = _row_rms(cq_ref[...].astype(F32), gqi_ref[...]).astype(BF16)
    qf = jnp.dot(cqn, wq_ref[...], preferred_element_type=F32)
    gq = gq_ref[...]
    inv = 1.0 / MLA_QK
    for h in range(MLA_HEADS):
        nope = qf[:, h * 128:(h + 1) * 128]
        t = qf[:, hw + h * 128:hw + (h + 1) * 128]
        ss = jnp.sum(nope * nope, axis=-1, keepdims=True) + jnp.sum(t * t, axis=-1, keepdims=True)
        r = lax.rsqrt(ss * inv + EPS)
        q_ref[h, :, 0:128] = (nope * r * gq[:, 0:128]).astype(BF16)
        q_ref[h, :, 128:256] = rot(t * r * gq[:, 128:256]).astype(BF16)

    ckvn = _row_rms(ckv_ref[...].astype(F32), gkvi_ref[...]).astype(BF16)
    kvf = jnp.dot(ckvn, wk_ref[...], preferred_element_type=F32)
    kr = kr_ref[...].astype(F32)
    ssr = jnp.sum(kr * kr, axis=-1, keepdims=True)
    yrot = rot(kr * gkr_ref[...])
    gkn = gkn_ref[...]
    for h in range(MLA_HEADS):
        nope = kvf[:, h * 128:(h + 1) * 128]
        r = lax.rsqrt((jnp.sum(nope * nope, axis=-1, keepdims=True) + ssr) * inv + EPS)
        k_ref[h, :, 0:128] = (nope * r * gkn).astype(BF16)
        k_ref[h, :, 128:256] = (yrot * r).astype(BF16)
        vt_ref[h] = lax.dot_general(wvt_ref[h * MLA_V:(h + 1) * MLA_V, :], ckvn, _NT,
                                    preferred_element_type=F32).astype(BF16)


def _mla_prep(proj, p, tabs):
    tm = TOK_TM
    hw = MLA_HEADS * MLA_NOPE
    wq = p['mla_w_uq'].reshape(MLA_Q_RANK, MLA_HEADS, MLA_QK)
    wq_rope = jnp.pad(wq[:, :, MLA_NOPE:], ((0, 0), (0, 0), (0, 128 - MLA_ROPE)))
    wq = jnp.concatenate([wq[:, :, :MLA_NOPE].reshape(MLA_Q_RANK, hw),
                          wq_rope.reshape(MLA_Q_RANK, hw)], axis=1).astype(BF16)
    wkv = p['mla_w_ukv'].reshape(MLA_KV_RANK, MLA_HEADS, MLA_NOPE + MLA_V)
    wk = wkv[:, :, :MLA_NOPE].reshape(MLA_KV_RANK, hw).astype(BF16)
    wvt = jnp.transpose(wkv[:, :, MLA_NOPE:], (1, 2, 0)).reshape(MLA_HEADS * MLA_V, MLA_KV_RANK).astype(BF16)
    zpad = jnp.zeros((128 - MLA_ROPE,), F32)
    gq = (jnp.concatenate([p['mla_qn_g'], zpad]) * (MLA_QK ** -0.5 * _LOG2E)).reshape(1, _MLA_PAD)
    gkn = p['mla_kn_g'][:MLA_NOPE].reshape(1, 128)
    gkr = jnp.concatenate([p['mla_kn_g'][MLA_NOPE:], zpad]).reshape(1, 128)

    def const(shape):
        return pl.BlockSpec(shape, lambda i: (0,) * len(shape))

    return pl.pallas_call(
        _mla_prep_kernel,
        grid=(N_TOK // tm,),
        in_specs=[pl.BlockSpec((tm, MLA_Q_RANK), lambda i: (i, _pcol(0, MLA_Q_RANK))),
                  pl.BlockSpec((tm, MLA_KV_RANK), lambda i: (i, _pcol(1, MLA_KV_RANK))),
                  pl.BlockSpec((tm, 128), lambda i: (i, _pcol(2, 128))),
                  const((MLA_Q_RANK, 2 * hw)), const((MLA_KV_RANK, hw)),
                  const((MLA_HEADS * MLA_V, MLA_KV_RANK)),
                  const((1, MLA_Q_RANK)), const((1, MLA_KV_RANK)),
                  const((1, _MLA_PAD)), const((1, 128)), const((1, 128)),
                  pl.BlockSpec((tm, 128), lambda i: (_rope_block(i), 0)),
                  pl.BlockSpec((tm, 128), lambda i: (_rope_block(i), 0))],
        out_specs=[pl.BlockSpec((MLA_HEADS, tm, _MLA_PAD), lambda i: (0, i, 0)),
                   pl.BlockSpec((MLA_HEADS, tm, _MLA_PAD), lambda i: (0, i, 0)),
                   pl.BlockSpec((MLA_HEADS, MLA_V, tm), lambda i: (0, 0, i))],
        out_shape=[jax.ShapeDtypeStruct((MLA_HEADS, N_TOK, _MLA_PAD), BF16),
                   jax.ShapeDtypeStruct((MLA_HEADS, N_TOK, _MLA_PAD), BF16),
                   jax.ShapeDtypeStruct((MLA_HEADS, MLA_V, N_TOK), BF16)],
        compiler_params=_cparams(("arbitrary",)),
        name="mla_prep",
    )(proj, proj, proj, wq, wk, wvt, p['mla_q_norm_g'].reshape(1, -1), p['mla_kv_norm_g'].reshape(1, -1),
      gq, gkn, gkr, tabs['mla_c'], tabs['mla_s'])


def _mla_kernel(q_ref, k1_ref, vt1_ref, k2_ref, vt2_ref, o_ref, sa_ref, sb_ref, sc_ref, acc_ref,
                *, tk, n_chunks):
    q = q_ref[0]
    tq = q.shape[0]
    hq = tq // 2
    qs = (q[:hq], q[hq:])

    def scores(dst_ref, kc):
        out = []
        for j in range(2):
            st = lax.dot_general(kc, qs[j], _NT, preferred_element_type=F32)
            dst_ref[j] = st
            out.append(jnp.max(st, axis=0, keepdims=True))
        return out

    def k_chunk(c):
        return k1_ref[0, pl.ds(pl.multiple_of(c * tk, tk), tk), :]

    def vt_chunk(c):
        return vt1_ref[0, :, pl.ds(pl.multiple_of(c * tk, tk), tk)]

    def accumulate(s_ref, smax, vtc, m, l):
        m_out, l_out = [], []
        for j in range(2):
            m_new = jnp.maximum(m[j], smax[j])
            a = jnp.exp2(m[j] - m_new)
            p = jnp.exp2(s_ref[j] - m_new)
            l_out.append(a * l[j] + jnp.sum(p, axis=0, keepdims=True))
            acc_ref[j] = a * acc_ref[j] + jnp.dot(vtc, p.astype(BF16), preferred_element_type=F32)
            m_out.append(m_new)
        return m_out, l_out

    m = [jnp.full((1, hq), NEG_INF, F32)] * 2
    l = [jnp.zeros((1, hq), F32)] * 2
    acc_ref[...] = jnp.zeros_like(acc_ref)
    mx_c = scores(sc_ref, k2_ref[0])
    mx_a = scores(sa_ref, k_chunk(0))
    m, l = accumulate(sc_ref, mx_c, vt2_ref[0], m, l)

    for i in range(n_chunks // 2 - 1):
        mx_b = scores(sb_ref, k_chunk(2 * i + 1))
        m, l = accumulate(sa_ref, mx_a, vt_chunk(2 * i), m, l)
        mx_a = scores(sa_ref, k_chunk(2 * i + 2))
        m, l = accumulate(sb_ref, mx_b, vt_chunk(2 * i + 1), m, l)
    mx_b = scores(sb_ref, k_chunk(n_chunks - 1))
    m, l = accumulate(sa_ref, mx_a, vt_chunk(n_chunks - 2), m, l)
    m, l = accumulate(sb_ref, mx_b, vt_chunk(n_chunks - 1), m, l)
    for j in range(2):
        o_ref[j * hq:(j + 1) * hq, :] = (acc_ref[j] / l[j]).T.astype(o_ref.dtype)


def _mla_attn(q, k, vt, *, tq, tk):
    nq = SEQ // tq
    return pl.pallas_call(
        functools.partial(_mla_kernel, tk=tk, n_chunks=SEQ // tk),
        grid=(BATCH, MLA_HEADS, nq),
        in_specs=[pl.BlockSpec((1, tq, _MLA_PAD), lambda b, h, i: (h, b * nq + i, 0)),
                  pl.BlockSpec((1, SEQ, _MLA_PAD), lambda b, h, i: (h, b, 0)),
                  pl.BlockSpec((1, MLA_V, SEQ), lambda b, h, i: (h, 0, b)),
                  pl.BlockSpec((1, CTX_LEN, _MLA_PAD), lambda b, h, i: (h, CTX_BLK + b, 0)),
                  pl.BlockSpec((1, MLA_V, CTX_LEN), lambda b, h, i: (h, 0, CTX_BLK + b))],
        out_specs=pl.BlockSpec((tq, MLA_V), lambda b, h, i: (b * nq + i, h)),
        out_shape=jax.ShapeDtypeStruct((N_LAT, MLA_HEADS * MLA_V), BF16),
        scratch_shapes=[pltpu.VMEM((2, tk, tq // 2), F32), pltpu.VMEM((2, tk, tq // 2), F32),
                        pltpu.VMEM((2, CTX_LEN, tq // 2), F32), pltpu.VMEM((2, MLA_V, tq // 2), F32)],
        compiler_params=_cparams(("arbitrary", "arbitrary", "arbitrary")),
        name="mla_attn",
    )(q, k, vt, k, vt)


NA_RB = 8
NA_WIN_ROWS = 16
NA_QCHUNKS = 2
_NA_Q = NA_RB * GRID_W
_NA_WIN = NA_WIN_ROWS * GRID_W
assert NA_WIN_ROWS >= NA_RB + NA_ROWS - 1 and ROWS % NA_RB == 0


def _na_win_start(r0):
    return np.clip(r0 - NA_ROWS // 2, 0, ROWS - NA_WIN_ROWS)


def _na_small_pattern(r):
    half = NA_ROWS // 2
    return r if r < half else (half if r <= ROWS - half else r - (ROWS - NA_ROWS))


def _na_step_layout():
    out = []
    for r0 in (0, NA_RB, ROWS - NA_RB):
        u = int(_na_win_start(r0))
        rows = []
        for j in range(NA_RB):
            start = int(np.clip(r0 + j - NA_ROWS // 2, 0, ROWS - NA_ROWS))
            rows.append((start - u, _na_small_pattern(r0 + j)))
        out.append(rows)
    return out


def _na_kernel(q_ref, k_ref, v_ref, kc_ref, vc_ref, bias_ref, gq_ref, gk_ref, o_ref,
               kn_ref, kcn_ref, big_ref):
    rb = pl.program_id(2)
    nrb = pl.num_programs(2)

    @pl.when(rb == 0)
    def _():
        kn_ref[...] = _row_rms(k_ref[...].astype(F32), gk_ref[...]).astype(BF16)
        kcn_ref[...] = _row_rms(kc_ref[...].astype(F32), gk_ref[...]).astype(BF16)
        big_ref[...] = jnp.full(big_ref.shape, NEG_INF, F32)
        for p, rows in enumerate(_na_step_layout()):
            for j, (off, pat) in enumerate(rows):
                big_ref[p, j * GRID_W:(j + 1) * GRID_W, off * GRID_W:(off + NA_ROWS) * GRID_W] = bias_ref[0, pat]

    pattern = jnp.where(rb == 0, 0, jnp.where(rb == nrb - 1, 2, 1))
    q = _row_rms(q_ref[...].astype(F32), gq_ref[...]).astype(BF16)
    u = jnp.clip(rb * NA_RB - NA_ROWS // 2, 0, ROWS - NA_WIN_ROWS) * GRID_W
    u = pl.multiple_of(u, GRID_W)
    k = kn_ref[pl.ds(u, _NA_WIN), :]
    v = v_ref[pl.ds(u, _NA_WIN), :]
    kc = kcn_ref[...]
    vc = vc_ref[...]
    nq = _NA_Q // NA_QCHUNKS
    for j in range(NA_QCHUNKS):
        rows = slice(j * nq, (j + 1) * nq)
        qj = q[rows]
        s = lax.dot_general(qj, k, _NT, preferred_element_type=F32) + big_ref[pattern, rows, :]
        sc = lax.dot_general(qj, kc, _NT, preferred_element_type=F32)
        m = jnp.maximum(jnp.max(s, axis=-1, keepdims=True), jnp.max(sc, axis=-1, keepdims=True))
        p = jnp.exp(s - m)
        pc = jnp.exp(sc - m)
        l = jnp.sum(p, axis=-1, keepdims=True) + jnp.sum(pc, axis=-1, keepdims=True)
        o = (jnp.dot(p.astype(BF16), v, preferred_element_type=F32)
             + jnp.dot(pc.astype(BF16), vc, preferred_element_type=F32))
        o_ref[rows, :] = (o / l).astype(o_ref.dtype)


def _na_attn(proj, bias, gq, gk):
    nrb = ROWS // NA_RB
    cq, ck, cv = _pcol(3, NA_HD), _pcol(4, NA_HD), _pcol(5, NA_HD)
    return pl.pallas_call(
        _na_kernel,
        grid=(BATCH, NA_HEADS, nrb),
        in_specs=[pl.BlockSpec((_NA_Q, NA_HD), lambda b, h, r: (b * nrb + r, cq + h)),
                  pl.BlockSpec((SEQ, NA_HD), lambda b, h, r: (b, ck + h)),
                  pl.BlockSpec((SEQ, NA_HD), lambda b, h, r: (b, cv + h)),
                  pl.BlockSpec((CTX_LEN, NA_HD), lambda b, h, r: (CTX_BLK + b, ck + h)),
                  pl.BlockSpec((CTX_LEN, NA_HD), lambda b, h, r: (CTX_BLK + b, cv + h)),
                  pl.BlockSpec((1, NA_ROWS, GRID_W, NA_ROWS * GRID_W), lambda b, h, r: (h, 0, 0, 0)),
                  pl.BlockSpec((1, NA_HD), lambda b, h, r: (0, 0)),
                  pl.BlockSpec((1, NA_HD), lambda b, h, r: (0, 0))],
        out_specs=pl.BlockSpec((_NA_Q, NA_HD), lambda b, h, r: (b * nrb + r, h)),
        out_shape=jax.ShapeDtypeStruct((N_LAT, NA_HEADS * NA_HD), BF16),
        scratch_shapes=[pltpu.VMEM((SEQ, NA_HD), BF16), pltpu.VMEM((CTX_LEN, NA_HD), BF16),
                        pltpu.VMEM((3, _NA_Q, _NA_WIN), F32)],
        compiler_params=_cparams(("arbitrary", "arbitrary", "arbitrary")),
        name="na_attn",
    )(proj, proj, proj, proj, proj, bias, gq, gk)


def _na_bias_table(rpb):
    half = NA_ROWS // 2
    r_rep = np.array(list(range(half)) + [half] + list(range(ROWS - half + 1, ROWS)))
    assert all(_na_small_pattern(int(r)) == i for i, r in enumerate(r_rep))
    start = np.clip(r_rep - half, 0, ROWS - NA_ROWS)
    dr = start[:, None] + np.arange(NA_ROWS)[None, :] - r_rep[:, None] + NA_ROWS - 1
    qc = np.arange(GRID_W)
    kcol = np.arange(GRID_W)
    col_start = np.clip(qc - NA_COLS // 2, 0, GRID_W - NA_COLS)
    in_win = (kcol[None, :] >= col_start[:, None]) & (kcol[None, :] < col_start[:, None] + NA_COLS)
    dc = np.clip(kcol[None, :] - qc[:, None], 1 - NA_COLS, NA_COLS - 1) + NA_COLS - 1
    rsel = (dr[:, :, None] == np.arange(2 * NA_ROWS - 1)).astype(np.float32)
    csel = (dc[:, :, None] == np.arange(2 * NA_COLS - 1)).astype(np.float32)
    b = jnp.einsum('pja,hab,qkb->hpqjk', rsel, rpb.astype(F32), csel,
                   precision=lax.Precision.HIGHEST)
    b = jnp.where(in_win[None, None, :, None, :], b.astype(F32), NEG_INF)
    return b.reshape(NA_HEADS, NA_ROWS, GRID_W, NA_ROWS * GRID_W)


_GQA_G = GQA_HEADS // GQA_KV_HEADS
_GQA_BAND = 3 * GQA_WINDOW
_GQA_QW = GQA_HEADS * V7X_LANES


def _gqa_prep_kernel(q_ref, k_ref, gq_ref, gk_ref, c_ref, s1_ref, s2_ref, qd_ref, kn_ref):
    c = c_ref[...]
    s1 = s1_ref[...]
    s2 = s2_ref[...]
    lo = lax.broadcasted_iota(jnp.int32, (1, 128), 1) < GQA_HD

    def head_rms(x, g):
        x2 = x * x
        s_lo = jnp.sum(jnp.where(lo, x2, 0.0), axis=-1, keepdims=True)
        s_hi = jnp.sum(jnp.where(lo, 0.0, x2), axis=-1, keepdims=True)
        inv = 1.0 / GQA_HD
        r = jnp.where(lo, lax.rsqrt(s_lo * inv + EPS), lax.rsqrt(s_hi * inv + EPS))
        return x * r * g

    def rot(x):
        return x * c + pltpu.roll(x, 96, 1) * s1 + pltpu.roll(x, 32, 1) * s2

    gq = gq_ref[...]
    for j in range(GQA_HEADS // 2):
        y = rot(head_rms(q_ref[:, j * 128:(j + 1) * 128].astype(F32), gq))
        sw = pltpu.roll(y, 64, 1)
        hk = (2 * j) // _GQA_G
        if hk == 0:
            even, odd = jnp.where(lo, y, 0.0), jnp.where(lo, sw, 0.0)
        else:
            even, odd = jnp.where(lo, 0.0, sw), jnp.where(lo, 0.0, y)
        qd_ref[:, (2 * j) * 128:(2 * j + 1) * 128] = even.astype(BF16)
        qd_ref[:, (2 * j + 1) * 128:(2 * j + 2) * 128] = odd.astype(BF16)
    kn_ref[...] = rot(head_rms(k_ref[...].astype(F32), gk_ref[...])).astype(BF16)


def _gqa_prep(proj, p, tabs):
    tm = TOK_TM
    gq = (jnp.tile(p['gqa_qn_g'], 2) * (GQA_HD ** -0.5)).reshape(1, 128)
    gk = jnp.tile(p['gqa_kn_g'], 2).reshape(1, 128)
    row = pl.BlockSpec((tm, 128), lambda i: (i, 0))
    tab = pl.BlockSpec((tm, 128), lambda i: (_rope_block(i), 0))
    vec = pl.BlockSpec((1, 128), lambda i: (0, 0))
    return pl.pallas_call(
        _gqa_prep_kernel,
        grid=(N_TOK // tm,),
        in_specs=[pl.BlockSpec((tm, GQA_HEADS * GQA_HD), lambda i: (i, _pcol(6, GQA_HEADS * GQA_HD))),
                  pl.BlockSpec((tm, 128), lambda i: (i, _pcol(7, 128))),
                  vec, vec, tab, tab, tab],
        out_specs=[pl.BlockSpec((tm, _GQA_QW), lambda i: (i, 0)), row],
        out_shape=[jax.ShapeDtypeStruct((N_TOK, _GQA_QW), BF16),
                   jax.ShapeDtypeStruct((N_TOK, 128), BF16)],
        compiler_params=_cparams(("arbitrary",)),
        name="gqa_prep",
    )(proj, proj, gq, gk, tabs['gqa_c'], tabs['gqa_s1'], tabs['gqa_s2'])


def _gqa_kernel(sink_ref, q_ref, k_ref, v_ref, kc_ref, vc_ref, o_ref):
    hk = pl.program_id(1)
    n = pl.program_id(2)
    w = GQA_WINDOW
    start = pl.multiple_of(jnp.clip((n - 1) * w, 0, SEQ - _GQA_BAND), w)
    q = jnp.concatenate([q_ref[:, g * 128:(g + 1) * 128] for g in range(_GQA_G)], axis=0)
    k = k_ref[pl.ds(start, _GQA_BAND), :]
    v = v_ref[pl.ds(start, _GQA_BAND), :]
    s = lax.dot_general(q, k, _NT, preferred_element_type=F32)
    rows = lax.broadcasted_iota(jnp.int32, s.shape, 0)
    cols = lax.broadcasted_iota(jnp.int32, s.shape, 1)
    qpos = n * w + (rows & (w - 1))
    kpos = start + cols
    s = jnp.where(jnp.abs(kpos - qpos) <= GQA_WINDOW, s, NEG_INF)
    sc = lax.dot_general(q, kc_ref[...], _NT, preferred_element_type=F32)
    grow = lax.broadcasted_iota(jnp.int32, (_GQA_G * w, 1), 0) // w
    snk = jnp.full((_GQA_G * w, 1), sink_ref[hk * _GQA_G], F32)
    for g in range(1, _GQA_G):
        snk = jnp.where(grow == g, sink_ref[hk * _GQA_G + g], snk)
    m = jnp.maximum(jnp.maximum(jnp.max(s, axis=-1, keepdims=True),
                                jnp.max(sc, axis=-1, keepdims=True)), snk)
    p = jnp.exp(s - m)
    pc = jnp.exp(sc - m)
    l = jnp.sum(p, axis=-1, keepdims=True) + jnp.sum(pc, axis=-1, keepdims=True) + jnp.exp(snk - m)
    o = (jnp.dot(p.astype(BF16), v, preferred_element_type=F32)
         + jnp.dot(pc.astype(BF16), vc_ref[...], preferred_element_type=F32)) / l
    half = lax.broadcasted_iota(jnp.int32, (1, 128), 1) // GQA_HD
    o = jnp.where(half == hk, o, 0.0).astype(o_ref.dtype)
    for g in range(_GQA_G):
        o_ref[:, g * 128:(g + 1) * 128] = o[g * w:(g + 1) * w]


def _gqa_attn(qd, kn, proj, sink):
    nb = SEQ // GQA_WINDOW
    qw = _GQA_G * 128
    cv = _pcol(8, 128)
    gs = pltpu.PrefetchScalarGridSpec(
        num_scalar_prefetch=1,
        grid=(BATCH, GQA_KV_HEADS, nb),
        in_specs=[pl.BlockSpec((GQA_WINDOW, qw), lambda b, h, n, *_: (b * nb + n, h)),
                  pl.BlockSpec((SEQ, 128), lambda b, h, n, *_: (b, 0)),
                  pl.BlockSpec((SEQ, 128), lambda b, h, n, *_: (b, cv)),
                  pl.BlockSpec((CTX_LEN, 128), lambda b, h, n, *_: (CTX_BLK + b, 0)),
                  pl.BlockSpec((CTX_LEN, 128), lambda b, h, n, *_: (CTX_BLK + b, cv))],
        out_specs=pl.BlockSpec((GQA_WINDOW, qw), lambda b, h, n, *_: (b * nb + n, h)),
    )
    return pl.pallas_call(
        _gqa_kernel,
        grid_spec=gs,
        out_shape=jax.ShapeDtypeStruct((N_LAT, _GQA_QW), BF16),
        compiler_params=_cparams(("arbitrary", "arbitrary", "arbitrary")),
        name="gqa_attn",
    )(sink.astype(F32), qd, kn, proj, kn, proj)


def _ctx_kernel(sink_ref, mq_ref, mk_ref, mvt_ref, nq_ref, nk_ref, nv_ref, gnq_ref, gnk_ref,
                gq_ref, gk_ref, gv_ref, oa_ref, ob_ref, oc_ref):
    for h in range(MLA_HEADS):
        st = lax.dot_general(mk_ref[h], mq_ref[h], _NT, preferred_element_type=F32)
        p = jnp.exp2(st - jnp.max(st, axis=0, keepdims=True))
        l = jnp.sum(p, axis=0, keepdims=True)
        ot = jnp.dot(mvt_ref[h], p.astype(BF16), preferred_element_type=F32) / l
        oa_ref[:, h * MLA_V:(h + 1) * MLA_V] = ot.T.astype(oa_ref.dtype)
    for h in range(NA_HEADS):
        sl = slice(h * NA_HD, (h + 1) * NA_HD)
        q = _row_rms(nq_ref[:, sl].astype(F32), gnq_ref[...]).astype(BF16)
        k = _row_rms(nk_ref[:, sl].astype(F32), gnk_ref[...]).astype(BF16)
        s = lax.dot_general(q, k, _NT, preferred_element_type=F32)
        p = jnp.exp(s - jnp.max(s, axis=-1, keepdims=True))
        l = jnp.sum(p, axis=-1, keepdims=True)
        o = jnp.dot(p.astype(BF16), nv_ref[:, sl], preferred_element_type=F32) / l
        ob_ref[:, sl] = o.astype(ob_ref.dtype)
    half = lax.broadcasted_iota(jnp.int32, (1, 128), 1) // GQA_HD
    k = gk_ref[...]
    v = gv_ref[...]
    for h in range(GQA_HEADS):
        sl = slice(h * 128, (h + 1) * 128)
        s = lax.dot_general(gq_ref[:, sl], k, _NT, preferred_element_type=F32)
        snk = sink_ref[h]
        m = jnp.maximum(jnp.max(s, axis=-1, keepdims=True), snk)
        p = jnp.exp(s - m)
        l = jnp.sum(p, axis=-1, keepdims=True) + jnp.exp(snk - m)
        o = jnp.dot(p.astype(BF16), v, preferred_element_type=F32) / l
        oc_ref[:, sl] = jnp.where(half == h // _GQA_G, o, 0.0).astype(oc_ref.dtype)


def _ctx_attn(sink, mq, mk, mvt, proj, gnq, gnk, qd, kn):
    c = CTX_LEN
    nwid = NA_HEADS * NA_HD

    def row(width, col=0):
        return pl.BlockSpec((c, width), lambda b, *_: (CTX_BLK + b, col))

    def out(width):
        return pl.BlockSpec((c, width), lambda b, *_: (b, 0))

    vec = pl.BlockSpec((1, NA_HD), lambda b, *_: (0, 0))
    gs = pltpu.PrefetchScalarGridSpec(
        num_scalar_prefetch=1,
        grid=(BATCH,),
        in_specs=[pl.BlockSpec((MLA_HEADS, c, _MLA_PAD), lambda b, *_: (0, CTX_BLK + b, 0)),
                  pl.BlockSpec((MLA_HEADS, c, _MLA_PAD), lambda b, *_: (0, CTX_BLK + b, 0)),
                  pl.BlockSpec((MLA_HEADS, MLA_V, c), lambda b, *_: (0, 0, CTX_BLK + b)),
                  row(nwid, _pcol(3, nwid)), row(nwid, _pcol(4, nwid)), row(nwid, _pcol(5, nwid)),
                  vec, vec,
                  row(_GQA_QW), row(128), row(128, _pcol(8, 128))],
        out_specs=[out(MLA_HEADS * MLA_V), out(nwid), out(_GQA_QW)],
    )
    return pl.pallas_call(
        _ctx_kernel,
        grid_spec=gs,
        out_shape=[jax.ShapeDtypeStruct((N_CTX, MLA_HEADS * MLA_V), BF16),
                   jax.ShapeDtypeStruct((N_CTX, nwid), BF16),
                   jax.ShapeDtypeStruct((N_CTX, _GQA_QW), BF16)],
        compiler_params=_cparams(("arbitrary",)),
        name="ctx_attn",
    )(sink.astype(F32), mq, mk, mvt, proj, proj, proj, gnq, gnk, qd, kn, proj)


def _merge_kernel(*refs, has_ctx, n_lat_tiles):
    ga_ref, gb_ref, gc_ref, oa_ref, ob_ref, oc_ref = refs[:6]
    refs = refs[6:]
    if has_ctx:
        ca_ref, cb_ref, cc_ref = refs[:3]
        refs = refs[3:]
    wa_ref, wb_ref, wc_ref, y_ref = refs
    is_ctx = pl.program_id(0) >= n_lat_tiles

    def branch(g_ref, o_ref, c_ref, w_ref):
        o = o_ref[...]
        if has_ctx:
            o = jnp.where(is_ctx, c_ref[...], o)
        return jax.nn.sigmoid(g_ref[...].astype(F32)) * jnp.dot(o, w_ref[...], preferred_element_type=F32)

    y = (branch(ga_ref, oa_ref, ca_ref if has_ctx else None, wa_ref)
         + branch(gb_ref, ob_ref, cb_ref if has_ctx else None, wb_ref)
         + branch(gc_ref, oc_ref, cc_ref if has_ctx else None, wc_ref))
    y_ref[...] = y.astype(y_ref.dtype)


def _merge(proj, lat, ctx, wa, wb, wc):
    tm = TOK_TM
    has_ctx = ctx is not None
    m = N_TOK if has_ctx else N_LAT
    nl = N_LAT // tm
    ks = [o.shape[1] for o in lat]
    in_specs = [pl.BlockSpec((tm, D_MODEL), lambda i: (i, 0)),
                pl.BlockSpec((tm, D_MODEL), lambda i: (i, 1)),
                pl.BlockSpec((tm, D_MODEL), lambda i: (i, 2))]
    in_specs += [pl.BlockSpec((tm, k), lambda i: (jnp.minimum(i, nl - 1), 0)) for k in ks]
    args = [proj, proj, proj] + list(lat)
    if has_ctx:
        in_specs += [pl.BlockSpec((tm, k), lambda i: (jnp.maximum(i - nl, 0), 0)) for k in ks]
        args += list(ctx)
    in_specs += [pl.BlockSpec((k, D_MODEL), lambda i: (0, 0)) for k in ks]
    args += [wa, wb, wc]
    return pl.pallas_call(
        functools.partial(_merge_kernel, has_ctx=has_ctx, n_lat_tiles=nl),
        grid=(m // tm,),
        in_specs=in_specs,
        out_specs=pl.BlockSpec((tm, D_MODEL), lambda i: (i, 0)),
        out_shape=jax.ShapeDtypeStruct((m, D_MODEL), BF16),
        compiler_params=_cparams(("arbitrary",)),
        name="merge",
    )(*args)


def _moe_ffn_kernel(te_ref, tok_ref, nu_ref, en_ref, es_ref, h_hbm, wg_hbm, wu_hbm, wd_hbm, y_ref,
                    xbuf, sem, wgs, wus, wds, wsem, wgb, wub, wdb, *, layer):
    i = pl.program_id(0)
    tm = MOE_TM
    slot = i % 2
    n_used = nu_ref[0]

    def row_copy(tok, s, r):
        return pltpu.make_async_copy(h_hbm.at[pl.ds(tok, 1)], xbuf.at[s, pl.ds(r, 1)], sem.at[s])

    def start_gather(tile, s):
        base = tile * tm

        def body(r, c):
            row_copy(tok_ref[base + r], s, r).start()
            return c

        lax.fori_loop(0, tm, body, 0, unroll=8)

    def weight_copies(e, s):
        return (pltpu.make_async_copy(wg_hbm.at[layer, e], wgs.at[s], wsem.at[s]),
                pltpu.make_async_copy(wu_hbm.at[layer, e], wus.at[s], wsem.at[s]),
                pltpu.make_async_copy(wd_hbm.at[layer, e], wds.at[s], wsem.at[s]))

    @pl.when(i == 0)
    def _():
        for cp in weight_copies(te_ref[0], es_ref[0]):
            cp.start(priority=1)
        start_gather(0, 0)

    @pl.when(i + 1 < n_used)
    def _():
        start_gather(i + 1, 1 - slot)

    @pl.when(i < n_used)
    def _():
        @pl.when((i == 0) | (te_ref[i] != te_ref[jnp.maximum(i - 1, 0)]))
        def _():
            s = es_ref[i]

            @pl.when(en_ref[i] >= 0)
            def _():
                for cp in weight_copies(en_ref[i], 1 - s):
                    cp.start(priority=1)

            for cp in weight_copies(te_ref[i], s):
                cp.wait()
            wgb[...] = wgs[s].astype(BF16)
            wub[...] = wus[s].astype(BF16)
            wdb[...] = wds[s].astype(BF16)

        pltpu.make_async_copy(h_hbm.at[pl.ds(0, tm)], xbuf.at[slot], sem.at[slot]).wait()
        x = xbuf[slot].astype(BF16)
        hg = jnp.dot(x, wgb[...], preferred_element_type=F32)
        hu = jnp.dot(x, wub[...], preferred_element_type=F32)
        act = (hg * jax.nn.sigmoid(hg)) * hu
        y_ref[...] = jnp.dot(act.astype(BF16), wdb[...], preferred_element_type=F32)

    @pl.when(i >= n_used)
    def _():
        y_ref[...] = jnp.zeros_like(y_ref)


def _moe_ffn(h, plan, wg, wu, wd, layer):
    tile_expert, slot_token, n_used, next_expert, expert_slot = plan
    p = slot_token.shape[0]
    tm = MOE_TM
    nt = p // tm
    anyspec = pl.BlockSpec(memory_space=pl.ANY)
    gs = pltpu.PrefetchScalarGridSpec(
        num_scalar_prefetch=5,
        grid=(nt,),
        in_specs=[anyspec, anyspec, anyspec, anyspec],
        out_specs=pl.BlockSpec((tm, D_MODEL), lambda i, *_: (i, 0)),
        scratch_shapes=[pltpu.VMEM((2, tm, D_MODEL), F32),
                        pltpu.SemaphoreType.DMA((2,)),
                        pltpu.VMEM((2, D_MODEL, MOE_HIDDEN), F32),
                        pltpu.VMEM((2, D_MODEL, MOE_HIDDEN), F32),
                        pltpu.VMEM((2, MOE_HIDDEN, D_MODEL), F32),
                        pltpu.SemaphoreType.DMA((2,)),
                        pltpu.VMEM((D_MODEL, MOE_HIDDEN), BF16),
                        pltpu.VMEM((D_MODEL, MOE_HIDDEN), BF16),
                        pltpu.VMEM((MOE_HIDDEN, D_MODEL), BF16)],
    )
    return pl.pallas_call(
        functools.partial(_moe_ffn_kernel, layer=layer),
        grid_spec=gs,
        out_shape=jax.ShapeDtypeStruct((p, D_MODEL), F32),
        compiler_params=_cparams(("arbitrary",)),
        name="moe_ffn",
    )(tile_expert, slot_token, n_used, next_expert, expert_slot, h, wg, wu, wd)


def _moe_combine_kernel(pos_ref, y_hbm, x_ref, w_ref, g_ref, *rest, next_norm):
    if next_norm:
        gn_ref, sh_ref, sc_ref, o_ref, h_ref, ybuf, sem = rest
    else:
        o_ref, ybuf, sem = rest
    i = pl.program_id(0)
    nt = pl.num_programs(0)
    tm = TOK_TM
    slot = i % 2

    def row_copy(src, s, r):
        return pltpu.make_async_copy(y_hbm.at[pl.ds(src, 1)], ybuf.at[s, pl.ds(r, 1)], sem.at[s])

    def start_gather(tile, s):
        base = tile * tm

        def body(r, c):
            row_copy(pos_ref[2 * (base + r)], s, r).start(priority=0)
            row_copy(pos_ref[2 * (base + r) + 1], s, tm + r).start(priority=1)
            return c

        lax.fori_loop(0, tm, body, 0, unroll=4)

    @pl.when(i == 0)
    def _():
        start_gather(0, 0)

    @pl.when(i + 1 < nt)
    def _():
        start_gather(i + 1, 1 - slot)

    pltpu.make_async_copy(y_hbm.at[pl.ds(0, 2 * tm)], ybuf.at[slot], sem.at[slot]).wait()
    w = w_ref[...]
    y = ybuf[slot, pl.ds(0, tm), :] * w[:, 0:1] + ybuf[slot, pl.ds(tm, tm), :] * w[:, 1:2]
    xo = x_ref[...] + g_ref[0] * y
    o_ref[...] = xo
    if next_norm:
        xn = xo * lax.rsqrt(jnp.mean(xo * xo, axis=-1, keepdims=True) + EPS) * gn_ref[...]
        h_ref[...] = (xn * (1.0 + sc_ref[0]) + sh_ref[0]).astype(h_ref.dtype)


def _moe_combine(y, pos, w_sel, x, mod, k_gate, next_norm=None):
    m = x.shape[0]
    tm = TOK_TM
    row = pl.BlockSpec((tm, D_MODEL), lambda i, pos: (i, 0))
    in_specs = [pl.BlockSpec(memory_space=pl.ANY), row,
                pl.BlockSpec((tm, MOE_TOPK), lambda i, pos: (i, 0)),
                pl.BlockSpec((1, 1, D_MODEL), lambda i, pos: (_seg(i, tm) * 6 + k_gate, 0, 0))]
    args = [pos, y, x, w_sel, mod]
    out_specs = row
    out_shape = jax.ShapeDtypeStruct((m, D_MODEL), F32)
    if next_norm is not None:
        gn, mod_next = next_norm
        in_specs += [pl.BlockSpec((1, D_MODEL), lambda i, pos: (0, 0)),
                     pl.BlockSpec((1, 1, D_MODEL), lambda i, pos: (_seg(i, tm) * 6 + 0, 0, 0)),
                     pl.BlockSpec((1, 1, D_MODEL), lambda i, pos: (_seg(i, tm) * 6 + 1, 0, 0))]
        args += [gn.reshape(1, D_MODEL), mod_next, mod_next]
        out_specs = [row, row]
        out_shape = [out_shape, jax.ShapeDtypeStruct((m, D_MODEL), BF16)]
    gs = pltpu.PrefetchScalarGridSpec(
        num_scalar_prefetch=1,
        grid=(m // tm,),
        in_specs=in_specs,
        out_specs=out_specs,
        scratch_shapes=[pltpu.VMEM((2, 2 * tm, D_MODEL), F32),
                        pltpu.SemaphoreType.DMA((2,))],
    )
    return pl.pallas_call(
        functools.partial(_moe_combine_kernel, next_norm=next_norm is not None),
        grid_spec=gs,
        out_shape=out_shape,
        compiler_params=_cparams(("arbitrary",)),
        name="moe_combine",
    )(*args)


def _route(logits, m):
    tm = MOE_TM
    gp = jax.nn.softmax(logits[:, :MOE_GROUPS], axis=-1)
    g_idx = jnp.argmax(gp, axis=-1).astype(jnp.int32)[:, None]
    g_w = jnp.max(gp, axis=-1, keepdims=True)
    el = logits[:, MOE_GROUPS:MOE_GROUPS + MOE_EXPERTS].reshape(m, MOE_GROUPS, MOE_PER_GROUP)
    g_onehot = (g_idx == jnp.arange(MOE_GROUPS, dtype=jnp.int32)[None, :]).astype(F32)
    el_g = jnp.sum(el * g_onehot[:, :, None], axis=1)
    i0 = jnp.argmax(el_g, axis=-1).astype(jnp.int32)[:, None]
    l0 = jnp.max(el_g, axis=-1, keepdims=True)
    rest = jnp.where(jnp.arange(MOE_PER_GROUP, dtype=jnp.int32)[None, :] == i0, -jnp.inf, el_g)
    i1 = jnp.argmax(rest, axis=-1).astype(jnp.int32)[:, None]
    l1 = jnp.max(rest, axis=-1, keepdims=True)
    top_l = jnp.concatenate([l0, l1], axis=-1)
    top_i = jnp.concatenate([i0, i1], axis=-1)
    w_sel = jax.nn.softmax(top_l, axis=-1) * g_w
    eid = (g_idx * MOE_PER_GROUP + top_i).astype(jnp.int32)

    a = m * MOE_TOPK
    e_flat = eid.reshape(a)
    onehot = (e_flat[:, None] == jnp.arange(MOE_EXPERTS, dtype=jnp.int32)[None, :]).astype(jnp.int32)
    csum = jnp.cumsum(onehot, axis=0)
    rank = jnp.sum(csum * onehot, axis=1) - 1
    counts = csum[-1]
    padded = ((counts + tm - 1) // tm) * tm
    ends = jnp.cumsum(padded)
    starts = ends - padded
    pos = (jnp.sum(onehot * starts[None, :], axis=1) + rank).astype(jnp.int32)
    p = a + MOE_EXPERTS * tm
    slot_token = jnp.zeros((p,), jnp.int32).at[pos].set(jnp.arange(a, dtype=jnp.int32) // MOE_TOPK)
    n_used = (ends[-1] // tm).astype(jnp.int32).reshape(1)
    tile_start = jnp.arange(p // tm, dtype=jnp.int32) * tm
    last_e = jnp.max(jnp.where(counts > 0, jnp.arange(MOE_EXPERTS, dtype=jnp.int32), 0))
    tile_expert = jnp.minimum(
        jnp.sum((ends[None, :] <= tile_start[:, None]).astype(jnp.int32), axis=1), last_e)
    eidx = jnp.arange(MOE_EXPERTS, dtype=jnp.int32)
    used = counts > 0
    later = used[None, :] & (eidx[None, :] > eidx[:, None])
    next_used = jnp.min(jnp.where(later, eidx[None, :], MOE_EXPERTS), axis=1)
    next_used = jnp.where(next_used == MOE_EXPERTS, -1, next_used).astype(jnp.int32)
    ordinal = (jnp.cumsum(used.astype(jnp.int32)) - 1) % 2
    t_onehot = (tile_expert[:, None] == eidx[None, :]).astype(jnp.int32)
    next_expert = jnp.sum(t_onehot * next_used[None, :], axis=1).astype(jnp.int32)
    expert_slot = jnp.sum(t_onehot * ordinal[None, :], axis=1).astype(jnp.int32)
    return (tile_expert, slot_token, n_used, next_expert, expert_slot), w_sel, pos


def _rope_angles(rot_dim):
    t = np.arange(SEQ)
    row = (t // GRID_W).astype(np.float64)
    col = (t % GRID_W).astype(np.float64)
    n_freq = rot_dim // 4
    inv = ROPE_THETA ** (-np.arange(n_freq, dtype=np.float64) / n_freq)
    ang = np.concatenate([row[:, None] * inv, col[:, None] * inv], axis=-1)
    return np.cos(ang), np.sin(ang)


def _rope_tables():
    def rows(lat, ident):
        ctx = np.zeros((TOK_TM, 128)) + ident
        return jnp.asarray(np.concatenate([lat, ctx], axis=0).astype(np.float32))

    z32 = np.zeros((SEQ, 32))
    z64 = np.zeros((SEQ, 64))
    lane = np.arange(128)
    cm, sm = _rope_angles(MLA_ROPE)
    mla_c = rows(np.concatenate([cm, cm, z64], axis=1), (lane < 64).astype(np.float64))
    mla_s = rows(np.concatenate([-sm, sm, z64], axis=1), 0.0)
    cg, sg = _rope_angles(GQA_HD)
    gqa_c = rows(np.concatenate([cg, cg, cg, cg], axis=1), 1.0)
    gqa_s1 = rows(np.concatenate([-sg, z32, -sg, z32], axis=1), 0.0)
    gqa_s2 = rows(np.concatenate([z32, sg, z32, sg], axis=1), 0.0)
    return {'mla_c': mla_c, 'mla_s': mla_s, 'gqa_c': gqa_c, 'gqa_s1': gqa_s1, 'gqa_s2': gqa_s2}


def _rope_block(i):
    per_seq = SEQ // TOK_TM
    return jnp.where(i < BATCH * per_seq, i % per_seq, per_seq)


def _pack_w_in_kernel(w_ref, o_ref):
    bounds = (0,) + IN_SPLITS + (sum(IN_SIZES),)
    off = 0
    for i in _PACK_ORDER:
        size = bounds[i + 1] - bounds[i]
        o_ref[0, off:off + size, :] = w_ref[0, bounds[i]:bounds[i + 1], :].astype(BF16)
        off += size
    o_ref[0, off:, :] = jnp.zeros((_PACK_COLS - off, o_ref.shape[2]), BF16)


def _pack_w_in(w_in):
    tk = 256
    n_in = sum(IN_SIZES)
    wt = jnp.swapaxes(w_in, 1, 2)
    return pl.pallas_call(
        _pack_w_in_kernel,
        grid=(DEPTH, D_MODEL // tk),
        in_specs=[pl.BlockSpec((1, n_in, tk), lambda l, i: (l, 0, i))],
        out_specs=pl.BlockSpec((1, _PACK_COLS, tk), lambda l, i: (l, 0, i)),
        out_shape=jax.ShapeDtypeStruct((DEPTH, _PACK_COLS, D_MODEL), BF16),
        compiler_params=_cparams(("arbitrary", "arbitrary")),
        name="pack_w_in",
    )(wt)


def _pad_w_o_gqa(w):
    w4 = w.reshape(GQA_KV_HEADS, _GQA_G, GQA_HD, D_MODEL)
    z = jnp.zeros_like(w4[0:1])
    halves = [jnp.concatenate([w4[hk:hk + 1] if hk == half else z for hk in range(GQA_KV_HEADS)], axis=0)
              for half in range(GQA_KV_HEADS)]
    return jnp.stack(halves, axis=2).reshape(_GQA_QW, D_MODEL).astype(BF16)


def _token_mixer(h, p, tabs, ctx_out):
    proj = _mm(h, p['w_in_packed'], p['layer'], BF16, _PACK_COLS // 4)
    mq, mk, mvt = _mla_prep(proj, p, tabs)
    oa = _mla_attn(mq, mk, mvt, tq=1024, tk=1024)
    gnq = (p['na_qn_g'] * (NA_HD ** -0.5)).reshape(1, NA_HD)
    gnk = p['na_kn_g'].reshape(1, NA_HD)
    ob = _na_attn(proj, _na_bias_table(p['na_rpb']), gnq, gnk)
    qd, kn = _gqa_prep(proj, p, tabs)
    oc = _gqa_attn(qd, kn, proj, p['gqa_sink'])
    ctx = _ctx_attn(p['gqa_sink'], mq, mk, mvt, proj, gnq, gnk, qd, kn) if ctx_out else None
    return _merge(proj, (oa, ob, oc), ctx, p['w_o_mla'].astype(BF16), p['w_o_na'].astype(BF16),
                  _pad_w_o_gqa(p['w_o_gqa']))


def _post_mixer(y, xt, mod, norm_g, w_out_l, p, next_norm=None):
    m = y.shape[0]
    wr = jnp.concatenate([p['moe_w_group'], p['moe_w_expert'],
                          jnp.zeros((D_MODEL, ROUTE_COLS - MOE_GROUPS - MOE_EXPERTS), F32)], axis=1)
    br = jnp.concatenate([p['moe_b_group'], p['moe_b_expert'],
                          jnp.zeros((ROUTE_COLS - MOE_GROUPS - MOE_EXPERTS,), F32)]).reshape(1, ROUTE_COLS)
    x_mid, h, logits = _out_route(y, w_out_l.astype(BF16), xt, mod, norm_g, wr, br)
    plan, w_sel, pos = _route(logits, m)
    yy = _moe_ffn(h, plan, p['moe_w_gate'], p['moe_w_up'], p['moe_w_down'], p['layer'])
    return _moe_combine(yy, pos, w_sel, x_mid, mod, 5, next_norm)


def kernel(x, c, ctx, c_ctx, ada_w, ada_b, norm_mix_g, norm_ffn_g, w_in,
           mla_q_norm_g, mla_w_uq, mla_kv_norm_g, mla_w_ukv, mla_qn_g, mla_kn_g,
           na_qn_g, na_kn_g, na_rpb, gqa_qn_g, gqa_kn_g, gqa_sink,
           w_o_mla, w_o_na, w_o_gqa, w_out,
           moe_w_group, moe_b_group, moe_w_expert, moe_b_expert,
           moe_w_gate, moe_w_up, moe_w_down):
    xt = (x.reshape(N_LAT, D_MODEL), ctx.reshape(N_CTX, D_MODEL))
    c_rows = jnp.concatenate([c, c_ctx[None, :], jnp.zeros((8 - BATCH - 1, D_MODEL), F32)], axis=0)
    mod_all = _ada(c_rows, ada_w, ada_b)
    tabs = _rope_tables()
    w_in_packed = _pack_w_in(w_in)
    for l in range(DEPTH):
        ctx_out = l < DEPTH - 1
        p = {
            'w_in_packed': w_in_packed, 'mla_q_norm_g': mla_q_norm_g[l], 'mla_w_uq': mla_w_uq[l],
            'mla_kv_norm_g': mla_kv_norm_g[l], 'mla_w_ukv': mla_w_ukv[l],
            'mla_qn_g': mla_qn_g[l], 'mla_kn_g': mla_kn_g[l],
            'na_qn_g': na_qn_g[l], 'na_kn_g': na_kn_g[l], 'na_rpb': na_rpb[l],
            'gqa_qn_g': gqa_qn_g[l], 'gqa_kn_g': gqa_kn_g[l], 'gqa_sink': gqa_sink[l],
            'w_o_mla': w_o_mla[l], 'w_o_na': w_o_na[l], 'w_o_gqa': w_o_gqa[l],
            'moe_w_group': moe_w_group[l], 'moe_b_group': moe_b_group[l],
            'moe_w_expert': moe_w_expert[l], 'moe_b_expert': moe_b_expert[l],
            'moe_w_gate': moe_w_gate, 'moe_w_up': moe_w_up, 'moe_w_down': moe_w_down, 'layer': l,
        }
        mod = mod_all[l].reshape(8 * 6, 1, D_MODEL)
        if l == 0:
            h = _norm_mod(xt, norm_mix_g[l], mod, 0, 1)
        y = _token_mixer(h, p, tabs, ctx_out)
        if ctx_out:
            nxt = (norm_mix_g[l + 1], mod_all[l + 1].reshape(8 * 6, 1, D_MODEL))
            xt, h = _post_mixer(y, xt, mod, norm_ffn_g[l], w_out[l], p, nxt)
        else:
            xt = _post_mixer(y, xt, mod, norm_ffn_g[l], w_out[l], p)
    return xt[:N_LAT].reshape(BATCH, SEQ, D_MODEL)
```
